```python
import math
import jax, jax.numpy as jnp
from jax import lax
import numpy as np

D_MODEL = 1024
BATCH = 2
SEQ = 8192
DEPTH = 4

NSA_HEADS = 8
NSA_KV_GROUPS = 2
NSA_HPG = NSA_HEADS // NSA_KV_GROUPS
NSA_HEAD_DIM = 64
CMP_LEN = 32
CMP_STRIDE = 16
SLC_BLOCK = 64
N_SELECT = 16
WINDOW = 512
Q_BLOCK = 128
FORCE_SCORE = 1.0e4
RET_HEADS = 8
RET_QK_DIM = 64
RET_V_DIM = 128
RET_CHUNK = 128
ROPE_BASE = 10000.0
REL_BUCKETS = 32
REL_MAX_EXACT = 16
REL_MAX_DIST = 2048
N_EXPERTS = 16
N_GROUPS = 4
EXPERTS_PER_GROUP = N_EXPERTS // N_GROUPS
TOP_K = 2
D_EXPERT = 512
MOE_BLOCK = 128
DN_ALPHA = (2 * DEPTH) ** 0.25
DN_BETA = (8 * DEPTH) ** -0.25
LN_EPS = 1e-5
GN_EPS = 1e-5
NEG_INF = -1e30

W_NSA_Q = NSA_HEADS * NSA_HEAD_DIM
W_NSA_KV = NSA_KV_GROUPS * NSA_HEAD_DIM
W_NSA_GATE = NSA_HEADS * 3
W_RET_QK = RET_HEADS * RET_QK_DIM
W_RET_V = RET_HEADS * RET_V_DIM
IN_WIDTHS = (W_NSA_Q, W_NSA_KV, W_NSA_KV, W_NSA_KV, W_NSA_KV, W_NSA_KV, W_NSA_KV, W_NSA_GATE,
             W_RET_QK, W_RET_QK, W_RET_V, W_RET_V, D_MODEL, D_MODEL)
D_IN = sum(IN_WIDTHS)

kernel_name = 'nsa_retention_gated_moe_deepnorm'


def layer_norm(x, g, b):
    xf = x.astype(jnp.float32)
    mu = jnp.mean(xf, axis=-1, keepdims=True)
    var = jnp.mean(jnp.square(xf - mu), axis=-1, keepdims=True)
    return ((xf - mu) * lax.rsqrt(var + LN_EPS) * g + b).astype(x.dtype)


def t5_bucket(rel):
    n = jnp.maximum(rel, 0)
    nf = jnp.maximum(n, 1).astype(jnp.float32)
    large = REL_MAX_EXACT + (jnp.log(nf / REL_MAX_EXACT) / math.log(REL_MAX_DIST / REL_MAX_EXACT)
                             * (REL_BUCKETS - REL_MAX_EXACT)).astype(jnp.int32)
    large = jnp.minimum(large, REL_BUCKETS - 1)
    return jnp.where(n < REL_MAX_EXACT, n, large)


def masked_softmax(s, mask):
    s = jnp.where(mask, s, NEG_INF)
    s = s - jnp.max(s, axis=-1, keepdims=True)
    p = jnp.exp(s) * mask
    return p / jnp.maximum(jnp.sum(p, axis=-1, keepdims=True), 1e-30)


def nsa_compress(kv, pos, w1, w2):
    b, s, g, d = kv.shape
    nc = (s - CMP_LEN) // CMP_STRIDE + 1
    idx = (jnp.arange(nc) * CMP_STRIDE)[:, None] + jnp.arange(CMP_LEN)[None, :]
    blocks = kv[:, idx] + pos[None, None, :, None, :]
    flat = blocks.transpose(0, 3, 1, 2, 4).reshape(b, g, nc, CMP_LEN * d)
    return jax.nn.gelu(flat @ w1) @ w2


def nsa_attention(q, k_cmp, v_cmp, k_slc, v_slc, k_win, v_win, gate_logits,
                  cmp_pos, cmp_w1, cmp_w2, rel_bias):
    b, s, h, d = q.shape
    g, r = NSA_KV_GROUPS, NSA_HPG
    scale = d ** -0.5
    qg = q.reshape(b, s, g, r, d).transpose(0, 2, 3, 1, 4)
    gates = jax.nn.sigmoid(gate_logits.astype(jnp.float32)).reshape(b, s, g, r, 3).transpose(0, 2, 3, 1, 4)
    kc = nsa_compress(k_cmp, cmp_pos[0], cmp_w1[0], cmp_w2[0])
    vc = nsa_compress(v_cmp, cmp_pos[1], cmp_w1[1], cmp_w2[1])
    nc = kc.shape[2]
    c_start = jnp.arange(nc) * CMP_STRIDE
    cmp_end = c_start + CMP_LEN - 1
    ns = s // SLC_BLOCK
    n_sel = min(N_SELECT, ns)
    ks_blk = k_slc.transpose(0, 2, 1, 3).reshape(b, g, ns, SLC_BLOCK, d)
    vs_blk = v_slc.transpose(0, 2, 1, 3).reshape(b, g, ns, SLC_BLOCK, d)
    s_start = jnp.arange(ns) * SLC_BLOCK
    cover = jnp.maximum(jnp.minimum(c_start[:, None] + CMP_LEN, s_start[None, :] + SLC_BLOCK)
                        - jnp.maximum(c_start[:, None], s_start[None, :]), 0).astype(jnp.float32) / CMP_LEN
    kw = jnp.pad(k_win.transpose(0, 2, 1, 3), ((0, 0), (0, 0), (WINDOW, 0), (0, 0)))
    vw = jnp.pad(v_win.transpose(0, 2, 1, 3), ((0, 0), (0, 0), (WINDOW, 0), (0, 0)))
    bias_g = rel_bias.reshape(REL_BUCKETS, g, r)
    b_ix = jnp.arange(b)[:, None, None, None]
    g_ix = jnp.arange(g)[None, :, None, None]
    blk_ids = jnp.arange(ns)

    def head_bias(bucket):
        return rel_bias[bucket].transpose(2, 0, 1).reshape(g, r, *bucket.shape)

    def query_block(qi):
        q0 = qi * Q_BLOCK
        t = q0 + jnp.arange(Q_BLOCK)
        qb = lax.dynamic_slice_in_dim(qg, q0, Q_BLOCK, axis=3)
        gb = lax.dynamic_slice_in_dim(gates, q0, Q_BLOCK, axis=3)
        s_c = jnp.einsum('bgrqd,bgkd->bgrqk', qb, kc).astype(jnp.float32) * scale \
            + head_bias(t5_bucket(t[:, None] - cmp_end[None, :]))
        p_c = masked_softmax(s_c, cmp_end[None, :] <= t[:, None])
        o_c = jnp.einsum('bgrqk,bgkd->bgrqd', p_c.astype(vc.dtype), vc)
        imp = jnp.einsum('bgrqk,kj->bgqj', p_c, cover)
        cur = (t // SLC_BLOCK)[:, None]
        eligible = blk_ids[None, :] <= cur
        forced = (blk_ids[None, :] == 0) | (blk_ids[None, :] == cur) | (blk_ids[None, :] == cur - 1)
        score = jnp.where(eligible, imp + jnp.where(forced, FORCE_SCORE, 0.0), -1.0)
        top_s, sel = lax.top_k(score, n_sel)
        ks = ks_blk[b_ix, g_ix, sel].reshape(b, g, Q_BLOCK, n_sel * SLC_BLOCK, d)
        vs = vs_blk[b_ix, g_ix, sel].reshape(b, g, Q_BLOCK, n_sel * SLC_BLOCK, d)
        kpos = (sel[..., None] * SLC_BLOCK + jnp.arange(SLC_BLOCK)).reshape(b, g, Q_BLOCK, n_sel * SLC_BLOCK)
        valid = jnp.broadcast_to((top_s >= 0.0)[..., None], sel.shape + (SLC_BLOCK,)).reshape(kpos.shape)
        rel_s = t[None, None, :, None] - kpos
        mask_s = valid & (rel_s >= 0)
        bias_s = jnp.moveaxis(bias_g[t5_bucket(rel_s), g_ix], -1, 2)
        s_s = jnp.einsum('bgrqd,bgqkd->bgrqk', qb, ks).astype(jnp.float32) * scale + bias_s
        p_s = masked_softmax(s_s, mask_s[:, :, None])
        o_s = jnp.einsum('bgrqk,bgqkd->bgrqd', p_s.astype(vs.dtype), vs)
        kwb = lax.dynamic_slice_in_dim(kw, q0, Q_BLOCK + WINDOW, axis=2)
        vwb = lax.dynamic_slice_in_dim(vw, q0, Q_BLOCK + WINDOW, axis=2)
        kpos_w = q0 - WINDOW + jnp.arange(Q_BLOCK + WINDOW)
        rel_w = t[:, None] - kpos_w[None, :]
        mask_w = (kpos_w[None, :] >= 0) & (rel_w >= 0) & (rel_w < WINDOW)
        s_w = jnp.einsum('bgrqd,bgkd->bgrqk', qb, kwb).astype(jnp.float32) * scale + head_bias(t5_bucket(rel_w))
        p_w = masked_softmax(s_w, mask_w)
        o_w = jnp.einsum('bgrqk,bgkd->bgrqd', p_w.astype(vwb.dtype), vwb)
        o = gb[..., 0:1] * o_c + gb[..., 1:2] * o_s + gb[..., 2:3] * o_w
        return o.astype(q.dtype).transpose(0, 3, 1, 2, 4).reshape(b, Q_BLOCK, h * d)

    out = lax.map(query_block, jnp.arange(s // Q_BLOCK))
    return out.transpose(1, 0, 2, 3).reshape(b, s, h * d)


def rotary(x):
    s, dk = x.shape[1], x.shape[-1]
    half = dk // 2
    inv = ROPE_BASE ** (-jnp.arange(half, dtype=jnp.float32) * 2.0 / dk)
    ang = jnp.arange(s, dtype=jnp.float32)[:, None] * inv[None, :]
    cos = jnp.cos(ang)[None, :, None, :]
    sin = jnp.sin(ang)[None, :, None, :]
    x1, x2 = x[..., :half], x[..., half:]
    return jnp.concatenate([x1 * cos - x2 * sin, x1 * sin + x2 * cos], axis=-1)


def retention(q, k, v, gate):
    b, s, h, dk = q.shape
    dv = v.shape[-1]
    c = RET_CHUNK
    n = s // c

    def chunks(a):
        return a.reshape(b, n, c, h, -1).transpose(0, 3, 1, 2, 4)

    qc = chunks(rotary(q.astype(jnp.float32)))
    kc = chunks(rotary(k.astype(jnp.float32)) * dk ** -0.5)
    vc = chunks(v.astype(jnp.float32))
    lg = jnp.log(1.0 - 2.0 ** (-5.0 - jnp.arange(h, dtype=jnp.float32)))
    i = jnp.arange(c, dtype=jnp.float32)
    diff = i[:, None] - i[None, :]
    dmat = jnp.where(diff >= 0, jnp.exp(jnp.maximum(diff, 0.0)[None] * lg[:, None, None]), 0.0)
    intra = jnp.einsum('bhncj,bhnje->bhnce',
                       jnp.einsum('bhncd,bhnjd->bhncj', qc, kc) * dmat[None, :, None], vc)
    k_dec = kc * jnp.exp((c - 1.0 - i)[None, :] * lg[:, None])[None, :, None, :, None]
    kv = jnp.einsum('bhncd,bhnce->bhnde', k_dec, vc)
    chunk_decay = jnp.exp(c * lg)[None, :, None, None]

    def step(state, kv_n):
        return chunk_decay * state + kv_n, state

    _, s_prev = lax.scan(step, jnp.zeros((b, h, dk, dv), jnp.float32), jnp.moveaxis(kv, 2, 0))
    s_prev = jnp.moveaxis(s_prev, 0, 2)
    q_dec = qc * jnp.exp((i + 1.0)[None, :] * lg[:, None])[None, :, None, :, None]
    y = intra + jnp.einsum('bhncd,bhnde->bhnce', q_dec, s_prev)
    y = y.transpose(0, 2, 3, 1, 4).reshape(b, s, h, dv)
    mu = jnp.mean(y, axis=-1, keepdims=True)
    var = jnp.mean(jnp.square(y - mu), axis=-1, keepdims=True)
    y = ((y - mu) * lax.rsqrt(var + GN_EPS)).reshape(b, s, h * dv)
    return (jax.nn.silu(gate.astype(jnp.float32)) * y).astype(v.dtype)


def moe_ffn(h, w_router, router_bias, w_gate, w_up, w_down):
    b, s, d = h.shape
    t = b * s
    hf = h.reshape(t, d)
    probs = jax.nn.softmax((hf @ w_router).astype(jnp.float32), axis=-1)
    sel = (probs + router_bias.astype(jnp.float32)).reshape(t, N_GROUPS, EXPERTS_PER_GROUP)
    group_score = jnp.sum(lax.top_k(sel, TOP_K)[0], axis=-1)
    best = jnp.argmax(group_score, axis=-1)
    in_group = sel[jnp.arange(t), best]
    _, local = lax.top_k(in_group, TOP_K)
    eidx = best[:, None] * EXPERTS_PER_GROUP + local
    wts = jnp.take_along_axis(probs, eidx, axis=1)
    wts = wts / jnp.sum(wts, axis=-1, keepdims=True)
    tk = t * TOP_K
    flat_e = eidx.reshape(tk)
    flat_tok = jnp.arange(tk) // TOP_K
    flat_w = wts.reshape(tk)
    order = jnp.argsort(flat_e)
    se = flat_e[order]
    counts = jnp.bincount(flat_e, length=N_EXPERTS)
    padded = ((counts + MOE_BLOCK - 1) // MOE_BLOCK) * MOE_BLOCK
    pad_end = jnp.cumsum(padded)
    pad_start = pad_end - padded
    start = jnp.cumsum(counts) - counts
    dest = pad_start[se] + jnp.arange(tk) - start[se]
    n_blk = -(-(tk + N_EXPERTS * MOE_BLOCK) // MOE_BLOCK)
    rows = n_blk * MOE_BLOCK
    row_tok = jnp.zeros((rows,), jnp.int32).at[dest].set(flat_tok[order])
    row_w = jnp.zeros((rows,), jnp.float32).at[dest].set(flat_w[order])
    blk_e = jnp.minimum(jnp.searchsorted(pad_end, jnp.arange(n_blk) * MOE_BLOCK, side='right'), N_EXPERTS - 1)
    xs = hf[row_tok].reshape(n_blk, MOE_BLOCK, d)

    def expert_block(args):
        xb, e = args
        return (jax.nn.silu(xb @ w_gate[e]) * (xb @ w_up[e])) @ w_down[e]

    ys = lax.map(expert_block, (xs, blk_e)).reshape(rows, d)
    out = jax.ops.segment_sum(ys * row_w[:, None].astype(ys.dtype), row_tok, num_segments=t)
    return out.reshape(b, s, d)


def setup_inputs(seed: int = 0) -> dict:
    key = jax.random.key(seed)
    ks = jax.random.split(key, 18)

    def nrm(k, shape, scale):
        return jax.random.normal(k, shape, jnp.float32) * scale

    return {
        'x': nrm(ks[0], (BATCH, SEQ, D_MODEL), 1.0),
        'w_in': nrm(ks[1], (DEPTH, D_MODEL, D_IN), D_MODEL ** -0.5),
        'cmp_pos': nrm(ks[2], (DEPTH, 2, CMP_LEN, NSA_HEAD_DIM), 0.1),
        'cmp_w1': nrm(ks[3], (DEPTH, 2, CMP_LEN * NSA_HEAD_DIM, NSA_HEAD_DIM), (CMP_LEN * NSA_HEAD_DIM) ** -0.5),
        'cmp_w2': nrm(ks[4], (DEPTH, 2, NSA_HEAD_DIM, NSA_HEAD_DIM), NSA_HEAD_DIM ** -0.5),
        'w_br_a': nrm(ks[5], (DEPTH, W_NSA_Q, D_MODEL), W_NSA_Q ** -0.5),
        'w_br_b': nrm(ks[6], (DEPTH, W_RET_V, D_MODEL), W_RET_V ** -0.5),
        'w_o': nrm(ks[7], (DEPTH, D_MODEL, D_MODEL), D_MODEL ** -0.5 * DN_BETA),
        'ln1_g': 1.0 + nrm(ks[8], (DEPTH, D_MODEL), 0.02),
        'ln1_b': nrm(ks[9], (DEPTH, D_MODEL), 0.02),
        'w_router': nrm(ks[10], (D_MODEL, N_EXPERTS), D_MODEL ** -0.5),
        'router_bias': nrm(ks[11], (N_EXPERTS,), 0.01),
        'moe_w_gate': nrm(ks[12], (DEPTH, N_EXPERTS, D_MODEL, D_EXPERT), D_MODEL ** -0.5),
        'moe_w_up': nrm(ks[13], (DEPTH, N_EXPERTS, D_MODEL, D_EXPERT), D_MODEL ** -0.5),
        'moe_w_down': nrm(ks[14], (DEPTH, N_EXPERTS, D_EXPERT, D_MODEL), D_EXPERT ** -0.5 * DN_BETA),
        'ln2_g': 1.0 + nrm(ks[15], (DEPTH, D_MODEL), 0.02),
        'ln2_b': nrm(ks[16], (DEPTH, D_MODEL), 0.02),
        'rel_bias': nrm(ks[17], (REL_BUCKETS, NSA_HEADS), 0.2),
    }


def reference(x, w_in, cmp_pos, cmp_w1, cmp_w2, w_br_a, w_br_b, w_o, ln1_g, ln1_b,
              w_router, router_bias, moe_w_gate, moe_w_up, moe_w_down, ln2_g, ln2_b, rel_bias):
    b, s, _ = x.shape
    offsets = [int(o) for o in np.cumsum(IN_WIDTHS)[:-1]]
    for l in range(DEPTH):
        proj = x @ w_in[l]
        (q_a, k_c, v_c, k_s, v_s, k_w, v_w, g_a,
         q_r, k_r, v_r, g_r, m_a, m_b) = jnp.split(proj, offsets, axis=-1)
        y_a = nsa_attention(q_a.reshape(b, s, NSA_HEADS, NSA_HEAD_DIM),
                            k_c.reshape(b, s, NSA_KV_GROUPS, NSA_HEAD_DIM),
                            v_c.reshape(b, s, NSA_KV_GROUPS, NSA_HEAD_DIM),
                            k_s.reshape(b, s, NSA_KV_GROUPS, NSA_HEAD_DIM),
                            v_s.reshape(b, s, NSA_KV_GROUPS, NSA_HEAD_DIM),
                            k_w.reshape(b, s, NSA_KV_GROUPS, NSA_HEAD_DIM),
                            v_w.reshape(b, s, NSA_KV_GROUPS, NSA_HEAD_DIM),
                            g_a.reshape(b, s, NSA_HEADS, 3),
                            cmp_pos[l], cmp_w1[l], cmp_w2[l], rel_bias)
        y_b = retention(q_r.reshape(b, s, RET_HEADS, RET_QK_DIM),
                        k_r.reshape(b, s, RET_HEADS, RET_QK_DIM),
                        v_r.reshape(b, s, RET_HEADS, RET_V_DIM), g_r)
        merged = jax.nn.sigmoid(m_a) * (y_a @ w_br_a[l]) + jax.nn.sigmoid(m_b) * (y_b @ w_br_b[l])
        x = layer_norm(DN_ALPHA * x + merged @ w_o[l], ln1_g[l], ln1_b[l])
        x = layer_norm(DN_ALPHA * x + moe_ffn(x, w_router, router_bias, moe_w_gate[l], moe_w_up[l], moe_w_down[l]),
                       ln2_g[l], ln2_b[l])
    return x
```

```python
import functools
import math

import jax
import jax.numpy as jnp
import numpy as np
from jax import lax
from jax.experimental import pallas as pl
from jax.experimental.pallas import tpu as pltpu

F32 = jnp.float32
BF16 = jnp.bfloat16
I32 = jnp.int32

NSA_HEADS = 8
NSA_GROUPS = 2
NSA_HPG = 4
NSA_DH = 64
CMP_LEN = 32
CMP_STRIDE = 16
SLC_BLOCK = 64
N_SELECT = 16
WINDOW = 512
QB = 128
FORCE_SCORE = 1.0e4
RET_HEADS = 8
RET_DK = 64
RET_DV = 128
RET_CHUNK = 128
ROPE_BASE = 10000.0
REL_BUCKETS = 32
REL_MAX_EXACT = 16
REL_MAX_DIST = 2048
N_EXPERTS = 16
N_GROUPS = 4
EPG = 4
D_EXPERT = 512
LN_EPS = 1e-5
GN_EPS = 1e-5
NEG_INF = -1e30

LANES = 128
VMEM_LIMIT = 56 * 1024 * 1024

SEL_TK = 256
N_OFF = 14
MOE_BM = 256
DISPATCH_CHUNK = 128


def _cparams(sem):
    return pltpu.CompilerParams(dimension_semantics=sem, vmem_limit_bytes=VMEM_LIMIT)


def _dot(a, b):
    return jnp.dot(a, b, preferred_element_type=F32)


def _dot_nt(a, b):
    return lax.dot_general(a, b, (((1,), (1,)), ((), ())), preferred_element_type=F32)


def _mm_kernel(a_ref, b_ref, o_ref):
    o_ref[...] = _dot(a_ref[...], b_ref[...]).astype(o_ref.dtype)


def _matmul(a, b, out_dtype, tm, tn):
    m, k = a.shape
    n = b.shape[1]
    return pl.pallas_call(
        _mm_kernel,
        grid=(n // tn, m // tm),
        in_specs=[pl.BlockSpec((tm, k), lambda j, i: (i, 0)),
                  pl.BlockSpec((k, tn), lambda j, i: (0, j))],
        out_specs=pl.BlockSpec((tm, tn), lambda j, i: (i, j)),
        out_shape=jax.ShapeDtypeStruct((m, n), out_dtype),
        compiler_params=_cparams(("arbitrary", "arbitrary")),
        name="proj_matmul",
    )(a, b)


def _bias_kernel(tab_ref, rel_ref, o_ref):
    rel = rel_ref[...]
    n = jnp.maximum(rel, 0)
    nf = jnp.maximum(n, 1).astype(F32)
    large = REL_MAX_EXACT + (jnp.log(nf / REL_MAX_EXACT) / math.log(REL_MAX_DIST / REL_MAX_EXACT)
                             * (REL_BUCKETS - REL_MAX_EXACT)).astype(I32)
    large = jnp.minimum(large, REL_BUCKETS - 1)
    bucket = jnp.where(n < REL_MAX_EXACT, n, large)
    eqs = [bucket == k for k in range(REL_BUCKETS)]
    for h in range(NSA_HEADS):
        acc = jnp.full(rel.shape, NEG_INF, F32)
        for k in range(REL_BUCKETS):
            acc = jnp.where(eqs[k], tab_ref[k * NSA_HEADS + h], acc)
        o_ref[h] = jnp.where(rel >= 0, acc, NEG_INF)


def _bias_table(rel_bias_flat, rel, tr):
    r, c = rel.shape
    return pl.pallas_call(
        _bias_kernel,
        grid_spec=pltpu.PrefetchScalarGridSpec(
            num_scalar_prefetch=1,
            grid=(r // tr,),
            in_specs=[pl.BlockSpec((tr, c), lambda i, tab: (i, 0))],
            out_specs=pl.BlockSpec((NSA_HEADS, tr, c), lambda i, tab: (0, i, 0)),
        ),
        out_shape=jax.ShapeDtypeStruct((NSA_HEADS, r, c), F32),
        compiler_params=_cparams(("arbitrary",)),
        name="t5_bias_table",
    )(rel_bias_flat, rel)


def _compress_kernel(x_ref, w1_ref, pos_ref, w2_ref, o_ref):
    ab = _dot(x_ref[...], w1_ref[...])
    nb = ab.shape[0]
    nxt = pltpu.roll(ab, nb - 1, axis=0)
    nxt = pltpu.roll(nxt, NSA_DH, axis=1)
    pre = (ab + nxt)[:, :NSA_DH] + pos_ref[...]
    hid = jax.nn.gelu(pre)
    o_ref[...] = _dot(hid.astype(BF16), w2_ref[...]).astype(o_ref.dtype)


def _compress(x16, w1ab, posterm, w2):
    _, b, g, nb, wd = x16.shape
    return pl.pallas_call(
        _compress_kernel,
        grid=(2, b, g),
        in_specs=[pl.BlockSpec((None, None, None, nb, wd), lambda s, bi, gi: (s, bi, gi, 0, 0)),
                  pl.BlockSpec((None, wd, 2 * NSA_DH), lambda s, bi, gi: (s, 0, 0)),
                  pl.BlockSpec((None, 1, NSA_DH), lambda s, bi, gi: (s, 0, 0)),
                  pl.BlockSpec((None, NSA_DH, NSA_DH), lambda s, bi, gi: (s, 0, 0))],
        out_specs=pl.BlockSpec((None, None, None, nb, NSA_DH), lambda s, bi, gi: (s, bi, gi, 0, 0)),
        out_shape=jax.ShapeDtypeStruct((2, b, g, nb, NSA_DH), BF16),
        compiler_params=_cparams(("arbitrary",) * 3),
        name="nsa_compress",
    )(x16, w1ab, posterm, w2)


def _posterm_kernel(p_ref, w_ref, o_ref):
    o_ref[...] = _dot(p_ref[...], w_ref[...])[0:1]


def _posterm(pos8, w1):
    return pl.pallas_call(
        _posterm_kernel,
        grid=(2,),
        in_specs=[pl.BlockSpec((None, 8, pos8.shape[2]), lambda s: (s, 0, 0)),
                  pl.BlockSpec((None, w1.shape[1], NSA_DH), lambda s: (s, 0, 0))],
        out_specs=pl.BlockSpec((None, 1, NSA_DH), lambda s: (s, 0, 0)),
        out_shape=jax.ShapeDtypeStruct((2, 1, NSA_DH), F32),
        name="nsa_posterm",
    )(pos8, w1)


def _softmax_rows(s):
    m = jnp.max(s, axis=-1, keepdims=True)
    p = jnp.exp(s - m)
    return p, jnp.sum(p, axis=-1, keepdims=True)


def _nsa_kernel(q_ref, kc_ref, vc_ref, ks_ref, vs_ref, kw_ref, vw_ref, gate_ref,
                cb_ref, bt_ref, wb_ref, cov_ref, exp_ref, o_ref, *, n_sel):
    i = pl.program_id(2)
    q0 = i * QB
    r4 = NSA_HPG * QB
    q = (q_ref[...].astype(F32) * (NSA_DH ** -0.5)).astype(BF16)

    cb = cb_ref[...].reshape(r4, cb_ref.shape[-1])
    s_c = _dot_nt(q, kc_ref[...]) + cb
    valid_c = cb > 0.5 * NEG_INF
    p_c, l_c = _softmax_rows(s_c)
    p_c = jnp.where(valid_c, p_c, 0.0)
    l_c = jnp.sum(p_c, axis=-1, keepdims=True)
    p_c = p_c / jnp.maximum(l_c, 1e-30)
    o_c = _dot(p_c.astype(BF16), vc_ref[...])

    psum = p_c[0:QB] + p_c[QB:2 * QB] + p_c[2 * QB:3 * QB] + p_c[3 * QB:4 * QB]
    p_hi = psum.astype(BF16)
    p_lo = (psum - p_hi.astype(F32)).astype(BF16)
    cov = cov_ref[...]
    imp_t = _dot_nt(cov, p_hi) + _dot_nt(cov, p_lo)
    ns = imp_t.shape[0]
    jidx = lax.broadcasted_iota(I32, (ns, QB), 0)
    t = q0 + lax.broadcasted_iota(I32, (ns, QB), 1)
    cur = jnp.right_shift(t, 6)
    eligible = jidx <= cur
    forced = (jidx == 0) | (jidx == cur) | (jidx == cur - 1)
    score = jnp.where(eligible, imp_t + jnp.where(forced, FORCE_SCORE, 0.0), -1.0)
    sel = jnp.zeros((ns, QB), F32)
    for _ in range(n_sel):
        mx = jnp.max(score, axis=0, keepdims=True)
        first = jnp.min(jnp.where(score == mx, jidx, ns), axis=0, keepdims=True)
        pick = jidx == first
        sel = jnp.where(pick, 1.0, sel)
        score = jnp.where(pick, -2.0, score)
    sel = jnp.where(eligible, sel, 0.0)
    selmask = sel.T.astype(BF16)

    sub = SEL_TK // QB
    n_tiles = i // sub + 1

    def sel_body(jt, carry):
        m_prev, l_prev, acc = carry
        k0 = pl.multiple_of(jt * SEL_TK, SEL_TK)
        k = ks_ref[pl.ds(k0, SEL_TK), :]
        v = vs_ref[pl.ds(k0, SEL_TK), :]
        s = _dot_nt(q, k).reshape(NSA_HPG, QB, SEL_TK)
        parts = []
        for u in range(sub):
            off = jnp.clip(i - sub * jt - u, 0, N_OFF - 1)
            parts.append(bt_ref[off])
        bias = parts[0] if sub == 1 else jnp.concatenate(parts, axis=-1)
        maskf = _dot(selmask, exp_ref[jt])
        s = jnp.where((maskf > 0.5)[None], s + bias, NEG_INF).reshape(r4, SEL_TK)
        m_new = jnp.maximum(m_prev, jnp.max(s, axis=-1, keepdims=True))
        alpha = jnp.exp(m_prev - m_new)
        p = jnp.exp(s - m_new)
        l_new = alpha * l_prev + jnp.sum(p, axis=-1, keepdims=True)
        acc = alpha * acc + _dot(p.astype(BF16), v)
        return m_new, l_new, acc

    m0 = jnp.full((r4, 1), NEG_INF, F32)
    l0 = jnp.zeros((r4, 1), F32)
    a0 = jnp.zeros((r4, NSA_DH), F32)
    _, l_s, acc_s = lax.fori_loop(0, n_tiles, sel_body, (m0, l0, a0))
    o_s = acc_s / l_s

    wk = WINDOW + QB
    kq0 = pl.multiple_of(q0, QB)
    kwin = kw_ref[pl.ds(kq0, wk), :]
    vwin = vw_ref[pl.ds(kq0, wk), :]
    s_w = _dot_nt(q, kwin) + wb_ref[...].reshape(r4, wk)
    kpos = q0 - WINDOW + lax.broadcasted_iota(I32, (r4, wk), 1)
    s_w = jnp.where(kpos >= 0, s_w, NEG_INF)
    p_w, l_w = _softmax_rows(s_w)
    o_w = _dot(p_w.astype(BF16), vwin) / l_w

    gates = jax.nn.sigmoid(gate_ref[...])
    o = gates[:, 0:1] * o_c + gates[:, 1:2] * o_s + gates[:, 2:3] * o_w
    o_ref[...] = o.astype(o_ref.dtype)


def _nsa_attention(q4, kc, vc, ks, vs, kw, vw, gates, cb, bt, wb, cov_t, expand, n_sel):
    b, g, nqb, r4, dh = q4.shape
    s = ks.shape[2]
    nc = kc.shape[2]
    ns = cov_t.shape[0]
    wk = WINDOW + QB
    kv_spec = lambda rows: pl.BlockSpec((None, None, rows, dh), lambda bi, gi, i: (bi, gi, 0, 0))
    return pl.pallas_call(
        functools.partial(_nsa_kernel, n_sel=n_sel),
        grid=(b, g, nqb),
        in_specs=[
            pl.BlockSpec((None, None, None, r4, dh), lambda bi, gi, i: (bi, gi, i, 0, 0)),
            kv_spec(nc), kv_spec(nc), kv_spec(s), kv_spec(s), kv_spec(s + WINDOW), kv_spec(s + WINDOW),
            pl.BlockSpec((None, None, None, r4, 3), lambda bi, gi, i: (bi, gi, i, 0, 0)),
            pl.BlockSpec((NSA_HPG, QB, nc), lambda bi, gi, i: (gi, i, 0)),
            pl.BlockSpec((None, N_OFF, NSA_HPG, QB, QB), lambda bi, gi, i: (gi, 0, 0, 0, 0)),
            pl.BlockSpec((NSA_HPG, QB, wk), lambda bi, gi, i: (gi, 0, 0)),
            pl.BlockSpec((ns, nc), lambda bi, gi, i: (0, 0)),
            pl.BlockSpec(expand.shape, lambda bi, gi, i: (0, 0, 0)),
        ],
        out_specs=pl.BlockSpec((None, None, None, r4, dh), lambda bi, gi, i: (bi, gi, i, 0, 0)),
        out_shape=jax.ShapeDtypeStruct((b, g, nqb, r4, dh), BF16),
        compiler_params=_cparams(("arbitrary",) * 3),
        name="nsa_attention",
    )(q4, kc, vc, ks, vs, kw, vw, gates, cb, bt, wb, cov_t, expand)


def _ret_kernel(q_ref, k_ref, v_ref, g_ref, cos_ref, sin_ref, dm_ref, qd_ref, kd_ref, cd_ref,
                o_ref, state_ref):
    n = pl.program_id(2)

    @pl.when(n == 0)
    def _():
        state_ref[...] = jnp.zeros_like(state_ref)

    lane = lax.broadcasted_iota(I32, (RET_CHUNK, LANES), 1)
    first_half = (lane & (RET_DK - 1)) < (RET_DK // 2)
    cosv = cos_ref[...]
    sinv = sin_ref[...]

    def rot(x):
        partner = jnp.where(first_half, pltpu.roll(x, LANES - RET_DK // 2, axis=1),
                            pltpu.roll(x, RET_DK // 2, axis=1))
        return x * cosv + partner * sinv

    q2 = rot(q_ref[...])
    k2 = rot(k_ref[...]) * (RET_DK ** -0.5)
    qdec = (q2 * qd_ref[...]).astype(BF16)
    kdec_t = (k2 * kd_ref[...]).T.astype(BF16)
    q2b = q2.astype(BF16)
    k2b = k2.astype(BF16)
    for e in range(2):
        hs = slice(e * RET_DK, (e + 1) * RET_DK)
        vs = slice(e * RET_DV, (e + 1) * RET_DV)
        v = v_ref[:, vs]
        st = state_ref[e]
        sc = _dot_nt(q2b[:, hs], k2b[:, hs]) * dm_ref[e]
        y = _dot(sc.astype(BF16), v) + _dot(qdec[:, hs], st.astype(BF16))
        state_ref[e] = cd_ref[e] * st + _dot(kdec_t[hs, :], v)
        mu = jnp.mean(y, axis=-1, keepdims=True)
        var = jnp.mean(jnp.square(y - mu), axis=-1, keepdims=True)
        yn = (y - mu) * lax.rsqrt(var + GN_EPS)
        o_ref[:, vs] = (jax.nn.silu(g_ref[:, vs]) * yn).astype(o_ref.dtype)


def _retention(pf, pb, b, s, col_q, col_k, col_v, col_g, tabs):
    cosf, sinf, dm, qd, kd, cd = tabs
    n = s // RET_CHUNK
    hp = RET_HEADS // 2
    t = b * s
    c = RET_CHUNK
    row = lambda bi, h, ni: bi * n + ni
    return pl.pallas_call(
        _ret_kernel,
        grid=(b, hp, n),
        in_specs=[
            pl.BlockSpec((c, LANES), lambda bi, h, ni: (row(bi, h, ni), col_q // LANES + h)),
            pl.BlockSpec((c, LANES), lambda bi, h, ni: (row(bi, h, ni), col_k // LANES + h)),
            pl.BlockSpec((c, 2 * RET_DV), lambda bi, h, ni: (row(bi, h, ni), col_v // (2 * RET_DV) + h)),
            pl.BlockSpec((c, 2 * RET_DV), lambda bi, h, ni: (row(bi, h, ni), col_g // (2 * RET_DV) + h)),
            pl.BlockSpec((c, LANES), lambda bi, h, ni: (ni, 0)),
            pl.BlockSpec((c, LANES), lambda bi, h, ni: (ni, 0)),
            pl.BlockSpec((2, c, c), lambda bi, h, ni: (h, 0, 0)),
            pl.BlockSpec((c, LANES), lambda bi, h, ni: (0, h)),
            pl.BlockSpec((c, LANES), lambda bi, h, ni: (0, h)),
            pl.BlockSpec((2, 1, LANES), lambda bi, h, ni: (h, 0, 0)),
        ],
        out_specs=pl.BlockSpec((c, 2 * RET_DV), lambda bi, h, ni: (row(bi, h, ni), h)),
        out_shape=jax.ShapeDtypeStruct((t, RET_HEADS * RET_DV), BF16),
        scratch_shapes=[pltpu.VMEM((2, RET_DK, RET_DV), F32)],
        compiler_params=_cparams(("arbitrary",) * 3),
        name="retention",
    )(pf, pf, pb, pf, cosf, sinf, dm, qd, kd, cd)


def _retention_tables(s):
    half = RET_DK // 2
    inv = ROPE_BASE ** (-jnp.arange(half, dtype=F32) * 2.0 / RET_DK)
    ang = jnp.arange(s, dtype=F32)[:, None] * inv[None, :]
    cos, sin = jnp.cos(ang), jnp.sin(ang)
    cosf = jnp.concatenate([cos, cos, cos, cos], axis=1)
    sinf = jnp.concatenate([-sin, sin, -sin, sin], axis=1)
    lg = jnp.log(1.0 - 2.0 ** (-5.0 - jnp.arange(RET_HEADS, dtype=F32)))
    i = jnp.arange(RET_CHUNK, dtype=F32)
    diff = i[:, None] - i[None, :]
    dm = jnp.where(diff >= 0, jnp.exp(jnp.maximum(diff, 0.0)[None] * lg[:, None, None]), 0.0)
    kdec = jnp.exp((RET_CHUNK - 1.0 - i)[None, :] * lg[:, None])
    qdec = jnp.exp((i + 1.0)[None, :] * lg[:, None])
    expand = lambda a: jnp.repeat(a.T, RET_DK, axis=1)
    cd = jnp.broadcast_to(jnp.exp(RET_CHUNK * lg)[:, None, None], (RET_HEADS, 1, LANES))
    return cosf, sinf, dm, expand(qdec), expand(kdec), cd


def _layer_norm(z, g, b):
    mu = jnp.mean(z, axis=-1, keepdims=True)
    var = jnp.mean(jnp.square(z - mu), axis=-1, keepdims=True)
    return (z - mu) * lax.rsqrt(var + LN_EPS) * g + b


def _merge_kernel(ya_ref, yb_ref, ma_ref, mb_ref, x_ref, wa_ref, wb_ref, wo_ref, g_ref, b_ref,
                  o_ref, *, alpha):
    merged = (jax.nn.sigmoid(ma_ref[...]) * _dot(ya_ref[...], wa_ref[...])
              + jax.nn.sigmoid(mb_ref[...]) * _dot(yb_ref[...], wb_ref[...]))
    z = alpha * x_ref[...] + _dot(merged.astype(BF16), wo_ref[...])
    o_ref[...] = _layer_norm(z, g_ref[...], b_ref[...])


def _merge(ya, yb, pf, col_ma, col_mb, x, wa, wb, wo, g, bb, alpha, tm):
    t, d = x.shape
    full = lambda a: pl.BlockSpec(a.shape, lambda i: (0,) * a.ndim)
    return pl.pallas_call(
        functools.partial(_merge_kernel, alpha=alpha),
        grid=(t // tm,),
        in_specs=[pl.BlockSpec((tm, ya.shape[1]), lambda i: (i, 0)),
                  pl.BlockSpec((tm, yb.shape[1]), lambda i: (i, 0)),
                  pl.BlockSpec((tm, d), lambda i: (i, col_ma // d)),
                  pl.BlockSpec((tm, d), lambda i: (i, col_mb // d)),
                  pl.BlockSpec((tm, d), lambda i: (i, 0)),
                  full(wa), full(wb), full(wo), full(g), full(bb)],
        out_specs=pl.BlockSpec((tm, d), lambda i: (i, 0)),
        out_shape=jax.ShapeDtypeStruct((t, d), F32),
        compiler_params=_cparams(("arbitrary",)),
        name="merge_ln1",
    )(ya, yb, pf, pf, x, wa, wb, wo, g, bb)


def _router_kernel(x_ref, wr_ref, rb_ref, tri_ref, eidx_ref, wts_ref, rank_ref, cnt_ref, carry_ref):
    step = pl.program_id(0)

    @pl.when(step == 0)
    def _():
        carry_ref[...] = jnp.zeros_like(carry_ref)

    logits = lax.dot_general(wr_ref[...], x_ref[...], (((1,), (1,)), ((), ())),
                             precision=lax.Precision.HIGHEST, preferred_element_type=F32)
    tm = logits.shape[1]
    mx = jnp.max(logits, axis=0, keepdims=True)
    ex = jnp.exp(logits - mx)
    probs = ex / jnp.sum(ex, axis=0, keepdims=True)
    sel = probs + rb_ref[...]
    rows = [sel[e:e + 1] for e in range(N_EXPERTS)]
    gscore = []
    for gi in range(N_GROUPS):
        r = rows[gi * EPG:(gi + 1) * EPG]
        best2 = None
        for a in range(EPG):
            for c in range(a + 1, EPG):
                pair = r[a] + r[c]
                best2 = pair if best2 is None else jnp.maximum(best2, pair)
        gscore.append(best2)
    best = jnp.zeros((1, tm), I32)
    bscore = gscore[0]
    for gi in range(1, N_GROUPS):
        better = gscore[gi] > bscore
        best = jnp.where(better, gi, best)
        bscore = jnp.where(better, gscore[gi], bscore)
    ing = []
    for a in range(EPG):
        v = rows[a]
        for gi in range(1, N_GROUPS):
            v = jnp.where(best == gi, rows[gi * EPG + a], v)
        ing.append(v)
    i1 = jnp.zeros((1, tm), I32)
    v1 = ing[0]
    for a in range(1, EPG):
        better = ing[a] > v1
        i1 = jnp.where(better, a, i1)
        v1 = jnp.where(better, ing[a], v1)
    i2 = jnp.full((1, tm), -1, I32)
    v2 = jnp.full((1, tm), -jnp.inf, F32)
    for a in range(EPG):
        better = (i1 != a) & (ing[a] > v2)
        i2 = jnp.where(better, a, i2)
        v2 = jnp.where(better, ing[a], v2)
    e0 = best * EPG + i1
    e1 = best * EPG + i2
    eio = lax.broadcasted_iota(I32, (N_EXPERTS, tm), 0)
    oh0 = (eio == e0).astype(F32)
    oh1 = (eio == e1).astype(F32)
    p0 = jnp.sum(oh0 * probs, axis=0, keepdims=True)
    p1 = jnp.sum(oh1 * probs, axis=0, keepdims=True)
    den = p0 + p1
    eidx_ref[0:1, :] = e0
    eidx_ref[1:2, :] = e1
    wts_ref[0:1, :] = p0 / den
    wts_ref[1:2, :] = p1 / den
    tot = oh0 + oh1
    prefix = _dot(tot.astype(BF16), tri_ref[...]) + carry_ref[:, 0:1]
    rank_ref[0:1, :] = jnp.sum(oh0 * prefix, axis=0, keepdims=True).astype(I32)
    rank_ref[1:2, :] = jnp.sum(oh1 * prefix, axis=0, keepdims=True).astype(I32)
    carry_ref[...] = carry_ref[...] + jnp.sum(tot, axis=1, keepdims=True)
    cnt_ref[...] = carry_ref[...]


def _router(x, wr_t, rb_col, tri, tm):
    t, d = x.shape
    out2 = lambda dt: jax.ShapeDtypeStruct((2, t), dt)
    row2 = pl.BlockSpec((2, tm), lambda i: (0, i))
    return pl.pallas_call(
        _router_kernel,
        grid=(t // tm,),
        in_specs=[pl.BlockSpec((tm, d), lambda i: (i, 0)),
                  pl.BlockSpec(wr_t.shape, lambda i: (0, 0)),
                  pl.BlockSpec(rb_col.shape, lambda i: (0, 0)),
                  pl.BlockSpec(tri.shape, lambda i: (0, 0))],
        out_specs=[row2, row2, row2, pl.BlockSpec((N_EXPERTS, LANES), lambda i: (0, 0))],
        out_shape=[out2(I32), out2(F32), out2(I32), jax.ShapeDtypeStruct((N_EXPERTS, LANES), F32)],
        scratch_shapes=[pltpu.VMEM((N_EXPERTS, LANES), F32)],
        compiler_params=_cparams(("arbitrary",)),
        name="moe_router",
    )(x, wr_t, rb_col, tri)


def _dispatch_kernel(dest_ref, x_ref, xs_in_ref, xs_ref, sem):
    del xs_in_ref
    t = x_ref.shape[0]
    n_chunks = t // DISPATCH_CHUNK

    def copy(tok, slot):
        dst = dest_ref[slot * t + tok]
        return pltpu.make_async_copy(x_ref.at[pl.ds(tok, 1)], xs_ref.at[pl.ds(dst, 1)], sem)

    def issue(c):
        def body(r, _):
            tok = c * DISPATCH_CHUNK + r
            copy(tok, 0).start()
            copy(tok, 1).start()
            return 0
        lax.fori_loop(0, DISPATCH_CHUNK, body, 0)

    def drain(c):
        def body(r, _):
            tok = c * DISPATCH_CHUNK + r
            copy(tok, 0).wait()
            copy(tok, 1).wait()
            return 0
        lax.fori_loop(0, DISPATCH_CHUNK, body, 0)

    issue(0)

    def chunk_body(c, _):
        issue(c)
        drain(c - 1)
        return 0

    lax.fori_loop(1, n_chunks, chunk_body, 0)
    drain(n_chunks - 1)


def _dispatch(dest_flat, x, xs_zero):
    return pl.pallas_call(
        _dispatch_kernel,
        grid_spec=pltpu.PrefetchScalarGridSpec(
            num_scalar_prefetch=1,
            grid=(1,),
            in_specs=[pl.BlockSpec(memory_space=pl.ANY), pl.BlockSpec(memory_space=pl.ANY)],
            out_specs=pl.BlockSpec(memory_space=pl.ANY),
            scratch_shapes=[pltpu.SemaphoreType.DMA],
        ),
        out_shape=jax.ShapeDtypeStruct(xs_zero.shape, xs_zero.dtype),
        input_output_aliases={2: 0},
        compiler_params=pltpu.CompilerParams(dimension_semantics=("arbitrary",), has_side_effects=True),
        name="moe_dispatch",
    )(dest_flat, x, xs_zero)


def _expert_kernel(be_ref, na_ref, x_ref, wg_ref, wu_ref, wd_ref, o_ref):
    @pl.when(pl.program_id(0) < na_ref[0])
    def _():
        xb = x_ref[...].astype(BF16)
        gate = _dot(xb, wg_ref[...])
        up = _dot(xb, wu_ref[...])
        hid = (jax.nn.silu(gate) * up).astype(BF16)
        o_ref[...] = _dot(hid, wd_ref[...])


def _experts(blk_e, n_active, xs, wg, wu, wd):
    rows, d = xs.shape
    nb = rows // MOE_BM
    act = lambda i, be, na: jnp.minimum(i, na[0] - 1)
    return pl.pallas_call(
        _expert_kernel,
        grid_spec=pltpu.PrefetchScalarGridSpec(
            num_scalar_prefetch=2,
            grid=(nb,),
            in_specs=[pl.BlockSpec((MOE_BM, d), lambda i, be, na: (act(i, be, na), 0)),
                      pl.BlockSpec((None, d, D_EXPERT), lambda i, be, na: (be[i], 0, 0)),
                      pl.BlockSpec((None, d, D_EXPERT), lambda i, be, na: (be[i], 0, 0)),
                      pl.BlockSpec((None, D_EXPERT, d), lambda i, be, na: (be[i], 0, 0))],
            out_specs=pl.BlockSpec((MOE_BM, d), lambda i, be, na: (act(i, be, na), 0)),
        ),
        out_shape=jax.ShapeDtypeStruct((rows, d), F32),
        compiler_params=_cparams(("arbitrary",)),
        name="moe_experts",
    )(blk_e, n_active, xs, wg, wu, wd)


def _combine_kernel(dest_ref, x_ref, w_ref, g_ref, b_ref, ys_ref, o_ref, buf_ref, sem, *, alpha, t_total):
    step = pl.program_id(0)
    tm = x_ref.shape[0]

    def copy(r, slot):
        src = dest_ref[slot * t_total + step * tm + r]
        return pltpu.make_async_copy(ys_ref.at[pl.ds(src, 1)], buf_ref.at[slot, pl.ds(r, 1)], sem)

    def issue(r, _):
        copy(r, 0).start()
        copy(r, 1).start()
        return 0

    def drain(r, _):
        copy(r, 0).wait()
        copy(r, 1).wait()
        return 0

    lax.fori_loop(0, tm, issue, 0)
    lax.fori_loop(0, tm, drain, 0)
    w = w_ref[...]
    z = alpha * x_ref[...] + w[:, 0:1] * buf_ref[0] + w[:, 1:2] * buf_ref[1]
    o_ref[...] = _layer_norm(z, g_ref[...], b_ref[...])


def _combine(dest_flat, x, wts_t, g, bb, ys, alpha, tm):
    t, d = x.shape
    return pl.pallas_call(
        functools.partial(_combine_kernel, alpha=alpha, t_total=t),
        grid_spec=pltpu.PrefetchScalarGridSpec(
            num_scalar_prefetch=1,
            grid=(t // tm,),
            in_specs=[pl.BlockSpec((tm, d), lambda i, dr: (i, 0)),
                      pl.BlockSpec((tm, 2), lambda i, dr: (i, 0)),
                      pl.BlockSpec(g.shape, lambda i, dr: (0, 0)),
                      pl.BlockSpec(bb.shape, lambda i, dr: (0, 0)),
                      pl.BlockSpec(memory_space=pl.ANY)],
            out_specs=pl.BlockSpec((tm, d), lambda i, dr: (i, 0)),
            scratch_shapes=[pltpu.VMEM((2, tm, d), F32), pltpu.SemaphoreType.DMA],
        ),
        out_shape=jax.ShapeDtypeStruct((t, d), F32),
        compiler_params=_cparams(("arbitrary",)),
        name="moe_combine_ln2",
    )(dest_flat, x, wts_t, g, bb, ys)


def _attention_tables(rel_bias, s):
    nc = s // CMP_STRIDE
    t = jnp.arange(s, dtype=I32)
    c = jnp.arange(nc, dtype=I32)
    rel_c = t[:, None] - (c[None, :] * CMP_STRIDE + CMP_LEN - 1)
    rel_c = jnp.where(c[None, :] < nc - 1, rel_c, -1)
    iq = jnp.arange(QB, dtype=I32)
    rel_t = (jnp.arange(N_OFF, dtype=I32)[:, None, None] * QB + iq[None, :, None] - iq[None, None, :])
    rel_t = rel_t.reshape(N_OFF * QB, QB)
    jw = jnp.arange(WINDOW + QB, dtype=I32)
    rel_w = iq[:, None] - jw[None, :] + WINDOW
    rel_w = jnp.where(rel_w < WINDOW, rel_w, -1)
    flat = rel_bias.reshape(-1)
    cb = _bias_table(flat, jnp.maximum(rel_c, -1), QB)
    bt = _bias_table(flat, jnp.maximum(rel_t, -1), QB)
    wb = _bias_table(flat, jnp.maximum(rel_w, -1), QB)
    bt = bt.reshape(NSA_GROUPS, NSA_HPG, N_OFF, QB, QB).transpose(0, 2, 1, 3, 4)
    return cb, bt, wb


def _selection_tables(s):
    nc = s // CMP_STRIDE
    ns = s // SLC_BLOCK
    c_start = np.arange(nc) * CMP_STRIDE
    s_start = np.arange(ns) * SLC_BLOCK
    cover = np.maximum(np.minimum(c_start[:, None] + CMP_LEN, s_start[None, :] + SLC_BLOCK)
                       - np.maximum(c_start[:, None], s_start[None, :]), 0).astype(np.float32) / CMP_LEN
    cover[nc - 1] = 0.0
    n_tiles = s // SEL_TK
    kpos = np.arange(s).reshape(n_tiles, 1, SEL_TK)
    expand = (kpos // SLC_BLOCK == np.arange(ns)[None, :, None]).astype(np.float32)
    return jnp.asarray(cover.T, BF16), jnp.asarray(expand, BF16)


def _split_in_weights(w_in_l):
    o = np.cumsum((0, 512, 128, 128, 128, 128, 128, 128, 24, 512, 512, 1024, 1024, 1024, 1024))
    seg = lambda a: w_in_l[:, o[a]:o[a + 1]]
    wb = jnp.concatenate([seg(0), seg(1), seg(2), seg(3), seg(4), seg(5), seg(6), seg(10)], axis=1)
    gpad = jnp.pad(seg(7), ((0, 0), (0, LANES - 24)))
    wf = jnp.concatenate([seg(8), seg(9), seg(11), seg(12), seg(13), gpad], axis=1)
    return wb.astype(BF16), wf.astype(BF16)


def kernel(x, w_in, cmp_pos, cmp_w1, cmp_w2, w_br_a, w_br_b, w_o, ln1_g, ln1_b, w_router, router_bias,
           moe_w_gate, moe_w_up, moe_w_down, ln2_g, ln2_b, rel_bias):
    b, s, d = x.shape
    depth = w_in.shape[0]
    t = b * s
    alpha = (2 * depth) ** 0.25
    g, r, dh = NSA_GROUPS, NSA_HPG, NSA_DH
    nqb = s // QB
    nb16 = s // CMP_STRIDE
    n_sel = min(N_SELECT, s // SLC_BLOCK)

    cb, bt, wb = _attention_tables(rel_bias, s)
    cov_t, expand = _selection_tables(s)
    ret_tabs = _retention_tables(s)
    tri = jnp.asarray(np.triu(np.ones((512, 512), np.float32), 1), BF16)
    wr_t = w_router.T
    rb_col = router_bias.reshape(N_EXPERTS, 1)
    n_blk = t * 2 // MOE_BM + N_EXPERTS
    rows = n_blk * MOE_BM

    xf = x.reshape(t, d)
    for l in range(depth):
        w_b, w_f = _split_in_weights(w_in[l])
        xb = xf.astype(BF16)
        pb = _matmul(xb, w_b, BF16, 1024, 1152)
        pf = _matmul(xb, w_f, F32, 1024, 1408)

        def kv_heads(col):
            return pb[:, col:col + g * dh].reshape(b, s, g, dh).transpose(0, 2, 1, 3)
        x16 = jnp.stack([kv_heads(512), kv_heads(640)]).reshape(2, b, g, nb16, CMP_STRIDE * dh)
        w1 = cmp_w1[l].astype(BF16)
        half = CMP_STRIDE * dh
        w1ab = jnp.concatenate([w1[:, :half], w1[:, half:]], axis=2)
        pos8 = jnp.broadcast_to(cmp_pos[l].reshape(2, 1, CMP_LEN * dh), (2, 8, CMP_LEN * dh)).astype(BF16)
        kvc = _compress(x16, w1ab, _posterm(pos8, w1), cmp_w2[l].astype(BF16))
        q4 = (pb[:, 0:512].reshape(b, nqb, QB, g, r, dh).transpose(0, 3, 1, 4, 2, 5)
              .reshape(b, g, nqb, r * QB, dh))
        gl = (pf[:, 4096:4096 + 24].reshape(b, nqb, QB, g, r, 3).transpose(0, 3, 1, 4, 2, 5)
              .reshape(b, g, nqb, r * QB, 3))
        padw = lambda a: jnp.pad(a, ((0, 0), (0, 0), (WINDOW, 0), (0, 0)))
        o4 = _nsa_attention(q4, kvc[0], kvc[1], kv_heads(768), kv_heads(896),
                            padw(kv_heads(1024)), padw(kv_heads(1152)), gl, cb, bt, wb, cov_t, expand, n_sel)
        ya = (o4.reshape(b, g, nqb, r, QB, dh).transpose(0, 2, 4, 1, 3, 5).reshape(t, g * r * dh))

        yb = _retention(pf, pb, b, s, 0, 512, 1280, 1024, ret_tabs)

        x1 = _merge(ya, yb, pf, 2048, 3072, xf, w_br_a[l].astype(BF16), w_br_b[l].astype(BF16),
                    w_o[l].astype(BF16), ln1_g[l].reshape(1, d), ln1_b[l].reshape(1, d), alpha, 512)

        eidx, wts, rank, cnt = _router(x1, wr_t, rb_col, tri, 512)
        counts = cnt[:, 0].astype(I32)
        padded = ((counts + MOE_BM - 1) // MOE_BM) * MOE_BM
        pad_end = jnp.cumsum(padded)
        pad_start = pad_end - padded
        dest = (pad_start[eidx] + rank).reshape(-1)
        n_active = (pad_end[-1] // MOE_BM).astype(I32).reshape(1)
        blk = jnp.minimum(jnp.arange(n_blk, dtype=I32), n_active[0] - 1) * MOE_BM
        blk_e = jnp.minimum(jnp.searchsorted(pad_end, blk, side='right'), N_EXPERTS - 1).astype(I32)
        xs = _dispatch(dest, x1, jnp.zeros((rows, d), F32))
        ys = _experts(blk_e, n_active, xs, moe_w_gate[l].astype(BF16), moe_w_up[l].astype(BF16),
                      moe_w_down[l].astype(BF16))
        xf = _combine(dest, x1, wts.T, ln2_g[l].reshape(1, d), ln2_b[l].reshape(1, d), ys, alpha, 256)
    return xf.reshape(b, s, d)
```

```python
import functools
import math

import jax
import jax.numpy as jnp
import numpy as np
from jax import lax
from jax.experimental import pallas as pl
from jax.experimental.pallas import tpu as pltpu

F32 = jnp.float32
BF16 = jnp.bfloat16
I32 = jnp.int32

NSA_HEADS = 8
NSA_GROUPS = 2
NSA_HPG = 4
NSA_DH = 64
CMP_LEN = 32
CMP_STRIDE = 16
SLC_BLOCK = 64
N_SELECT = 16
WINDOW = 512
QB = 128
FORCE_SCORE = 1.0e4
RET_HEADS = 8
RET_DK = 64
RET_DV = 128
RET_CHUNK = 128
ROPE_BASE = 10000.0
REL_BUCKETS = 32
REL_MAX_EXACT = 16
REL_MAX_DIST = 2048
N_EXPERTS = 16
N_GROUPS = 4
EPG = 4
D_EXPERT = 512
LN_EPS = 1e-5
GN_EPS = 1e-5
NEG_INF = -1e30

LANES = 128
VMEM_LIMIT = 56 * 1024 * 1024

SEL_TK = 256
N_OFF = 14
MOE_BM = 256
ROW_DMA_UNROLL = 8


def _cparams(sem):
    return pltpu.CompilerParams(dimension_semantics=sem, vmem_limit_bytes=VMEM_LIMIT)


def _dot(a, b):
    return jnp.dot(a, b, preferred_element_type=F32)


def _dot_nt(a, b):
    return lax.dot_general(a, b, (((1,), (1,)), ((), ())), preferred_element_type=F32)


def _mm_kernel(a_ref, b_ref, o_ref):
    o_ref[...] = _dot(a_ref[...], b_ref[...]).astype(o_ref.dtype)


def _matmul(a, b, out_dtype, tm, tn):
    m, k = a.shape
    n = b.shape[1]
    return pl.pallas_call(
        _mm_kernel,
        grid=(n // tn, m // tm),
        in_specs=[pl.BlockSpec((tm, k), lambda j, i: (i, 0)),
                  pl.BlockSpec((k, tn), lambda j, i: (0, j))],
        out_specs=pl.BlockSpec((tm, tn), lambda j, i: (i, j)),
        out_shape=jax.ShapeDtypeStruct((m, n), out_dtype),
        compiler_params=_cparams(("arbitrary", "arbitrary")),
        name="proj_matmul",
    )(a, b)


def _bias_kernel(tab_ref, rel_ref, o_ref):
    rel = rel_ref[...]
    n = jnp.maximum(rel, 0)
    nf = jnp.maximum(n, 1).astype(F32)
    large = REL_MAX_EXACT + (jnp.log(nf / REL_MAX_EXACT) / math.log(REL_MAX_DIST / REL_MAX_EXACT)
                             * (REL_BUCKETS - REL_MAX_EXACT)).astype(I32)
    large = jnp.minimum(large, REL_BUCKETS - 1)
    bucket = jnp.where(n < REL_MAX_EXACT, n, large)
    eqs = [bucket == k for k in range(REL_BUCKETS)]
    for h in range(NSA_HEADS):
        acc = jnp.full(rel.shape, NEG_INF, F32)
        for k in range(REL_BUCKETS):
            acc = jnp.where(eqs[k], tab_ref[k * NSA_HEADS + h], acc)
        o_ref[h] = jnp.where(rel >= 0, acc, NEG_INF)


def _bias_table(rel_bias_flat, rel, tr):
    r, c = rel.shape
    return pl.pallas_call(
        _bias_kernel,
        grid_spec=pltpu.PrefetchScalarGridSpec(
            num_scalar_prefetch=1,
            grid=(r // tr,),
            in_specs=[pl.BlockSpec((tr, c), lambda i, tab: (i, 0))],
            out_specs=pl.BlockSpec((NSA_HEADS, tr, c), lambda i, tab: (0, i, 0)),
        ),
        out_shape=jax.ShapeDtypeStruct((NSA_HEADS, r, c), F32),
        compiler_params=_cparams(("arbitrary",)),
        name="t5_bias_table",
    )(rel_bias_flat, rel)


def _compress_kernel(x_ref, w1_ref, pos_ref, w2_ref, o_ref):
    ab = _dot(x_ref[...], w1_ref[...])
    nb = ab.shape[0]
    nxt = pltpu.roll(ab, nb - 1, axis=0)
    nxt = pltpu.roll(nxt, NSA_DH, axis=1)
    pre = (ab + nxt)[:, :NSA_DH] + pos_ref[...]
    hid = jax.nn.gelu(pre)
    o_ref[...] = _dot(hid.astype(BF16), w2_ref[...]).astype(o_ref.dtype)


def _compress(x16, w1ab, posterm, w2):
    _, b, g, nb, wd = x16.shape
    return pl.pallas_call(
        _compress_kernel,
        grid=(2, b, g),
        in_specs=[pl.BlockSpec((None, None, None, nb, wd), lambda s, bi, gi: (s, bi, gi, 0, 0)),
                  pl.BlockSpec((None, wd, 2 * NSA_DH), lambda s, bi, gi: (s, 0, 0)),
                  pl.BlockSpec((None, 1, NSA_DH), lambda s, bi, gi: (s, 0, 0)),
                  pl.BlockSpec((None, NSA_DH, NSA_DH), lambda s, bi, gi: (s, 0, 0))],
        out_specs=pl.BlockSpec((None, None, None, nb, NSA_DH), lambda s, bi, gi: (s, bi, gi, 0, 0)),
        out_shape=jax.ShapeDtypeStruct((2, b, g, nb, NSA_DH), BF16),
        compiler_params=_cparams(("arbitrary",) * 3),
        name="nsa_compress",
    )(x16, w1ab, posterm, w2)


def _posterm_kernel(p_ref, w_ref, o_ref):
    o_ref[...] = _dot(p_ref[...], w_ref[...])[0:1]


def _posterm(pos8, w1):
    return pl.pallas_call(
        _posterm_kernel,
        grid=(2,),
        in_specs=[pl.BlockSpec((None, 8, pos8.shape[2]), lambda s: (s, 0, 0)),
                  pl.BlockSpec((None, w1.shape[1], NSA_DH), lambda s: (s, 0, 0))],
        out_specs=pl.BlockSpec((None, 1, NSA_DH), lambda s: (s, 0, 0)),
        out_shape=jax.ShapeDtypeStruct((2, 1, NSA_DH), F32),
        name="nsa_posterm",
    )(pos8, w1)


def _softmax_rows(s):
    m = jnp.max(s, axis=-1, keepdims=True)
    p = jnp.exp(s - m)
    return p, jnp.sum(p, axis=-1, keepdims=True)


def _nsa_kernel(q_ref, kc_ref, vc_ref, ks_ref, vs_ref, kw_ref, vw_ref, gate_ref,
                cb_ref, bt_ref, wb_ref, cov_ref, exp_ref, o_ref, *, n_sel):
    i = pl.program_id(2)
    q0 = i * QB
    r4 = NSA_HPG * QB
    q = (q_ref[...].astype(F32) * (NSA_DH ** -0.5)).astype(BF16)

    cb = cb_ref[...].reshape(r4, cb_ref.shape[-1])
    s_c = _dot_nt(q, kc_ref[...]) + cb
    valid_c = cb > 0.5 * NEG_INF
    p_c, l_c = _softmax_rows(s_c)
    p_c = jnp.where(valid_c, p_c, 0.0)
    l_c = jnp.sum(p_c, axis=-1, keepdims=True)
    p_c = p_c / jnp.maximum(l_c, 1e-30)
    o_c = _dot(p_c.astype(BF16), vc_ref[...])

    psum = p_c[0:QB] + p_c[QB:2 * QB] + p_c[2 * QB:3 * QB] + p_c[3 * QB:4 * QB]
    p_hi = psum.astype(BF16)
    p_lo = (psum - p_hi.astype(F32)).astype(BF16)
    cov = cov_ref[...]
    imp_t = _dot_nt(cov, p_hi) + _dot_nt(cov, p_lo)
    ns = imp_t.shape[0]
    jidx = lax.broadcasted_iota(I32, (ns, QB), 0)
    t = q0 + lax.broadcasted_iota(I32, (ns, QB), 1)
    cur = jnp.right_shift(t, 6)
    eligible = jidx <= cur
    forced = (jidx == 0) | (jidx == cur) | (jidx == cur - 1)
    score = jnp.where(eligible, imp_t + jnp.where(forced, FORCE_SCORE, 0.0), -1.0)
    sel = jnp.zeros((ns, QB), F32)
    for _ in range(n_sel):
        mx = jnp.max(score, axis=0, keepdims=True)
        first = jnp.min(jnp.where(score == mx, jidx, ns), axis=0, keepdims=True)
        pick = jidx == first
        sel = jnp.where(pick, 1.0, sel)
        score = jnp.where(pick, -2.0, score)
    sel = jnp.where(eligible, sel, 0.0)
    selmask = sel.T.astype(BF16)

    sub = SEL_TK // QB
    n_tiles = i // sub + 1

    def sel_body(jt, carry):
        m_prev, l_prev, acc = carry
        k0 = pl.multiple_of(jt * SEL_TK, SEL_TK)
        k = ks_ref[pl.ds(k0, SEL_TK), :]
        v = vs_ref[pl.ds(k0, SEL_TK), :]
        s = _dot_nt(q, k).reshape(NSA_HPG, QB, SEL_TK)
        parts = []
        for u in range(sub):
            off = jnp.clip(i - sub * jt - u, 0, N_OFF - 1)
            parts.append(bt_ref[off])
        bias = parts[0] if sub == 1 else jnp.concatenate(parts, axis=-1)
        maskf = _dot(selmask, exp_ref[jt])
        s = jnp.where((maskf > 0.5)[None], s + bias, NEG_INF).reshape(r4, SEL_TK)
        m_new = jnp.maximum(m_prev, jnp.max(s, axis=-1, keepdims=True))
        alpha = jnp.exp(m_prev - m_new)
        p = jnp.exp(s - m_new)
        l_new = alpha * l_prev + jnp.sum(p, axis=-1, keepdims=True)
        acc = alpha * acc + _dot(p.astype(BF16), v)
        return m_new, l_new, acc

    m0 = jnp.full((r4, 1), NEG_INF, F32)
    l0 = jnp.zeros((r4, 1), F32)
    a0 = jnp.zeros((r4, NSA_DH), F32)
    _, l_s, acc_s = lax.fori_loop(0, n_tiles, sel_body, (m0, l0, a0))
    o_s = acc_s / l_s

    wk = WINDOW + QB
    kq0 = pl.multiple_of(q0, QB)
    kwin = kw_ref[pl.ds(kq0, wk), :]
    vwin = vw_ref[pl.ds(kq0, wk), :]
    s_w = _dot_nt(q, kwin) + wb_ref[...].reshape(r4, wk)
    kpos = q0 - WINDOW + lax.broadcasted_iota(I32, (r4, wk), 1)
    s_w = jnp.where(kpos >= 0, s_w, NEG_INF)
    p_w, l_w = _softmax_rows(s_w)
    o_w = _dot(p_w.astype(BF16), vwin) / l_w

    gates = jax.nn.sigmoid(gate_ref[...])
    o = gates[:, 0:1] * o_c + gates[:, 1:2] * o_s + gates[:, 2:3] * o_w
    o_ref[...] = o.astype(o_ref.dtype)


def _nsa_attention(q4, kc, vc, ks, vs, kw, vw, gates, cb, bt, wb, cov_t, expand, n_sel):
    b, g, nqb, r4, dh = q4.shape
    s = ks.shape[2]
    nc = kc.shape[2]
    ns = cov_t.shape[0]
    wk = WINDOW + QB
    kv_spec = lambda rows: pl.BlockSpec((None, None, rows, dh), lambda bi, gi, i: (bi, gi, 0, 0))
    return pl.pallas_call(
        functools.partial(_nsa_kernel, n_sel=n_sel),
        grid=(b, g, nqb),
        in_specs=[
            pl.BlockSpec((None, None, None, r4, dh), lambda bi, gi, i: (bi, gi, i, 0, 0)),
            kv_spec(nc), kv_spec(nc), kv_spec(s), kv_spec(s), kv_spec(s + WINDOW), kv_spec(s + WINDOW),
            pl.BlockSpec((None, None, None, r4, 3), lambda bi, gi, i: (bi, gi, i, 0, 0)),
            pl.BlockSpec((NSA_HPG, QB, nc), lambda bi, gi, i: (gi, i, 0)),
            pl.BlockSpec((None, N_OFF, NSA_HPG, QB, QB), lambda bi, gi, i: (gi, 0, 0, 0, 0)),
            pl.BlockSpec((NSA_HPG, QB, wk), lambda bi, gi, i: (gi, 0, 0)),
            pl.BlockSpec((ns, nc), lambda bi, gi, i: (0, 0)),
            pl.BlockSpec(expand.shape, lambda bi, gi, i: (0, 0, 0)),
        ],
        out_specs=pl.BlockSpec((None, None, None, r4, dh), lambda bi, gi, i: (bi, gi, i, 0, 0)),
        out_shape=jax.ShapeDtypeStruct((b, g, nqb, r4, dh), BF16),
        compiler_params=_cparams(("arbitrary",) * 3),
        name="nsa_attention",
    )(q4, kc, vc, ks, vs, kw, vw, gates, cb, bt, wb, cov_t, expand)


def _ret_kernel(q_ref, k_ref, v_ref, g_ref, cos_ref, sin_ref, dm_ref, qd_ref, kd_ref, cd_ref,
                o_ref, state_ref):
    n = pl.program_id(2)

    @pl.when(n == 0)
    def _():
        state_ref[...] = jnp.zeros_like(state_ref)

    lane = lax.broadcasted_iota(I32, (RET_CHUNK, LANES), 1)
    first_half = (lane & (RET_DK - 1)) < (RET_DK // 2)
    cosv = cos_ref[...]
    sinv = sin_ref[...]

    def rot(x):
        partner = jnp.where(first_half, pltpu.roll(x, LANES - RET_DK // 2, axis=1),
                            pltpu.roll(x, RET_DK // 2, axis=1))
        return x * cosv + partner * sinv

    q2 = rot(q_ref[...])
    k2 = rot(k_ref[...]) * (RET_DK ** -0.5)
    qdec = (q2 * qd_ref[...]).astype(BF16)
    kdec_t = (k2 * kd_ref[...]).T.astype(BF16)
    q2b = q2.astype(BF16)
    k2b = k2.astype(BF16)
    for e in range(2):
        hs = slice(e * RET_DK, (e + 1) * RET_DK)
        vs = slice(e * RET_DV, (e + 1) * RET_DV)
        v = v_ref[:, vs]
        st = state_ref[e]
        sc = _dot_nt(q2b[:, hs], k2b[:, hs]) * dm_ref[e]
        y = _dot(sc.astype(BF16), v) + _dot(qdec[:, hs], st.astype(BF16))
        state_ref[e] = cd_ref[e] * st + _dot(kdec_t[hs, :], v)
        mu = jnp.mean(y, axis=-1, keepdims=True)
        var = jnp.mean(jnp.square(y - mu), axis=-1, keepdims=True)
        yn = (y - mu) * lax.rsqrt(var + GN_EPS)
        o_ref[:, vs] = (jax.nn.silu(g_ref[:, vs]) * yn).astype(o_ref.dtype)


def _retention(pf, pb, b, s, col_q, col_k, col_v, col_g, tabs):
    cosf, sinf, dm, qd, kd, cd = tabs
    n = s // RET_CHUNK
    hp = RET_HEADS // 2
    t = b * s
    c = RET_CHUNK
    row = lambda bi, h, ni: bi * n + ni
    return pl.pallas_call(
        _ret_kernel,
        grid=(b, hp, n),
        in_specs=[
            pl.BlockSpec((c, LANES), lambda bi, h, ni: (row(bi, h, ni), col_q // LANES + h)),
            pl.BlockSpec((c, LANES), lambda bi, h, ni: (row(bi, h, ni), col_k // LANES + h)),
            pl.BlockSpec((c, 2 * RET_DV), lambda bi, h, ni: (row(bi, h, ni), col_v // (2 * RET_DV) + h)),
            pl.BlockSpec((c, 2 * RET_DV), lambda bi, h, ni: (row(bi, h, ni), col_g // (2 * RET_DV) + h)),
            pl.BlockSpec((c, LANES), lambda bi, h, ni: (ni, 0)),
            pl.BlockSpec((c, LANES), lambda bi, h, ni: (ni, 0)),
            pl.BlockSpec((2, c, c), lambda bi, h, ni: (h, 0, 0)),
            pl.BlockSpec((c, LANES), lambda bi, h, ni: (0, h)),
            pl.BlockSpec((c, LANES), lambda bi, h, ni: (0, h)),
            pl.BlockSpec((2, 1, LANES), lambda bi, h, ni: (h, 0, 0)),
        ],
        out_specs=pl.BlockSpec((c, 2 * RET_DV), lambda bi, h, ni: (row(bi, h, ni), h)),
        out_shape=jax.ShapeDtypeStruct((t, RET_HEADS * RET_DV), BF16),
        scratch_shapes=[pltpu.VMEM((2, RET_DK, RET_DV), F32)],
        compiler_params=_cparams(("arbitrary",) * 3),
        name="retention",
    )(pf, pf, pb, pf, cosf, sinf, dm, qd, kd, cd)


def _retention_tables(s):
    half = RET_DK // 2
    inv = ROPE_BASE ** (-jnp.arange(half, dtype=F32) * 2.0 / RET_DK)
    ang = jnp.arange(s, dtype=F32)[:, None] * inv[None, :]
    cos, sin = jnp.cos(ang), jnp.sin(ang)
    cosf = jnp.concatenate([cos, cos, cos, cos], axis=1)
    sinf = jnp.concatenate([-sin, sin, -sin, sin], axis=1)
    lg = jnp.log(1.0 - 2.0 ** (-5.0 - jnp.arange(RET_HEADS, dtype=F32)))
    i = jnp.arange(RET_CHUNK, dtype=F32)
    diff = i[:, None] - i[None, :]
    dm = jnp.where(diff >= 0, jnp.exp(jnp.maximum(diff, 0.0)[None] * lg[:, None, None]), 0.0)
    kdec = jnp.exp((RET_CHUNK - 1.0 - i)[None, :] * lg[:, None])
    qdec = jnp.exp((i + 1.0)[None, :] * lg[:, None])
    expand = lambda a: jnp.repeat(a.T, RET_DK, axis=1)
    cd = jnp.broadcast_to(jnp.exp(RET_CHUNK * lg)[:, None, None], (RET_HEADS, 1, LANES))
    return cosf, sinf, dm, expand(qdec), expand(kdec), cd


def _layer_norm(z, g, b):
    mu = jnp.mean(z, axis=-1, keepdims=True)
    var = jnp.mean(jnp.square(z - mu), axis=-1, keepdims=True)
    return (z - mu) * lax.rsqrt(var + LN_EPS) * g + b


def _merge_kernel(ya_ref, yb_ref, ma_ref, mb_ref, x_ref, wa_ref, wb_ref, wo_ref, g_ref, b_ref,
                  o_ref, *, alpha):
    merged = (jax.nn.sigmoid(ma_ref[...]) * _dot(ya_ref[...], wa_ref[...])
              + jax.nn.sigmoid(mb_ref[...]) * _dot(yb_ref[...], wb_ref[...]))
    z = alpha * x_ref[...] + _dot(merged.astype(BF16), wo_ref[...])
    o_ref[...] = _layer_norm(z, g_ref[...], b_ref[...])


def _merge(ya, yb, pf, col_ma, col_mb, x, wa, wb, wo, g, bb, alpha, tm):
    t, d = x.shape
    full = lambda a: pl.BlockSpec(a.shape, lambda i: (0,) * a.ndim)
    return pl.pallas_call(
        functools.partial(_merge_kernel, alpha=alpha),
        grid=(t // tm,),
        in_specs=[pl.BlockSpec((tm, ya.shape[1]), lambda i: (i, 0)),
                  pl.BlockSpec((tm, yb.shape[1]), lambda i: (i, 0)),
                  pl.BlockSpec((tm, d), lambda i: (i, col_ma // d)),
                  pl.BlockSpec((tm, d), lambda i: (i, col_mb // d)),
                  pl.BlockSpec((tm, d), lambda i: (i, 0)),
                  full(wa), full(wb), full(wo), full(g), full(bb)],
        out_specs=pl.BlockSpec((tm, d), lambda i: (i, 0)),
        out_shape=jax.ShapeDtypeStruct((t, d), F32),
        compiler_params=_cparams(("arbitrary",)),
        name="merge_ln1",
    )(ya, yb, pf, pf, x, wa, wb, wo, g, bb)


def _router_kernel(x_ref, wr_ref, rb_ref, tri_ref, eidx_ref, wts_ref, rank_ref, cnt_ref, carry_ref):
    step = pl.program_id(0)

    @pl.when(step == 0)
    def _():
        carry_ref[...] = jnp.zeros_like(carry_ref)

    logits = lax.dot_general(wr_ref[...], x_ref[...], (((1,), (1,)), ((), ())),
                             precision=lax.Precision.HIGHEST, preferred_element_type=F32)
    tm = logits.shape[1]
    mx = jnp.max(logits, axis=0, keepdims=True)
    ex = jnp.exp(logits - mx)
    probs = ex / jnp.sum(ex, axis=0, keepdims=True)
    sel = probs + rb_ref[...]
    rows = [sel[e:e + 1] for e in range(N_EXPERTS)]
    gscore = []
    for gi in range(N_GROUPS):
        r = rows[gi * EPG:(gi + 1) * EPG]
        best2 = None
        for a in range(EPG):
            for c in range(a + 1, EPG):
                pair = r[a] + r[c]
                best2 = pair if best2 is None else jnp.maximum(best2, pair)
        gscore.append(best2)
    best = jnp.zeros((1, tm), I32)
    bscore = gscore[0]
    for gi in range(1, N_GROUPS):
        better = gscore[gi] > bscore
        best = jnp.where(better, gi, best)
        bscore = jnp.where(better, gscore[gi], bscore)
    ing = []
    for a in range(EPG):
        v = rows[a]
        for gi in range(1, N_GROUPS):
            v = jnp.where(best == gi, rows[gi * EPG + a], v)
        ing.append(v)
    i1 = jnp.zeros((1, tm), I32)
    v1 = ing[0]
    for a in range(1, EPG):
        better = ing[a] > v1
        i1 = jnp.where(better, a, i1)
        v1 = jnp.where(better, ing[a], v1)
    i2 = jnp.full((1, tm), -1, I32)
    v2 = jnp.full((1, tm), -jnp.inf, F32)
    for a in range(EPG):
        better = (i1 != a) & (ing[a] > v2)
        i2 = jnp.where(better, a, i2)
        v2 = jnp.where(better, ing[a], v2)
    e0 = best * EPG + i1
    e1 = best * EPG + i2
    eio = lax.broadcasted_iota(I32, (N_EXPERTS, tm), 0)
    oh0 = (eio == e0).astype(F32)
    oh1 = (eio == e1).astype(F32)
    p0 = jnp.sum(oh0 * probs, axis=0, keepdims=True)
    p1 = jnp.sum(oh1 * probs, axis=0, keepdims=True)
    den = p0 + p1
    eidx_ref[0:1, :] = e0
    eidx_ref[1:2, :] = e1
    wts_ref[0:1, :] = p0 / den
    wts_ref[1:2, :] = p1 / den
    tot = oh0 + oh1
    prefix = _dot(tot.astype(BF16), tri_ref[...]) + carry_ref[:, 0:1]
    rank_ref[0:1, :] = jnp.sum(oh0 * prefix, axis=0, keepdims=True).astype(I32)
    rank_ref[1:2, :] = jnp.sum(oh1 * prefix, axis=0, keepdims=True).astype(I32)
    carry_ref[...] = carry_ref[...] + jnp.sum(tot, axis=1, keepdims=True)
    cnt_ref[...] = carry_ref[...]


def _router(x, wr_t, rb_col, tri, tm):
    t, d = x.shape
    out2 = lambda dt: jax.ShapeDtypeStruct((2, t), dt)
    row2 = pl.BlockSpec((2, tm), lambda i: (0, i))
    return pl.pallas_call(
        _router_kernel,
        grid=(t // tm,),
        in_specs=[pl.BlockSpec((tm, d), lambda i: (i, 0)),
                  pl.BlockSpec(wr_t.shape, lambda i: (0, 0)),
                  pl.BlockSpec(rb_col.shape, lambda i: (0, 0)),
                  pl.BlockSpec(tri.shape, lambda i: (0, 0))],
        out_specs=[row2, row2, row2, pl.BlockSpec((N_EXPERTS, LANES), lambda i: (0, 0))],
        out_shape=[out2(I32), out2(F32), out2(I32), jax.ShapeDtypeStruct((N_EXPERTS, LANES), F32)],
        scratch_shapes=[pltpu.VMEM((N_EXPERTS, LANES), F32)],
        compiler_params=_cparams(("arbitrary",)),
        name="moe_router",
    )(x, wr_t, rb_col, tri)


def _dispatch_kernel(dest_ref, x_ref, xs_in_ref, xs_ref, sem, *, t_total):
    del xs_in_ref
    step = pl.program_id(0)
    tm = x_ref.shape[0]

    def copy(r, slot):
        dst = dest_ref[slot * t_total + step * tm + r]
        return pltpu.make_async_copy(x_ref.at[pl.ds(r, 1)], xs_ref.at[pl.ds(dst, 1)], sem)

    def issue(r, _):
        copy(r, 0).start()
        copy(r, 1).start()
        return 0

    def drain(r, _):
        copy(r, 0).wait()
        copy(r, 1).wait()
        return 0

    lax.fori_loop(0, tm, issue, 0, unroll=ROW_DMA_UNROLL)
    lax.fori_loop(0, tm, drain, 0, unroll=ROW_DMA_UNROLL)


def _dispatch(dest_flat, x, xs_zero, tm):
    t, d = x.shape
    return pl.pallas_call(
        functools.partial(_dispatch_kernel, t_total=t),
        grid_spec=pltpu.PrefetchScalarGridSpec(
            num_scalar_prefetch=1,
            grid=(t // tm,),
            in_specs=[pl.BlockSpec((tm, d), lambda i, dr: (i, 0)), pl.BlockSpec(memory_space=pl.ANY)],
            out_specs=pl.BlockSpec(memory_space=pl.ANY),
            scratch_shapes=[pltpu.SemaphoreType.DMA],
        ),
        out_shape=jax.ShapeDtypeStruct(xs_zero.shape, xs_zero.dtype),
        input_output_aliases={2: 0},
        compiler_params=pltpu.CompilerParams(dimension_semantics=("arbitrary",), has_side_effects=True,
                                             vmem_limit_bytes=VMEM_LIMIT),
        name="moe_dispatch",
    )(dest_flat, x, xs_zero)


def _expert_kernel(be_ref, na_ref, x_ref, wg_ref, wu_ref, wd_ref, o_ref):
    @pl.when(pl.program_id(0) < na_ref[0])
    def _():
        xb = x_ref[...].astype(BF16)
        gate = _dot(xb, wg_ref[...])
        up = _dot(xb, wu_ref[...])
        hid = (jax.nn.silu(gate) * up).astype(BF16)
        o_ref[...] = _dot(hid, wd_ref[...])


def _experts(blk_e, n_active, xs, wg, wu, wd):
    rows, d = xs.shape
    nb = rows // MOE_BM
    act = lambda i, be, na: jnp.minimum(i, na[0] - 1)
    return pl.pallas_call(
        _expert_kernel,
        grid_spec=pltpu.PrefetchScalarGridSpec(
            num_scalar_prefetch=2,
            grid=(nb,),
            in_specs=[pl.BlockSpec((MOE_BM, d), lambda i, be, na: (act(i, be, na), 0)),
                      pl.BlockSpec((None, d, D_EXPERT), lambda i, be, na: (be[i], 0, 0)),
                      pl.BlockSpec((None, d, D_EXPERT), lambda i, be, na: (be[i], 0, 0)),
                      pl.BlockSpec((None, D_EXPERT, d), lambda i, be, na: (be[i], 0, 0))],
            out_specs=pl.BlockSpec((MOE_BM, d), lambda i, be, na: (act(i, be, na), 0)),
        ),
        out_shape=jax.ShapeDtypeStruct((rows, d), F32),
        compiler_params=_cparams(("arbitrary",)),
        name="moe_experts",
    )(blk_e, n_active, xs, wg, wu, wd)


def _combine_kernel(dest_ref, x_ref, w_ref, g_ref, b_ref, ys_ref, o_ref, buf_ref, sem, *, alpha, t_total):
    step = pl.program_id(0)
    tm = x_ref.shape[0]

    def copy(r, slot):
        src = dest_ref[slot * t_total + step * tm + r]
        return pltpu.make_async_copy(ys_ref.at[pl.ds(src, 1)], buf_ref.at[slot, pl.ds(r, 1)], sem)

    def issue(r, _):
        copy(r, 0).start()
        copy(r, 1).start()
        return 0

    def drain(r, _):
        copy(r, 0).wait()
        copy(r, 1).wait()
        return 0

    lax.fori_loop(0, tm, issue, 0, unroll=ROW_DMA_UNROLL)
    lax.fori_loop(0, tm, drain, 0, unroll=ROW_DMA_UNROLL)
    w = w_ref[...]
    z = alpha * x_ref[...] + w[:, 0:1] * buf_ref[0] + w[:, 1:2] * buf_ref[1]
    o_ref[...] = _layer_norm(z, g_ref[...], b_ref[...])


def _combine(dest_flat, x, wts_t, g, bb, ys, alpha, tm):
    t, d = x.shape
    return pl.pallas_call(
        functools.partial(_combine_kernel, alpha=alpha, t_total=t),
        grid_spec=pltpu.PrefetchScalarGridSpec(
            num_scalar_prefetch=1,
            grid=(t // tm,),
            in_specs=[pl.BlockSpec((tm, d), lambda i, dr: (i, 0)),
                      pl.BlockSpec((tm, 2), lambda i, dr: (i, 0)),
                      pl.BlockSpec(g.shape, lambda i, dr: (0, 0)),
                      pl.BlockSpec(bb.shape, lambda i, dr: (0, 0)),
                      pl.BlockSpec(memory_space=pl.ANY)],
            out_specs=pl.BlockSpec((tm, d), lambda i, dr: (i, 0)),
            scratch_shapes=[pltpu.VMEM((2, tm, d), F32), pltpu.SemaphoreType.DMA],
        ),
        out_shape=jax.ShapeDtypeStruct((t, d), F32),
        compiler_params=_cparams(("arbitrary",)),
        name="moe_combine_ln2",
    )(dest_flat, x, wts_t, g, bb, ys)


def _attention_tables(rel_bias, s):
    nc = s // CMP_STRIDE
    t = jnp.arange(s, dtype=I32)
    c = jnp.arange(nc, dtype=I32)
    rel_c = t[:, None] - (c[None, :] * CMP_STRIDE + CMP_LEN - 1)
    rel_c = jnp.where(c[None, :] < nc - 1, rel_c, -1)
    iq = jnp.arange(QB, dtype=I32)
    rel_t = (jnp.arange(N_OFF, dtype=I32)[:, None, None] * QB + iq[None, :, None] - iq[None, None, :])
    rel_t = rel_t.reshape(N_OFF * QB, QB)
    jw = jnp.arange(WINDOW + QB, dtype=I32)
    rel_w = iq[:, None] - jw[None, :] + WINDOW
    rel_w = jnp.where(rel_w < WINDOW, rel_w, -1)
    flat = rel_bias.reshape(-1)
    cb = _bias_table(flat, jnp.maximum(rel_c, -1), QB)
    bt = _bias_table(flat, jnp.maximum(rel_t, -1), QB)
    wb = _bias_table(flat, jnp.maximum(rel_w, -1), QB)
    bt = bt.reshape(NSA_GROUPS, NSA_HPG, N_OFF, QB, QB).transpose(0, 2, 1, 3, 4)
    return cb, bt, wb


def _selection_tables(s):
    nc = s // CMP_STRIDE
    ns = s // SLC_BLOCK
    c_start = np.arange(nc) * CMP_STRIDE
    s_start = np.arange(ns) * SLC_BLOCK
    cover = np.maximum(np.minimum(c_start[:, None] + CMP_LEN, s_start[None, :] + SLC_BLOCK)
                       - np.maximum(c_start[:, None], s_start[None, :]), 0).astype(np.float32) / CMP_LEN
    cover[nc - 1] = 0.0
    n_tiles = s // SEL_TK
    kpos = np.arange(s).reshape(n_tiles, 1, SEL_TK)
    expand = (kpos // SLC_BLOCK == np.arange(ns)[None, :, None]).astype(np.float32)
    return jnp.asarray(cover.T, BF16), jnp.asarray(expand, BF16)


def _split_in_weights(w_in_l):
    o = np.cumsum((0, 512, 128, 128, 128, 128, 128, 128, 24, 512, 512, 1024, 1024, 1024, 1024))
    seg = lambda a: w_in_l[:, o[a]:o[a + 1]]
    wb = jnp.concatenate([seg(0), seg(1), seg(2), seg(3), seg(4), seg(5), seg(6), seg(10)], axis=1)
    gpad = jnp.pad(seg(7), ((0, 0), (0, LANES - 24)))
    wf = jnp.concatenate([seg(8), seg(9), seg(11), seg(12), seg(13), gpad], axis=1)
    return wb.astype(BF16), wf.astype(BF16)


def kernel(x, w_in, cmp_pos, cmp_w1, cmp_w2, w_br_a, w_br_b, w_o, ln1_g, ln1_b, w_router, router_bias,
           moe_w_gate, moe_w_up, moe_w_down, ln2_g, ln2_b, rel_bias):
    b, s, d = x.shape
    depth = w_in.shape[0]
    t = b * s
    alpha = (2 * depth) ** 0.25
    g, r, dh = NSA_GROUPS, NSA_HPG, NSA_DH
    nqb = s // QB
    nb16 = s // CMP_STRIDE
    n_sel = min(N_SELECT, s // SLC_BLOCK)

    cb, bt, wb = _attention_tables(rel_bias, s)
    cov_t, expand = _selection_tables(s)
    ret_tabs = _retention_tables(s)
    tri = jnp.asarray(np.triu(np.ones((512, 512), np.float32), 1), BF16)
    wr_t = w_router.T
    rb_col = router_bias.reshape(N_EXPERTS, 1)
    n_blk = t * 2 // MOE_BM + N_EXPERTS
    rows = n_blk * MOE_BM

    xf = x.reshape(t, d)
    for l in range(depth):
        w_b, w_f = _split_in_weights(w_in[l])
        xb = xf.astype(BF16)
        pb = _matmul(xb, w_b, BF16, 1024, 1152)
        pf = _matmul(xb, w_f, F32, 1024, 1408)

        def kv_heads(col):
            return pb[:, col:col + g * dh].reshape(b, s, g, dh).transpose(0, 2, 1, 3)
        x16 = jnp.stack([kv_heads(512), kv_heads(640)]).reshape(2, b, g, nb16, CMP_STRIDE * dh)
        w1 = cmp_w1[l].astype(BF16)
        half = CMP_STRIDE * dh
        w1ab = jnp.concatenate([w1[:, :half], w1[:, half:]], axis=2)
        pos8 = jnp.broadcast_to(cmp_pos[l].reshape(2, 1, CMP_LEN * dh), (2, 8, CMP_LEN * dh)).astype(BF16)
        kvc = _compress(x16, w1ab, _posterm(pos8, w1), cmp_w2[l].astype(BF16))
        q4 = (pb[:, 0:512].reshape(b, nqb, QB, g, r, dh).transpose(0, 3, 1, 4, 2, 5)
              .reshape(b, g, nqb, r * QB, dh))
        gl = (pf[:, 4096:4096 + 24].reshape(b, nqb, QB, g, r, 3).transpose(0, 3, 1, 4, 2, 5)
              .reshape(b, g, nqb, r * QB, 3))
        padw = lambda a: jnp.pad(a, ((0, 0), (0, 0), (WINDOW, 0), (0, 0)))
        o4 = _nsa_attention(q4, kvc[0], kvc[1], kv_heads(768), kv_heads(896),
                            padw(kv_heads(1024)), padw(kv_heads(1152)), gl, cb, bt, wb, cov_t, expand, n_sel)
        ya = (o4.reshape(b, g, nqb, r, QB, dh).transpose(0, 2, 4, 1, 3, 5).reshape(t, g * r * dh))

        yb = _retention(pf, pb, b, s, 0, 512, 1280, 1024, ret_tabs)

        x1 = _merge(ya, yb, pf, 2048, 3072, xf, w_br_a[l].astype(BF16), w_br_b[l].astype(BF16),
                    w_o[l].astype(BF16), ln1_g[l].reshape(1, d), ln1_b[l].reshape(1, d), alpha, 512)

        eidx, wts, rank, cnt = _router(x1, wr_t, rb_col, tri, 512)
        counts = cnt[:, 0].astype(I32)
        padded = ((counts + MOE_BM - 1) // MOE_BM) * MOE_BM
        pad_end = jnp.cumsum(padded)
        pad_start = pad_end - padded
        dest = (pad_start[eidx] + rank).reshape(-1)
        n_active = (pad_end[-1] // MOE_BM).astype(I32).reshape(1)
        blk = jnp.minimum(jnp.arange(n_blk, dtype=I32), n_active[0] - 1) * MOE_BM
        blk_e = jnp.minimum(jnp.searchsorted(pad_end, blk, side='right'), N_EXPERTS - 1).astype(I32)
        xs = _dispatch(dest, x1, jnp.zeros((rows, d), F32), 512)
        ys = _experts(blk_e, n_active, xs, moe_w_gate[l].astype(BF16), moe_w_up[l].astype(BF16),
                      moe_w_down[l].astype(BF16))
        xf = _combine(dest, x1, wts.T, ln2_g[l].reshape(1, d), ln2_b[l].reshape(1, d), ys, alpha, 256)
    return xf.reshape(b, s, d)
```

```python
import functools
import math

import jax
import jax.numpy as jnp
import numpy as np
from jax import lax
from jax.experimental import pallas as pl
from jax.experimental.pallas import tpu as pltpu

F32 = jnp.float32
BF16 = jnp.bfloat16
I32 = jnp.int32

NSA_HEADS = 8
NSA_GROUPS = 2
NSA_HPG = 4
NSA_DH = 64
CMP_LEN = 32
CMP_STRIDE = 16
SLC_BLOCK = 64
N_SELECT = 16
WINDOW = 512
QB = 128
FORCE_SCORE = 1.0e4
RET_HEADS = 8
RET_DK = 64
RET_DV = 128
RET_CHUNK = 128
ROPE_BASE = 10000.0
REL_BUCKETS = 32
REL_MAX_EXACT = 16
REL_MAX_DIST = 2048
N_EXPERTS = 16
N_GROUPS = 4
EPG = 4
D_EXPERT = 512
LN_EPS = 1e-5
GN_EPS = 1e-5
NEG_INF = -1e30

LANES = 128
VMEM_LIMIT = 56 * 1024 * 1024

SEL_TK = 1024
SEL_CW = 512
AUG_ROWS = 16
MASK_BIG = 2.0 ** 100
N_OFF = 14
MOE_BM = 256
ROW_DMA_UNROLL = 8


def _cparams(sem):
    return pltpu.CompilerParams(dimension_semantics=sem, vmem_limit_bytes=VMEM_LIMIT)


def _dot(a, b):
    return jnp.dot(a, b, preferred_element_type=F32)


def _dot_nt(a, b):
    return lax.dot_general(a, b, (((1,), (1,)), ((), ())), preferred_element_type=F32)


def _mm_kernel(a_ref, b_ref, o_ref):
    o_ref[...] = _dot(a_ref[...], b_ref[...]).astype(o_ref.dtype)


def _matmul(a, b, out_dtype, tm, tn):
    m, k = a.shape
    n = b.shape[1]
    return pl.pallas_call(
        _mm_kernel,
        grid=(n // tn, m // tm),
        in_specs=[pl.BlockSpec((tm, k), lambda j, i: (i, 0)),
                  pl.BlockSpec((k, tn), lambda j, i: (0, j))],
        out_specs=pl.BlockSpec((tm, tn), lambda j, i: (i, j)),
        out_shape=jax.ShapeDtypeStruct((m, n), out_dtype),
        compiler_params=_cparams(("arbitrary", "arbitrary")),
        name="proj_matmul",
    )(a, b)


def _bias_kernel(tab_ref, rel_ref, o_ref, *, delta):
    rel = rel_ref[...]
    n = jnp.maximum(rel, 0)
    nf = jnp.maximum(n, 1).astype(F32)
    large = REL_MAX_EXACT + (jnp.log(nf / REL_MAX_EXACT) / math.log(REL_MAX_DIST / REL_MAX_EXACT)
                             * (REL_BUCKETS - REL_MAX_EXACT)).astype(I32)
    large = jnp.minimum(large, REL_BUCKETS - 1)
    bucket = jnp.where(n < REL_MAX_EXACT, n, large)
    eqs = [bucket == k for k in range(REL_BUCKETS)]
    for h in range(NSA_HEADS):
        acc = jnp.full(rel.shape, NEG_INF, F32)
        for k in range(REL_BUCKETS):
            acc = jnp.where(eqs[k], tab_ref[k * NSA_HEADS + h], acc)
        if delta:
            acc = acc - tab_ref[(REL_BUCKETS - 1) * NSA_HEADS + h]
        hh = h % NSA_HPG
        o_ref[h // NSA_HPG, :, hh * QB:(hh + 1) * QB] = jnp.where(rel >= 0, acc, NEG_INF)


def _bias_table(rel_bias_flat, rel, tr, delta=False):
    r, c = rel.shape
    return pl.pallas_call(
        functools.partial(_bias_kernel, delta=delta),
        grid_spec=pltpu.PrefetchScalarGridSpec(
            num_scalar_prefetch=1,
            grid=(r // tr,),
            in_specs=[pl.BlockSpec((tr, c), lambda i, tab: (i, 0))],
            out_specs=pl.BlockSpec((NSA_GROUPS, tr, NSA_HPG * c), lambda i, tab: (0, i, 0)),
        ),
        out_shape=jax.ShapeDtypeStruct((NSA_GROUPS, r, NSA_HPG * c), F32),
        compiler_params=_cparams(("arbitrary",)),
        name="t5_bias_table",
    )(rel_bias_flat, rel)


def _compress_kernel(x_ref, w1_ref, pos_ref, w2_ref, o_ref):
    ab = _dot(x_ref[...], w1_ref[...])
    nb = ab.shape[0]
    nxt = pltpu.roll(ab, nb - 1, axis=0)
    nxt = pltpu.roll(nxt, NSA_DH, axis=1)
    pre = (ab + nxt)[:, :NSA_DH] + pos_ref[...]
    hid = jax.nn.gelu(pre)
    o_ref[...] = _dot(hid.astype(BF16), w2_ref[...]).astype(o_ref.dtype)


def _compress(x16, w1ab, posterm, w2):
    _, b, g, nb, wd = x16.shape
    return pl.pallas_call(
        _compress_kernel,
        grid=(2, b, g),
        in_specs=[pl.BlockSpec((None, None, None, nb, wd), lambda s, bi, gi: (s, bi, gi, 0, 0)),
                  pl.BlockSpec((None, wd, 2 * NSA_DH), lambda s, bi, gi: (s, 0, 0)),
                  pl.BlockSpec((None, 1, NSA_DH), lambda s, bi, gi: (s, 0, 0)),
                  pl.BlockSpec((None, NSA_DH, NSA_DH), lambda s, bi, gi: (s, 0, 0))],
        out_specs=pl.BlockSpec((None, None, None, nb, NSA_DH), lambda s, bi, gi: (s, bi, gi, 0, 0)),
        out_shape=jax.ShapeDtypeStruct((2, b, g, nb, NSA_DH), BF16),
        compiler_params=_cparams(("arbitrary",) * 3),
        name="nsa_compress",
    )(x16, w1ab, posterm, w2)


def _posterm_kernel(p_ref, w_ref, o_ref):
    o_ref[...] = _dot(p_ref[...], w_ref[...])[0:1]


def _posterm(pos8, w1):
    return pl.pallas_call(
        _posterm_kernel,
        grid=(2,),
        in_specs=[pl.BlockSpec((None, 8, pos8.shape[2]), lambda s: (s, 0, 0)),
                  pl.BlockSpec((None, w1.shape[1], NSA_DH), lambda s: (s, 0, 0))],
        out_specs=pl.BlockSpec((None, 1, NSA_DH), lambda s: (s, 0, 0)),
        out_shape=jax.ShapeDtypeStruct((2, 1, NSA_DH), F32),
        name="nsa_posterm",
    )(pos8, w1)


def _nsa_kernel(q_ref, kc_ref, vc_ref, ks_ref, vs_ref, kw_ref, vw_ref, gate_ref,
                cb_ref, bt_ref, wb_ref, cov_ref, b31_ref, o_ref, *, n_sel):
    i = pl.program_id(2)
    q0 = i * QB
    r4 = NSA_HPG * QB
    q = (q_ref[...].astype(F32) * (NSA_DH ** -0.5)).astype(BF16)

    cb = cb_ref[...]
    s_c = _dot(kc_ref[...], q) + cb
    m_c = jnp.max(s_c, axis=0, keepdims=True)
    p_c = jnp.where(cb > 0.5 * NEG_INF, jnp.exp(s_c - m_c), 0.0)
    l_c = jnp.sum(p_c, axis=0, keepdims=True)
    p_c = p_c / jnp.maximum(l_c, 1e-30)
    o_c = _dot(vc_ref[...], p_c.astype(BF16))

    psum = p_c[:, 0:QB] + p_c[:, QB:2 * QB] + p_c[:, 2 * QB:3 * QB] + p_c[:, 3 * QB:4 * QB]
    p_hi = psum.astype(BF16)
    p_lo = (psum - p_hi.astype(F32)).astype(BF16)
    cov = cov_ref[...]
    imp_t = _dot(cov, p_hi) + _dot(cov, p_lo)
    ns = imp_t.shape[0]
    jidx = lax.broadcasted_iota(I32, (ns, QB), 0)
    t = q0 + lax.broadcasted_iota(I32, (ns, QB), 1)
    cur = jnp.right_shift(t, 6)
    eligible = jidx <= cur
    forced = (jidx == 0) | (jidx == cur) | (jidx == cur - 1)
    score = jnp.where(eligible, imp_t + jnp.where(forced, FORCE_SCORE, 0.0), -1.0)
    sel = jnp.zeros((ns, QB), F32)
    for _ in range(n_sel):
        mx = jnp.max(score, axis=0, keepdims=True)
        first = jnp.min(jnp.where(score == mx, jidx, ns), axis=0, keepdims=True)
        pick = jidx == first
        sel = jnp.where(pick, 1.0, sel)
        score = jnp.where(pick, -2.0, score)
    notsel = jnp.where(eligible, 1.0 - sel, 1.0).astype(BF16)

    depth = ks_ref.shape[-1]
    pad_rows = depth - ns - NSA_DH - AUG_ROWS
    qa = jnp.concatenate([jnp.concatenate([notsel] * NSA_HPG, axis=1), q, b31_ref[...],
                          jnp.zeros((pad_rows, r4), BF16)], axis=0)
    sub = SEL_TK // QB
    n_tiles = i // sub + 1
    n_far = jnp.maximum(i - N_OFF + 2, 0) // sub
    n_chunks = r4 // SEL_CW

    def sel_body(jt, carry, near):
        k0 = pl.multiple_of(jt * SEL_TK, SEL_TK)
        ka = ks_ref[pl.ds(k0, SEL_TK), :]
        va = vs_ref[jt]
        if near:
            offs = [jnp.clip(i - sub * jt - u, 0, N_OFF - 1) for u in range(sub)]
        out = []
        for c in range(n_chunks):
            m_prev, acc = carry[c]
            cs = slice(c * SEL_CW, (c + 1) * SEL_CW)
            s = _dot(ka, qa[:, cs])
            if near:
                s = s + jnp.concatenate([bt_ref[o, :, cs] for o in offs], axis=0)
            m_new = jnp.maximum(m_prev, jnp.max(s, axis=0, keepdims=True))
            alpha = jnp.exp(m_prev - m_new)
            p = jnp.exp(s - m_new).astype(BF16)
            out.append((m_new, alpha * acc + _dot(va, p)))
        return tuple(out)

    init = tuple((jnp.full((1, SEL_CW), -1e38, F32), jnp.zeros((NSA_DH + AUG_ROWS, SEL_CW), F32))
                 for _ in range(n_chunks))
    carry = lax.fori_loop(0, n_far, functools.partial(sel_body, near=False), init)
    carry = lax.fori_loop(n_far, n_tiles, functools.partial(sel_body, near=True), carry)
    acc_s = jnp.concatenate([a for _, a in carry], axis=1)
    o_s = acc_s[0:NSA_DH] / acc_s[NSA_DH:NSA_DH + 1]

    wk = WINDOW + QB
    kwin = kw_ref[pl.ds(pl.multiple_of(q0, QB), wk), :]
    s_w = _dot(kwin, q) + wb_ref[...]
    kpos = q0 - WINDOW + lax.broadcasted_iota(I32, (wk, r4), 0)
    s_w = jnp.where(kpos >= 0, s_w, NEG_INF)
    m_w = jnp.max(s_w, axis=0, keepdims=True)
    p_w = jnp.exp(s_w - m_w).astype(BF16)
    acc_w = jnp.zeros((NSA_DH + AUG_ROWS, r4), F32)
    for u in range(wk // QB):
        acc_w = acc_w + _dot(vw_ref[i + u], p_w[u * QB:(u + 1) * QB])
    o_w = acc_w[0:NSA_DH] / acc_w[NSA_DH:NSA_DH + 1]

    gates = jax.nn.sigmoid(gate_ref[...])
    o = gates[0:1] * o_c + gates[1:2] * o_s + gates[2:3] * o_w
    o_ref[...] = o.astype(o_ref.dtype)


def _nsa_attention(qt, kc, vct, ksa, vst, kw, vwt, gates, cb, bt, wb, cov_t, b31, n_sel):
    b, g, nqb, dh, r4 = qt.shape
    nc = kc.shape[2]
    per_bg = lambda a: pl.BlockSpec((None, None) + a.shape[2:], lambda bi, gi, i: (bi, gi) + (0,) * (a.ndim - 2))
    per_g = lambda a: pl.BlockSpec((None,) + a.shape[1:], lambda bi, gi, i: (gi,) + (0,) * (a.ndim - 1))
    per_step = lambda a: pl.BlockSpec((None, None, None) + a.shape[3:],
                                      lambda bi, gi, i: (bi, gi, i) + (0,) * (a.ndim - 3))
    return pl.pallas_call(
        functools.partial(_nsa_kernel, n_sel=n_sel),
        grid=(b, g, nqb),
        in_specs=[
            per_step(qt), per_bg(kc), per_bg(vct), per_bg(ksa), per_bg(vst), per_bg(kw), per_bg(vwt),
            per_step(gates),
            pl.BlockSpec((None, None, nc, r4), lambda bi, gi, i: (gi, i, 0, 0)),
            per_g(bt), per_g(wb),
            pl.BlockSpec(cov_t.shape, lambda bi, gi, i: (0, 0)),
            per_g(b31),
        ],
        out_specs=per_step(qt),
        out_shape=jax.ShapeDtypeStruct(qt.shape, BF16),
        compiler_params=_cparams(("arbitrary",) * 3),
        name="nsa_attention",
    )(qt, kc, vct, ksa, vst, kw, vwt, gates, cb, bt, wb, cov_t, b31)


def _ret_kernel(q_ref, k_ref, v_ref, g_ref, cos_ref, sin_ref, dm_ref, qd_ref, kd_ref, cd_ref,
                o_ref, state_ref):
    n = pl.program_id(2)

    @pl.when(n == 0)
    def _():
        state_ref[...] = jnp.zeros_like(state_ref)

    lane = lax.broadcasted_iota(I32, (RET_CHUNK, LANES), 1)
    first_half = (lane & (RET_DK - 1)) < (RET_DK // 2)
    cosv = cos_ref[...]
    sinv = sin_ref[...]

    def rot(x):
        partner = jnp.where(first_half, pltpu.roll(x, LANES - RET_DK // 2, axis=1),
                            pltpu.roll(x, RET_DK // 2, axis=1))
        return x * cosv + partner * sinv

    q2 = rot(q_ref[...])
    k2 = rot(k_ref[...]) * (RET_DK ** -0.5)
    qdec = (q2 * qd_ref[...]).astype(BF16)
    kdec_t = (k2 * kd_ref[...]).T.astype(BF16)
    q2b = q2.astype(BF16)
    k2b = k2.astype(BF16)
    for e in range(2):
        hs = slice(e * RET_DK, (e + 1) * RET_DK)
        vs = slice(e * RET_DV, (e + 1) * RET_DV)
        v = v_ref[:, vs]
        st = state_ref[e]
        sc = _dot_nt(q2b[:, hs], k2b[:, hs]) * dm_ref[e]
        y = _dot(sc.astype(BF16), v) + _dot(qdec[:, hs], st.astype(BF16))
        state_ref[e] = cd_ref[e] * st + _dot(kdec_t[hs, :], v)
        mu = jnp.mean(y, axis=-1, keepdims=True)
        var = jnp.mean(jnp.square(y - mu), axis=-1, keepdims=True)
        yn = (y - mu) * lax.rsqrt(var + GN_EPS)
        o_ref[:, vs] = (jax.nn.silu(g_ref[:, vs]) * yn).astype(o_ref.dtype)


def _retention(pf, pb, b, s, col_q, col_k, col_v, col_g, tabs):
    cosf, sinf, dm, qd, kd, cd = tabs
    n = s // RET_CHUNK
    hp = RET_HEADS // 2
    t = b * s
    c = RET_CHUNK
    row = lambda bi, h, ni: bi * n + ni
    return pl.pallas_call(
        _ret_kernel,
        grid=(b, hp, n),
        in_specs=[
            pl.BlockSpec((c, LANES), lambda bi, h, ni: (row(bi, h, ni), col_q // LANES + h)),
            pl.BlockSpec((c, LANES), lambda bi, h, ni: (row(bi, h, ni), col_k // LANES + h)),
            pl.BlockSpec((c, 2 * RET_DV), lambda bi, h, ni: (row(bi, h, ni), col_v // (2 * RET_DV) + h)),
            pl.BlockSpec((c, 2 * RET_DV), lambda bi, h, ni: (row(bi, h, ni), col_g // (2 * RET_DV) + h)),
            pl.BlockSpec((c, LANES), lambda bi, h, ni: (ni, 0)),
            pl.BlockSpec((c, LANES), lambda bi, h, ni: (ni, 0)),
            pl.BlockSpec((2, c, c), lambda bi, h, ni: (h, 0, 0)),
            pl.BlockSpec((c, LANES), lambda bi, h, ni: (0, h)),
            pl.BlockSpec((c, LANES), lambda bi, h, ni: (0, h)),
            pl.BlockSpec((2, 1, LANES), lambda bi, h, ni: (h, 0, 0)),
        ],
        out_specs=pl.BlockSpec((c, 2 * RET_DV), lambda bi, h, ni: (row(bi, h, ni), h)),
        out_shape=jax.ShapeDtypeStruct((t, RET_HEADS * RET_DV), BF16),
        scratch_shapes=[pltpu.VMEM((2, RET_DK, RET_DV), F32)],
        compiler_params=_cparams(("arbitrary",) * 3),
        name="retention",
    )(pf, pf, pb, pf, cosf, sinf, dm, qd, kd, cd)


def _retention_tables(s):
    half = RET_DK // 2
    inv = ROPE_BASE ** (-jnp.arange(half, dtype=F32) * 2.0 / RET_DK)
    ang = jnp.arange(s, dtype=F32)[:, None] * inv[None, :]
    cos, sin = jnp.cos(ang), jnp.sin(ang)
    cosf = jnp.concatenate([cos, cos, cos, cos], axis=1)
    sinf = jnp.concatenate([-sin, sin, -sin, sin], axis=1)
    lg = jnp.log(1.0 - 2.0 ** (-5.0 - jnp.arange(RET_HEADS, dtype=F32)))
    i = jnp.arange(RET_CHUNK, dtype=F32)
    diff = i[:, None] - i[None, :]
    dm = jnp.where(diff >= 0, jnp.exp(jnp.maximum(diff, 0.0)[None] * lg[:, None, None]), 0.0)
    kdec = jnp.exp((RET_CHUNK - 1.0 - i)[None, :] * lg[:, None])
    qdec = jnp.exp((i + 1.0)[None, :] * lg[:, None])
    expand = lambda a: jnp.repeat(a.T, RET_DK, axis=1)
    cd = jnp.broadcast_to(jnp.exp(RET_CHUNK * lg)[:, None, None], (RET_HEADS, 1, LANES))
    return cosf, sinf, dm, expand(qdec), expand(kdec), cd


def _layer_norm(z, g, b):
    mu = jnp.mean(z, axis=-1, keepdims=True)
    var = jnp.mean(jnp.square(z - mu), axis=-1, keepdims=True)
    return (z - mu) * lax.rsqrt(var + LN_EPS) * g + b


def _merge_kernel(ya_ref, yb_ref, ma_ref, mb_ref, x_ref, wa_ref, wb_ref, wo_ref, g_ref, b_ref,
                  o_ref, *, alpha):
    merged = (jax.nn.sigmoid(ma_ref[...]) * _dot(ya_ref[...], wa_ref[...])
              + jax.nn.sigmoid(mb_ref[...]) * _dot(yb_ref[...], wb_ref[...]))
    z = alpha * x_ref[...] + _dot(merged.astype(BF16), wo_ref[...])
    o_ref[...] = _layer_norm(z, g_ref[...], b_ref[...])


def _merge(ya, yb, pf, col_ma, col_mb, x, wa, wb, wo, g, bb, alpha, tm):
    t, d = x.shape
    full = lambda a: pl.BlockSpec(a.shape, lambda i: (0,) * a.ndim)
    return pl.pallas_call(
        functools.partial(_merge_kernel, alpha=alpha),
        grid=(t // tm,),
        in_specs=[pl.BlockSpec((tm, ya.shape[1]), lambda i: (i, 0)),
                  pl.BlockSpec((tm, yb.shape[1]), lambda i: (i, 0)),
                  pl.BlockSpec((tm, d), lambda i: (i, col_ma // d)),
                  pl.BlockSpec((tm, d), lambda i: (i, col_mb // d)),
                  pl.BlockSpec((tm, d), lambda i: (i, 0)),
                  full(wa), full(wb), full(wo), full(g), full(bb)],
        out_specs=pl.BlockSpec((tm, d), lambda i: (i, 0)),
        out_shape=jax.ShapeDtypeStruct((t, d), F32),
        compiler_params=_cparams(("arbitrary",)),
        name="merge_ln1",
    )(ya, yb, pf, pf, x, wa, wb, wo, g, bb)


def _router_kernel(x_ref, wr_ref, rb_ref, tri_ref, eidx_ref, wts_ref, rank_ref, cnt_ref, carry_ref):
    step = pl.program_id(0)

    @pl.when(step == 0)
    def _():
        carry_ref[...] = jnp.zeros_like(carry_ref)

    logits = lax.dot_general(wr_ref[...], x_ref[...], (((1,), (1,)), ((), ())),
                             precision=lax.Precision.HIGHEST, preferred_element_type=F32)
    tm = logits.shape[1]
    mx = jnp.max(logits, axis=0, keepdims=True)
    ex = jnp.exp(logits - mx)
    probs = ex / jnp.sum(ex, axis=0, keepdims=True)
    sel = probs + rb_ref[...]
    rows = [sel[e:e + 1] for e in range(N_EXPERTS)]
    gscore = []
    for gi in range(N_GROUPS):
        r = rows[gi * EPG:(gi + 1) * EPG]
        best2 = None
        for a in range(EPG):
            for c in range(a + 1, EPG):
                pair = r[a] + r[c]
                best2 = pair if best2 is None else jnp.maximum(best2, pair)
        gscore.append(best2)
    best = jnp.zeros((1, tm), I32)
    bscore = gscore[0]
    for gi in range(1, N_GROUPS):
        better = gscore[gi] > bscore
        best = jnp.where(better, gi, best)
        bscore = jnp.where(better, gscore[gi], bscore)
    ing = []
    for a in range(EPG):
        v = rows[a]
        for gi in range(1, N_GROUPS):
            v = jnp.where(best == gi, rows[gi * EPG + a], v)
        ing.append(v)
    i1 = jnp.zeros((1, tm), I32)
    v1 = ing[0]
    for a in range(1, EPG):
        better = ing[a] > v1
        i1 = jnp.where(better, a, i1)
        v1 = jnp.where(better, ing[a], v1)
    i2 = jnp.full((1, tm), -1, I32)
    v2 = jnp.full((1, tm), -jnp.inf, F32)
    for a in range(EPG):
        better = (i1 != a) & (ing[a] > v2)
        i2 = jnp.where(better, a, i2)
        v2 = jnp.where(better, ing[a], v2)
    e0 = best * EPG + i1
    e1 = best * EPG + i2
    eio = lax.broadcasted_iota(I32, (N_EXPERTS, tm), 0)
    oh0 = (eio == e0).astype(F32)
    oh1 = (eio == e1).astype(F32)
    p0 = jnp.sum(oh0 * probs, axis=0, keepdims=True)
    p1 = jnp.sum(oh1 * probs, axis=0, keepdims=True)
    den = p0 + p1
    eidx_ref[0:1, :] = e0
    eidx_ref[1:2, :] = e1
    wts_ref[0:1, :] = p0 / den
    wts_ref[1:2, :] = p1 / den
    tot = oh0 + oh1
    prefix = _dot(tot.astype(BF16), tri_ref[...]) + carry_ref[:, 0:1]
    rank_ref[0:1, :] = jnp.sum(oh0 * prefix, axis=0, keepdims=True).astype(I32)
    rank_ref[1:2, :] = jnp.sum(oh1 * prefix, axis=0, keepdims=True).astype(I32)
    carry_ref[...] = carry_ref[...] + jnp.sum(tot, axis=1, keepdims=True)
    cnt_ref[...] = carry_ref[...]


def _router(x, wr_t, rb_col, tri, tm):
    t, d = x.shape
    out2 = lambda dt: jax.ShapeDtypeStruct((2, t), dt)
    row2 = pl.BlockSpec((2, tm), lambda i: (0, i))
    return pl.pallas_call(
        _router_kernel,
        grid=(t // tm,),
        in_specs=[pl.BlockSpec((tm, d), lambda i: (i, 0)),
                  pl.BlockSpec(wr_t.shape, lambda i: (0, 0)),
                  pl.BlockSpec(rb_col.shape, lambda i: (0, 0)),
                  pl.BlockSpec(tri.shape, lambda i: (0, 0))],
        out_specs=[row2, row2, row2, pl.BlockSpec((N_EXPERTS, LANES), lambda i: (0, 0))],
        out_shape=[out2(I32), out2(F32), out2(I32), jax.ShapeDtypeStruct((N_EXPERTS, LANES), F32)],
        scratch_shapes=[pltpu.VMEM((N_EXPERTS, LANES), F32)],
        compiler_params=_cparams(("arbitrary",)),
        name="moe_router",
    )(x, wr_t, rb_col, tri)


def _dispatch_kernel(dest_ref, x_ref, xs_in_ref, xs_ref, sem, *, t_total):
    del xs_in_ref
    step = pl.program_id(0)
    tm = x_ref.shape[0]

    def copy(r, slot):
        dst = dest_ref[slot * t_total + step * tm + r]
        return pltpu.make_async_copy(x_ref.at[pl.ds(r, 1)], xs_ref.at[pl.ds(dst, 1)], sem)

    def issue(r, _):
        copy(r, 0).start()
        copy(r, 1).start()
        return 0

    def drain(r, _):
        copy(r, 0).wait()
        copy(r, 1).wait()
        return 0

    lax.fori_loop(0, tm, issue, 0, unroll=ROW_DMA_UNROLL)
    lax.fori_loop(0, tm, drain, 0, unroll=ROW_DMA_UNROLL)


def _dispatch(dest_flat, x, xs_zero, tm):
    t, d = x.shape
    return pl.pallas_call(
        functools.partial(_dispatch_kernel, t_total=t),
        grid_spec=pltpu.PrefetchScalarGridSpec(
            num_scalar_prefetch=1,
            grid=(t // tm,),
            in_specs=[pl.BlockSpec((tm, d), lambda i, dr: (i, 0)), pl.BlockSpec(memory_space=pl.ANY)],
            out_specs=pl.BlockSpec(memory_space=pl.ANY),
            scratch_shapes=[pltpu.SemaphoreType.DMA],
        ),
        out_shape=jax.ShapeDtypeStruct(xs_zero.shape, xs_zero.dtype),
        input_output_aliases={2: 0},
        compiler_params=pltpu.CompilerParams(dimension_semantics=("arbitrary",), has_side_effects=True,
                                             vmem_limit_bytes=VMEM_LIMIT),
        name="moe_dispatch",
    )(dest_flat, x, xs_zero)


def _expert_kernel(be_ref, na_ref, x_ref, wg_ref, wu_ref, wd_ref, o_ref):
    @pl.when(pl.program_id(0) < na_ref[0])
    def _():
        xb = x_ref[...].astype(BF16)
        gate = _dot(xb, wg_ref[...])
        up = _dot(xb, wu_ref[...])
        hid = (jax.nn.silu(gate) * up).astype(BF16)
        o_ref[...] = _dot(hid, wd_ref[...])


def _experts(blk_e, n_active, xs, wg, wu, wd):
    rows, d = xs.shape
    nb = rows // MOE_BM
    act = lambda i, be, na: jnp.minimum(i, na[0] - 1)
    return pl.pallas_call(
        _expert_kernel,
        grid_spec=pltpu.PrefetchScalarGridSpec(
            num_scalar_prefetch=2,
            grid=(nb,),
            in_specs=[pl.BlockSpec((MOE_BM, d), lambda i, be, na: (act(i, be, na), 0)),
                      pl.BlockSpec((None, d, D_EXPERT), lambda i, be, na: (be[i], 0, 0)),
                      pl.BlockSpec((None, d, D_EXPERT), lambda i, be, na: (be[i], 0, 0)),
                      pl.BlockSpec((None, D_EXPERT, d), lambda i, be, na: (be[i], 0, 0))],
            out_specs=pl.BlockSpec((MOE_BM, d), lambda i, be, na: (act(i, be, na), 0)),
        ),
        out_shape=jax.ShapeDtypeStruct((rows, d), F32),
        compiler_params=_cparams(("arbitrary",)),
        name="moe_experts",
    )(blk_e, n_active, xs, wg, wu, wd)


def _combine_kernel(dest_ref, x_ref, w_ref, g_ref, b_ref, ys_ref, o_ref, buf_ref, sem, *, alpha, t_total):
    step = pl.program_id(0)
    tm = x_ref.shape[0]

    def copy(r, slot):
        src = dest_ref[slot * t_total + step * tm + r]
        return pltpu.make_async_copy(ys_ref.at[pl.ds(src, 1)], buf_ref.at[slot, pl.ds(r, 1)], sem)

    def issue(r, _):
        copy(r, 0).start()
        copy(r, 1).start()
        return 0

    def drain(r, _):
        copy(r, 0).wait()
        copy(r, 1).wait()
        return 0

    lax.fori_loop(0, tm, issue, 0, unroll=ROW_DMA_UNROLL)
    lax.fori_loop(0, tm, drain, 0, unroll=ROW_DMA_UNROLL)
    w = w_ref[...]
    z = alpha * x_ref[...] + w[:, 0:1] * buf_ref[0] + w[:, 1:2] * buf_ref[1]
    o_ref[...] = _layer_norm(z, g_ref[...], b_ref[...])


def _combine(dest_flat, x, wts_t, g, bb, ys, alpha, tm):
    t, d = x.shape
    return pl.pallas_call(
        functools.partial(_combine_kernel, alpha=alpha, t_total=t),
        grid_spec=pltpu.PrefetchScalarGridSpec(
            num_scalar_prefetch=1,
            grid=(t // tm,),
            in_specs=[pl.BlockSpec((tm, d), lambda i, dr: (i, 0)),
                      pl.BlockSpec((tm, 2), lambda i, dr: (i, 0)),
                      pl.BlockSpec(g.shape, lambda i, dr: (0, 0)),
                      pl.BlockSpec(bb.shape, lambda i, dr: (0, 0)),
                      pl.BlockSpec(memory_space=pl.ANY)],
            out_specs=pl.BlockSpec((tm, d), lambda i, dr: (i, 0)),
            scratch_shapes=[pltpu.VMEM((2, tm, d), F32), pltpu.SemaphoreType.DMA],
        ),
        out_shape=jax.ShapeDtypeStruct((t, d), F32),
        compiler_params=_cparams(("arbitrary",)),
        name="moe_combine_ln2",
    )(dest_flat, x, wts_t, g, bb, ys)


def _attention_tables(rel_bias, s):
    nc = s // CMP_STRIDE
    nqb = s // QB
    iq = jnp.arange(QB, dtype=I32)[None, None, :]
    c = jnp.arange(nc, dtype=I32)[None, :, None]
    blk = jnp.arange(nqb, dtype=I32)[:, None, None]
    rel_c = blk * QB + iq - (c * CMP_STRIDE + CMP_LEN - 1)
    rel_c = jnp.where(c < nc - 1, rel_c, -1)
    jk = jnp.arange(QB, dtype=I32)[None, :, None]
    rel_t = jnp.arange(N_OFF, dtype=I32)[:, None, None] * QB + iq - jk
    jw = jnp.arange(WINDOW + QB, dtype=I32)[:, None]
    rel_w = iq[0] - jw + WINDOW
    rel_w = jnp.where(rel_w < WINDOW, rel_w, -1)
    flat = rel_bias.reshape(-1)
    r4 = NSA_HPG * QB
    cb = _bias_table(flat, jnp.maximum(rel_c, -1).reshape(nqb * nc, QB), nc).reshape(NSA_GROUPS, nqb, nc, r4)
    bt = _bias_table(flat, jnp.maximum(rel_t, -1).reshape(N_OFF * QB, QB), QB, delta=True)
    bt = bt.reshape(NSA_GROUPS, N_OFF, QB, r4)
    wb = _bias_table(flat, jnp.maximum(rel_w, -1), WINDOW + QB)
    sat = jnp.repeat(rel_bias[REL_BUCKETS - 1].reshape(NSA_GROUPS, 1, NSA_HPG), QB, axis=2)
    hi = sat.astype(BF16)
    lo = (sat - hi.astype(F32)).astype(BF16)
    b31 = jnp.concatenate([hi, lo, jnp.zeros((NSA_GROUPS, AUG_ROWS - 2, r4), BF16)], axis=1)
    return cb, bt, wb, b31


def _selection_tables(s):
    nc = s // CMP_STRIDE
    ns = s // SLC_BLOCK
    c_start = np.arange(nc) * CMP_STRIDE
    s_start = np.arange(ns) * SLC_BLOCK
    cover = np.maximum(np.minimum(c_start[:, None] + CMP_LEN, s_start[None, :] + SLC_BLOCK)
                       - np.maximum(c_start[:, None], s_start[None, :]), 0).astype(np.float32) / CMP_LEN
    cover[nc - 1] = 0.0
    depth = -(-(ns + NSA_DH + AUG_ROWS) // LANES) * LANES
    blockcols = np.where(np.arange(s)[:, None] // SLC_BLOCK == np.arange(ns)[None, :], -MASK_BIG, 0.0)
    tail = np.zeros((s, depth - ns - NSA_DH), np.float32)
    tail[:, 0:2] = 1.0
    return jnp.asarray(cover.T, BF16), jnp.asarray(blockcols, BF16), jnp.asarray(tail, BF16)


def _split_in_weights(w_in_l):
    o = np.cumsum((0, 512, 128, 128, 128, 128, 128, 128, 24, 512, 512, 1024, 1024, 1024, 1024))
    seg = lambda a: w_in_l[:, o[a]:o[a + 1]]
    wb = jnp.concatenate([seg(0), seg(1), seg(2), seg(3), seg(4), seg(5), seg(6), seg(10)], axis=1)
    gpad = jnp.pad(seg(7), ((0, 0), (0, LANES - 24)))
    wf = jnp.concatenate([seg(8), seg(9), seg(11), seg(12), seg(13), gpad], axis=1)
    return wb.astype(BF16), wf.astype(BF16)


def kernel(x, w_in, cmp_pos, cmp_w1, cmp_w2, w_br_a, w_br_b, w_o, ln1_g, ln1_b, w_router, router_bias,
           moe_w_gate, moe_w_up, moe_w_down, ln2_g, ln2_b, rel_bias):
    b, s, d = x.shape
    depth = w_in.shape[0]
    t = b * s
    alpha = (2 * depth) ** 0.25
    g, r, dh = NSA_GROUPS, NSA_HPG, NSA_DH
    nqb = s // QB
    nb16 = s // CMP_STRIDE
    n_sel = min(N_SELECT, s // SLC_BLOCK)

    cb, bt, wb, b31 = _attention_tables(rel_bias, s)
    cov_t, blockcols, tailcols = _selection_tables(s)
    ret_tabs = _retention_tables(s)
    tri = jnp.asarray(np.triu(np.ones((512, 512), np.float32), 1), BF16)
    wr_t = w_router.T
    rb_col = router_bias.reshape(N_EXPERTS, 1)
    n_blk = t * 2 // MOE_BM + N_EXPERTS
    rows = n_blk * MOE_BM

    xf = x.reshape(t, d)
    for l in range(depth):
        w_b, w_f = _split_in_weights(w_in[l])
        xb = xf.astype(BF16)
        pb = _matmul(xb, w_b, BF16, 1024, 1152)
        pf = _matmul(xb, w_f, F32, 1024, 1408)

        def kv_heads(col):
            return pb[:, col:col + g * dh].reshape(b, s, g, dh).transpose(0, 2, 1, 3)
        x16 = jnp.stack([kv_heads(512), kv_heads(640)]).reshape(2, b, g, nb16, CMP_STRIDE * dh)
        w1 = cmp_w1[l].astype(BF16)
        half = CMP_STRIDE * dh
        w1ab = jnp.concatenate([w1[:, :half], w1[:, half:]], axis=2)
        pos8 = jnp.broadcast_to(cmp_pos[l].reshape(2, 1, CMP_LEN * dh), (2, 8, CMP_LEN * dh)).astype(BF16)
        kvc = _compress(x16, w1ab, _posterm(pos8, w1), cmp_w2[l].astype(BF16))
        qt = (pb[:, 0:512].reshape(b, nqb, QB, g, r, dh).transpose(0, 3, 1, 5, 4, 2)
              .reshape(b, g, nqb, dh, r * QB))
        gl = (pf[:, 4096:4096 + 24].reshape(b, nqb, QB, g, r, 3).transpose(0, 3, 1, 5, 4, 2)
              .reshape(b, g, nqb, 3, r * QB))

        def v_aug_t(v, width):
            length = v.shape[2]
            vt = v.transpose(0, 1, 3, 2)
            extra = jnp.zeros((b, g, AUG_ROWS, length), BF16).at[:, :, 0].set(1.0)
            va = jnp.concatenate([vt, extra], axis=2)
            return va.reshape(b, g, dh + AUG_ROWS, length // width, width).transpose(0, 1, 3, 2, 4)

        padw = lambda a: jnp.pad(a, ((0, 0), (0, 0), (WINDOW, 0), (0, 0)))
        bcast = lambda a: jnp.broadcast_to(a[None, None], (b, g) + a.shape)
        ksa = jnp.concatenate([bcast(blockcols), kv_heads(768), bcast(tailcols)], axis=3)
        ot = _nsa_attention(qt, kvc[0], kvc[1].transpose(0, 1, 3, 2), ksa, v_aug_t(kv_heads(896), SEL_TK),
                            padw(kv_heads(1024)), v_aug_t(padw(kv_heads(1152)), QB), gl, cb, bt, wb,
                            cov_t, b31, n_sel)
        ya = (ot.reshape(b, g, nqb, dh, r, QB).transpose(0, 2, 5, 1, 4, 3).reshape(t, g * r * dh))

        yb = _retention(pf, pb, b, s, 0, 512, 1280, 1024, ret_tabs)

        x1 = _merge(ya, yb, pf, 2048, 3072, xf, w_br_a[l].astype(BF16), w_br_b[l].astype(BF16),
                    w_o[l].astype(BF16), ln1_g[l].reshape(1, d), ln1_b[l].reshape(1, d), alpha, 512)

        eidx, wts, rank, cnt = _router(x1, wr_t, rb_col, tri, 512)
        counts = cnt[:, 0].astype(I32)
        padded = ((counts + MOE_BM - 1) // MOE_BM) * MOE_BM
        pad_end = jnp.cumsum(padded)
        pad_start = pad_end - padded
        e_iota = jnp.arange(N_EXPERTS, dtype=I32)[:, None, None]
        start_of = jnp.sum(jnp.where(eidx[None] == e_iota, pad_start[:, None, None], 0), axis=0)
        dest = (start_of + rank).reshape(-1)
        n_active = (pad_end[-1] // MOE_BM).astype(I32).reshape(1)
        blk = jnp.minimum(jnp.arange(n_blk, dtype=I32), n_active[0] - 1) * MOE_BM
        blk_e = jnp.minimum(jnp.searchsorted(pad_end, blk, side='right'), N_EXPERTS - 1).astype(I32)
        xs = _dispatch(dest, x1, jnp.zeros((rows, d), F32), 512)
        ys = _experts(blk_e, n_active, xs, moe_w_gate[l].astype(BF16), moe_w_up[l].astype(BF16),
                      moe_w_down[l].astype(BF16))
        xf = _combine(dest, x1, wts.T, ln2_g[l].reshape(1, d), ln2_b[l].reshape(1, d), ys, alpha, 256)
    return xf.reshape(b, s, d)
```

```python
import functools
import math

import jax
import jax.numpy as jnp
import numpy as np
from jax import lax
from jax.experimental import pallas as pl
from jax.experimental.pallas import tpu as pltpu

F32 = jnp.float32
BF16 = jnp.bfloat16
I32 = jnp.int32

NSA_HEADS = 8
NSA_GROUPS = 2
NSA_HPG = 4
NSA_DH = 64
CMP_LEN = 32
CMP_STRIDE = 16
SLC_BLOCK = 64
N_SELECT = 16
WINDOW = 512
QB = 128
FORCE_SCORE = 1.0e4
RET_HEADS = 8
RET_DK = 64
RET_DV = 128
RET_CHUNK = 128
ROPE_BASE = 10000.0
REL_BUCKETS = 32
REL_MAX_EXACT = 16
REL_MAX_DIST = 2048
N_EXPERTS = 16
N_GROUPS = 4
EPG = 4
D_EXPERT = 512
LN_EPS = 1e-5
GN_EPS = 1e-5
NEG_INF = -1e30

LANES = 128
VMEM_LIMIT = 56 * 1024 * 1024

SEL_TK = 512
SEL_CW = 512
AUG_ROWS = 16
MASK_BIG = 2.0 ** 100
N_OFF = 14
MOE_BM = 256
ROW_DMA_UNROLL = 8


def _cparams(sem):
    return pltpu.CompilerParams(dimension_semantics=sem, vmem_limit_bytes=VMEM_LIMIT)


def _dot(a, b):
    return jnp.dot(a, b, preferred_element_type=F32)


def _dot_nt(a, b):
    return lax.dot_general(a, b, (((1,), (1,)), ((), ())), preferred_element_type=F32)


def _mm_kernel(a_ref, b_ref, o_ref):
    o_ref[...] = _dot(a_ref[...], b_ref[...]).astype(o_ref.dtype)


def _matmul(a, b, out_dtype, tm, tn):
    m, k = a.shape
    n = b.shape[1]
    return pl.pallas_call(
        _mm_kernel,
        grid=(n // tn, m // tm),
        in_specs=[pl.BlockSpec((tm, k), lambda j, i: (i, 0)),
                  pl.BlockSpec((k, tn), lambda j, i: (0, j))],
        out_specs=pl.BlockSpec((tm, tn), lambda j, i: (i, j)),
        out_shape=jax.ShapeDtypeStruct((m, n), out_dtype),
        compiler_params=_cparams(("arbitrary", "arbitrary")),
        name="proj_matmul",
    )(a, b)


def _bias_kernel(tab_ref, rel_ref, o_ref, *, delta):
    rel = rel_ref[...]
    n = jnp.maximum(rel, 0)
    nf = jnp.maximum(n, 1).astype(F32)
    large = REL_MAX_EXACT + (jnp.log(nf / REL_MAX_EXACT) / math.log(REL_MAX_DIST / REL_MAX_EXACT)
                             * (REL_BUCKETS - REL_MAX_EXACT)).astype(I32)
    large = jnp.minimum(large, REL_BUCKETS - 1)
    bucket = jnp.where(n < REL_MAX_EXACT, n, large)
    eqs = [bucket == k for k in range(REL_BUCKETS)]
    for h in range(NSA_HEADS):
        acc = jnp.full(rel.shape, NEG_INF, F32)
        for k in range(REL_BUCKETS):
            acc = jnp.where(eqs[k], tab_ref[k * NSA_HEADS + h], acc)
        if delta:
            acc = acc - tab_ref[(REL_BUCKETS - 1) * NSA_HEADS + h]
        hh = h % NSA_HPG
        o_ref[h // NSA_HPG, :, hh * QB:(hh + 1) * QB] = jnp.where(rel >= 0, acc, NEG_INF)


def _bias_table(rel_bias_flat, rel, tr, delta=False):
    r, c = rel.shape
    return pl.pallas_call(
        functools.partial(_bias_kernel, delta=delta),
        grid_spec=pltpu.PrefetchScalarGridSpec(
            num_scalar_prefetch=1,
            grid=(r // tr,),
            in_specs=[pl.BlockSpec((tr, c), lambda i, tab: (i, 0))],
            out_specs=pl.BlockSpec((NSA_GROUPS, tr, NSA_HPG * c), lambda i, tab: (0, i, 0)),
        ),
        out_shape=jax.ShapeDtypeStruct((NSA_GROUPS, r, NSA_HPG * c), F32),
        compiler_params=_cparams(("arbitrary",)),
        name="t5_bias_table",
    )(rel_bias_flat, rel)


def _compress_kernel(x_ref, w1_ref, pos_ref, w2_ref, o_ref):
    ab = _dot(x_ref[...], w1_ref[...])
    nb = ab.shape[0]
    nxt = pltpu.roll(ab, nb - 1, axis=0)
    nxt = pltpu.roll(nxt, NSA_DH, axis=1)
    pre = (ab + nxt)[:, :NSA_DH] + pos_ref[...]
    hid = jax.nn.gelu(pre)
    o_ref[...] = _dot(hid.astype(BF16), w2_ref[...]).astype(o_ref.dtype)


def _compress(x16, w1ab, posterm, w2):
    _, b, g, nb, wd = x16.shape
    return pl.pallas_call(
        _compress_kernel,
        grid=(2, b, g),
        in_specs=[pl.BlockSpec((None, None, None, nb, wd), lambda s, bi, gi: (s, bi, gi, 0, 0)),
                  pl.BlockSpec((None, wd, 2 * NSA_DH), lambda s, bi, gi: (s, 0, 0)),
                  pl.BlockSpec((None, 1, NSA_DH), lambda s, bi, gi: (s, 0, 0)),
                  pl.BlockSpec((None, NSA_DH, NSA_DH), lambda s, bi, gi: (s, 0, 0))],
        out_specs=pl.BlockSpec((None, None, None, nb, NSA_DH), lambda s, bi, gi: (s, bi, gi, 0, 0)),
        out_shape=jax.ShapeDtypeStruct((2, b, g, nb, NSA_DH), BF16),
        compiler_params=_cparams(("arbitrary",) * 3),
        name="nsa_compress",
    )(x16, w1ab, posterm, w2)


def _posterm_kernel(p_ref, w_ref, o_ref):
    o_ref[...] = _dot(p_ref[...], w_ref[...])[0:1]


def _posterm(pos8, w1):
    return pl.pallas_call(
        _posterm_kernel,
        grid=(2,),
        in_specs=[pl.BlockSpec((None, 8, pos8.shape[2]), lambda s: (s, 0, 0)),
                  pl.BlockSpec((None, w1.shape[1], NSA_DH), lambda s: (s, 0, 0))],
        out_specs=pl.BlockSpec((None, 1, NSA_DH), lambda s: (s, 0, 0)),
        out_shape=jax.ShapeDtypeStruct((2, 1, NSA_DH), F32),
        name="nsa_posterm",
    )(pos8, w1)


def _nsa_kernel(q_ref, kc_ref, vc_ref, ks_ref, vs_ref, kw_ref, vw_ref, gate_ref,
                cb_ref, bt_ref, wb_ref, cov_ref, b31_ref, o_ref, s_ref, *, n_sel):
    i = pl.program_id(2)
    q0 = i * QB
    r4 = NSA_HPG * QB
    q = (q_ref[...].astype(F32) * (NSA_DH ** -0.5)).astype(BF16)

    cb = cb_ref[...]
    s_c = _dot(kc_ref[...], q) + cb
    m_c = jnp.max(s_c, axis=0, keepdims=True)
    p_c = jnp.where(cb > 0.5 * NEG_INF, jnp.exp(s_c - m_c), 0.0)
    l_c = jnp.sum(p_c, axis=0, keepdims=True)
    p_c = p_c / jnp.maximum(l_c, 1e-30)
    o_c = _dot(vc_ref[...], p_c.astype(BF16))

    psum = p_c[:, 0:QB] + p_c[:, QB:2 * QB] + p_c[:, 2 * QB:3 * QB] + p_c[:, 3 * QB:4 * QB]
    p_hi = psum.astype(BF16)
    p_lo = (psum - p_hi.astype(F32)).astype(BF16)
    cov = cov_ref[...]
    imp_t = _dot(cov, p_hi) + _dot(cov, p_lo)
    ns = imp_t.shape[0]
    jidx = lax.broadcasted_iota(I32, (ns, QB), 0)
    t = q0 + lax.broadcasted_iota(I32, (ns, QB), 1)
    cur = jnp.right_shift(t, 6)
    eligible = jidx <= cur
    forced = (jidx == 0) | (jidx == cur) | (jidx == cur - 1)
    score = jnp.where(eligible, imp_t + jnp.where(forced, FORCE_SCORE, 0.0), -1.0)
    sel = jnp.zeros((ns, QB), F32)
    for _ in range(n_sel):
        mx = jnp.max(score, axis=0, keepdims=True)
        first = jnp.min(jnp.where(score == mx, jidx, ns), axis=0, keepdims=True)
        pick = jidx == first
        sel = jnp.where(pick, 1.0, sel)
        score = jnp.where(pick, -2.0, score)
    notsel = jnp.where(eligible, 1.0 - sel, 1.0).astype(BF16)

    depth = ks_ref.shape[-1]
    pad_rows = depth - ns - NSA_DH - AUG_ROWS
    qa = jnp.concatenate([jnp.concatenate([notsel] * NSA_HPG, axis=1), q, b31_ref[...],
                          jnp.zeros((pad_rows, r4), BF16)], axis=0)
    sub = SEL_TK // QB
    n_tiles = i // sub + 1
    n_far = jnp.maximum(i - N_OFF + 2, 0) // sub

    def put_scores(jt, slot):
        k0 = pl.multiple_of(jt * SEL_TK, SEL_TK)
        s_ref[slot] = _dot(ks_ref[pl.ds(k0, SEL_TK), :], qa)

    def absorb(jt, slot, carry, near):
        m_prev, acc = carry
        s = s_ref[slot]
        if near:
            offs = [jnp.clip(i - sub * jt - u, 0, N_OFF - 1) for u in range(sub)]
            s = s + jnp.concatenate([bt_ref[o] for o in offs], axis=0)
        m_new = jnp.maximum(m_prev, jnp.max(s, axis=0, keepdims=True))
        alpha = jnp.exp(m_prev - m_new)
        p = jnp.exp(s - m_new).astype(BF16)
        return m_new, alpha * acc + _dot(vs_ref[jt], p)

    def pair_body(pp, carry, near):
        ja = 2 * pp
        put_scores(ja + 1, 1)
        carry = absorb(ja, 0, carry, near)
        put_scores(jnp.minimum(ja + 2, n_tiles - 1), 0)
        return absorb(ja + 1, 1, carry, near)

    n_pairs = n_tiles // 2
    far_pairs = n_far // 2
    put_scores(0, 0)
    carry = (jnp.full((1, r4), -1e38, F32), jnp.zeros((NSA_DH + AUG_ROWS, r4), F32))
    carry = lax.fori_loop(0, far_pairs, functools.partial(pair_body, near=False), carry)
    carry = lax.fori_loop(far_pairs, n_pairs, functools.partial(pair_body, near=True), carry)
    _, acc_s = lax.cond(n_tiles % 2 == 1, lambda c: absorb(n_tiles - 1, 0, c, True), lambda c: c, carry)
    o_s = acc_s[0:NSA_DH] / acc_s[NSA_DH:NSA_DH + 1]

    wk = WINDOW + QB
    kwin = kw_ref[pl.ds(pl.multiple_of(q0, QB), wk), :]
    s_w = _dot(kwin, q) + wb_ref[...]
    kpos = q0 - WINDOW + lax.broadcasted_iota(I32, (wk, r4), 0)
    s_w = jnp.where(kpos >= 0, s_w, NEG_INF)
    m_w = jnp.max(s_w, axis=0, keepdims=True)
    p_w = jnp.exp(s_w - m_w).astype(BF16)
    acc_w = jnp.zeros((NSA_DH + AUG_ROWS, r4), F32)
    for u in range(wk // QB):
        acc_w = acc_w + _dot(vw_ref[i + u], p_w[u * QB:(u + 1) * QB])
    o_w = acc_w[0:NSA_DH] / acc_w[NSA_DH:NSA_DH + 1]

    gates = jax.nn.sigmoid(gate_ref[...])
    o = gates[0:1] * o_c + gates[1:2] * o_s + gates[2:3] * o_w
    o_ref[...] = o.astype(o_ref.dtype)


def _nsa_attention(qt, kc, vct, ksa, vst, kw, vwt, gates, cb, bt, wb, cov_t, b31, n_sel):
    b, g, nqb, dh, r4 = qt.shape
    nc = kc.shape[2]
    per_bg = lambda a: pl.BlockSpec((None, None) + a.shape[2:], lambda bi, gi, i: (bi, gi) + (0,) * (a.ndim - 2))
    per_g = lambda a: pl.BlockSpec((None,) + a.shape[1:], lambda bi, gi, i: (gi,) + (0,) * (a.ndim - 1))
    per_step = lambda a: pl.BlockSpec((None, None, None) + a.shape[3:],
                                      lambda bi, gi, i: (bi, gi, i) + (0,) * (a.ndim - 3))
    return pl.pallas_call(
        functools.partial(_nsa_kernel, n_sel=n_sel),
        grid=(b, g, nqb),
        in_specs=[
            per_step(qt), per_bg(kc), per_bg(vct), per_bg(ksa), per_bg(vst), per_bg(kw), per_bg(vwt),
            per_step(gates),
            pl.BlockSpec((None, None, nc, r4), lambda bi, gi, i: (gi, i, 0, 0)),
            per_g(bt), per_g(wb),
            pl.BlockSpec(cov_t.shape, lambda bi, gi, i: (0, 0)),
            per_g(b31),
        ],
        out_specs=per_step(qt),
        out_shape=jax.ShapeDtypeStruct(qt.shape, BF16),
        scratch_shapes=[pltpu.VMEM((2, SEL_TK, r4), F32)],
        compiler_params=_cparams(("arbitrary",) * 3),
        name="nsa_attention",
    )(qt, kc, vct, ksa, vst, kw, vwt, gates, cb, bt, wb, cov_t, b31)


def _ret_kernel(q_ref, k_ref, v_ref, g_ref, cos_ref, sin_ref, dm_ref, qd_ref, kd_ref, cd_ref,
                o_ref, state_ref):
    n = pl.program_id(2)

    @pl.when(n == 0)
    def _():
        state_ref[...] = jnp.zeros_like(state_ref)

    lane = lax.broadcasted_iota(I32, (RET_CHUNK, LANES), 1)
    first_half = (lane & (RET_DK - 1)) < (RET_DK // 2)
    cosv = cos_ref[...]
    sinv = sin_ref[...]

    def rot(x):
        partner = jnp.where(first_half, pltpu.roll(x, LANES - RET_DK // 2, axis=1),
                            pltpu.roll(x, RET_DK // 2, axis=1))
        return x * cosv + partner * sinv

    q2 = rot(q_ref[...])
    k2 = rot(k_ref[...]) * (RET_DK ** -0.5)
    qdec = (q2 * qd_ref[...]).astype(BF16)
    kdec_t = (k2 * kd_ref[...]).T.astype(BF16)
    q2b = q2.astype(BF16)
    k2b = k2.astype(BF16)
    for e in range(2):
        hs = slice(e * RET_DK, (e + 1) * RET_DK)
        vs = slice(e * RET_DV, (e + 1) * RET_DV)
        v = v_ref[:, vs]
        st = state_ref[e]
        sc = _dot_nt(q2b[:, hs], k2b[:, hs]) * dm_ref[e]
        y = _dot(sc.astype(BF16), v) + _dot(qdec[:, hs], st.astype(BF16))
        state_ref[e] = cd_ref[e] * st + _dot(kdec_t[hs, :], v)
        mu = jnp.mean(y, axis=-1, keepdims=True)
        var = jnp.mean(jnp.square(y - mu), axis=-1, keepdims=True)
        yn = (y - mu) * lax.rsqrt(var + GN_EPS)
        o_ref[:, vs] = (jax.nn.silu(g_ref[:, vs]) * yn).astype(o_ref.dtype)


def _retention(pf, pb, b, s, col_q, col_k, col_v, col_g, tabs):
    cosf, sinf, dm, qd, kd, cd = tabs
    n = s // RET_CHUNK
    hp = RET_HEADS // 2
    t = b * s
    c = RET_CHUNK
    row = lambda bi, h, ni: bi * n + ni
    return pl.pallas_call(
        _ret_kernel,
        grid=(b, hp, n),
        in_specs=[
            pl.BlockSpec((c, LANES), lambda bi, h, ni: (row(bi, h, ni), col_q // LANES + h)),
            pl.BlockSpec((c, LANES), lambda bi, h, ni: (row(bi, h, ni), col_k // LANES + h)),
            pl.BlockSpec((c, 2 * RET_DV), lambda bi, h, ni: (row(bi, h, ni), col_v // (2 * RET_DV) + h)),
            pl.BlockSpec((c, 2 * RET_DV), lambda bi, h, ni: (row(bi, h, ni), col_g // (2 * RET_DV) + h)),
            pl.BlockSpec((c, LANES), lambda bi, h, ni: (ni, 0)),
            pl.BlockSpec((c, LANES), lambda bi, h, ni: (ni, 0)),
            pl.BlockSpec((2, c, c), lambda bi, h, ni: (h, 0, 0)),
            pl.BlockSpec((c, LANES), lambda bi, h, ni: (0, h)),
            pl.BlockSpec((c, LANES), lambda bi, h, ni: (0, h)),
            pl.BlockSpec((2, 1, LANES), lambda bi, h, ni: (h, 0, 0)),
        ],
        out_specs=pl.BlockSpec((c, 2 * RET_DV), lambda bi, h, ni: (row(bi, h, ni), h)),
        out_shape=jax.ShapeDtypeStruct((t, RET_HEADS * RET_DV), BF16),
        scratch_shapes=[pltpu.VMEM((2, RET_DK, RET_DV), F32)],
        compiler_params=_cparams(("arbitrary",) * 3),
        name="retention",
    )(pf, pf, pb, pf, cosf, sinf, dm, qd, kd, cd)


def _retention_tables(s):
    half = RET_DK // 2
    inv = ROPE_BASE ** (-jnp.arange(half, dtype=F32) * 2.0 / RET_DK)
    ang = jnp.arange(s, dtype=F32)[:, None] * inv[None, :]
    cos, sin = jnp.cos(ang), jnp.sin(ang)
    cosf = jnp.concatenate([cos, cos, cos, cos], axis=1)
    sinf = jnp.concatenate([-sin, sin, -sin, sin], axis=1)
    lg = jnp.log(1.0 - 2.0 ** (-5.0 - jnp.arange(RET_HEADS, dtype=F32)))
    i = jnp.arange(RET_CHUNK, dtype=F32)
    diff = i[:, None] - i[None, :]
    dm = jnp.where(diff >= 0, jnp.exp(jnp.maximum(diff, 0.0)[None] * lg[:, None, None]), 0.0)
    kdec = jnp.exp((RET_CHUNK - 1.0 - i)[None, :] * lg[:, None])
    qdec = jnp.exp((i + 1.0)[None, :] * lg[:, None])
    expand = lambda a: jnp.repeat(a.T, RET_DK, axis=1)
    cd = jnp.broadcast_to(jnp.exp(RET_CHUNK * lg)[:, None, None], (RET_HEADS, 1, LANES))
    return cosf, sinf, dm, expand(qdec), expand(kdec), cd


def _layer_norm(z, g, b):
    mu = jnp.mean(z, axis=-1, keepdims=True)
    var = jnp.mean(jnp.square(z - mu), axis=-1, keepdims=True)
    return (z - mu) * lax.rsqrt(var + LN_EPS) * g + b


def _merge_kernel(ya_ref, yb_ref, ma_ref, mb_ref, x_ref, wa_ref, wb_ref, wo_ref, g_ref, b_ref,
                  o_ref, *, alpha):
    merged = (jax.nn.sigmoid(ma_ref[...]) * _dot(ya_ref[...], wa_ref[...])
              + jax.nn.sigmoid(mb_ref[...]) * _dot(yb_ref[...], wb_ref[...]))
    z = alpha * x_ref[...] + _dot(merged.astype(BF16), wo_ref[...])
    o_ref[...] = _layer_norm(z, g_ref[...], b_ref[...])


def _merge(ya, yb, pf, col_ma, col_mb, x, wa, wb, wo, g, bb, alpha, tm):
    t, d = x.shape
    full = lambda a: pl.BlockSpec(a.shape, lambda i: (0,) * a.ndim)
    return pl.pallas_call(
        functools.partial(_merge_kernel, alpha=alpha),
        grid=(t // tm,),
        in_specs=[pl.BlockSpec((tm, ya.shape[1]), lambda i: (i, 0)),
                  pl.BlockSpec((tm, yb.shape[1]), lambda i: (i, 0)),
                  pl.BlockSpec((tm, d), lambda i: (i, col_ma // d)),
                  pl.BlockSpec((tm, d), lambda i: (i, col_mb // d)),
                  pl.BlockSpec((tm, d), lambda i: (i, 0)),
                  full(wa), full(wb), full(wo), full(g), full(bb)],
        out_specs=pl.BlockSpec((tm, d), lambda i: (i, 0)),
        out_shape=jax.ShapeDtypeStruct((t, d), F32),
        compiler_params=_cparams(("arbitrary",)),
        name="merge_ln1",
    )(ya, yb, pf, pf, x, wa, wb, wo, g, bb)


def _router_kernel(x_ref, wr_ref, rb_ref, tri_ref, eidx_ref, wts_ref, rank_ref, cnt_ref, carry_ref):
    step = pl.program_id(0)

    @pl.when(step == 0)
    def _():
        carry_ref[...] = jnp.zeros_like(carry_ref)

    logits = lax.dot_general(wr_ref[...], x_ref[...], (((1,), (1,)), ((), ())),
                             precision=lax.Precision.HIGHEST, preferred_element_type=F32)
    tm = logits.shape[1]
    mx = jnp.max(logits, axis=0, keepdims=True)
    ex = jnp.exp(logits - mx)
    probs = ex / jnp.sum(ex, axis=0, keepdims=True)
    sel = probs + rb_ref[...]
    rows = [sel[e:e + 1] for e in range(N_EXPERTS)]
    gscore = []
    for gi in range(N_GROUPS):
        r = rows[gi * EPG:(gi + 1) * EPG]
        best2 = None
        for a in range(EPG):
            for c in range(a + 1, EPG):
                pair = r[a] + r[c]
                best2 = pair if best2 is None else jnp.maximum(best2, pair)
        gscore.append(best2)
    best = jnp.zeros((1, tm), I32)
    bscore = gscore[0]
    for gi in range(1, N_GROUPS):
        better = gscore[gi] > bscore
        best = jnp.where(better, gi, best)
        bscore = jnp.where(better, gscore[gi], bscore)
    ing = []
    for a in range(EPG):
        v = rows[a]
        for gi in range(1, N_GROUPS):
            v = jnp.where(best == gi, rows[gi * EPG + a], v)
        ing.append(v)
    i1 = jnp.zeros((1, tm), I32)
    v1 = ing[0]
    for a in range(1, EPG):
        better = ing[a] > v1
        i1 = jnp.where(better, a, i1)
        v1 = jnp.where(better, ing[a], v1)
    i2 = jnp.full((1, tm), -1, I32)
    v2 = jnp.full((1, tm), -jnp.inf, F32)
    for a in range(EPG):
        better = (i1 != a) & (ing[a] > v2)
        i2 = jnp.where(better, a, i2)
        v2 = jnp.where(better, ing[a], v2)
    e0 = best * EPG + i1
    e1 = best * EPG + i2
    eio = lax.broadcasted_iota(I32, (N_EXPERTS, tm), 0)
    oh0 = (eio == e0).astype(F32)
    oh1 = (eio == e1).astype(F32)
    p0 = jnp.sum(oh0 * probs, axis=0, keepdims=True)
    p1 = jnp.sum(oh1 * probs, axis=0, keepdims=True)
    den = p0 + p1
    eidx_ref[0:1, :] = e0
    eidx_ref[1:2, :] = e1
    wts_ref[0:1, :] = p0 / den
    wts_ref[1:2, :] = p1 / den
    tot = oh0 + oh1
    prefix = _dot(tot.astype(BF16), tri_ref[...]) + carry_ref[:, 0:1]
    rank_ref[0:1, :] = jnp.sum(oh0 * prefix, axis=0, keepdims=True).astype(I32)
    rank_ref[1:2, :] = jnp.sum(oh1 * prefix, axis=0, keepdims=True).astype(I32)
    carry_ref[...] = carry_ref[...] + jnp.sum(tot, axis=1, keepdims=True)
    cnt_ref[...] = carry_ref[...]


def _router(x, wr_t, rb_col, tri, tm):
    t, d = x.shape
    out2 = lambda dt: jax.ShapeDtypeStruct((2, t), dt)
    row2 = pl.BlockSpec((2, tm), lambda i: (0, i))
    return pl.pallas_call(
        _router_kernel,
        grid=(t // tm,),
        in_specs=[pl.BlockSpec((tm, d), lambda i: (i, 0)),
                  pl.BlockSpec(wr_t.shape, lambda i: (0, 0)),
                  pl.BlockSpec(rb_col.shape, lambda i: (0, 0)),
                  pl.BlockSpec(tri.shape, lambda i: (0, 0))],
        out_specs=[row2, row2, row2, pl.BlockSpec((N_EXPERTS, LANES), lambda i: (0, 0))],
        out_shape=[out2(I32), out2(F32), out2(I32), jax.ShapeDtypeStruct((N_EXPERTS, LANES), F32)],
        scratch_shapes=[pltpu.VMEM((N_EXPERTS, LANES), F32)],
        compiler_params=_cparams(("arbitrary",)),
        name="moe_router",
    )(x, wr_t, rb_col, tri)


def _dispatch_kernel(dest_ref, x_ref, xs_in_ref, xs_ref, sem, *, t_total):
    del xs_in_ref
    step = pl.program_id(0)
    tm = x_ref.shape[0]

    def copy(r, slot):
        dst = dest_ref[slot * t_total + step * tm + r]
        return pltpu.make_async_copy(x_ref.at[pl.ds(r, 1)], xs_ref.at[pl.ds(dst, 1)], sem)

    def issue(r, _):
        copy(r, 0).start()
        copy(r, 1).start()
        return 0

    def drain(r, _):
        copy(r, 0).wait()
        copy(r, 1).wait()
        return 0

    lax.fori_loop(0, tm, issue, 0, unroll=ROW_DMA_UNROLL)
    lax.fori_loop(0, tm, drain, 0, unroll=ROW_DMA_UNROLL)


def _dispatch(dest_flat, x, xs_zero, tm):
    t, d = x.shape
    return pl.pallas_call(
        functools.partial(_dispatch_kernel, t_total=t),
        grid_spec=pltpu.PrefetchScalarGridSpec(
            num_scalar_prefetch=1,
            grid=(t // tm,),
            in_specs=[pl.BlockSpec((tm, d), lambda i, dr: (i, 0)), pl.BlockSpec(memory_space=pl.ANY)],
            out_specs=pl.BlockSpec(memory_space=pl.ANY),
            scratch_shapes=[pltpu.SemaphoreType.DMA],
        ),
        out_shape=jax.ShapeDtypeStruct(xs_zero.shape, xs_zero.dtype),
        input_output_aliases={2: 0},
        compiler_params=pltpu.CompilerParams(dimension_semantics=("arbitrary",), has_side_effects=True,
                                             vmem_limit_bytes=VMEM_LIMIT),
        name="moe_dispatch",
    )(dest_flat, x, xs_zero)


def _expert_kernel(be_ref, na_ref, x_ref, wg_ref, wu_ref, wd_ref, o_ref):
    @pl.when(pl.program_id(0) < na_ref[0])
    def _():
        xb = x_ref[...].astype(BF16)
        gate = _dot(xb, wg_ref[...])
        up = _dot(xb, wu_ref[...])
        hid = (jax.nn.silu(gate) * up).astype(BF16)
        o_ref[...] = _dot(hid, wd_ref[...])


def _experts(blk_e, n_active, xs, wg, wu, wd):
    rows, d = xs.shape
    nb = rows // MOE_BM
    act = lambda i, be, na: jnp.minimum(i, na[0] - 1)
    return pl.pallas_call(
        _expert_kernel,
        grid_spec=pltpu.PrefetchScalarGridSpec(
            num_scalar_prefetch=2,
            grid=(nb,),
            in_specs=[pl.BlockSpec((MOE_BM, d), lambda i, be, na: (act(i, be, na), 0)),
                      pl.BlockSpec((None, d, D_EXPERT), lambda i, be, na: (be[i], 0, 0)),
                      pl.BlockSpec((None, d, D_EXPERT), lambda i, be, na: (be[i], 0, 0)),
                      pl.BlockSpec((None, D_EXPERT, d), lambda i, be, na: (be[i], 0, 0))],
            out_specs=pl.BlockSpec((MOE_BM, d), lambda i, be, na: (act(i, be, na), 0)),
        ),
        out_shape=jax.ShapeDtypeStruct((rows, d), F32),
        compiler_params=_cparams(("arbitrary",)),
        name="moe_experts",
    )(blk_e, n_active, xs, wg, wu, wd)


def _combine_kernel(dest_ref, x_ref, w_ref, g_ref, b_ref, ys_ref, o_ref, buf_ref, sem, *, alpha, t_total):
    step = pl.program_id(0)
    tm = x_ref.shape[0]

    def copy(r, slot):
        src = dest_ref[slot * t_total + step * tm + r]
        return pltpu.make_async_copy(ys_ref.at[pl.ds(src, 1)], buf_ref.at[slot, pl.ds(r, 1)], sem)

    def issue(r, _):
        copy(r, 0).start()
        copy(r, 1).start()
        return 0

    def drain(r, _):
        copy(r, 0).wait()
        copy(r, 1).wait()
        return 0

    lax.fori_loop(0, tm, issue, 0, unroll=ROW_DMA_UNROLL)
    lax.fori_loop(0, tm, drain, 0, unroll=ROW_DMA_UNROLL)
    w = w_ref[...]
    z = alpha * x_ref[...] + w[:, 0:1] * buf_ref[0] + w[:, 1:2] * buf_ref[1]
    o_ref[...] = _layer_norm(z, g_ref[...], b_ref[...])


def _combine(dest_flat, x, wts_t, g, bb, ys, alpha, tm):
    t, d = x.shape
    return pl.pallas_call(
        functools.partial(_combine_kernel, alpha=alpha, t_total=t),
        grid_spec=pltpu.PrefetchScalarGridSpec(
            num_scalar_prefetch=1,
            grid=(t // tm,),
            in_specs=[pl.BlockSpec((tm, d), lambda i, dr: (i, 0)),
                      pl.BlockSpec((tm, 2), lambda i, dr: (i, 0)),
                      pl.BlockSpec(g.shape, lambda i, dr: (0, 0)),
                      pl.BlockSpec(bb.shape, lambda i, dr: (0, 0)),
                      pl.BlockSpec(memory_space=pl.ANY)],
            out_specs=pl.BlockSpec((tm, d), lambda i, dr: (i, 0)),
            scratch_shapes=[pltpu.VMEM((2, tm, d), F32), pltpu.SemaphoreType.DMA],
        ),
        out_shape=jax.ShapeDtypeStruct((t, d), F32),
        compiler_params=_cparams(("arbitrary",)),
        name="moe_combine_ln2",
    )(dest_flat, x, wts_t, g, bb, ys)


def _attention_tables(rel_bias, s):
    nc = s // CMP_STRIDE
    nqb = s // QB
    iq = jnp.arange(QB, dtype=I32)[None, None, :]
    c = jnp.arange(nc, dtype=I32)[None, :, None]
    blk = jnp.arange(nqb, dtype=I32)[:, None, None]
    rel_c = blk * QB + iq - (c * CMP_STRIDE + CMP_LEN - 1)
    rel_c = jnp.where(c < nc - 1, rel_c, -1)
    jk = jnp.arange(QB, dtype=I32)[None, :, None]
    rel_t = jnp.arange(N_OFF, dtype=I32)[:, None, None] * QB + iq - jk
    jw = jnp.arange(WINDOW + QB, dtype=I32)[:, None]
    rel_w = iq[0] - jw + WINDOW
    rel_w = jnp.where(rel_w < WINDOW, rel_w, -1)
    flat = rel_bias.reshape(-1)
    r4 = NSA_HPG * QB
    cb = _bias_table(flat, jnp.maximum(rel_c, -1).reshape(nqb * nc, QB), nc).reshape(NSA_GROUPS, nqb, nc, r4)
    bt = _bias_table(flat, jnp.maximum(rel_t, -1).reshape(N_OFF * QB, QB), QB, delta=True)
    bt = bt.reshape(NSA_GROUPS, N_OFF, QB, r4)
    wb = _bias_table(flat, jnp.maximum(rel_w, -1), WINDOW + QB)
    sat = jnp.repeat(rel_bias[REL_BUCKETS - 1].reshape(NSA_GROUPS, 1, NSA_HPG), QB, axis=2)
    hi = sat.astype(BF16)
    lo = (sat - hi.astype(F32)).astype(BF16)
    b31 = jnp.concatenate([hi, lo, jnp.zeros((NSA_GROUPS, AUG_ROWS - 2, r4), BF16)], axis=1)
    return cb, bt, wb, b31


def _selection_tables(s):
    nc = s // CMP_STRIDE
    ns = s // SLC_BLOCK
    c_start = np.arange(nc) * CMP_STRIDE
    s_start = np.arange(ns) * SLC_BLOCK
    cover = np.maximum(np.minimum(c_start[:, None] + CMP_LEN, s_start[None, :] + SLC_BLOCK)
                       - np.maximum(c_start[:, None], s_start[None, :]), 0).astype(np.float32) / CMP_LEN
    cover[nc - 1] = 0.0
    depth = -(-(ns + NSA_DH + AUG_ROWS) // LANES) * LANES
    blockcols = np.where(np.arange(s)[:, None] // SLC_BLOCK == np.arange(ns)[None, :], -MASK_BIG, 0.0)
    tail = np.zeros((s, depth - ns - NSA_DH), np.float32)
    tail[:, 0:2] = 1.0
    return jnp.asarray(cover.T, BF16), jnp.asarray(blockcols, BF16), jnp.asarray(tail, BF16)


def _split_in_weights(w_in_l):
    o = np.cumsum((0, 512, 128, 128, 128, 128, 128, 128, 24, 512, 512, 1024, 1024, 1024, 1024))
    seg = lambda a: w_in_l[:, o[a]:o[a + 1]]
    wb = jnp.concatenate([seg(0), seg(1), seg(2), seg(3), seg(4), seg(5), seg(6), seg(10)], axis=1)
    gpad = jnp.pad(seg(7), ((0, 0), (0, LANES - 24)))
    wf = jnp.concatenate([seg(8), seg(9), seg(11), seg(12), seg(13), gpad], axis=1)
    return wb.astype(BF16), wf.astype(BF16)


def kernel(x, w_in, cmp_pos, cmp_w1, cmp_w2, w_br_a, w_br_b, w_o, ln1_g, ln1_b, w_router, router_bias,
           moe_w_gate, moe_w_up, moe_w_down, ln2_g, ln2_b, rel_bias):
    b, s, d = x.shape
    depth = w_in.shape[0]
    t = b * s
    alpha = (2 * depth) ** 0.25
    g, r, dh = NSA_GROUPS, NSA_HPG, NSA_DH
    nqb = s // QB
    nb16 = s // CMP_STRIDE
    n_sel = min(N_SELECT, s // SLC_BLOCK)

    cb, bt, wb, b31 = _attention_tables(rel_bias, s)
    cov_t, blockcols, tailcols = _selection_tables(s)
    ret_tabs = _retention_tables(s)
    tri = jnp.asarray(np.triu(np.ones((512, 512), np.float32), 1), BF16)
    wr_t = w_router.T
    rb_col = router_bias.reshape(N_EXPERTS, 1)
    n_blk = t * 2 // MOE_BM + N_EXPERTS
    rows = n_blk * MOE_BM

    xf = x.reshape(t, d)
    for l in range(depth):
        w_b, w_f = _split_in_weights(w_in[l])
        xb = xf.astype(BF16)
        pb = _matmul(xb, w_b, BF16, 1024, 1152)
        pf = _matmul(xb, w_f, F32, 1024, 1408)

        def kv_heads(col):
            return pb[:, col:col + g * dh].reshape(b, s, g, dh).transpose(0, 2, 1, 3)
        x16 = jnp.stack([kv_heads(512), kv_heads(640)]).reshape(2, b, g, nb16, CMP_STRIDE * dh)
        w1 = cmp_w1[l].astype(BF16)
        half = CMP_STRIDE * dh
        w1ab = jnp.concatenate([w1[:, :half], w1[:, half:]], axis=2)
        pos8 = jnp.broadcast_to(cmp_pos[l].reshape(2, 1, CMP_LEN * dh), (2, 8, CMP_LEN * dh)).astype(BF16)
        kvc = _compress(x16, w1ab, _posterm(pos8, w1), cmp_w2[l].astype(BF16))
        qt = (pb[:, 0:512].reshape(b, nqb, QB, g, r, dh).transpose(0, 3, 1, 5, 4, 2)
              .reshape(b, g, nqb, dh, r * QB))
        gl = (pf[:, 4096:4096 + 24].reshape(b, nqb, QB, g, r, 3).transpose(0, 3, 1, 5, 4, 2)
              .reshape(b, g, nqb, 3, r * QB))

        def v_aug_t(v, width):
            length = v.shape[2]
            vt = v.transpose(0, 1, 3, 2)
            extra = jnp.zeros((b, g, AUG_ROWS, length), BF16).at[:, :, 0].set(1.0)
            va = jnp.concatenate([vt, extra], axis=2)
            return va.reshape(b, g, dh + AUG_ROWS, length // width, width).transpose(0, 1, 3, 2, 4)

        padw = lambda a: jnp.pad(a, ((0, 0), (0, 0), (WINDOW, 0), (0, 0)))
        bcast = lambda a: jnp.broadcast_to(a[None, None], (b, g) + a.shape)
        ksa = jnp.concatenate([bcast(blockcols), kv_heads(768), bcast(tailcols)], axis=3)
        ot = _nsa_attention(qt, kvc[0], kvc[1].transpose(0, 1, 3, 2), ksa, v_aug_t(kv_heads(896), SEL_TK),
                            padw(kv_heads(1024)), v_aug_t(padw(kv_heads(1152)), QB), gl, cb, bt, wb,
                            cov_t, b31, n_sel)
        ya = (ot.reshape(b, g, nqb, dh, r, QB).transpose(0, 2, 5, 1, 4, 3).reshape(t, g * r * dh))

        yb = _retention(pf, pb, b, s, 0, 512, 1280, 1024, ret_tabs)

        x1 = _merge(ya, yb, pf, 2048, 3072, xf, w_br_a[l].astype(BF16), w_br_b[l].astype(BF16),
                    w_o[l].astype(BF16), ln1_g[l].reshape(1, d), ln1_b[l].reshape(1, d), alpha, 512)

        eidx, wts, rank, cnt = _router(x1, wr_t, rb_col, tri, 512)
        counts = cnt[:, 0].astype(I32)
        padded = ((counts + MOE_BM - 1) // MOE_BM) * MOE_BM
        pad_end = jnp.cumsum(padded)
        pad_start = pad_end - padded
        e_iota = jnp.arange(N_EXPERTS, dtype=I32)[:, None, None]
        start_of = jnp.sum(jnp.where(eidx[None] == e_iota, pad_start[:, None, None], 0), axis=0)
        dest = (start_of + rank).reshape(-1)
        n_active = (pad_end[-1] // MOE_BM).astype(I32).reshape(1)
        blk = jnp.minimum(jnp.arange(n_blk, dtype=I32), n_active[0] - 1) * MOE_BM
        blk_e = jnp.minimum(jnp.searchsorted(pad_end, blk, side='right'), N_EXPERTS - 1).astype(I32)
        xs = _dispatch(dest, x1, jnp.zeros((rows, d), F32), 512)
        ys = _experts(blk_e, n_active, xs, moe_w_gate[l].astype(BF16), moe_w_up[l].astype(BF16),
                      moe_w_down[l].astype(BF16))
        xf = _combine(dest, x1, wts.T, ln2_g[l].reshape(1, d), ln2_b[l].reshape(1, d), ys, alpha, 256)
    return xf.reshape(b, s, d)
```

```python
import functools
import math

import jax
import jax.numpy as jnp
import numpy as np
from jax import lax
from jax.experimental import pallas as pl
from jax.experimental.pallas import tpu as pltpu

F32 = jnp.float32
BF16 = jnp.bfloat16
I32 = jnp.int32

NSA_HEADS = 8
NSA_GROUPS = 2
NSA_HPG = 4
NSA_DH = 64
CMP_LEN = 32
CMP_STRIDE = 16
SLC_BLOCK = 64
N_SELECT = 16
WINDOW = 512
QB = 128
FORCE_SCORE = 1.0e4
RET_HEADS = 8
RET_DK = 64
RET_DV = 128
RET_CHUNK = 128
ROPE_BASE = 10000.0
REL_BUCKETS = 32
REL_MAX_EXACT = 16
REL_MAX_DIST = 2048
N_EXPERTS = 16
N_GROUPS = 4
EPG = 4
D_EXPERT = 512
LN_EPS = 1e-5
GN_EPS = 1e-5
NEG_INF = -1e30

LANES = 128
VMEM_LIMIT = 56 * 1024 * 1024

SEL_TK = 512
SEL_CW = 512
AUG_ROWS = 16
MASK_BIG = 2.0 ** 100
N_OFF = 14
MOE_BM = 256
ROW_DMA_UNROLL = 8


def _cparams(sem):
    return pltpu.CompilerParams(dimension_semantics=sem, vmem_limit_bytes=VMEM_LIMIT)


def _dot(a, b):
    return jnp.dot(a, b, preferred_element_type=F32)


def _dot_nt(a, b):
    return lax.dot_general(a, b, (((1,), (1,)), ((), ())), preferred_element_type=F32)


def _mm_kernel(a_ref, b_ref, o_ref):
    o_ref[...] = _dot(a_ref[...], b_ref[...]).astype(o_ref.dtype)


def _matmul(a, b, out_dtype, tm, tn):
    m, k = a.shape
    n = b.shape[1]
    return pl.pallas_call(
        _mm_kernel,
        grid=(n // tn, m // tm),
        in_specs=[pl.BlockSpec((tm, k), lambda j, i: (i, 0)),
                  pl.BlockSpec((k, tn), lambda j, i: (0, j))],
        out_specs=pl.BlockSpec((tm, tn), lambda j, i: (i, j)),
        out_shape=jax.ShapeDtypeStruct((m, n), out_dtype),
        compiler_params=_cparams(("arbitrary", "arbitrary")),
        name="proj_matmul",
    )(a, b)


def _bias_kernel(tab_ref, rel_ref, o_ref, *, delta):
    rel = rel_ref[...]
    n = jnp.maximum(rel, 0)
    nf = jnp.maximum(n, 1).astype(F32)
    large = REL_MAX_EXACT + (jnp.log(nf / REL_MAX_EXACT) / math.log(REL_MAX_DIST / REL_MAX_EXACT)
                             * (REL_BUCKETS - REL_MAX_EXACT)).astype(I32)
    large = jnp.minimum(large, REL_BUCKETS - 1)
    bucket = jnp.where(n < REL_MAX_EXACT, n, large)
    eqs = [bucket == k for k in range(REL_BUCKETS)]
    for h in range(NSA_HEADS):
        acc = jnp.full(rel.shape, NEG_INF, F32)
        for k in range(REL_BUCKETS):
            acc = jnp.where(eqs[k], tab_ref[k * NSA_HEADS + h], acc)
        if delta:
            acc = acc - tab_ref[(REL_BUCKETS - 1) * NSA_HEADS + h]
        hh = h % NSA_HPG
        o_ref[h // NSA_HPG, :, hh * QB:(hh + 1) * QB] = jnp.where(rel >= 0, acc, NEG_INF)


def _bias_table(rel_bias_flat, rel, tr, delta=False):
    r, c = rel.shape
    return pl.pallas_call(
        functools.partial(_bias_kernel, delta=delta),
        grid_spec=pltpu.PrefetchScalarGridSpec(
            num_scalar_prefetch=1,
            grid=(r // tr,),
            in_specs=[pl.BlockSpec((tr, c), lambda i, tab: (i, 0))],
            out_specs=pl.BlockSpec((NSA_GROUPS, tr, NSA_HPG * c), lambda i, tab: (0, i, 0)),
        ),
        out_shape=jax.ShapeDtypeStruct((NSA_GROUPS, r, NSA_HPG * c), F32),
        compiler_params=_cparams(("arbitrary",)),
        name="t5_bias_table",
    )(rel_bias_flat, rel)


def _compress_kernel(x_ref, w1_ref, pos_ref, w2_ref, o_ref):
    ab = _dot(x_ref[...], w1_ref[...])
    nb = ab.shape[0]
    nxt = pltpu.roll(ab, nb - 1, axis=0)
    nxt = pltpu.roll(nxt, NSA_DH, axis=1)
    pre = (ab + nxt)[:, :NSA_DH] + pos_ref[...]
    hid = jax.nn.gelu(pre)
    o_ref[...] = _dot(hid.astype(BF16), w2_ref[...]).astype(o_ref.dtype)


def _compress(x16, w1ab, posterm, w2):
    _, b, g, nb, wd = x16.shape
    return pl.pallas_call(
        _compress_kernel,
        grid=(2, b, g),
        in_specs=[pl.BlockSpec((None, None, None, nb, wd), lambda s, bi, gi: (s, bi, gi, 0, 0)),
                  pl.BlockSpec((None, wd, 2 * NSA_DH), lambda s, bi, gi: (s, 0, 0)),
                  pl.BlockSpec((None, 1, NSA_DH), lambda s, bi, gi: (s, 0, 0)),
                  pl.BlockSpec((None, NSA_DH, NSA_DH), lambda s, bi, gi: (s, 0, 0))],
        out_specs=pl.BlockSpec((None, None, None, nb, NSA_DH), lambda s, bi, gi: (s, bi, gi, 0, 0)),
        out_shape=jax.ShapeDtypeStruct((2, b, g, nb, NSA_DH), BF16),
        compiler_params=_cparams(("arbitrary",) * 3),
        name="nsa_compress",
    )(x16, w1ab, posterm, w2)


def _posterm_kernel(p_ref, w_ref, o_ref):
    o_ref[...] = _dot(p_ref[...], w_ref[...])[0:1]


def _posterm(pos8, w1):
    return pl.pallas_call(
        _posterm_kernel,
        grid=(2,),
        in_specs=[pl.BlockSpec((None, 8, pos8.shape[2]), lambda s: (s, 0, 0)),
                  pl.BlockSpec((None, w1.shape[1], NSA_DH), lambda s: (s, 0, 0))],
        out_specs=pl.BlockSpec((None, 1, NSA_DH), lambda s: (s, 0, 0)),
        out_shape=jax.ShapeDtypeStruct((2, 1, NSA_DH), F32),
        name="nsa_posterm",
    )(pos8, w1)


def _nsa_kernel(q_ref, kc_ref, vc_ref, ks_ref, vs_ref, kw_ref, vw_ref, gate_ref,
                cb_ref, bt_ref, wb_ref, cov_ref, b31_ref, o_ref, s_ref, *, n_sel):
    i = pl.program_id(2)
    q0 = i * QB
    r4 = NSA_HPG * QB
    q = (q_ref[...].astype(F32) * (NSA_DH ** -0.5)).astype(BF16)

    cb = cb_ref[...]
    s_c = _dot(kc_ref[...], q) + cb
    m_c = jnp.max(s_c, axis=0, keepdims=True)
    p_c = jnp.where(cb > 0.5 * NEG_INF, jnp.exp(s_c - m_c), 0.0)
    l_c = jnp.sum(p_c, axis=0, keepdims=True)
    p_c = p_c / jnp.maximum(l_c, 1e-30)
    o_c = _dot(vc_ref[...], p_c.astype(BF16))

    psum = p_c[:, 0:QB] + p_c[:, QB:2 * QB] + p_c[:, 2 * QB:3 * QB] + p_c[:, 3 * QB:4 * QB]
    p_hi = psum.astype(BF16)
    p_lo = (psum - p_hi.astype(F32)).astype(BF16)
    cov = cov_ref[...]
    imp_t = _dot(cov, p_hi) + _dot(cov, p_lo)
    ns = imp_t.shape[0]
    jidx = lax.broadcasted_iota(I32, (ns, QB), 0)
    t = q0 + lax.broadcasted_iota(I32, (ns, QB), 1)
    cur = jnp.right_shift(t, 6)
    eligible = jidx <= cur
    forced = (jidx == 0) | (jidx == cur) | (jidx == cur - 1)
    score = jnp.where(eligible, imp_t + jnp.where(forced, FORCE_SCORE, 0.0), -1.0)
    sel = jnp.zeros((ns, QB), F32)
    for _ in range(n_sel):
        mx = jnp.max(score, axis=0, keepdims=True)
        first = jnp.min(jnp.where(score == mx, jidx, ns), axis=0, keepdims=True)
        pick = jidx == first
        sel = jnp.where(pick, 1.0, sel)
        score = jnp.where(pick, -2.0, score)
    notsel = jnp.where(eligible, 1.0 - sel, 1.0).astype(BF16)

    depth = ks_ref.shape[-1]
    pad_rows = depth - ns - NSA_DH - AUG_ROWS
    qa = jnp.concatenate([jnp.concatenate([notsel] * NSA_HPG, axis=1), q, b31_ref[...],
                          jnp.zeros((pad_rows, r4), BF16)], axis=0)
    sub = SEL_TK // QB
    n_tiles = i // sub + 1
    n_far = jnp.maximum(i - N_OFF + 2, 0) // sub

    def put_scores(jt, slot):
        k0 = pl.multiple_of(jt * SEL_TK, SEL_TK)
        s_ref[slot] = _dot(ks_ref[pl.ds(k0, SEL_TK), :], qa)

    def absorb(jt, slot, carry, near):
        m_prev, acc = carry
        s = s_ref[slot]
        if near:
            offs = [jnp.clip(i - sub * jt - u, 0, N_OFF - 1) for u in range(sub)]
            s = s + jnp.concatenate([bt_ref[o] for o in offs], axis=0)
        m_new = jnp.maximum(m_prev, jnp.max(s, axis=0, keepdims=True))
        alpha = jnp.exp(m_prev - m_new)
        p = jnp.exp(s - m_new).astype(BF16)
        return m_new, alpha * acc + _dot(vs_ref[jt], p)

    def pair_body(pp, carry, near):
        ja = 2 * pp
        put_scores(ja + 1, 1)
        carry = absorb(ja, 0, carry, near)
        put_scores(jnp.minimum(ja + 2, n_tiles - 1), 0)
        return absorb(ja + 1, 1, carry, near)

    n_pairs = n_tiles // 2
    far_pairs = n_far // 2
    put_scores(0, 0)
    carry = (jnp.full((1, r4), -1e38, F32), jnp.zeros((NSA_DH + AUG_ROWS, r4), F32))
    carry = lax.fori_loop(0, far_pairs, functools.partial(pair_body, near=False), carry)
    carry = lax.fori_loop(far_pairs, n_pairs, functools.partial(pair_body, near=True), carry)
    _, acc_s = lax.cond(n_tiles % 2 == 1, lambda c: absorb(n_tiles - 1, 0, c, True), lambda c: c, carry)
    o_s = acc_s[0:NSA_DH] / acc_s[NSA_DH:NSA_DH + 1]

    wk = WINDOW + QB
    kwin = kw_ref[pl.ds(pl.multiple_of(q0, QB), wk), :]
    s_w = _dot(kwin, q) + wb_ref[...]
    kpos = q0 - WINDOW + lax.broadcasted_iota(I32, (wk, r4), 0)
    s_w = jnp.where(kpos >= 0, s_w, NEG_INF)
    m_w = jnp.max(s_w, axis=0, keepdims=True)
    p_w = jnp.exp(s_w - m_w).astype(BF16)
    acc_w = jnp.zeros((NSA_DH + AUG_ROWS, r4), F32)
    for u in range(wk // QB):
        acc_w = acc_w + _dot(vw_ref[i + u], p_w[u * QB:(u + 1) * QB])
    o_w = acc_w[0:NSA_DH] / acc_w[NSA_DH:NSA_DH + 1]

    gates = jax.nn.sigmoid(gate_ref[...])
    o = gates[0:1] * o_c + gates[1:2] * o_s + gates[2:3] * o_w
    o_ref[...] = o.astype(o_ref.dtype)


def _nsa_attention(qt, kc, vct, ksa, vst, kw, vwt, gates, cb, bt, wb, cov_t, b31, n_sel):
    b, g, nqb, dh, r4 = qt.shape
    nc = kc.shape[2]
    per_bg = lambda a: pl.BlockSpec((None, None) + a.shape[2:], lambda bi, gi, i: (bi, gi) + (0,) * (a.ndim - 2))
    per_g = lambda a: pl.BlockSpec((None,) + a.shape[1:], lambda bi, gi, i: (gi,) + (0,) * (a.ndim - 1))
    per_step = lambda a: pl.BlockSpec((None, None, None) + a.shape[3:],
                                      lambda bi, gi, i: (bi, gi, i) + (0,) * (a.ndim - 3))
    return pl.pallas_call(
        functools.partial(_nsa_kernel, n_sel=n_sel),
        grid=(b, g, nqb),
        in_specs=[
            per_step(qt), per_bg(kc), per_bg(vct), per_bg(ksa), per_bg(vst), per_bg(kw), per_bg(vwt),
            per_step(gates),
            pl.BlockSpec((None, None, nc, r4), lambda bi, gi, i: (gi, i, 0, 0)),
            per_g(bt), per_g(wb),
            pl.BlockSpec(cov_t.shape, lambda bi, gi, i: (0, 0)),
            per_g(b31),
        ],
        out_specs=per_step(qt),
        out_shape=jax.ShapeDtypeStruct(qt.shape, BF16),
        scratch_shapes=[pltpu.VMEM((2, SEL_TK, r4), F32)],
        compiler_params=_cparams(("arbitrary",) * 3),
        name="nsa_attention",
    )(qt, kc, vct, ksa, vst, kw, vwt, gates, cb, bt, wb, cov_t, b31)


def _ret_kernel(q_ref, k_ref, v_ref, g_ref, cos_ref, sin_ref, dm_ref, qd_ref, kd_ref, cd_ref,
                o_ref, state_ref):
    n = pl.program_id(1)

    @pl.when(n == 0)
    def _():
        state_ref[...] = jnp.zeros_like(state_ref)

    lane = lax.broadcasted_iota(I32, (RET_CHUNK, LANES), 1)
    first_half = (lane & (RET_DK - 1)) < (RET_DK // 2)
    cosv = cos_ref[...]
    sinv = sin_ref[...]

    def rot(x):
        partner = jnp.where(first_half, pltpu.roll(x, LANES - RET_DK // 2, axis=1),
                            pltpu.roll(x, RET_DK // 2, axis=1))
        return x * cosv + partner * sinv

    for hp in range(RET_HEADS // 2):
        ls = slice(hp * LANES, (hp + 1) * LANES)
        q2 = rot(q_ref[:, ls])
        k2 = rot(k_ref[:, ls]) * (RET_DK ** -0.5)
        qdec = (q2 * qd_ref[:, ls]).astype(BF16)
        kdec_t = (k2 * kd_ref[:, ls]).T.astype(BF16)
        q2b = q2.astype(BF16)
        k2b = k2.astype(BF16)
        for e in range(2):
            h = 2 * hp + e
            hs = slice(e * RET_DK, (e + 1) * RET_DK)
            vs = slice(h * RET_DV, (h + 1) * RET_DV)
            v = v_ref[:, vs]
            st = state_ref[h]
            sc = _dot_nt(q2b[:, hs], k2b[:, hs]) * dm_ref[h]
            y = _dot(sc.astype(BF16), v) + _dot(qdec[:, hs], st.astype(BF16))
            state_ref[h] = cd_ref[h] * st + _dot(kdec_t[hs, :], v)
            mu = jnp.mean(y, axis=-1, keepdims=True)
            var = jnp.mean(jnp.square(y - mu), axis=-1, keepdims=True)
            yn = (y - mu) * lax.rsqrt(var + GN_EPS)
            o_ref[:, vs] = (jax.nn.silu(g_ref[:, vs]) * yn).astype(o_ref.dtype)


def _retention(pf, pb, b, s, col_q, col_k, col_v, col_g, tabs):
    cosf, sinf, dm, qd, kd, cd = tabs
    n = s // RET_CHUNK
    t = b * s
    c = RET_CHUNK
    wqk = RET_HEADS * RET_DK
    wv = RET_HEADS * RET_DV
    row = lambda bi, ni: bi * n + ni
    full = lambda a: pl.BlockSpec(a.shape, lambda bi, ni: (0,) * a.ndim)
    return pl.pallas_call(
        _ret_kernel,
        grid=(b, n),
        in_specs=[
            pl.BlockSpec((c, wqk), lambda bi, ni: (row(bi, ni), col_q // wqk)),
            pl.BlockSpec((c, wqk), lambda bi, ni: (row(bi, ni), col_k // wqk)),
            pl.BlockSpec((c, wv), lambda bi, ni: (row(bi, ni), col_v // wv)),
            pl.BlockSpec((c, wv), lambda bi, ni: (row(bi, ni), col_g // wv)),
            pl.BlockSpec((c, LANES), lambda bi, ni: (ni, 0)),
            pl.BlockSpec((c, LANES), lambda bi, ni: (ni, 0)),
            full(dm), full(qd), full(kd), full(cd),
        ],
        out_specs=pl.BlockSpec((c, wv), lambda bi, ni: (row(bi, ni), 0)),
        out_shape=jax.ShapeDtypeStruct((t, wv), BF16),
        scratch_shapes=[pltpu.VMEM((RET_HEADS, RET_DK, RET_DV), F32)],
        compiler_params=_cparams(("arbitrary",) * 2),
        name="retention",
    )(pf, pf, pb, pf, cosf, sinf, dm, qd, kd, cd)


def _retention_tables(s):
    half = RET_DK // 2
    inv = ROPE_BASE ** (-jnp.arange(half, dtype=F32) * 2.0 / RET_DK)
    ang = jnp.arange(s, dtype=F32)[:, None] * inv[None, :]
    cos, sin = jnp.cos(ang), jnp.sin(ang)
    cosf = jnp.concatenate([cos, cos, cos, cos], axis=1)
    sinf = jnp.concatenate([-sin, sin, -sin, sin], axis=1)
    lg = jnp.log(1.0 - 2.0 ** (-5.0 - jnp.arange(RET_HEADS, dtype=F32)))
    i = jnp.arange(RET_CHUNK, dtype=F32)
    diff = i[:, None] - i[None, :]
    dm = jnp.where(diff >= 0, jnp.exp(jnp.maximum(diff, 0.0)[None] * lg[:, None, None]), 0.0)
    kdec = jnp.exp((RET_CHUNK - 1.0 - i)[None, :] * lg[:, None])
    qdec = jnp.exp((i + 1.0)[None, :] * lg[:, None])
    expand = lambda a: jnp.repeat(a.T, RET_DK, axis=1)
    cd = jnp.broadcast_to(jnp.exp(RET_CHUNK * lg)[:, None, None], (RET_HEADS, 1, LANES))
    return cosf, sinf, dm, expand(qdec), expand(kdec), cd


def _layer_norm(z, g, b):
    mu = jnp.mean(z, axis=-1, keepdims=True)
    var = jnp.mean(jnp.square(z - mu), axis=-1, keepdims=True)
    return (z - mu) * lax.rsqrt(var + LN_EPS) * g + b


def _merge_kernel(ya_ref, yb_ref, ma_ref, mb_ref, x_ref, wa_ref, wb_ref, wo_ref, g_ref, b_ref,
                  o_ref, *, alpha):
    merged = (jax.nn.sigmoid(ma_ref[...]) * _dot(ya_ref[...], wa_ref[...])
              + jax.nn.sigmoid(mb_ref[...]) * _dot(yb_ref[...], wb_ref[...]))
    z = alpha * x_ref[...] + _dot(merged.astype(BF16), wo_ref[...])
    o_ref[...] = _layer_norm(z, g_ref[...], b_ref[...])


def _merge(ya, yb, pf, col_ma, col_mb, x, wa, wb, wo, g, bb, alpha, tm):
    t, d = x.shape
    full = lambda a: pl.BlockSpec(a.shape, lambda i: (0,) * a.ndim)
    return pl.pallas_call(
        functools.partial(_merge_kernel, alpha=alpha),
        grid=(t // tm,),
        in_specs=[pl.BlockSpec((tm, ya.shape[1]), lambda i: (i, 0)),
                  pl.BlockSpec((tm, yb.shape[1]), lambda i: (i, 0)),
                  pl.BlockSpec((tm, d), lambda i: (i, col_ma // d)),
                  pl.BlockSpec((tm, d), lambda i: (i, col_mb // d)),
                  pl.BlockSpec((tm, d), lambda i: (i, 0)),
                  full(wa), full(wb), full(wo), full(g), full(bb)],
        out_specs=pl.BlockSpec((tm, d), lambda i: (i, 0)),
        out_shape=jax.ShapeDtypeStruct((t, d), F32),
        compiler_params=_cparams(("arbitrary",)),
        name="merge_ln1",
    )(ya, yb, pf, pf, x, wa, wb, wo, g, bb)


def _router_kernel(x_ref, wr_ref, rb_ref, tri_ref, eidx_ref, wts_ref, rank_ref, cnt_ref, carry_ref):
    step = pl.program_id(0)

    @pl.when(step == 0)
    def _():
        carry_ref[...] = jnp.zeros_like(carry_ref)

    logits = lax.dot_general(wr_ref[...], x_ref[...], (((1,), (1,)), ((), ())),
                             precision=lax.Precision.HIGHEST, preferred_element_type=F32)
    tm = logits.shape[1]
    mx = jnp.max(logits, axis=0, keepdims=True)
    ex = jnp.exp(logits - mx)
    probs = ex / jnp.sum(ex, axis=0, keepdims=True)
    sel = probs + rb_ref[...]
    rows = [sel[e:e + 1] for e in range(N_EXPERTS)]
    gscore = []
    for gi in range(N_GROUPS):
        r = rows[gi * EPG:(gi + 1) * EPG]
        best2 = None
        for a in range(EPG):
            for c in range(a + 1, EPG):
                pair = r[a] + r[c]
                best2 = pair if best2 is None else jnp.maximum(best2, pair)
        gscore.append(best2)
    best = jnp.zeros((1, tm), I32)
    bscore = gscore[0]
    for gi in range(1, N_GROUPS):
        better = gscore[gi] > bscore
        best = jnp.where(better, gi, best)
        bscore = jnp.where(better, gscore[gi], bscore)
    ing = []
    for a in range(EPG):
        v = rows[a]
        for gi in range(1, N_GROUPS):
            v = jnp.where(best == gi, rows[gi * EPG + a], v)
        ing.append(v)
    i1 = jnp.zeros((1, tm), I32)
    v1 = ing[0]
    for a in range(1, EPG):
        better = ing[a] > v1
        i1 = jnp.where(better, a, i1)
        v1 = jnp.where(better, ing[a], v1)
    i2 = jnp.full((1, tm), -1, I32)
    v2 = jnp.full((1, tm), -jnp.inf, F32)
    for a in range(EPG):
        better = (i1 != a) & (ing[a] > v2)
        i2 = jnp.where(better, a, i2)
        v2 = jnp.where(better, ing[a], v2)
    e0 = best * EPG + i1
    e1 = best * EPG + i2
    eio = lax.broadcasted_iota(I32, (N_EXPERTS, tm), 0)
    oh0 = (eio == e0).astype(F32)
    oh1 = (eio == e1).astype(F32)
    p0 = jnp.sum(oh0 * probs, axis=0, keepdims=True)
    p1 = jnp.sum(oh1 * probs, axis=0, keepdims=True)
    den = p0 + p1
    eidx_ref[0:1, :] = e0
    eidx_ref[1:2, :] = e1
    wts_ref[0:1, :] = p0 / den
    wts_ref[1:2, :] = p1 / den
    tot = oh0 + oh1
    prefix = _dot(tot.astype(BF16), tri_ref[...]) + carry_ref[:, 0:1]
    rank_ref[0:1, :] = jnp.sum(oh0 * prefix, axis=0, keepdims=True).astype(I32)
    rank_ref[1:2, :] = jnp.sum(oh1 * prefix, axis=0, keepdims=True).astype(I32)
    carry_ref[...] = carry_ref[...] + jnp.sum(tot, axis=1, keepdims=True)
    cnt_ref[...] = carry_ref[...]


def _router(x, wr_t, rb_col, tri, tm):
    t, d = x.shape
    out2 = lambda dt: jax.ShapeDtypeStruct((2, t), dt)
    row2 = pl.BlockSpec((2, tm), lambda i: (0, i))
    return pl.pallas_call(
        _router_kernel,
        grid=(t // tm,),
        in_specs=[pl.BlockSpec((tm, d), lambda i: (i, 0)),
                  pl.BlockSpec(wr_t.shape, lambda i: (0, 0)),
                  pl.BlockSpec(rb_col.shape, lambda i: (0, 0)),
                  pl.BlockSpec(tri.shape, lambda i: (0, 0))],
        out_specs=[row2, row2, row2, pl.BlockSpec((N_EXPERTS, LANES), lambda i: (0, 0))],
        out_shape=[out2(I32), out2(F32), out2(I32), jax.ShapeDtypeStruct((N_EXPERTS, LANES), F32)],
        scratch_shapes=[pltpu.VMEM((N_EXPERTS, LANES), F32)],
        compiler_params=_cparams(("arbitrary",)),
        name="moe_router",
    )(x, wr_t, rb_col, tri)


def _dispatch_kernel(dest_ref, zflag_ref, x_ref, xs_ref, zero_ref, sem, zsem, *, t_total):
    step = pl.program_id(0)
    tm = x_ref.shape[0]

    @pl.when(step == 0)
    def _():
        zero_ref[...] = jnp.zeros_like(zero_ref)

        def zero_block(bi, _):
            @pl.when(zflag_ref[bi] == 1)
            def _():
                r0 = pl.multiple_of(bi * MOE_BM, MOE_BM)
                cp = pltpu.make_async_copy(zero_ref, xs_ref.at[pl.ds(r0, MOE_BM)], zsem)
                cp.start()
                cp.wait()
            return 0

        lax.fori_loop(0, zflag_ref.shape[0], zero_block, 0)

    def copy(r, slot):
        dst = dest_ref[slot * t_total + step * tm + r]
        return pltpu.make_async_copy(x_ref.at[pl.ds(r, 1)], xs_ref.at[pl.ds(dst, 1)], sem)

    def issue(r, _):
        copy(r, 0).start()
        copy(r, 1).start()
        return 0

    def drain(r, _):
        copy(r, 0).wait()
        copy(r, 1).wait()
        return 0

    lax.fori_loop(0, tm, issue, 0, unroll=ROW_DMA_UNROLL)
    lax.fori_loop(0, tm, drain, 0, unroll=ROW_DMA_UNROLL)


def _dispatch(dest_flat, zflag, x, rows, tm):
    t, d = x.shape
    return pl.pallas_call(
        functools.partial(_dispatch_kernel, t_total=t),
        grid_spec=pltpu.PrefetchScalarGridSpec(
            num_scalar_prefetch=2,
            grid=(t // tm,),
            in_specs=[pl.BlockSpec((tm, d), lambda i, dr, zf: (i, 0))],
            out_specs=pl.BlockSpec(memory_space=pl.ANY),
            scratch_shapes=[pltpu.VMEM((MOE_BM, d), x.dtype), pltpu.SemaphoreType.DMA,
                            pltpu.SemaphoreType.DMA],
        ),
        out_shape=jax.ShapeDtypeStruct((rows, d), x.dtype),
        compiler_params=pltpu.CompilerParams(dimension_semantics=("arbitrary",), has_side_effects=True,
                                             vmem_limit_bytes=VMEM_LIMIT),
        name="moe_dispatch",
    )(dest_flat, zflag, x)


def _expert_kernel(be_ref, na_ref, x_ref, wg_ref, wu_ref, wd_ref, o_ref):
    active = pl.program_id(0) < na_ref[0]

    @pl.when(active)
    def _():
        xb = x_ref[...].astype(BF16)
        gate = _dot(xb, wg_ref[...])
        up = _dot(xb, wu_ref[...])
        hid = (jax.nn.silu(gate) * up).astype(BF16)
        o_ref[...] = _dot(hid, wd_ref[...])

    @pl.when(jnp.logical_not(active))
    def _():
        o_ref[...] = jnp.zeros_like(o_ref)


def _experts(blk_e, n_active, xs, wg, wu, wd):
    rows, d = xs.shape
    nb = rows // MOE_BM
    act = lambda i, be, na: jnp.minimum(i, na[0] - 1)
    return pl.pallas_call(
        _expert_kernel,
        grid_spec=pltpu.PrefetchScalarGridSpec(
            num_scalar_prefetch=2,
            grid=(nb,),
            in_specs=[pl.BlockSpec((MOE_BM, d), lambda i, be, na: (act(i, be, na), 0)),
                      pl.BlockSpec((None, d, D_EXPERT), lambda i, be, na: (be[i], 0, 0)),
                      pl.BlockSpec((None, d, D_EXPERT), lambda i, be, na: (be[i], 0, 0)),
                      pl.BlockSpec((None, D_EXPERT, d), lambda i, be, na: (be[i], 0, 0))],
            out_specs=pl.BlockSpec((MOE_BM, d), lambda i, be, na: (i, 0)),
        ),
        out_shape=jax.ShapeDtypeStruct((rows, d), F32),
        compiler_params=_cparams(("arbitrary",)),
        name="moe_experts",
    )(blk_e, n_active, xs, wg, wu, wd)


def _combine_kernel(dest_ref, x_ref, w_ref, g_ref, b_ref, ys_ref, o_ref, buf_ref, sem, *, alpha, t_total):
    step = pl.program_id(0)
    tm = x_ref.shape[0]

    def copy(r, slot):
        src = dest_ref[slot * t_total + step * tm + r]
        return pltpu.make_async_copy(ys_ref.at[pl.ds(src, 1)], buf_ref.at[slot, pl.ds(r, 1)], sem)

    def issue(r, _):
        copy(r, 0).start()
        copy(r, 1).start()
        return 0

    def drain(r, _):
        copy(r, 0).wait()
        copy(r, 1).wait()
        return 0

    lax.fori_loop(0, tm, issue, 0, unroll=ROW_DMA_UNROLL)
    lax.fori_loop(0, tm, drain, 0, unroll=ROW_DMA_UNROLL)
    w = w_ref[...]
    z = alpha * x_ref[...] + w[:, 0:1] * buf_ref[0] + w[:, 1:2] * buf_ref[1]
    o_ref[...] = _layer_norm(z, g_ref[...], b_ref[...])


def _combine(dest_flat, x, wts_t, g, bb, ys, alpha, tm):
    t, d = x.shape
    return pl.pallas_call(
        functools.partial(_combine_kernel, alpha=alpha, t_total=t),
        grid_spec=pltpu.PrefetchScalarGridSpec(
            num_scalar_prefetch=1,
            grid=(t // tm,),
            in_specs=[pl.BlockSpec((tm, d), lambda i, dr: (i, 0)),
                      pl.BlockSpec((tm, 2), lambda i, dr: (i, 0)),
                      pl.BlockSpec(g.shape, lambda i, dr: (0, 0)),
                      pl.BlockSpec(bb.shape, lambda i, dr: (0, 0)),
                      pl.BlockSpec(memory_space=pl.ANY)],
            out_specs=pl.BlockSpec((tm, d), lambda i, dr: (i, 0)),
            scratch_shapes=[pltpu.VMEM((2, tm, d), F32), pltpu.SemaphoreType.DMA],
        ),
        out_shape=jax.ShapeDtypeStruct((t, d), F32),
        compiler_params=_cparams(("arbitrary",)),
        name="moe_combine_ln2",
    )(dest_flat, x, wts_t, g, bb, ys)


def _attention_tables(rel_bias, s):
    nc = s // CMP_STRIDE
    nqb = s // QB
    iq = jnp.arange(QB, dtype=I32)[None, None, :]
    c = jnp.arange(nc, dtype=I32)[None, :, None]
    blk = jnp.arange(nqb, dtype=I32)[:, None, None]
    rel_c = blk * QB + iq - (c * CMP_STRIDE + CMP_LEN - 1)
    rel_c = jnp.where(c < nc - 1, rel_c, -1)
    jk = jnp.arange(QB, dtype=I32)[None, :, None]
    rel_t = jnp.arange(N_OFF, dtype=I32)[:, None, None] * QB + iq - jk
    jw = jnp.arange(WINDOW + QB, dtype=I32)[:, None]
    rel_w = iq[0] - jw + WINDOW
    rel_w = jnp.where(rel_w < WINDOW, rel_w, -1)
    flat = rel_bias.reshape(-1)
    r4 = NSA_HPG * QB
    cb = _bias_table(flat, jnp.maximum(rel_c, -1).reshape(nqb * nc, QB), nc).reshape(NSA_GROUPS, nqb, nc, r4)
    bt = _bias_table(flat, jnp.maximum(rel_t, -1).reshape(N_OFF * QB, QB), QB, delta=True)
    bt = bt.reshape(NSA_GROUPS, N_OFF, QB, r4)
    wb = _bias_table(flat, jnp.maximum(rel_w, -1), WINDOW + QB)
    sat = jnp.repeat(rel_bias[REL_BUCKETS - 1].reshape(NSA_GROUPS, 1, NSA_HPG), QB, axis=2)
    hi = sat.astype(BF16)
    lo = (sat - hi.astype(F32)).astype(BF16)
    b31 = jnp.concatenate([hi, lo, jnp.zeros((NSA_GROUPS, AUG_ROWS - 2, r4), BF16)], axis=1)
    return cb, bt, wb, b31


def _selection_tables(s):
    nc = s // CMP_STRIDE
    ns = s // SLC_BLOCK
    c_start = np.arange(nc) * CMP_STRIDE
    s_start = np.arange(ns) * SLC_BLOCK
    cover = np.maximum(np.minimum(c_start[:, None] + CMP_LEN, s_start[None, :] + SLC_BLOCK)
                       - np.maximum(c_start[:, None], s_start[None, :]), 0).astype(np.float32) / CMP_LEN
    cover[nc - 1] = 0.0
    depth = -(-(ns + NSA_DH + AUG_ROWS) // LANES) * LANES
    blockcols = np.where(np.arange(s)[:, None] // SLC_BLOCK == np.arange(ns)[None, :], -MASK_BIG, 0.0)
    tail = np.zeros((s, depth - ns - NSA_DH), np.float32)
    tail[:, 0:2] = 1.0
    return jnp.asarray(cover.T, BF16), jnp.asarray(blockcols, BF16), jnp.asarray(tail, BF16)


def _split_in_weights(w_in_l):
    o = np.cumsum((0, 512, 128, 128, 128, 128, 128, 128, 24, 512, 512, 1024, 1024, 1024, 1024))
    seg = lambda a: w_in_l[:, o[a]:o[a + 1]]
    wb = jnp.concatenate([seg(10), seg(0), seg(1), seg(2), seg(3), seg(4), seg(5), seg(6)], axis=1)
    gpad = jnp.pad(seg(7), ((0, 0), (0, LANES - 24)))
    wf = jnp.concatenate([seg(8), seg(9), seg(11), seg(12), seg(13), gpad], axis=1)
    return wb.astype(BF16), wf.astype(BF16)


def kernel(x, w_in, cmp_pos, cmp_w1, cmp_w2, w_br_a, w_br_b, w_o, ln1_g, ln1_b, w_router, router_bias,
           moe_w_gate, moe_w_up, moe_w_down, ln2_g, ln2_b, rel_bias):
    b, s, d = x.shape
    depth = w_in.shape[0]
    t = b * s
    alpha = (2 * depth) ** 0.25
    g, r, dh = NSA_GROUPS, NSA_HPG, NSA_DH
    nqb = s // QB
    nb16 = s // CMP_STRIDE
    n_sel = min(N_SELECT, s // SLC_BLOCK)

    cb, bt, wb, b31 = _attention_tables(rel_bias, s)
    cov_t, blockcols, tailcols = _selection_tables(s)
    ret_tabs = _retention_tables(s)
    tri = jnp.asarray(np.triu(np.ones((512, 512), np.float32), 1), BF16)
    wr_t = w_router.T
    rb_col = router_bias.reshape(N_EXPERTS, 1)
    n_blk = t * 2 // MOE_BM + N_EXPERTS
    rows = n_blk * MOE_BM

    xf = x.reshape(t, d)
    for l in range(depth):
        w_b, w_f = _split_in_weights(w_in[l])
        xb = xf.astype(BF16)
        pb = _matmul(xb, w_b, BF16, 1024, 1152)
        pf = _matmul(xb, w_f, F32, 1024, 1408)
        c_qa = RET_HEADS * RET_DV
        c_kv = c_qa + NSA_HEADS * dh

        def kv_heads(slab):
            col = c_kv + slab * g * dh
            return pb[:, col:col + g * dh].reshape(b, s, g, dh).transpose(0, 2, 1, 3)
        x16 = jnp.stack([kv_heads(0), kv_heads(1)]).reshape(2, b, g, nb16, CMP_STRIDE * dh)
        w1 = cmp_w1[l].astype(BF16)
        half = CMP_STRIDE * dh
        w1ab = jnp.concatenate([w1[:, :half], w1[:, half:]], axis=2)
        pos8 = jnp.broadcast_to(cmp_pos[l].reshape(2, 1, CMP_LEN * dh), (2, 8, CMP_LEN * dh)).astype(BF16)
        kvc = _compress(x16, w1ab, _posterm(pos8, w1), cmp_w2[l].astype(BF16))
        qt = (pb[:, c_qa:c_kv].reshape(b, nqb, QB, g, r, dh).transpose(0, 3, 1, 5, 4, 2)
              .reshape(b, g, nqb, dh, r * QB))
        gl = (pf[:, 4096:4096 + 24].reshape(b, nqb, QB, g, r, 3).transpose(0, 3, 1, 5, 4, 2)
              .reshape(b, g, nqb, 3, r * QB))

        def v_aug_t(v, width):
            length = v.shape[2]
            vt = v.transpose(0, 1, 3, 2)
            extra = jnp.zeros((b, g, AUG_ROWS, length), BF16).at[:, :, 0].set(1.0)
            va = jnp.concatenate([vt, extra], axis=2)
            return va.reshape(b, g, dh + AUG_ROWS, length // width, width).transpose(0, 1, 3, 2, 4)

        padw = lambda a: jnp.pad(a, ((0, 0), (0, 0), (WINDOW, 0), (0, 0)))
        bcast = lambda a: jnp.broadcast_to(a[None, None], (b, g) + a.shape)
        ksa = jnp.concatenate([bcast(blockcols), kv_heads(2), bcast(tailcols)], axis=3)
        ot = _nsa_attention(qt, kvc[0], kvc[1].transpose(0, 1, 3, 2), ksa, v_aug_t(kv_heads(3), SEL_TK),
                            padw(kv_heads(4)), v_aug_t(padw(kv_heads(5)), QB), gl, cb, bt, wb,
                            cov_t, b31, n_sel)
        ya = (ot.reshape(b, g, nqb, dh, r, QB).transpose(0, 2, 5, 1, 4, 3).reshape(t, g * r * dh))

        yb = _retention(pf, pb, b, s, 0, 512, 0, 1024, ret_tabs)

        x1 = _merge(ya, yb, pf, 2048, 3072, xf, w_br_a[l].astype(BF16), w_br_b[l].astype(BF16),
                    w_o[l].astype(BF16), ln1_g[l].reshape(1, d), ln1_b[l].reshape(1, d), alpha, 512)

        eidx, wts, rank, cnt = _router(x1, wr_t, rb_col, tri, 512)
        counts = cnt[:, 0].astype(I32)
        padded = ((counts + MOE_BM - 1) // MOE_BM) * MOE_BM
        pad_end = jnp.cumsum(padded)
        pad_start = pad_end - padded
        e_iota = jnp.arange(N_EXPERTS, dtype=I32)[:, None, None]
        start_of = jnp.sum(jnp.where(eidx[None] == e_iota, pad_start[:, None, None], 0), axis=0)
        dest = (start_of + rank).reshape(-1)
        n_active = (pad_end[-1] // MOE_BM).astype(I32).reshape(1)
        blk = jnp.minimum(jnp.arange(n_blk, dtype=I32), n_active[0] - 1) * MOE_BM
        blk_e = jnp.minimum(jnp.sum(pad_end[None, :] <= blk[:, None], axis=1), N_EXPERTS - 1).astype(I32)
        blk_all = jnp.arange(n_blk, dtype=I32)
        last_of_expert = jnp.any(pad_end[None, :] == (blk_all[:, None] + 1) * MOE_BM, axis=1)
        zflag = (last_of_expert | (blk_all >= n_active[0])).astype(I32)
        xs = _dispatch(dest, zflag, x1, rows, 512)
        ys = _experts(blk_e, n_active, xs, moe_w_gate[l].astype(BF16), moe_w_up[l].astype(BF16),
                      moe_w_down[l].astype(BF16))
        xf = _combine(dest, x1, wts.T, ln2_g[l].reshape(1, d), ln2_b[l].reshape(1, d), ys, alpha, 256)
    return xf.reshape(b, s, d)
```

```python
import functools
import math

import jax
import jax.numpy as jnp
import numpy as np
from jax import lax
from jax.experimental import pallas as pl
from jax.experimental.pallas import tpu as pltpu

F32 = jnp.float32
BF16 = jnp.bfloat16
I32 = jnp.int32

NSA_HEADS = 8
NSA_GROUPS = 2
NSA_HPG = 4
NSA_DH = 64
CMP_LEN = 32
CMP_STRIDE = 16
SLC_BLOCK = 64
N_SELECT = 16
WINDOW = 512
QB = 128
FORCE_SCORE = 1.0e4
RET_HEADS = 8
RET_DK = 64
RET_DV = 128
RET_CHUNK = 128
ROPE_BASE = 10000.0
REL_BUCKETS = 32
REL_MAX_EXACT = 16
REL_MAX_DIST = 2048
N_EXPERTS = 16
N_GROUPS = 4
EPG = 4
D_EXPERT = 512
LN_EPS = 1e-5
GN_EPS = 1e-5
NEG_INF = -1e30

LANES = 128
VMEM_LIMIT = 56 * 1024 * 1024

SEL_TK = 512
SEL_CW = 512
AUG_ROWS = 16
MASK_BIG = 2.0 ** 100
N_OFF = 14
MOE_BM = 256
ROW_DMA_UNROLL = 8


def _cparams(sem):
    return pltpu.CompilerParams(dimension_semantics=sem, vmem_limit_bytes=VMEM_LIMIT)


def _dot(a, b):
    return jnp.dot(a, b, preferred_element_type=F32)


def _dot_nt(a, b):
    return lax.dot_general(a, b, (((1,), (1,)), ((), ())), preferred_element_type=F32)


def _mm_kernel(a_ref, b_ref, o_ref):
    o_ref[...] = _dot(a_ref[...], b_ref[...]).astype(o_ref.dtype)


def _matmul(a, b, out_dtype, tm, tn):
    m, k = a.shape
    n = b.shape[1]
    return pl.pallas_call(
        _mm_kernel,
        grid=(n // tn, m // tm),
        in_specs=[pl.BlockSpec((tm, k), lambda j, i: (i, 0)),
                  pl.BlockSpec((k, tn), lambda j, i: (0, j))],
        out_specs=pl.BlockSpec((tm, tn), lambda j, i: (i, j)),
        out_shape=jax.ShapeDtypeStruct((m, n), out_dtype),
        compiler_params=_cparams(("arbitrary", "arbitrary")),
        name="proj_matmul",
    )(a, b)


def _bias_kernel(tab_ref, rel_ref, o_ref, *, delta):
    rel = rel_ref[...]
    n = jnp.maximum(rel, 0)
    nf = jnp.maximum(n, 1).astype(F32)
    large = REL_MAX_EXACT + (jnp.log(nf / REL_MAX_EXACT) / math.log(REL_MAX_DIST / REL_MAX_EXACT)
                             * (REL_BUCKETS - REL_MAX_EXACT)).astype(I32)
    large = jnp.minimum(large, REL_BUCKETS - 1)
    bucket = jnp.where(n < REL_MAX_EXACT, n, large)
    eqs = [bucket == k for k in range(REL_BUCKETS)]
    for h in range(NSA_HEADS):
        acc = jnp.full(rel.shape, NEG_INF, F32)
        for k in range(REL_BUCKETS):
            acc = jnp.where(eqs[k], tab_ref[k * NSA_HEADS + h], acc)
        if delta:
            acc = acc - tab_ref[(REL_BUCKETS - 1) * NSA_HEADS + h]
        hh = h % NSA_HPG
        o_ref[h // NSA_HPG, :, hh * QB:(hh + 1) * QB] = jnp.where(rel >= 0, acc, NEG_INF)


def _bias_table(rel_bias_flat, rel, tr, delta=False):
    r, c = rel.shape
    return pl.pallas_call(
        functools.partial(_bias_kernel, delta=delta),
        grid_spec=pltpu.PrefetchScalarGridSpec(
            num_scalar_prefetch=1,
            grid=(r // tr,),
            in_specs=[pl.BlockSpec((tr, c), lambda i, tab: (i, 0))],
            out_specs=pl.BlockSpec((NSA_GROUPS, tr, NSA_HPG * c), lambda i, tab: (0, i, 0)),
        ),
        out_shape=jax.ShapeDtypeStruct((NSA_GROUPS, r, NSA_HPG * c), F32),
        compiler_params=_cparams(("arbitrary",)),
        name="t5_bias_table",
    )(rel_bias_flat, rel)


def _compress_kernel(x_ref, w1_ref, pos_ref, w2_ref, o_ref):
    ab = _dot(x_ref[...], w1_ref[...])
    nb = ab.shape[0]
    nxt = pltpu.roll(ab, nb - 1, axis=0)
    nxt = pltpu.roll(nxt, NSA_DH, axis=1)
    pre = (ab + nxt)[:, :NSA_DH] + pos_ref[...]
    hid = jax.nn.gelu(pre)
    o_ref[...] = _dot(hid.astype(BF16), w2_ref[...]).astype(o_ref.dtype)


def _compress(x16, w1ab, posterm, w2):
    _, b, g, nb, wd = x16.shape
    return pl.pallas_call(
        _compress_kernel,
        grid=(2, b, g),
        in_specs=[pl.BlockSpec((None, None, None, nb, wd), lambda s, bi, gi: (s, bi, gi, 0, 0)),
                  pl.BlockSpec((None, wd, 2 * NSA_DH), lambda s, bi, gi: (s, 0, 0)),
                  pl.BlockSpec((None, 1, NSA_DH), lambda s, bi, gi: (s, 0, 0)),
                  pl.BlockSpec((None, NSA_DH, NSA_DH), lambda s, bi, gi: (s, 0, 0))],
        out_specs=pl.BlockSpec((None, None, None, nb, NSA_DH), lambda s, bi, gi: (s, bi, gi, 0, 0)),
        out_shape=jax.ShapeDtypeStruct((2, b, g, nb, NSA_DH), BF16),
        compiler_params=_cparams(("arbitrary",) * 3),
        name="nsa_compress",
    )(x16, w1ab, posterm, w2)


def _posterm_kernel(p_ref, w_ref, o_ref):
    o_ref[...] = _dot(p_ref[...], w_ref[...])[0:1]


def _posterm(pos8, w1):
    return pl.pallas_call(
        _posterm_kernel,
        grid=(2,),
        in_specs=[pl.BlockSpec((None, 8, pos8.shape[2]), lambda s: (s, 0, 0)),
                  pl.BlockSpec((None, w1.shape[1], NSA_DH), lambda s: (s, 0, 0))],
        out_specs=pl.BlockSpec((None, 1, NSA_DH), lambda s: (s, 0, 0)),
        out_shape=jax.ShapeDtypeStruct((2, 1, NSA_DH), F32),
        name="nsa_posterm",
    )(pos8, w1)


def _nsa_kernel(q_ref, kc_ref, vc_ref, ks_ref, vs_ref, kw_ref, vw_ref, gate_ref,
                cb_ref, bt_ref, wb_ref, cov_ref, b31_ref, o_ref, s_ref, *, n_sel):
    i = pl.program_id(2)
    q0 = i * QB
    r4 = NSA_HPG * QB
    q = (q_ref[...].astype(F32) * (NSA_DH ** -0.5)).astype(BF16)

    cb = cb_ref[...]
    s_c = _dot(kc_ref[...], q) + cb
    m_c = jnp.max(s_c, axis=0, keepdims=True)
    p_c = jnp.where(cb > 0.5 * NEG_INF, jnp.exp(s_c - m_c), 0.0)
    l_c = jnp.sum(p_c, axis=0, keepdims=True)
    p_c = p_c / jnp.maximum(l_c, 1e-30)
    o_c = _dot(vc_ref[...], p_c.astype(BF16))

    psum = p_c[:, 0:QB] + p_c[:, QB:2 * QB] + p_c[:, 2 * QB:3 * QB] + p_c[:, 3 * QB:4 * QB]
    p_hi = psum.astype(BF16)
    p_lo = (psum - p_hi.astype(F32)).astype(BF16)
    cov = cov_ref[...]
    imp_t = _dot(cov, p_hi) + _dot(cov, p_lo)
    ns = imp_t.shape[0]

    wk = WINDOW + QB
    kwin = kw_ref[pl.ds(pl.multiple_of(q0, QB), wk), :]
    s_w = _dot(kwin, q) + wb_ref[...]
    kpos = q0 - WINDOW + lax.broadcasted_iota(I32, (wk, r4), 0)
    s_w = jnp.where(kpos >= 0, s_w, NEG_INF)
    m_w = jnp.max(s_w, axis=0, keepdims=True)
    p_w = jnp.exp(s_w - m_w).astype(BF16)
    acc_w = jnp.zeros((NSA_DH + AUG_ROWS, r4), F32)
    for u in range(wk // QB):
        acc_w = acc_w + _dot(vw_ref[i + u], p_w[u * QB:(u + 1) * QB])
    o_w = acc_w[0:NSA_DH] / acc_w[NSA_DH:NSA_DH + 1]

    jidx = lax.broadcasted_iota(I32, (ns, QB), 0)
    t = q0 + lax.broadcasted_iota(I32, (ns, QB), 1)
    cur = jnp.right_shift(t, 6)
    eligible = jidx <= cur
    forced = (jidx == 0) | (jidx == cur) | (jidx == cur - 1)
    score = jnp.where(eligible, jnp.where(forced, -2.0, imp_t), -1.0)
    sel = jnp.where(forced, 1.0, 0.0)
    for _ in range(n_sel - 3):
        mx = jnp.max(score, axis=0, keepdims=True)
        first = jnp.min(jnp.where(score == mx, jidx, ns), axis=0, keepdims=True)
        pick = jidx == first
        sel = jnp.where(pick, 1.0, sel)
        score = jnp.where(pick, -2.0, score)
    notsel = jnp.where(eligible, 1.0 - sel, 1.0).astype(BF16)

    depth = ks_ref.shape[-1]
    pad_rows = depth - ns - NSA_DH - AUG_ROWS
    qa = jnp.concatenate([jnp.concatenate([notsel] * NSA_HPG, axis=1), q, b31_ref[...],
                          jnp.zeros((pad_rows, r4), BF16)], axis=0)
    sub = SEL_TK // QB
    n_tiles = i // sub + 1
    n_far = jnp.maximum(i - N_OFF + 2, 0) // sub

    def put_scores(jt, slot):
        k0 = pl.multiple_of(jt * SEL_TK, SEL_TK)
        s_ref[slot] = _dot(ks_ref[pl.ds(k0, SEL_TK), :], qa)

    def absorb(jt, slot, carry, near):
        m_prev, acc = carry
        s = s_ref[slot]
        if near:
            offs = [jnp.clip(i - sub * jt - u, 0, N_OFF - 1) for u in range(sub)]
            s = s + jnp.concatenate([bt_ref[o] for o in offs], axis=0)
        m_new = jnp.maximum(m_prev, jnp.max(s, axis=0, keepdims=True))
        alpha = jnp.exp(m_prev - m_new)
        p = jnp.exp(s - m_new).astype(BF16)
        return m_new, alpha * acc + _dot(vs_ref[jt], p)

    def pair_body(pp, carry, near):
        ja = 2 * pp
        put_scores(ja + 1, 1)
        carry = absorb(ja, 0, carry, near)
        put_scores(jnp.minimum(ja + 2, n_tiles - 1), 0)
        return absorb(ja + 1, 1, carry, near)

    n_pairs = n_tiles // 2
    far_pairs = n_far // 2
    put_scores(0, 0)
    carry = (jnp.full((1, r4), -1e38, F32), jnp.zeros((NSA_DH + AUG_ROWS, r4), F32))
    carry = lax.fori_loop(0, far_pairs, functools.partial(pair_body, near=False), carry)
    carry = lax.fori_loop(far_pairs, n_pairs, functools.partial(pair_body, near=True), carry)
    _, acc_s = lax.cond(n_tiles % 2 == 1, lambda c: absorb(n_tiles - 1, 0, c, True), lambda c: c, carry)
    o_s = acc_s[0:NSA_DH] / acc_s[NSA_DH:NSA_DH + 1]

    gates = jax.nn.sigmoid(gate_ref[...])
    o = gates[0:1] * o_c + gates[1:2] * o_s + gates[2:3] * o_w
    o_ref[...] = o.astype(o_ref.dtype)


def _nsa_attention(qt, kc, vct, ksa, vst, kw, vwt, gates, cb, bt, wb, cov_t, b31, n_sel):
    b, g, nqb, dh, r4 = qt.shape
    nc = kc.shape[2]
    per_bg = lambda a: pl.BlockSpec((None, None) + a.shape[2:], lambda bi, gi, i: (bi, gi) + (0,) * (a.ndim - 2))
    per_g = lambda a: pl.BlockSpec((None,) + a.shape[1:], lambda bi, gi, i: (gi,) + (0,) * (a.ndim - 1))
    per_step = lambda a: pl.BlockSpec((None, None, None) + a.shape[3:],
                                      lambda bi, gi, i: (bi, gi, i) + (0,) * (a.ndim - 3))
    return pl.pallas_call(
        functools.partial(_nsa_kernel, n_sel=n_sel),
        grid=(b, g, nqb),
        in_specs=[
            per_step(qt), per_bg(kc), per_bg(vct), per_bg(ksa), per_bg(vst), per_bg(kw), per_bg(vwt),
            per_step(gates),
            pl.BlockSpec((None, None, nc, r4), lambda bi, gi, i: (gi, i, 0, 0)),
            per_g(bt), per_g(wb),
            pl.BlockSpec(cov_t.shape, lambda bi, gi, i: (0, 0)),
            per_g(b31),
        ],
        out_specs=per_step(qt),
        out_shape=jax.ShapeDtypeStruct(qt.shape, BF16),
        scratch_shapes=[pltpu.VMEM((2, SEL_TK, r4), F32)],
        compiler_params=_cparams(("arbitrary",) * 3),
        name="nsa_attention",
    )(qt, kc, vct, ksa, vst, kw, vwt, gates, cb, bt, wb, cov_t, b31)


def _ret_kernel(q_ref, k_ref, v_ref, g_ref, cos_ref, sin_ref, dm_ref, qd_ref, kd_ref, cd_ref,
                o_ref, state_ref):
    n = pl.program_id(1)

    @pl.when(n == 0)
    def _():
        state_ref[...] = jnp.zeros_like(state_ref)

    lane = lax.broadcasted_iota(I32, (RET_CHUNK, LANES), 1)
    first_half = (lane & (RET_DK - 1)) < (RET_DK // 2)
    cosv = cos_ref[...]
    sinv = sin_ref[...]

    def rot(x):
        partner = jnp.where(first_half, pltpu.roll(x, LANES - RET_DK // 2, axis=1),
                            pltpu.roll(x, RET_DK // 2, axis=1))
        return x * cosv + partner * sinv

    for hp in range(RET_HEADS // 2):
        ls = slice(hp * LANES, (hp + 1) * LANES)
        q2 = rot(q_ref[:, ls])
        k2 = rot(k_ref[:, ls]) * (RET_DK ** -0.5)
        qdec = (q2 * qd_ref[:, ls]).astype(BF16)
        kdec_t = (k2 * kd_ref[:, ls]).T.astype(BF16)
        q2b = q2.astype(BF16)
        k2b = k2.astype(BF16)
        for e in range(2):
            h = 2 * hp + e
            hs = slice(e * RET_DK, (e + 1) * RET_DK)
            vs = slice(h * RET_DV, (h + 1) * RET_DV)
            v = v_ref[:, vs]
            st = state_ref[h]
            sc = _dot_nt(q2b[:, hs], k2b[:, hs]) * dm_ref[h]
            y = _dot(sc.astype(BF16), v) + _dot(qdec[:, hs], st.astype(BF16))
            state_ref[h] = cd_ref[h] * st + _dot(kdec_t[hs, :], v)
            mu = jnp.mean(y, axis=-1, keepdims=True)
            var = jnp.mean(jnp.square(y - mu), axis=-1, keepdims=True)
            yn = (y - mu) * lax.rsqrt(var + GN_EPS)
            o_ref[:, vs] = (jax.nn.silu(g_ref[:, vs]) * yn).astype(o_ref.dtype)


def _retention(pf, pb, b, s, col_q, col_k, col_v, col_g, tabs):
    cosf, sinf, dm, qd, kd, cd = tabs
    n = s // RET_CHUNK
    t = b * s
    c = RET_CHUNK
    wqk = RET_HEADS * RET_DK
    wv = RET_HEADS * RET_DV
    row = lambda bi, ni: bi * n + ni
    full = lambda a: pl.BlockSpec(a.shape, lambda bi, ni: (0,) * a.ndim)
    return pl.pallas_call(
        _ret_kernel,
        grid=(b, n),
        in_specs=[
            pl.BlockSpec((c, wqk), lambda bi, ni: (row(bi, ni), col_q // wqk)),
            pl.BlockSpec((c, wqk), lambda bi, ni: (row(bi, ni), col_k // wqk)),
            pl.BlockSpec((c, wv), lambda bi, ni: (row(bi, ni), col_v // wv)),
            pl.BlockSpec((c, wv), lambda bi, ni: (row(bi, ni), col_g // wv)),
            pl.BlockSpec((c, LANES), lambda bi, ni: (ni, 0)),
            pl.BlockSpec((c, LANES), lambda bi, ni: (ni, 0)),
            full(dm), full(qd), full(kd), full(cd),
        ],
        out_specs=pl.BlockSpec((c, wv), lambda bi, ni: (row(bi, ni), 0)),
        out_shape=jax.ShapeDtypeStruct((t, wv), BF16),
        scratch_shapes=[pltpu.VMEM((RET_HEADS, RET_DK, RET_DV), F32)],
        compiler_params=_cparams(("arbitrary",) * 2),
        name="retention",
    )(pf, pf, pb, pf, cosf, sinf, dm, qd, kd, cd)


def _retention_tables(s):
    half = RET_DK // 2
    inv = ROPE_BASE ** (-jnp.arange(half, dtype=F32) * 2.0 / RET_DK)
    ang = jnp.arange(s, dtype=F32)[:, None] * inv[None, :]
    cos, sin = jnp.cos(ang), jnp.sin(ang)
    cosf = jnp.concatenate([cos, cos, cos, cos], axis=1)
    sinf = jnp.concatenate([-sin, sin, -sin, sin], axis=1)
    lg = jnp.log(1.0 - 2.0 ** (-5.0 - jnp.arange(RET_HEADS, dtype=F32)))
    i = jnp.arange(RET_CHUNK, dtype=F32)
    diff = i[:, None] - i[None, :]
    dm = jnp.where(diff >= 0, jnp.exp(jnp.maximum(diff, 0.0)[None] * lg[:, None, None]), 0.0)
    kdec = jnp.exp((RET_CHUNK - 1.0 - i)[None, :] * lg[:, None])
    qdec = jnp.exp((i + 1.0)[None, :] * lg[:, None])
    expand = lambda a: jnp.repeat(a.T, RET_DK, axis=1)
    cd = jnp.broadcast_to(jnp.exp(RET_CHUNK * lg)[:, None, None], (RET_HEADS, 1, LANES))
    return cosf, sinf, dm, expand(qdec), expand(kdec), cd


def _layer_norm(z, g, b):
    mu = jnp.mean(z, axis=-1, keepdims=True)
    var = jnp.mean(jnp.square(z - mu), axis=-1, keepdims=True)
    return (z - mu) * lax.rsqrt(var + LN_EPS) * g + b


def _merge_kernel(ya_ref, yb_ref, ma_ref, mb_ref, x_ref, wa_ref, wb_ref, wo_ref, g_ref, b_ref,
                  o_ref, *, alpha):
    merged = (jax.nn.sigmoid(ma_ref[...]) * _dot(ya_ref[...], wa_ref[...])
              + jax.nn.sigmoid(mb_ref[...]) * _dot(yb_ref[...], wb_ref[...]))
    z = alpha * x_ref[...] + _dot(merged.astype(BF16), wo_ref[...])
    o_ref[...] = _layer_norm(z, g_ref[...], b_ref[...])


def _merge(ya, yb, pf, col_ma, col_mb, x, wa, wb, wo, g, bb, alpha, tm):
    t, d = x.shape
    full = lambda a: pl.BlockSpec(a.shape, lambda i: (0,) * a.ndim)
    return pl.pallas_call(
        functools.partial(_merge_kernel, alpha=alpha),
        grid=(t // tm,),
        in_specs=[pl.BlockSpec((tm, ya.shape[1]), lambda i: (i, 0)),
                  pl.BlockSpec((tm, yb.shape[1]), lambda i: (i, 0)),
                  pl.BlockSpec((tm, d), lambda i: (i, col_ma // d)),
                  pl.BlockSpec((tm, d), lambda i: (i, col_mb // d)),
                  pl.BlockSpec((tm, d), lambda i: (i, 0)),
                  full(wa), full(wb), full(wo), full(g), full(bb)],
        out_specs=pl.BlockSpec((tm, d), lambda i: (i, 0)),
        out_shape=jax.ShapeDtypeStruct((t, d), F32),
        compiler_params=_cparams(("arbitrary",)),
        name="merge_ln1",
    )(ya, yb, pf, pf, x, wa, wb, wo, g, bb)


def _router_kernel(x_ref, wr_ref, rb_ref, tri_ref, eidx_ref, wts_ref, rank_ref, cnt_ref, carry_ref):
    step = pl.program_id(0)

    @pl.when(step == 0)
    def _():
        carry_ref[...] = jnp.zeros_like(carry_ref)

    logits = lax.dot_general(wr_ref[...], x_ref[...], (((1,), (1,)), ((), ())),
                             precision=lax.Precision.HIGHEST, preferred_element_type=F32)
    tm = logits.shape[1]
    mx = jnp.max(logits, axis=0, keepdims=True)
    ex = jnp.exp(logits - mx)
    probs = ex / jnp.sum(ex, axis=0, keepdims=True)
    sel = probs + rb_ref[...]
    rows = [sel[e:e + 1] for e in range(N_EXPERTS)]
    gscore = []
    for gi in range(N_GROUPS):
        r = rows[gi * EPG:(gi + 1) * EPG]
        best2 = None
        for a in range(EPG):
            for c in range(a + 1, EPG):
                pair = r[a] + r[c]
                best2 = pair if best2 is None else jnp.maximum(best2, pair)
        gscore.append(best2)
    best = jnp.zeros((1, tm), I32)
    bscore = gscore[0]
    for gi in range(1, N_GROUPS):
        better = gscore[gi] > bscore
        best = jnp.where(better, gi, best)
        bscore = jnp.where(better, gscore[gi], bscore)
    ing = []
    for a in range(EPG):
        v = rows[a]
        for gi in range(1, N_GROUPS):
            v = jnp.where(best == gi, rows[gi * EPG + a], v)
        ing.append(v)
    i1 = jnp.zeros((1, tm), I32)
    v1 = ing[0]
    for a in range(1, EPG):
        better = ing[a] > v1
        i1 = jnp.where(better, a, i1)
        v1 = jnp.where(better, ing[a], v1)
    i2 = jnp.full((1, tm), -1, I32)
    v2 = jnp.full((1, tm), -jnp.inf, F32)
    for a in range(EPG):
        better = (i1 != a) & (ing[a] > v2)
        i2 = jnp.where(better, a, i2)
        v2 = jnp.where(better, ing[a], v2)
    e0 = best * EPG + i1
    e1 = best * EPG + i2
    eio = lax.broadcasted_iota(I32, (N_EXPERTS, tm), 0)
    oh0 = (eio == e0).astype(F32)
    oh1 = (eio == e1).astype(F32)
    p0 = jnp.sum(oh0 * probs, axis=0, keepdims=True)
    p1 = jnp.sum(oh1 * probs, axis=0, keepdims=True)
    den = p0 + p1
    eidx_ref[0:1, :] = e0
    eidx_ref[1:2, :] = e1
    wts_ref[0:1, :] = p0 / den
    wts_ref[1:2, :] = p1 / den
    tot = oh0 + oh1
    prefix = _dot(tot.astype(BF16), tri_ref[...]) + carry_ref[:, 0:1]
    rank_ref[0:1, :] = jnp.sum(oh0 * prefix, axis=0, keepdims=True).astype(I32)
    rank_ref[1:2, :] = jnp.sum(oh1 * prefix, axis=0, keepdims=True).astype(I32)
    carry_ref[...] = carry_ref[...] + jnp.sum(tot, axis=1, keepdims=True)
    cnt_ref[...] = carry_ref[...]


def _router(x, wr_t, rb_col, tri, tm):
    t, d = x.shape
    out2 = lambda dt: jax.ShapeDtypeStruct((2, t), dt)
    row2 = pl.BlockSpec((2, tm), lambda i: (0, i))
    return pl.pallas_call(
        _router_kernel,
        grid=(t // tm,),
        in_specs=[pl.BlockSpec((tm, d), lambda i: (i, 0)),
                  pl.BlockSpec(wr_t.shape, lambda i: (0, 0)),
                  pl.BlockSpec(rb_col.shape, lambda i: (0, 0)),
                  pl.BlockSpec(tri.shape, lambda i: (0, 0))],
        out_specs=[row2, row2, row2, pl.BlockSpec((N_EXPERTS, LANES), lambda i: (0, 0))],
        out_shape=[out2(I32), out2(F32), out2(I32), jax.ShapeDtypeStruct((N_EXPERTS, LANES), F32)],
        scratch_shapes=[pltpu.VMEM((N_EXPERTS, LANES), F32)],
        compiler_params=_cparams(("arbitrary",)),
        name="moe_router",
    )(x, wr_t, rb_col, tri)


def _dispatch_kernel(dest_ref, zflag_ref, x_ref, xs_ref, zero_ref, sem, zsem, *, t_total):
    step = pl.program_id(0)
    tm = x_ref.shape[0]

    @pl.when(step == 0)
    def _():
        zero_ref[...] = jnp.zeros_like(zero_ref)

        def zero_block(bi, _):
            @pl.when(zflag_ref[bi] == 1)
            def _():
                r0 = pl.multiple_of(bi * MOE_BM, MOE_BM)
                cp = pltpu.make_async_copy(zero_ref, xs_ref.at[pl.ds(r0, MOE_BM)], zsem)
                cp.start()
                cp.wait()
            return 0

        lax.fori_loop(0, zflag_ref.shape[0], zero_block, 0)

    def copy(r, slot):
        dst = dest_ref[slot * t_total + step * tm + r]
        return pltpu.make_async_copy(x_ref.at[pl.ds(r, 1)], xs_ref.at[pl.ds(dst, 1)], sem)

    def issue(r, _):
        copy(r, 0).start()
        copy(r, 1).start()
        return 0

    def drain(r, _):
        copy(r, 0).wait()
        copy(r, 1).wait()
        return 0

    lax.fori_loop(0, tm, issue, 0, unroll=ROW_DMA_UNROLL)
    lax.fori_loop(0, tm, drain, 0, unroll=ROW_DMA_UNROLL)


def _dispatch(dest_flat, zflag, x, rows, tm):
    t, d = x.shape
    return pl.pallas_call(
        functools.partial(_dispatch_kernel, t_total=t),
        grid_spec=pltpu.PrefetchScalarGridSpec(
            num_scalar_prefetch=2,
            grid=(t // tm,),
            in_specs=[pl.BlockSpec((tm, d), lambda i, dr, zf: (i, 0))],
            out_specs=pl.BlockSpec(memory_space=pl.ANY),
            scratch_shapes=[pltpu.VMEM((MOE_BM, d), x.dtype), pltpu.SemaphoreType.DMA,
                            pltpu.SemaphoreType.DMA],
        ),
        out_shape=jax.ShapeDtypeStruct((rows, d), x.dtype),
        compiler_params=pltpu.CompilerParams(dimension_semantics=("arbitrary",), has_side_effects=True,
                                             vmem_limit_bytes=VMEM_LIMIT),
        name="moe_dispatch",
    )(dest_flat, zflag, x)


def _expert_kernel(be_ref, na_ref, x_ref, wg_ref, wu_ref, wd_ref, o_ref):
    active = pl.program_id(0) < na_ref[0]

    @pl.when(active)
    def _():
        xb = x_ref[...].astype(BF16)
        gate = _dot(xb, wg_ref[...])
        up = _dot(xb, wu_ref[...])
        hid = (jax.nn.silu(gate) * up).astype(BF16)
        o_ref[...] = _dot(hid, wd_ref[...])

    @pl.when(jnp.logical_not(active))
    def _():
        o_ref[...] = jnp.zeros_like(o_ref)


def _experts(blk_e, n_active, xs, wg, wu, wd):
    rows, d = xs.shape
    nb = rows // MOE_BM
    act = lambda i, be, na: jnp.minimum(i, na[0] - 1)
    return pl.pallas_call(
        _expert_kernel,
        grid_spec=pltpu.PrefetchScalarGridSpec(
            num_scalar_prefetch=2,
            grid=(nb,),
            in_specs=[pl.BlockSpec((MOE_BM, d), lambda i, be, na: (act(i, be, na), 0)),
                      pl.BlockSpec((None, d, D_EXPERT), lambda i, be, na: (be[i], 0, 0)),
                      pl.BlockSpec((None, d, D_EXPERT), lambda i, be, na: (be[i], 0, 0)),
                      pl.BlockSpec((None, D_EXPERT, d), lambda i, be, na: (be[i], 0, 0))],
            out_specs=pl.BlockSpec((MOE_BM, d), lambda i, be, na: (i, 0)),
        ),
        out_shape=jax.ShapeDtypeStruct((rows, d), F32),
        compiler_params=_cparams(("arbitrary",)),
        name="moe_experts",
    )(blk_e, n_active, xs, wg, wu, wd)


def _combine_kernel(dest_ref, x_ref, w_ref, g_ref, b_ref, ys_ref, o_ref, buf_ref, sem, *, alpha, t_total):
    step = pl.program_id(0)
    tm = x_ref.shape[0]

    def copy(r, slot):
        src = dest_ref[slot * t_total + step * tm + r]
        return pltpu.make_async_copy(ys_ref.at[pl.ds(src, 1)], buf_ref.at[slot, pl.ds(r, 1)], sem)

    def issue(r, _):
        copy(r, 0).start()
        copy(r, 1).start()
        return 0

    def drain(r, _):
        copy(r, 0).wait()
        copy(r, 1).wait()
        return 0

    lax.fori_loop(0, tm, issue, 0, unroll=ROW_DMA_UNROLL)
    lax.fori_loop(0, tm, drain, 0, unroll=ROW_DMA_UNROLL)
    w = w_ref[...]
    z = alpha * x_ref[...] + w[:, 0:1] * buf_ref[0] + w[:, 1:2] * buf_ref[1]
    o_ref[...] = _layer_norm(z, g_ref[...], b_ref[...])


def _combine(dest_flat, x, wts_t, g, bb, ys, alpha, tm):
    t, d = x.shape
    return pl.pallas_call(
        functools.partial(_combine_kernel, alpha=alpha, t_total=t),
        grid_spec=pltpu.PrefetchScalarGridSpec(
            num_scalar_prefetch=1,
            grid=(t // tm,),
            in_specs=[pl.BlockSpec((tm, d), lambda i, dr: (i, 0)),
                      pl.BlockSpec((tm, 2), lambda i, dr: (i, 0)),
                      pl.BlockSpec(g.shape, lambda i, dr: (0, 0)),
                      pl.BlockSpec(bb.shape, lambda i, dr: (0, 0)),
                      pl.BlockSpec(memory_space=pl.ANY)],
            out_specs=pl.BlockSpec((tm, d), lambda i, dr: (i, 0)),
            scratch_shapes=[pltpu.VMEM((2, tm, d), F32), pltpu.SemaphoreType.DMA],
        ),
        out_shape=jax.ShapeDtypeStruct((t, d), F32),
        compiler_params=_cparams(("arbitrary",)),
        name="moe_combine_ln2",
    )(dest_flat, x, wts_t, g, bb, ys)


def _attention_tables(rel_bias, s):
    nc = s // CMP_STRIDE
    nqb = s // QB
    iq = jnp.arange(QB, dtype=I32)[None, None, :]
    c = jnp.arange(nc, dtype=I32)[None, :, None]
    blk = jnp.arange(nqb, dtype=I32)[:, None, None]
    rel_c = blk * QB + iq - (c * CMP_STRIDE + CMP_LEN - 1)
    rel_c = jnp.where(c < nc - 1, rel_c, -1)
    jk = jnp.arange(QB, dtype=I32)[None, :, None]
    rel_t = jnp.arange(N_OFF, dtype=I32)[:, None, None] * QB + iq - jk
    jw = jnp.arange(WINDOW + QB, dtype=I32)[:, None]
    rel_w = iq[0] - jw + WINDOW
    rel_w = jnp.where(rel_w < WINDOW, rel_w, -1)
    flat = rel_bias.reshape(-1)
    r4 = NSA_HPG * QB
    cb = _bias_table(flat, jnp.maximum(rel_c, -1).reshape(nqb * nc, QB), nc).reshape(NSA_GROUPS, nqb, nc, r4)
    bt = _bias_table(flat, jnp.maximum(rel_t, -1).reshape(N_OFF * QB, QB), QB, delta=True)
    bt = bt.reshape(NSA_GROUPS, N_OFF, QB, r4)
    wb = _bias_table(flat, jnp.maximum(rel_w, -1), WINDOW + QB)
    sat = jnp.repeat(rel_bias[REL_BUCKETS - 1].reshape(NSA_GROUPS, 1, NSA_HPG), QB, axis=2)
    hi = sat.astype(BF16)
    lo = (sat - hi.astype(F32)).astype(BF16)
    b31 = jnp.concatenate([hi, lo, jnp.zeros((NSA_GROUPS, AUG_ROWS - 2, r4), BF16)], axis=1)
    return cb, bt, wb, b31


def _selection_tables(s):
    nc = s // CMP_STRIDE
    ns = s // SLC_BLOCK
    c_start = np.arange(nc) * CMP_STRIDE
    s_start = np.arange(ns) * SLC_BLOCK
    cover = np.maximum(np.minimum(c_start[:, None] + CMP_LEN, s_start[None, :] + SLC_BLOCK)
                       - np.maximum(c_start[:, None], s_start[None, :]), 0).astype(np.float32) / CMP_LEN
    cover[nc - 1] = 0.0
    depth = -(-(ns + NSA_DH + AUG_ROWS) // LANES) * LANES
    blockcols = np.where(np.arange(s)[:, None] // SLC_BLOCK == np.arange(ns)[None, :], -MASK_BIG, 0.0)
    tail = np.zeros((s, depth - ns - NSA_DH), np.float32)
    tail[:, 0:2] = 1.0
    return jnp.asarray(cover.T, BF16), jnp.asarray(blockcols, BF16), jnp.asarray(tail, BF16)


def _split_in_weights(w_in_l):
    o = np.cumsum((0, 512, 128, 128, 128, 128, 128, 128, 24, 512, 512, 1024, 1024, 1024, 1024))
    seg = lambda a: w_in_l[:, o[a]:o[a + 1]]
    wb = jnp.concatenate([seg(10), seg(0), seg(1), seg(2), seg(3), seg(4), seg(5), seg(6)], axis=1)
    gpad = jnp.pad(seg(7), ((0, 0), (0, LANES - 24)))
    wf = jnp.concatenate([seg(8), seg(9), seg(11), seg(12), seg(13), gpad], axis=1)
    return wb.astype(BF16), wf.astype(BF16)


def kernel(x, w_in, cmp_pos, cmp_w1, cmp_w2, w_br_a, w_br_b, w_o, ln1_g, ln1_b, w_router, router_bias,
           moe_w_gate, moe_w_up, moe_w_down, ln2_g, ln2_b, rel_bias):
    b, s, d = x.shape
    depth = w_in.shape[0]
    t = b * s
    alpha = (2 * depth) ** 0.25
    g, r, dh = NSA_GROUPS, NSA_HPG, NSA_DH
    nqb = s // QB
    nb16 = s // CMP_STRIDE
    n_sel = min(N_SELECT, s // SLC_BLOCK)

    cb, bt, wb, b31 = _attention_tables(rel_bias, s)
    cov_t, blockcols, tailcols = _selection_tables(s)
    ret_tabs = _retention_tables(s)
    tri = jnp.asarray(np.triu(np.ones((512, 512), np.float32), 1), BF16)
    wr_t = w_router.T
    rb_col = router_bias.reshape(N_EXPERTS, 1)
    n_blk = t * 2 // MOE_BM + N_EXPERTS
    rows = n_blk * MOE_BM

    xf = x.reshape(t, d)
    for l in range(depth):
        w_b, w_f = _split_in_weights(w_in[l])
        xb = xf.astype(BF16)
        pb = _matmul(xb, w_b, BF16, 1024, 1152)
        pf = _matmul(xb, w_f, F32, 1024, 1408)
        c_qa = RET_HEADS * RET_DV
        c_kv = c_qa + NSA_HEADS * dh

        def kv_heads(slab):
            col = c_kv + slab * g * dh
            return pb[:, col:col + g * dh].reshape(b, s, g, dh).transpose(0, 2, 1, 3)
        x16 = jnp.stack([kv_heads(0), kv_heads(1)]).reshape(2, b, g, nb16, CMP_STRIDE * dh)
        w1 = cmp_w1[l].astype(BF16)
        half = CMP_STRIDE * dh
        w1ab = jnp.concatenate([w1[:, :half], w1[:, half:]], axis=2)
        pos8 = jnp.broadcast_to(cmp_pos[l].reshape(2, 1, CMP_LEN * dh), (2, 8, CMP_LEN * dh)).astype(BF16)
        kvc = _compress(x16, w1ab, _posterm(pos8, w1), cmp_w2[l].astype(BF16))
        qt = (pb[:, c_qa:c_kv].reshape(b, nqb, QB, g, r, dh).transpose(0, 3, 1, 5, 4, 2)
              .reshape(b, g, nqb, dh, r * QB))
        gl = (pf[:, 4096:4096 + 24].reshape(b, nqb, QB, g, r, 3).transpose(0, 3, 1, 5, 4, 2)
              .reshape(b, g, nqb, 3, r * QB))

        def v_aug_t(v, width):
            length = v.shape[2]
            vt = v.transpose(0, 1, 3, 2)
            extra = jnp.zeros((b, g, AUG_ROWS, length), BF16).at[:, :, 0].set(1.0)
            va = jnp.concatenate([vt, extra], axis=2)
            return va.reshape(b, g, dh + AUG_ROWS, length // width, width).transpose(0, 1, 3, 2, 4)

        padw = lambda a: jnp.pad(a, ((0, 0), (0, 0), (WINDOW, 0), (0, 0)))
        bcast = lambda a: jnp.broadcast_to(a[None, None], (b, g) + a.shape)
        ksa = jnp.concatenate([bcast(blockcols), kv_heads(2), bcast(tailcols)], axis=3)
        ot = _nsa_attention(qt, kvc[0], kvc[1].transpose(0, 1, 3, 2), ksa, v_aug_t(kv_heads(3), SEL_TK),
                            padw(kv_heads(4)), v_aug_t(padw(kv_heads(5)), QB), gl, cb, bt, wb,
                            cov_t, b31, n_sel)
        ya = (ot.reshape(b, g, nqb, dh, r, QB).transpose(0, 2, 5, 1, 4, 3).reshape(t, g * r * dh))

        yb = _retention(pf, pb, b, s, 0, 512, 0, 1024, ret_tabs)

        x1 = _merge(ya, yb, pf, 2048, 3072, xf, w_br_a[l].astype(BF16), w_br_b[l].astype(BF16),
                    w_o[l].astype(BF16), ln1_g[l].reshape(1, d), ln1_b[l].reshape(1, d), alpha, 512)

        eidx, wts, rank, cnt = _router(x1, wr_t, rb_col, tri, 512)
        counts = cnt[:, 0].astype(I32)
        padded = ((counts + MOE_BM - 1) // MOE_BM) * MOE_BM
        pad_end = jnp.cumsum(padded)
        pad_start = pad_end - padded
        e_iota = jnp.arange(N_EXPERTS, dtype=I32)[:, None, None]
        start_of = jnp.sum(jnp.where(eidx[None] == e_iota, pad_start[:, None, None], 0), axis=0)
        dest = (start_of + rank).reshape(-1)
        n_active = (pad_end[-1] // MOE_BM).astype(I32).reshape(1)
        blk = jnp.minimum(jnp.arange(n_blk, dtype=I32), n_active[0] - 1) * MOE_BM
        blk_e = jnp.minimum(jnp.sum(pad_end[None, :] <= blk[:, None], axis=1), N_EXPERTS - 1).astype(I32)
        blk_all = jnp.arange(n_blk, dtype=I32)
        last_of_expert = jnp.any(pad_end[None, :] == (blk_all[:, None] + 1) * MOE_BM, axis=1)
        zflag = (last_of_expert | (blk_all >= n_active[0])).astype(I32)
        xs = _dispatch(dest, zflag, x1, rows, 512)
        ys = _experts(blk_e, n_active, xs, moe_w_gate[l].astype(BF16), moe_w_up[l].astype(BF16),
                      moe_w_down[l].astype(BF16))
        xf = _combine(dest, x1, wts.T, ln2_g[l].reshape(1, d), ln2_b[l].reshape(1, d), ys, alpha, 256)
    return xf.reshape(b, s, d)
```

```python
import functools
import math

import jax
import jax.numpy as jnp
import numpy as np
from jax import lax
from jax.experimental import pallas as pl
from jax.experimental.pallas import tpu as pltpu

F32 = jnp.float32
BF16 = jnp.bfloat16
I32 = jnp.int32

NSA_HEADS = 8
NSA_GROUPS = 2
NSA_HPG = 4
NSA_DH = 64
CMP_LEN = 32
CMP_STRIDE = 16
SLC_BLOCK = 64
N_SELECT = 16
WINDOW = 512
QB = 128
FORCE_SCORE = 1.0e4
RET_HEADS = 8
RET_DK = 64
RET_DV = 128
RET_CHUNK = 128
ROPE_BASE = 10000.0
REL_BUCKETS = 32
REL_MAX_EXACT = 16
REL_MAX_DIST = 2048
N_EXPERTS = 16
N_GROUPS = 4
EPG = 4
D_EXPERT = 512
LN_EPS = 1e-5
GN_EPS = 1e-5
NEG_INF = -1e30

LANES = 128
VMEM_LIMIT = 56 * 1024 * 1024

SEL_TK = 512
SEL_CW = 512
AUG_ROWS = 16
MASK_BIG = 2.0 ** 100
N_OFF = 14
MOE_BM = 256
ROW_DMA_UNROLL = 8


def _cparams(sem):
    return pltpu.CompilerParams(dimension_semantics=sem, vmem_limit_bytes=VMEM_LIMIT)


def _dot(a, b):
    return jnp.dot(a, b, preferred_element_type=F32)


def _dot_nt(a, b):
    return lax.dot_general(a, b, (((1,), (1,)), ((), ())), preferred_element_type=F32)


def _mm_kernel(a_ref, b_ref, o_ref):
    o_ref[...] = _dot(a_ref[...], b_ref[...]).astype(o_ref.dtype)


def _matmul(a, b, out_dtype, tm, tn):
    m, k = a.shape
    n = b.shape[1]
    return pl.pallas_call(
        _mm_kernel,
        grid=(n // tn, m // tm),
        in_specs=[pl.BlockSpec((tm, k), lambda j, i: (i, 0)),
                  pl.BlockSpec((k, tn), lambda j, i: (0, j))],
        out_specs=pl.BlockSpec((tm, tn), lambda j, i: (i, j)),
        out_shape=jax.ShapeDtypeStruct((m, n), out_dtype),
        compiler_params=_cparams(("arbitrary", "arbitrary")),
        name="proj_matmul",
    )(a, b)


def _bias_kernel(tab_ref, rel_ref, o_ref, *, delta):
    rel = rel_ref[...]
    n = jnp.maximum(rel, 0)
    nf = jnp.maximum(n, 1).astype(F32)
    large = REL_MAX_EXACT + (jnp.log(nf / REL_MAX_EXACT) / math.log(REL_MAX_DIST / REL_MAX_EXACT)
                             * (REL_BUCKETS - REL_MAX_EXACT)).astype(I32)
    large = jnp.minimum(large, REL_BUCKETS - 1)
    bucket = jnp.where(n < REL_MAX_EXACT, n, large)
    eqs = [bucket == k for k in range(REL_BUCKETS)]
    for h in range(NSA_HEADS):
        acc = jnp.full(rel.shape, NEG_INF, F32)
        for k in range(REL_BUCKETS):
            acc = jnp.where(eqs[k], tab_ref[k * NSA_HEADS + h], acc)
        if delta:
            acc = acc - tab_ref[(REL_BUCKETS - 1) * NSA_HEADS + h]
        hh = h % NSA_HPG
        o_ref[h // NSA_HPG, :, hh * QB:(hh + 1) * QB] = jnp.where(rel >= 0, acc, NEG_INF)


def _bias_table(rel_bias_flat, rel, tr, delta=False):
    r, c = rel.shape
    return pl.pallas_call(
        functools.partial(_bias_kernel, delta=delta),
        grid_spec=pltpu.PrefetchScalarGridSpec(
            num_scalar_prefetch=1,
            grid=(r // tr,),
            in_specs=[pl.BlockSpec((tr, c), lambda i, tab: (i, 0))],
            out_specs=pl.BlockSpec((NSA_GROUPS, tr, NSA_HPG * c), lambda i, tab: (0, i, 0)),
        ),
        out_shape=jax.ShapeDtypeStruct((NSA_GROUPS, r, NSA_HPG * c), F32),
        compiler_params=_cparams(("arbitrary",)),
        name="t5_bias_table",
    )(rel_bias_flat, rel)


def _compress_kernel(x_ref, w1_ref, pos_ref, w2_ref, o_ref):
    ab = _dot(x_ref[...], w1_ref[...])
    nb = ab.shape[0]
    nxt = pltpu.roll(ab, nb - 1, axis=0)
    nxt = pltpu.roll(nxt, NSA_DH, axis=1)
    pre = (ab + nxt)[:, :NSA_DH] + pos_ref[...]
    hid = jax.nn.gelu(pre)
    o_ref[...] = _dot(hid.astype(BF16), w2_ref[...]).astype(o_ref.dtype)


def _compress(x16, w1ab, posterm, w2):
    _, b, g, nb, wd = x16.shape
    return pl.pallas_call(
        _compress_kernel,
        grid=(2, b, g),
        in_specs=[pl.BlockSpec((None, None, None, nb, wd), lambda s, bi, gi: (s, bi, gi, 0, 0)),
                  pl.BlockSpec((None, wd, 2 * NSA_DH), lambda s, bi, gi: (s, 0, 0)),
                  pl.BlockSpec((None, 1, NSA_DH), lambda s, bi, gi: (s, 0, 0)),
                  pl.BlockSpec((None, NSA_DH, NSA_DH), lambda s, bi, gi: (s, 0, 0))],
        out_specs=pl.BlockSpec((None, None, None, nb, NSA_DH), lambda s, bi, gi: (s, bi, gi, 0, 0)),
        out_shape=jax.ShapeDtypeStruct((2, b, g, nb, NSA_DH), BF16),
        compiler_params=_cparams(("arbitrary",) * 3),
        name="nsa_compress",
    )(x16, w1ab, posterm, w2)


def _posterm_kernel(p_ref, w_ref, o_ref):
    o_ref[...] = _dot(p_ref[...], w_ref[...])[0:1]


def _posterm(pos8, w1):
    return pl.pallas_call(
        _posterm_kernel,
        grid=(2,),
        in_specs=[pl.BlockSpec((None, 8, pos8.shape[2]), lambda s: (s, 0, 0)),
                  pl.BlockSpec((None, w1.shape[1], NSA_DH), lambda s: (s, 0, 0))],
        out_specs=pl.BlockSpec((None, 1, NSA_DH), lambda s: (s, 0, 0)),
        out_shape=jax.ShapeDtypeStruct((2, 1, NSA_DH), F32),
        name="nsa_posterm",
    )(pos8, w1)


def _eye(n, scale=1.0):
    rows = lax.broadcasted_iota(I32, (n, n), 0)
    cols = lax.broadcasted_iota(I32, (n, n), 1)
    return jnp.where(rows == cols, scale, 0.0).astype(BF16)


def _nsa_kernel(q_ref, kc_ref, vcin_ref, ksin_ref, vsin_ref, kwin_ref, vwin_ref, gate_ref,
                cb_ref, bt_ref, wb_ref, cov_ref, b31_ref, bcols_ref, o_ref,
                s_ref, ks_ref, vs_ref, kw_ref, vw_ref, vc_ref, *, n_sel):
    gi = pl.program_id(1)
    i = pl.program_id(2)
    q0 = i * QB
    r4 = NSA_HPG * QB
    ns = cov_ref.shape[0]
    eye_dh = _eye(NSA_DH)

    def group_lanes(x):
        return jnp.where(gi == 0, x[:, 0:NSA_DH], x[:, NSA_DH:2 * NSA_DH])

    @pl.when(i == 0)
    def _():
        s_len = ksin_ref.shape[0]
        ones_rows = jnp.where(lax.broadcasted_iota(I32, (AUG_ROWS, SEL_TK), 0) == 0, 1.0, 0.0).astype(BF16)
        tail = jnp.where(lax.broadcasted_iota(I32, (SEL_TK, NSA_DH), 1) < 2, 1.0, 0.0).astype(BF16)
        depth = ks_ref.shape[-1]
        vc_ref[...] = _dot_nt(eye_dh, vcin_ref[...]).astype(BF16)
        kw_ref[0:WINDOW, :] = jnp.zeros((WINDOW, NSA_DH), BF16)
        for u in range(WINDOW // QB):
            vw_ref[u] = jnp.zeros(vw_ref.shape[1:], BF16)

        def chunk(c, _):
            r0 = pl.multiple_of(c * SEL_TK, SEL_TK)
            rows = pl.ds(r0, SEL_TK)
            k_aug = jnp.concatenate([group_lanes(ksin_ref[rows, :]), tail], axis=1)
            ks_ref[rows, 0:ns] = bcols_ref[rows, :]
            ks_ref[rows, ns:ns + 2 * NSA_DH] = k_aug
            if depth > ns + 2 * NSA_DH:
                ks_ref[rows, ns + 2 * NSA_DH:depth] = jnp.zeros((SEL_TK, depth - ns - 2 * NSA_DH), BF16)
            v_t = _dot_nt(eye_dh, group_lanes(vsin_ref[rows, :])).astype(BF16)
            vs_ref[c] = jnp.concatenate([v_t, ones_rows], axis=0)
            kw_ref[pl.ds(WINDOW + r0, SEL_TK), :] = group_lanes(kwin_ref[rows, :])
            vw_t = _dot_nt(eye_dh, group_lanes(vwin_ref[rows, :])).astype(BF16)
            for u in range(SEL_TK // QB):
                vw_ref[WINDOW // QB + c * (SEL_TK // QB) + u] = jnp.concatenate(
                    [vw_t[:, u * QB:(u + 1) * QB], ones_rows[:, 0:QB]], axis=0)
            return 0

        lax.fori_loop(0, s_len // SEL_TK, chunk, 0)

    eye_q = _eye(NSA_DH, NSA_DH ** -0.5)
    qb = q_ref[...]
    q = jnp.concatenate([_dot_nt(eye_q, qb[:, h * NSA_DH:(h + 1) * NSA_DH]) for h in range(NSA_HPG)],
                        axis=1).astype(BF16)

    cb = cb_ref[...]
    s_c = _dot(kc_ref[...], q) + cb
    m_c = jnp.max(s_c, axis=0, keepdims=True)
    p_c = jnp.where(cb > 0.5 * NEG_INF, jnp.exp(s_c - m_c), 0.0)
    l_c = jnp.sum(p_c, axis=0, keepdims=True)
    p_c = p_c / jnp.maximum(l_c, 1e-30)
    o_c = _dot(vc_ref[...], p_c.astype(BF16))

    psum = p_c[:, 0:QB] + p_c[:, QB:2 * QB] + p_c[:, 2 * QB:3 * QB] + p_c[:, 3 * QB:4 * QB]
    p_hi = psum.astype(BF16)
    p_lo = (psum - p_hi.astype(F32)).astype(BF16)
    cov = cov_ref[...]
    imp_t = _dot(cov, p_hi) + _dot(cov, p_lo)
    ns = imp_t.shape[0]

    wk = WINDOW + QB
    kwin = kw_ref[pl.ds(pl.multiple_of(q0, QB), wk), :]
    s_w = _dot(kwin, q) + wb_ref[...]
    kpos = q0 - WINDOW + lax.broadcasted_iota(I32, (wk, r4), 0)
    s_w = jnp.where(kpos >= 0, s_w, NEG_INF)
    m_w = jnp.max(s_w, axis=0, keepdims=True)
    p_w = jnp.exp(s_w - m_w).astype(BF16)
    acc_w = jnp.zeros((NSA_DH + AUG_ROWS, r4), F32)
    for u in range(wk // QB):
        acc_w = acc_w + _dot(vw_ref[i + u], p_w[u * QB:(u + 1) * QB])
    o_w = acc_w[0:NSA_DH] / acc_w[NSA_DH:NSA_DH + 1]

    jidx = lax.broadcasted_iota(I32, (ns, QB), 0)
    t = q0 + lax.broadcasted_iota(I32, (ns, QB), 1)
    cur = jnp.right_shift(t, 6)
    eligible = jidx <= cur
    forced = (jidx == 0) | (jidx == cur) | (jidx == cur - 1)
    score = jnp.where(eligible, jnp.where(forced, -2.0, imp_t), -1.0)
    sel = jnp.where(forced, 1.0, 0.0)
    for _ in range(n_sel - 3):
        mx = jnp.max(score, axis=0, keepdims=True)
        first = jnp.min(jnp.where(score == mx, jidx, ns), axis=0, keepdims=True)
        pick = jidx == first
        sel = jnp.where(pick, 1.0, sel)
        score = jnp.where(pick, -2.0, score)
    notsel = jnp.where(eligible, 1.0 - sel, 1.0).astype(BF16)

    depth = ks_ref.shape[-1]
    pad_rows = depth - ns - NSA_DH - AUG_ROWS
    qa = jnp.concatenate([jnp.concatenate([notsel] * NSA_HPG, axis=1), q, b31_ref[...],
                          jnp.zeros((pad_rows, r4), BF16)], axis=0)
    sub = SEL_TK // QB
    n_tiles = i // sub + 1
    n_far = jnp.maximum(i - N_OFF + 2, 0) // sub

    def put_scores(jt, slot):
        k0 = pl.multiple_of(jt * SEL_TK, SEL_TK)
        s_ref[slot] = _dot(ks_ref[pl.ds(k0, SEL_TK), :], qa)

    def absorb(jt, slot, carry, near):
        m_prev, acc = carry
        s = s_ref[slot]
        if near:
            offs = [jnp.clip(i - sub * jt - u, 0, N_OFF - 1) for u in range(sub)]
            s = s + jnp.concatenate([bt_ref[o] for o in offs], axis=0)
        m_new = jnp.maximum(m_prev, jnp.max(s, axis=0, keepdims=True))
        alpha = jnp.exp(m_prev - m_new)
        p = jnp.exp(s - m_new).astype(BF16)
        return m_new, alpha * acc + _dot(vs_ref[jt], p)

    def pair_body(pp, carry, near):
        ja = 2 * pp
        put_scores(ja + 1, 1)
        carry = absorb(ja, 0, carry, near)
        put_scores(jnp.minimum(ja + 2, last_tile), 0)
        return absorb(ja + 1, 1, carry, near)

    n_pairs = (n_tiles + 1) // 2
    last_tile = 2 * n_pairs - 1
    far_pairs = n_far // 2
    put_scores(0, 0)
    carry = (jnp.full((1, r4), -1e38, F32), jnp.zeros((NSA_DH + AUG_ROWS, r4), F32))
    carry = lax.fori_loop(0, far_pairs, functools.partial(pair_body, near=False), carry)
    _, acc_s = lax.fori_loop(far_pairs, n_pairs, functools.partial(pair_body, near=True), carry)
    o_s = acc_s[0:NSA_DH] / acc_s[NSA_DH:NSA_DH + 1]

    gates = jax.nn.sigmoid(gate_ref[...])
    o = (gates[0:1] * o_c + gates[1:2] * o_s + gates[2:3] * o_w).astype(BF16)
    eye_qb = _eye(QB)
    o_ref[...] = jnp.concatenate([_dot_nt(eye_qb, o[:, h * QB:(h + 1) * QB]) for h in range(NSA_HPG)],
                                 axis=1).astype(o_ref.dtype)


def _nsa_attention(pb, c_qa, c_kv, kvc, gates, cb, bt, wb, cov_t, b31, blockcols, b, s, n_sel):
    g, dh = NSA_GROUPS, NSA_DH
    r4 = NSA_HPG * QB
    nqb = s // QB
    nc = kvc.shape[3]
    ns = cov_t.shape[0]
    depth = -(-(ns + 2 * dh) // LANES) * LANES
    n_tiles = s // SEL_TK
    assert n_tiles % 2 == 0, "the selected sweep consumes key tiles in pairs"
    qw = NSA_HPG * dh
    slab = lambda k: pl.BlockSpec((s, 2 * dh), lambda bi, gi, i: (bi, c_kv // (2 * dh) + k))
    per_g = lambda a: pl.BlockSpec((None,) + a.shape[1:], lambda bi, gi, i: (gi,) + (0,) * (a.ndim - 1))
    const = lambda a: pl.BlockSpec(a.shape, lambda bi, gi, i: (0,) * a.ndim)
    return pl.pallas_call(
        functools.partial(_nsa_kernel, n_sel=n_sel),
        grid=(b, g, nqb),
        in_specs=[
            pl.BlockSpec((QB, qw), lambda bi, gi, i: (bi * nqb + i, c_qa // qw + gi)),
            pl.BlockSpec((None, None, None, nc, dh), lambda bi, gi, i: (0, bi, gi, 0, 0)),
            pl.BlockSpec((None, None, None, nc, dh), lambda bi, gi, i: (1, bi, gi, 0, 0)),
            slab(2), slab(3), slab(4), slab(5),
            pl.BlockSpec((None, None, None, 3, r4), lambda bi, gi, i: (bi, gi, i, 0, 0)),
            pl.BlockSpec((None, None, nc, r4), lambda bi, gi, i: (gi, i, 0, 0)),
            per_g(bt), per_g(wb), const(cov_t), per_g(b31), const(blockcols),
        ],
        out_specs=pl.BlockSpec((QB, qw), lambda bi, gi, i: (bi * nqb + i, gi)),
        out_shape=jax.ShapeDtypeStruct((b * s, g * qw), BF16),
        scratch_shapes=[pltpu.VMEM((2, SEL_TK, r4), F32),
                        pltpu.VMEM((s, depth), BF16),
                        pltpu.VMEM((n_tiles, dh + AUG_ROWS, SEL_TK), BF16),
                        pltpu.VMEM((s + WINDOW, dh), BF16),
                        pltpu.VMEM(((s + WINDOW) // QB, dh + AUG_ROWS, QB), BF16),
                        pltpu.VMEM((dh, nc), BF16)],
        compiler_params=_cparams(("arbitrary",) * 3),
        name="nsa_attention",
    )(pb, kvc, kvc, pb, pb, pb, pb, gates, cb, bt, wb, cov_t, b31, blockcols)


def _ret_kernel(q_ref, k_ref, v_ref, g_ref, cos_ref, sin_ref, dm_ref, qd_ref, kd_ref, cd_ref,
                o_ref, state_ref):
    n = pl.program_id(1)

    @pl.when(n == 0)
    def _():
        state_ref[...] = jnp.zeros_like(state_ref)

    lane = lax.broadcasted_iota(I32, (RET_CHUNK, LANES), 1)
    first_half = (lane & (RET_DK - 1)) < (RET_DK // 2)
    cosv = cos_ref[...]
    sinv = sin_ref[...]

    def rot(x):
        partner = jnp.where(first_half, pltpu.roll(x, LANES - RET_DK // 2, axis=1),
                            pltpu.roll(x, RET_DK // 2, axis=1))
        return x * cosv + partner * sinv

    for hp in range(RET_HEADS // 2):
        ls = slice(hp * LANES, (hp + 1) * LANES)
        q2 = rot(q_ref[:, ls])
        k2 = rot(k_ref[:, ls]) * (RET_DK ** -0.5)
        qdec = (q2 * qd_ref[:, ls]).astype(BF16)
        kdec_t = (k2 * kd_ref[:, ls]).T.astype(BF16)
        q2b = q2.astype(BF16)
        k2b = k2.astype(BF16)
        for e in range(2):
            h = 2 * hp + e
            hs = slice(e * RET_DK, (e + 1) * RET_DK)
            vs = slice(h * RET_DV, (h + 1) * RET_DV)
            v = v_ref[:, vs]
            st = state_ref[h]
            sc = _dot_nt(q2b[:, hs], k2b[:, hs]) * dm_ref[h]
            y = _dot(sc.astype(BF16), v) + _dot(qdec[:, hs], st.astype(BF16))
            state_ref[h] = cd_ref[h] * st + _dot(kdec_t[hs, :], v)
            mu = jnp.mean(y, axis=-1, keepdims=True)
            var = jnp.mean(jnp.square(y - mu), axis=-1, keepdims=True)
            yn = (y - mu) * lax.rsqrt(var + GN_EPS)
            o_ref[:, vs] = (jax.nn.silu(g_ref[:, vs]) * yn).astype(o_ref.dtype)


def _retention(pf, pb, b, s, col_q, col_k, col_v, col_g, tabs):
    cosf, sinf, dm, qd, kd, cd = tabs
    n = s // RET_CHUNK
    t = b * s
    c = RET_CHUNK
    wqk = RET_HEADS * RET_DK
    wv = RET_HEADS * RET_DV
    row = lambda bi, ni: bi * n + ni
    full = lambda a: pl.BlockSpec(a.shape, lambda bi, ni: (0,) * a.ndim)
    return pl.pallas_call(
        _ret_kernel,
        grid=(b, n),
        in_specs=[
            pl.BlockSpec((c, wqk), lambda bi, ni: (row(bi, ni), col_q // wqk)),
            pl.BlockSpec((c, wqk), lambda bi, ni: (row(bi, ni), col_k // wqk)),
            pl.BlockSpec((c, wv), lambda bi, ni: (row(bi, ni), col_v // wv)),
            pl.BlockSpec((c, wv), lambda bi, ni: (row(bi, ni), col_g // wv)),
            pl.BlockSpec((c, LANES), lambda bi, ni: (ni, 0)),
            pl.BlockSpec((c, LANES), lambda bi, ni: (ni, 0)),
            full(dm), full(qd), full(kd), full(cd),
        ],
        out_specs=pl.BlockSpec((c, wv), lambda bi, ni: (row(bi, ni), 0)),
        out_shape=jax.ShapeDtypeStruct((t, wv), BF16),
        scratch_shapes=[pltpu.VMEM((RET_HEADS, RET_DK, RET_DV), F32)],
        compiler_params=_cparams(("arbitrary",) * 2),
        name="retention",
    )(pf, pf, pb, pf, cosf, sinf, dm, qd, kd, cd)


def _retention_tables(s):
    half = RET_DK // 2
    inv = ROPE_BASE ** (-jnp.arange(half, dtype=F32) * 2.0 / RET_DK)
    ang = jnp.arange(s, dtype=F32)[:, None] * inv[None, :]
    cos, sin = jnp.cos(ang), jnp.sin(ang)
    cosf = jnp.concatenate([cos, cos, cos, cos], axis=1)
    sinf = jnp.concatenate([-sin, sin, -sin, sin], axis=1)
    lg = jnp.log(1.0 - 2.0 ** (-5.0 - jnp.arange(RET_HEADS, dtype=F32)))
    i = jnp.arange(RET_CHUNK, dtype=F32)
    diff = i[:, None] - i[None, :]
    dm = jnp.where(diff >= 0, jnp.exp(jnp.maximum(diff, 0.0)[None] * lg[:, None, None]), 0.0)
    kdec = jnp.exp((RET_CHUNK - 1.0 - i)[None, :] * lg[:, None])
    qdec = jnp.exp((i + 1.0)[None, :] * lg[:, None])
    expand = lambda a: jnp.repeat(a.T, RET_DK, axis=1)
    cd = jnp.broadcast_to(jnp.exp(RET_CHUNK * lg)[:, None, None], (RET_HEADS, 1, LANES))
    return cosf, sinf, dm, expand(qdec), expand(kdec), cd


def _layer_norm(z, g, b):
    mu = jnp.mean(z, axis=-1, keepdims=True)
    var = jnp.mean(jnp.square(z - mu), axis=-1, keepdims=True)
    return (z - mu) * lax.rsqrt(var + LN_EPS) * g + b


def _merge_kernel(ya_ref, yb_ref, ma_ref, mb_ref, x_ref, wa_ref, wb_ref, wo_ref, g_ref, b_ref,
                  o_ref, *, alpha):
    merged = (jax.nn.sigmoid(ma_ref[...]) * _dot(ya_ref[...], wa_ref[...])
              + jax.nn.sigmoid(mb_ref[...]) * _dot(yb_ref[...], wb_ref[...]))
    z = alpha * x_ref[...] + _dot(merged.astype(BF16), wo_ref[...])
    o_ref[...] = _layer_norm(z, g_ref[...], b_ref[...])


def _merge(ya, yb, pf, col_ma, col_mb, x, wa, wb, wo, g, bb, alpha, tm):
    t, d = x.shape
    full = lambda a: pl.BlockSpec(a.shape, lambda i: (0,) * a.ndim)
    return pl.pallas_call(
        functools.partial(_merge_kernel, alpha=alpha),
        grid=(t // tm,),
        in_specs=[pl.BlockSpec((tm, ya.shape[1]), lambda i: (i, 0)),
                  pl.BlockSpec((tm, yb.shape[1]), lambda i: (i, 0)),
                  pl.BlockSpec((tm, d), lambda i: (i, col_ma // d)),
                  pl.BlockSpec((tm, d), lambda i: (i, col_mb // d)),
                  pl.BlockSpec((tm, d), lambda i: (i, 0)),
                  full(wa), full(wb), full(wo), full(g), full(bb)],
        out_specs=pl.BlockSpec((tm, d), lambda i: (i, 0)),
        out_shape=jax.ShapeDtypeStruct((t, d), F32),
        compiler_params=_cparams(("arbitrary",)),
        name="merge_ln1",
    )(ya, yb, pf, pf, x, wa, wb, wo, g, bb)


def _router_kernel(x_ref, wr_ref, rb_ref, tri_ref, eidx_ref, wts_ref, rank_ref, cnt_ref, carry_ref):
    step = pl.program_id(0)

    @pl.when(step == 0)
    def _():
        carry_ref[...] = jnp.zeros_like(carry_ref)

    logits = lax.dot_general(wr_ref[...], x_ref[...], (((1,), (1,)), ((), ())),
                             precision=lax.Precision.HIGHEST, preferred_element_type=F32)
    tm = logits.shape[1]
    mx = jnp.max(logits, axis=0, keepdims=True)
    ex = jnp.exp(logits - mx)
    probs = ex / jnp.sum(ex, axis=0, keepdims=True)
    sel = probs + rb_ref[...]
    rows = [sel[e:e + 1] for e in range(N_EXPERTS)]
    gscore = []
    for gi in range(N_GROUPS):
        r = rows[gi * EPG:(gi + 1) * EPG]
        best2 = None
        for a in range(EPG):
            for c in range(a + 1, EPG):
                pair = r[a] + r[c]
                best2 = pair if best2 is None else jnp.maximum(best2, pair)
        gscore.append(best2)
    best = jnp.zeros((1, tm), I32)
    bscore = gscore[0]
    for gi in range(1, N_GROUPS):
        better = gscore[gi] > bscore
        best = jnp.where(better, gi, best)
        bscore = jnp.where(better, gscore[gi], bscore)
    ing = []
    for a in range(EPG):
        v = rows[a]
        for gi in range(1, N_GROUPS):
            v = jnp.where(best == gi, rows[gi * EPG + a], v)
        ing.append(v)
    i1 = jnp.zeros((1, tm), I32)
    v1 = ing[0]
    for a in range(1, EPG):
        better = ing[a] > v1
        i1 = jnp.where(better, a, i1)
        v1 = jnp.where(better, ing[a], v1)
    i2 = jnp.full((1, tm), -1, I32)
    v2 = jnp.full((1, tm), -jnp.inf, F32)
    for a in range(EPG):
        better = (i1 != a) & (ing[a] > v2)
        i2 = jnp.where(better, a, i2)
        v2 = jnp.where(better, ing[a], v2)
    e0 = best * EPG + i1
    e1 = best * EPG + i2
    eio = lax.broadcasted_iota(I32, (N_EXPERTS, tm), 0)
    oh0 = (eio == e0).astype(F32)
    oh1 = (eio == e1).astype(F32)
    p0 = jnp.sum(oh0 * probs, axis=0, keepdims=True)
    p1 = jnp.sum(oh1 * probs, axis=0, keepdims=True)
    den = p0 + p1
    eidx_ref[0:1, :] = e0
    eidx_ref[1:2, :] = e1
    wts_ref[0:1, :] = p0 / den
    wts_ref[1:2, :] = p1 / den
    tot = oh0 + oh1
    prefix = _dot(tot.astype(BF16), tri_ref[...]) + carry_ref[:, 0:1]
    rank_ref[0:1, :] = jnp.sum(oh0 * prefix, axis=0, keepdims=True).astype(I32)
    rank_ref[1:2, :] = jnp.sum(oh1 * prefix, axis=0, keepdims=True).astype(I32)
    carry_ref[...] = carry_ref[...] + jnp.sum(tot, axis=1, keepdims=True)
    cnt_ref[...] = carry_ref[...]


def _router(x, wr_t, rb_col, tri, tm):
    t, d = x.shape
    out2 = lambda dt: jax.ShapeDtypeStruct((2, t), dt)
    row2 = pl.BlockSpec((2, tm), lambda i: (0, i))
    return pl.pallas_call(
        _router_kernel,
        grid=(t // tm,),
        in_specs=[pl.BlockSpec((tm, d), lambda i: (i, 0)),
                  pl.BlockSpec(wr_t.shape, lambda i: (0, 0)),
                  pl.BlockSpec(rb_col.shape, lambda i: (0, 0)),
                  pl.BlockSpec(tri.shape, lambda i: (0, 0))],
        out_specs=[row2, row2, row2, pl.BlockSpec((N_EXPERTS, LANES), lambda i: (0, 0))],
        out_shape=[out2(I32), out2(F32), out2(I32), jax.ShapeDtypeStruct((N_EXPERTS, LANES), F32)],
        scratch_shapes=[pltpu.VMEM((N_EXPERTS, LANES), F32)],
        compiler_params=_cparams(("arbitrary",)),
        name="moe_router",
    )(x, wr_t, rb_col, tri)


def _dispatch_kernel(dest_ref, zflag_ref, x_ref, xs_ref, zero_ref, sem, zsem, *, t_total):
    step = pl.program_id(0)
    tm = x_ref.shape[0]

    @pl.when(step == 0)
    def _():
        zero_ref[...] = jnp.zeros_like(zero_ref)

        def zero_block(bi, _):
            @pl.when(zflag_ref[bi] == 1)
            def _():
                r0 = pl.multiple_of(bi * MOE_BM, MOE_BM)
                cp = pltpu.make_async_copy(zero_ref, xs_ref.at[pl.ds(r0, MOE_BM)], zsem)
                cp.start()
                cp.wait()
            return 0

        lax.fori_loop(0, zflag_ref.shape[0], zero_block, 0)

    def copy(r, slot):
        dst = dest_ref[slot * t_total + step * tm + r]
        return pltpu.make_async_copy(x_ref.at[pl.ds(r, 1)], xs_ref.at[pl.ds(dst, 1)], sem)

    def issue(r, _):
        copy(r, 0).start()
        copy(r, 1).start()
        return 0

    def drain(r, _):
        copy(r, 0).wait()
        copy(r, 1).wait()
        return 0

    lax.fori_loop(0, tm, issue, 0, unroll=ROW_DMA_UNROLL)
    lax.fori_loop(0, tm, drain, 0, unroll=ROW_DMA_UNROLL)


def _dispatch(dest_flat, zflag, x, rows, tm):
    t, d = x.shape
    return pl.pallas_call(
        functools.partial(_dispatch_kernel, t_total=t),
        grid_spec=pltpu.PrefetchScalarGridSpec(
            num_scalar_prefetch=2,
            grid=(t // tm,),
            in_specs=[pl.BlockSpec((tm, d), lambda i, dr, zf: (i, 0))],
            out_specs=pl.BlockSpec(memory_space=pl.ANY),
            scratch_shapes=[pltpu.VMEM((MOE_BM, d), x.dtype), pltpu.SemaphoreType.DMA,
                            pltpu.SemaphoreType.DMA],
        ),
        out_shape=jax.ShapeDtypeStruct((rows, d), x.dtype),
        compiler_params=pltpu.CompilerParams(dimension_semantics=("arbitrary",), has_side_effects=True,
                                             vmem_limit_bytes=VMEM_LIMIT),
        name="moe_dispatch",
    )(dest_flat, zflag, x)


def _expert_kernel(be_ref, na_ref, x_ref, wg_ref, wu_ref, wd_ref, o_ref):
    active = pl.program_id(0) < na_ref[0]

    @pl.when(active)
    def _():
        xb = x_ref[...].astype(BF16)
        gate = _dot(xb, wg_ref[...])
        up = _dot(xb, wu_ref[...])
        hid = (jax.nn.silu(gate) * up).astype(BF16)
        o_ref[...] = _dot(hid, wd_ref[...])

    @pl.when(jnp.logical_not(active))
    def _():
        o_ref[...] = jnp.zeros_like(o_ref)


def _experts(blk_e, n_active, xs, wg, wu, wd):
    rows, d = xs.shape
    nb = rows // MOE_BM
    act = lambda i, be, na: jnp.minimum(i, na[0] - 1)
    return pl.pallas_call(
        _expert_kernel,
        grid_spec=pltpu.PrefetchScalarGridSpec(
            num_scalar_prefetch=2,
            grid=(nb,),
            in_specs=[pl.BlockSpec((MOE_BM, d), lambda i, be, na: (act(i, be, na), 0)),
                      pl.BlockSpec((None, d, D_EXPERT), lambda i, be, na: (be[i], 0, 0)),
                      pl.BlockSpec((None, d, D_EXPERT), lambda i, be, na: (be[i], 0, 0)),
                      pl.BlockSpec((None, D_EXPERT, d), lambda i, be, na: (be[i], 0, 0))],
            out_specs=pl.BlockSpec((MOE_BM, d), lambda i, be, na: (i, 0)),
        ),
        out_shape=jax.ShapeDtypeStruct((rows, d), F32),
        compiler_params=_cparams(("arbitrary",)),
        name="moe_experts",
    )(blk_e, n_active, xs, wg, wu, wd)


def _combine_kernel(dest_ref, x_ref, w_ref, g_ref, b_ref, ys_ref, o_ref, buf_ref, sem, *, alpha, t_total):
    step = pl.program_id(0)
    tm = x_ref.shape[0]

    def copy(r, slot):
        src = dest_ref[slot * t_total + step * tm + r]
        return pltpu.make_async_copy(ys_ref.at[pl.ds(src, 1)], buf_ref.at[slot, pl.ds(r, 1)], sem)

    def issue(r, _):
        copy(r, 0).start()
        copy(r, 1).start()
        return 0

    def drain(r, _):
        copy(r, 0).wait()
        copy(r, 1).wait()
        return 0

    lax.fori_loop(0, tm, issue, 0, unroll=ROW_DMA_UNROLL)
    lax.fori_loop(0, tm, drain, 0, unroll=ROW_DMA_UNROLL)
    w = w_ref[...]
    z = alpha * x_ref[...] + w[:, 0:1] * buf_ref[0] + w[:, 1:2] * buf_ref[1]
    o_ref[...] = _layer_norm(z, g_ref[...], b_ref[...])


def _combine(dest_flat, x, wts_t, g, bb, ys, alpha, tm):
    t, d = x.shape
    return pl.pallas_call(
        functools.partial(_combine_kernel, alpha=alpha, t_total=t),
        grid_spec=pltpu.PrefetchScalarGridSpec(
            num_scalar_prefetch=1,
            grid=(t // tm,),
            in_specs=[pl.BlockSpec((tm, d), lambda i, dr: (i, 0)),
                      pl.BlockSpec((tm, 2), lambda i, dr: (i, 0)),
                      pl.BlockSpec(g.shape, lambda i, dr: (0, 0)),
                      pl.BlockSpec(bb.shape, lambda i, dr: (0, 0)),
                      pl.BlockSpec(memory_space=pl.ANY)],
            out_specs=pl.BlockSpec((tm, d), lambda i, dr: (i, 0)),
            scratch_shapes=[pltpu.VMEM((2, tm, d), F32), pltpu.SemaphoreType.DMA],
        ),
        out_shape=jax.ShapeDtypeStruct((t, d), F32),
        compiler_params=_cparams(("arbitrary",)),
        name="moe_combine_ln2",
    )(dest_flat, x, wts_t, g, bb, ys)


def _attention_tables(rel_bias, s):
    nc = s // CMP_STRIDE
    nqb = s // QB
    iq = jnp.arange(QB, dtype=I32)[None, None, :]
    c = jnp.arange(nc, dtype=I32)[None, :, None]
    blk = jnp.arange(nqb, dtype=I32)[:, None, None]
    rel_c = blk * QB + iq - (c * CMP_STRIDE + CMP_LEN - 1)
    rel_c = jnp.where(c < nc - 1, rel_c, -1)
    jk = jnp.arange(QB, dtype=I32)[None, :, None]
    rel_t = jnp.arange(N_OFF, dtype=I32)[:, None, None] * QB + iq - jk
    jw = jnp.arange(WINDOW + QB, dtype=I32)[:, None]
    rel_w = iq[0] - jw + WINDOW
    rel_w = jnp.where(rel_w < WINDOW, rel_w, -1)
    flat = rel_bias.reshape(-1)
    r4 = NSA_HPG * QB
    cb = _bias_table(flat, jnp.maximum(rel_c, -1).reshape(nqb * nc, QB), nc).reshape(NSA_GROUPS, nqb, nc, r4)
    bt = _bias_table(flat, jnp.maximum(rel_t, -1).reshape(N_OFF * QB, QB), QB, delta=True)
    bt = bt.reshape(NSA_GROUPS, N_OFF, QB, r4)
    wb = _bias_table(flat, jnp.maximum(rel_w, -1), WINDOW + QB)
    sat = jnp.repeat(rel_bias[REL_BUCKETS - 1].reshape(NSA_GROUPS, 1, NSA_HPG), QB, axis=2)
    hi = sat.astype(BF16)
    lo = (sat - hi.astype(F32)).astype(BF16)
    b31 = jnp.concatenate([hi, lo, jnp.zeros((NSA_GROUPS, AUG_ROWS - 2, r4), BF16)], axis=1)
    return cb, bt, wb, b31


def _selection_tables(s):
    nc = s // CMP_STRIDE
    ns = s // SLC_BLOCK
    c_start = np.arange(nc) * CMP_STRIDE
    s_start = np.arange(ns) * SLC_BLOCK
    cover = np.maximum(np.minimum(c_start[:, None] + CMP_LEN, s_start[None, :] + SLC_BLOCK)
                       - np.maximum(c_start[:, None], s_start[None, :]), 0).astype(np.float32) / CMP_LEN
    cover[nc - 1] = 0.0
    blockcols = np.where(np.arange(s)[:, None] // SLC_BLOCK == np.arange(ns)[None, :], -MASK_BIG, 0.0)
    return jnp.asarray(cover.T, BF16), jnp.asarray(blockcols, BF16)


def _split_in_weights(w_in_l):
    o = np.cumsum((0, 512, 128, 128, 128, 128, 128, 128, 24, 512, 512, 1024, 1024, 1024, 1024))
    seg = lambda a: w_in_l[:, o[a]:o[a + 1]]
    wb = jnp.concatenate([seg(10), seg(0), seg(1), seg(2), seg(3), seg(4), seg(5), seg(6)], axis=1)
    gpad = jnp.pad(seg(7), ((0, 0), (0, LANES - 24)))
    wf = jnp.concatenate([seg(8), seg(9), seg(11), seg(12), seg(13), gpad], axis=1)
    return wb.astype(BF16), wf.astype(BF16)


def kernel(x, w_in, cmp_pos, cmp_w1, cmp_w2, w_br_a, w_br_b, w_o, ln1_g, ln1_b, w_router, router_bias,
           moe_w_gate, moe_w_up, moe_w_down, ln2_g, ln2_b, rel_bias):
    b, s, d = x.shape
    depth = w_in.shape[0]
    t = b * s
    alpha = (2 * depth) ** 0.25
    g, r, dh = NSA_GROUPS, NSA_HPG, NSA_DH
    nqb = s // QB
    nb16 = s // CMP_STRIDE
    n_sel = min(N_SELECT, s // SLC_BLOCK)

    cb, bt, wb, b31 = _attention_tables(rel_bias, s)
    cov_t, blockcols = _selection_tables(s)
    ret_tabs = _retention_tables(s)
    tri = jnp.asarray(np.triu(np.ones((512, 512), np.float32), 1), BF16)
    wr_t = w_router.T
    rb_col = router_bias.reshape(N_EXPERTS, 1)
    n_blk = t * 2 // MOE_BM + N_EXPERTS
    rows = n_blk * MOE_BM

    xf = x.reshape(t, d)
    for l in range(depth):
        w_b, w_f = _split_in_weights(w_in[l])
        xb = xf.astype(BF16)
        pb = _matmul(xb, w_b, BF16, 1024, 1152)
        pf = _matmul(xb, w_f, F32, 1024, 1408)
        c_qa = RET_HEADS * RET_DV
        c_kv = c_qa + NSA_HEADS * dh

        def kv_heads(slab):
            col = c_kv + slab * g * dh
            return pb[:, col:col + g * dh].reshape(b, s, g, dh).transpose(0, 2, 1, 3)
        x16 = jnp.stack([kv_heads(0), kv_heads(1)]).reshape(2, b, g, nb16, CMP_STRIDE * dh)
        w1 = cmp_w1[l].astype(BF16)
        half = CMP_STRIDE * dh
        w1ab = jnp.concatenate([w1[:, :half], w1[:, half:]], axis=2)
        pos8 = jnp.broadcast_to(cmp_pos[l].reshape(2, 1, CMP_LEN * dh), (2, 8, CMP_LEN * dh)).astype(BF16)
        kvc = _compress(x16, w1ab, _posterm(pos8, w1), cmp_w2[l].astype(BF16))
        gl = (pf[:, 4096:4096 + 24].reshape(b, nqb, QB, g, r, 3).transpose(0, 3, 1, 5, 4, 2)
              .reshape(b, g, nqb, 3, r * QB))
        ya = _nsa_attention(pb, c_qa, c_kv, kvc, gl, cb, bt, wb, cov_t, b31, blockcols, b, s, n_sel)

        yb = _retention(pf, pb, b, s, 0, 512, 0, 1024, ret_tabs)

        x1 = _merge(ya, yb, pf, 2048, 3072, xf, w_br_a[l].astype(BF16), w_br_b[l].astype(BF16),
                    w_o[l].astype(BF16), ln1_g[l].reshape(1, d), ln1_b[l].reshape(1, d), alpha, 512)

        eidx, wts, rank, cnt = _router(x1, wr_t, rb_col, tri, 512)
        counts = cnt[:, 0].astype(I32)
        padded = ((counts + MOE_BM - 1) // MOE_BM) * MOE_BM
        pad_end = jnp.cumsum(padded)
        pad_start = pad_end - padded
        e_iota = jnp.arange(N_EXPERTS, dtype=I32)[:, None, None]
        start_of = jnp.sum(jnp.where(eidx[None] == e_iota, pad_start[:, None, None], 0), axis=0)
        dest = (start_of + rank).reshape(-1)
        n_active = (pad_end[-1] // MOE_BM).astype(I32).reshape(1)
        blk = jnp.minimum(jnp.arange(n_blk, dtype=I32), n_active[0] - 1) * MOE_BM
        blk_e = jnp.minimum(jnp.sum(pad_end[None, :] <= blk[:, None], axis=1), N_EXPERTS - 1).astype(I32)
        blk_all = jnp.arange(n_blk, dtype=I32)
        last_of_expert = jnp.any(pad_end[None, :] == (blk_all[:, None] + 1) * MOE_BM, axis=1)
        zflag = (last_of_expert | (blk_all >= n_active[0])).astype(I32)
        xs = _dispatch(dest, zflag, x1, rows, 512)
        ys = _experts(blk_e, n_active, xs, moe_w_gate[l].astype(BF16), moe_w_up[l].astype(BF16),
                      moe_w_down[l].astype(BF16))
        xf = _combine(dest, x1, wts.T, ln2_g[l].reshape(1, d), ln2_b[l].reshape(1, d), ys, alpha, 256)
    return xf.reshape(b, s, d)
```

```python
import functools
import math

import jax
import jax.numpy as jnp
import numpy as np
from jax import lax
from jax.experimental import pallas as pl
from jax.experimental.pallas import tpu as pltpu

F32 = jnp.float32
BF16 = jnp.bfloat16
I32 = jnp.int32

NSA_HEADS = 8
NSA_GROUPS = 2
NSA_HPG = 4
NSA_DH = 64
CMP_LEN = 32
CMP_STRIDE = 16
SLC_BLOCK = 64
N_SELECT = 16
WINDOW = 512
QB = 128
FORCE_SCORE = 1.0e4
RET_HEADS = 8
RET_DK = 64
RET_DV = 128
RET_CHUNK = 128
ROPE_BASE = 10000.0
REL_BUCKETS = 32
REL_MAX_EXACT = 16
REL_MAX_DIST = 2048
N_EXPERTS = 16
N_GROUPS = 4
EPG = 4
D_EXPERT = 512
LN_EPS = 1e-5
GN_EPS = 1e-5
NEG_INF = -1e30

LANES = 128
VMEM_LIMIT = 56 * 1024 * 1024

SEL_TK = 512
SEL_CW = 512
AUG_ROWS = 16
MASK_BIG = 2.0 ** 100
N_OFF = 14
MOE_BM = 256
ROW_DMA_UNROLL = 8


def _cparams(sem):
    return pltpu.CompilerParams(dimension_semantics=sem, vmem_limit_bytes=VMEM_LIMIT)


def _dot(a, b):
    return jnp.dot(a, b, preferred_element_type=F32)


def _dot_nt(a, b):
    return lax.dot_general(a, b, (((1,), (1,)), ((), ())), preferred_element_type=F32)


def _mm_kernel(a_ref, b_ref, o_ref):
    o_ref[...] = _dot(a_ref[...], b_ref[...]).astype(o_ref.dtype)


def _matmul(a, b, out_dtype, tm, tn):
    m, k = a.shape
    n = b.shape[1]
    return pl.pallas_call(
        _mm_kernel,
        grid=(n // tn, m // tm),
        in_specs=[pl.BlockSpec((tm, k), lambda j, i: (i, 0)),
                  pl.BlockSpec((k, tn), lambda j, i: (0, j))],
        out_specs=pl.BlockSpec((tm, tn), lambda j, i: (i, j)),
        out_shape=jax.ShapeDtypeStruct((m, n), out_dtype),
        compiler_params=_cparams(("arbitrary", "arbitrary")),
        name="proj_matmul",
    )(a, b)


def _bias_kernel(tab_ref, rel_ref, o_ref, *, delta):
    rel = rel_ref[...]
    n = jnp.maximum(rel, 0)
    nf = jnp.maximum(n, 1).astype(F32)
    large = REL_MAX_EXACT + (jnp.log(nf / REL_MAX_EXACT) / math.log(REL_MAX_DIST / REL_MAX_EXACT)
                             * (REL_BUCKETS - REL_MAX_EXACT)).astype(I32)
    large = jnp.minimum(large, REL_BUCKETS - 1)
    bucket = jnp.where(n < REL_MAX_EXACT, n, large)
    eqs = [bucket == k for k in range(REL_BUCKETS)]
    for h in range(NSA_HEADS):
        acc = jnp.full(rel.shape, NEG_INF, F32)
        for k in range(REL_BUCKETS):
            acc = jnp.where(eqs[k], tab_ref[k * NSA_HEADS + h], acc)
        if delta:
            acc = acc - tab_ref[(REL_BUCKETS - 1) * NSA_HEADS + h]
        hh = h % NSA_HPG
        o_ref[h // NSA_HPG, :, hh * QB:(hh + 1) * QB] = jnp.where(rel >= 0, acc, NEG_INF)


def _bias_table(rel_bias_flat, rel, tr, delta=False):
    r, c = rel.shape
    return pl.pallas_call(
        functools.partial(_bias_kernel, delta=delta),
        grid_spec=pltpu.PrefetchScalarGridSpec(
            num_scalar_prefetch=1,
            grid=(r // tr,),
            in_specs=[pl.BlockSpec((tr, c), lambda i, tab: (i, 0))],
            out_specs=pl.BlockSpec((NSA_GROUPS, tr, NSA_HPG * c), lambda i, tab: (0, i, 0)),
        ),
        out_shape=jax.ShapeDtypeStruct((NSA_GROUPS, r, NSA_HPG * c), F32),
        compiler_params=_cparams(("arbitrary",)),
        name="t5_bias_table",
    )(rel_bias_flat, rel)


def _compress_kernel(x_ref, w1_ref, pos_ref, w2_ref, o_ref):
    ab = _dot(x_ref[...], w1_ref[...])
    nxt = pltpu.roll(ab, 1, axis=0)
    nxt = pltpu.roll(nxt, NSA_DH, axis=1)
    pre = (ab + nxt)[:, :NSA_DH] + pos_ref[...]
    hid = jax.nn.gelu(pre)
    o_ref[...] = _dot(hid.astype(BF16), w2_ref[...]).astype(o_ref.dtype)


def _compress(x16, w1ab, posterm, w2):
    _, b, g, nb, wd = x16.shape
    return pl.pallas_call(
        _compress_kernel,
        grid=(2, b, g),
        in_specs=[pl.BlockSpec((None, None, None, nb, wd), lambda s, bi, gi: (s, bi, gi, 0, 0)),
                  pl.BlockSpec((None, wd, 2 * NSA_DH), lambda s, bi, gi: (s, 0, 0)),
                  pl.BlockSpec((None, 1, NSA_DH), lambda s, bi, gi: (s, 0, 0)),
                  pl.BlockSpec((None, NSA_DH, NSA_DH), lambda s, bi, gi: (s, 0, 0))],
        out_specs=pl.BlockSpec((None, None, None, nb, NSA_DH), lambda s, bi, gi: (s, bi, gi, 0, 0)),
        out_shape=jax.ShapeDtypeStruct((2, b, g, nb, NSA_DH), BF16),
        compiler_params=_cparams(("arbitrary",) * 3),
        name="nsa_compress",
    )(x16, w1ab, posterm, w2)


def _posterm_kernel(p_ref, w_ref, o_ref):
    o_ref[...] = _dot(p_ref[...], w_ref[...])[0:1]


def _posterm(pos8, w1):
    return pl.pallas_call(
        _posterm_kernel,
        grid=(2,),
        in_specs=[pl.BlockSpec((None, 8, pos8.shape[2]), lambda s: (s, 0, 0)),
                  pl.BlockSpec((None, w1.shape[1], NSA_DH), lambda s: (s, 0, 0))],
        out_specs=pl.BlockSpec((None, 1, NSA_DH), lambda s: (s, 0, 0)),
        out_shape=jax.ShapeDtypeStruct((2, 1, NSA_DH), F32),
        name="nsa_posterm",
    )(pos8, w1)


def _eye(n, scale=1.0):
    rows = lax.broadcasted_iota(I32, (n, n), 0)
    cols = lax.broadcasted_iota(I32, (n, n), 1)
    return jnp.where(rows == cols, scale, 0.0).astype(BF16)


def _nsa_kernel(q_ref, kc_ref, vcin_ref, ksin_ref, vsin_ref, kwin_ref, vwin_ref, gate_ref,
                cb_ref, bt_ref, wb_ref, cov_ref, b31_ref, bcols_ref, o_ref,
                s_ref, ks_ref, vs_ref, kw_ref, vw_ref, vc_ref, *, n_sel):
    gi = pl.program_id(1)
    i = pl.program_id(2)
    q0 = i * QB
    r4 = NSA_HPG * QB
    ns = cov_ref.shape[0]
    eye_dh = _eye(NSA_DH)

    def group_lanes(x):
        return jnp.where(gi == 0, x[:, 0:NSA_DH], x[:, NSA_DH:2 * NSA_DH])

    @pl.when(i == 0)
    def _():
        s_len = ksin_ref.shape[0]
        ones_rows = jnp.where(lax.broadcasted_iota(I32, (AUG_ROWS, SEL_TK), 0) == 0, 1.0, 0.0).astype(BF16)
        tail = jnp.where(lax.broadcasted_iota(I32, (SEL_TK, NSA_DH), 1) < 2, 1.0, 0.0).astype(BF16)
        depth = ks_ref.shape[-1]
        vc_ref[...] = _dot_nt(eye_dh, vcin_ref[...]).astype(BF16)
        kw_ref[0:WINDOW, :] = jnp.zeros((WINDOW, NSA_DH), BF16)
        for u in range(WINDOW // QB):
            vw_ref[u] = jnp.zeros(vw_ref.shape[1:], BF16)

        def chunk(c, _):
            r0 = pl.multiple_of(c * SEL_TK, SEL_TK)
            rows = pl.ds(r0, SEL_TK)
            k_aug = jnp.concatenate([group_lanes(ksin_ref[rows, :]), tail], axis=1)
            ks_ref[rows, 0:ns] = bcols_ref[rows, :]
            ks_ref[rows, ns:ns + 2 * NSA_DH] = k_aug
            if depth > ns + 2 * NSA_DH:
                ks_ref[rows, ns + 2 * NSA_DH:depth] = jnp.zeros((SEL_TK, depth - ns - 2 * NSA_DH), BF16)
            v_t = _dot_nt(eye_dh, group_lanes(vsin_ref[rows, :])).astype(BF16)
            vs_ref[c] = jnp.concatenate([v_t, ones_rows], axis=0)
            kw_ref[pl.ds(WINDOW + r0, SEL_TK), :] = group_lanes(kwin_ref[rows, :])
            vw_t = _dot_nt(eye_dh, group_lanes(vwin_ref[rows, :])).astype(BF16)
            for u in range(SEL_TK // QB):
                vw_ref[WINDOW // QB + c * (SEL_TK // QB) + u] = jnp.concatenate(
                    [vw_t[:, u * QB:(u + 1) * QB], ones_rows[:, 0:QB]], axis=0)
            return 0

        lax.fori_loop(0, s_len // SEL_TK, chunk, 0)

    eye_q = _eye(NSA_DH, NSA_DH ** -0.5)
    qb = q_ref[...]
    q = jnp.concatenate([_dot_nt(eye_q, qb[:, h * NSA_DH:(h + 1) * NSA_DH]) for h in range(NSA_HPG)],
                        axis=1).astype(BF16)

    step_rows = QB // CMP_STRIDE
    cb = cb_ref[pl.ds(pl.multiple_of(i * step_rows, step_rows), kc_ref.shape[0]), :]
    s_c = _dot(kc_ref[...], q) + cb
    m_c = jnp.max(s_c, axis=0, keepdims=True)
    p_c = jnp.where(cb > 0.5 * NEG_INF, jnp.exp(s_c - m_c), 0.0)
    l_c = jnp.sum(p_c, axis=0, keepdims=True)
    p_c = p_c / jnp.maximum(l_c, 1e-30)
    o_c = _dot(vc_ref[...], p_c.astype(BF16))

    psum = p_c[:, 0:QB] + p_c[:, QB:2 * QB] + p_c[:, 2 * QB:3 * QB] + p_c[:, 3 * QB:4 * QB]
    p_hi = psum.astype(BF16)
    p_lo = (psum - p_hi.astype(F32)).astype(BF16)
    cov = cov_ref[...]
    imp_t = _dot(cov, p_hi) + _dot(cov, p_lo)
    ns = imp_t.shape[0]

    wk = WINDOW + QB
    kwin = kw_ref[pl.ds(pl.multiple_of(q0, QB), wk), :]
    s_w = _dot(kwin, q) + wb_ref[...]
    kpos = q0 - WINDOW + lax.broadcasted_iota(I32, (wk, r4), 0)
    s_w = jnp.where(kpos >= 0, s_w, NEG_INF)
    m_w = jnp.max(s_w, axis=0, keepdims=True)
    p_w = jnp.exp(s_w - m_w).astype(BF16)
    acc_w = jnp.zeros((NSA_DH + AUG_ROWS, r4), F32)
    for u in range(wk // QB):
        acc_w = acc_w + _dot(vw_ref[i + u], p_w[u * QB:(u + 1) * QB])
    o_w = acc_w[0:NSA_DH] / acc_w[NSA_DH:NSA_DH + 1]

    jidx = lax.broadcasted_iota(I32, (ns, QB), 0)
    t = q0 + lax.broadcasted_iota(I32, (ns, QB), 1)
    cur = jnp.right_shift(t, 6)
    eligible = jidx <= cur
    forced = (jidx == 0) | (jidx == cur) | (jidx == cur - 1)
    score = jnp.where(eligible, jnp.where(forced, -2.0, imp_t), -1.0)
    sel = jnp.where(forced, 1.0, 0.0)
    for _ in range(n_sel - 3):
        mx = jnp.max(score, axis=0, keepdims=True)
        first = jnp.min(jnp.where(score == mx, jidx, ns), axis=0, keepdims=True)
        pick = jidx == first
        sel = jnp.where(pick, 1.0, sel)
        score = jnp.where(pick, -2.0, score)
    notsel = jnp.where(eligible, 1.0 - sel, 1.0).astype(BF16)

    depth = ks_ref.shape[-1]
    pad_rows = depth - ns - NSA_DH - AUG_ROWS
    qa = jnp.concatenate([jnp.concatenate([notsel] * NSA_HPG, axis=1), q, b31_ref[...],
                          jnp.zeros((pad_rows, r4), BF16)], axis=0)
    sub = SEL_TK // QB
    n_tiles = i // sub + 1
    n_far = jnp.maximum(i - N_OFF + 2, 0) // sub

    def put_scores(jt, slot):
        k0 = pl.multiple_of(jt * SEL_TK, SEL_TK)
        s_ref[slot] = _dot(ks_ref[pl.ds(k0, SEL_TK), :], qa)

    def absorb(jt, slot, carry, near):
        m_prev, acc = carry
        s = s_ref[slot]
        if near:
            offs = [jnp.clip(i - sub * jt - u, 0, N_OFF - 1) for u in range(sub)]
            s = s + jnp.concatenate([bt_ref[o] for o in offs], axis=0)
        m_new = jnp.maximum(m_prev, jnp.max(s, axis=0, keepdims=True))
        alpha = jnp.exp(m_prev - m_new)
        p = jnp.exp(s - m_new).astype(BF16)
        return m_new, alpha * acc + _dot(vs_ref[jt], p)

    def pair_body(pp, carry, near):
        ja = 2 * pp
        put_scores(ja + 1, 1)
        carry = absorb(ja, 0, carry, near)
        put_scores(jnp.minimum(ja + 2, last_tile), 0)
        return absorb(ja + 1, 1, carry, near)

    n_pairs = (n_tiles + 1) // 2
    last_tile = 2 * n_pairs - 1
    far_pairs = n_far // 2
    put_scores(0, 0)
    carry = (jnp.full((1, r4), -1e38, F32), jnp.zeros((NSA_DH + AUG_ROWS, r4), F32))
    carry = lax.fori_loop(0, far_pairs, functools.partial(pair_body, near=False), carry)
    _, acc_s = lax.fori_loop(far_pairs, n_pairs, functools.partial(pair_body, near=True), carry)
    o_s = acc_s[0:NSA_DH] / acc_s[NSA_DH:NSA_DH + 1]

    gates = jax.nn.sigmoid(gate_ref[...])
    o = (gates[0:1] * o_c + gates[1:2] * o_s + gates[2:3] * o_w).astype(BF16)
    eye_qb = _eye(QB)
    o_ref[...] = jnp.concatenate([_dot_nt(eye_qb, o[:, h * QB:(h + 1) * QB]) for h in range(NSA_HPG)],
                                 axis=1).astype(o_ref.dtype)


def _nsa_attention(pb, c_qa, c_kv, kvc, gates, cb, bt, wb, cov_t, b31, blockcols, b, s, n_sel):
    g, dh = NSA_GROUPS, NSA_DH
    r4 = NSA_HPG * QB
    nqb = s // QB
    nc = kvc.shape[3]
    ns = cov_t.shape[0]
    depth = -(-(ns + 2 * dh) // LANES) * LANES
    n_tiles = s // SEL_TK
    assert n_tiles % 2 == 0, "the selected sweep consumes key tiles in pairs"
    qw = NSA_HPG * dh
    slab = lambda k: pl.BlockSpec((s, 2 * dh), lambda bi, gi, i: (bi, c_kv // (2 * dh) + k))
    per_g = lambda a: pl.BlockSpec((None,) + a.shape[1:], lambda bi, gi, i: (gi,) + (0,) * (a.ndim - 1))
    const = lambda a: pl.BlockSpec(a.shape, lambda bi, gi, i: (0,) * a.ndim)
    return pl.pallas_call(
        functools.partial(_nsa_kernel, n_sel=n_sel),
        grid=(b, g, nqb),
        in_specs=[
            pl.BlockSpec((QB, qw), lambda bi, gi, i: (bi * nqb + i, c_qa // qw + gi)),
            pl.BlockSpec((None, None, None, nc, dh), lambda bi, gi, i: (0, bi, gi, 0, 0)),
            pl.BlockSpec((None, None, None, nc, dh), lambda bi, gi, i: (1, bi, gi, 0, 0)),
            slab(2), slab(3), slab(4), slab(5),
            pl.BlockSpec((None, None, None, 3, r4), lambda bi, gi, i: (bi, gi, i, 0, 0)),
            per_g(cb), per_g(bt), per_g(wb), const(cov_t), per_g(b31), const(blockcols),
        ],
        out_specs=pl.BlockSpec((QB, qw), lambda bi, gi, i: (bi * nqb + i, gi)),
        out_shape=jax.ShapeDtypeStruct((b * s, g * qw), BF16),
        scratch_shapes=[pltpu.VMEM((2, SEL_TK, r4), F32),
                        pltpu.VMEM((s, depth), BF16),
                        pltpu.VMEM((n_tiles, dh + AUG_ROWS, SEL_TK), BF16),
                        pltpu.VMEM((s + WINDOW, dh), BF16),
                        pltpu.VMEM(((s + WINDOW) // QB, dh + AUG_ROWS, QB), BF16),
                        pltpu.VMEM((dh, nc), BF16)],
        compiler_params=_cparams(("arbitrary",) * 3),
        name="nsa_attention",
    )(pb, kvc, kvc, pb, pb, pb, pb, gates, cb, bt, wb, cov_t, b31, blockcols)


def _ret_kernel(q_ref, k_ref, v_ref, g_ref, cos_ref, sin_ref, dm_ref, qd_ref, kd_ref, cd_ref,
                o_ref, state_ref):
    n = pl.program_id(1)

    @pl.when(n == 0)
    def _():
        state_ref[...] = jnp.zeros_like(state_ref)

    lane = lax.broadcasted_iota(I32, (RET_CHUNK, LANES), 1)
    first_half = (lane & (RET_DK - 1)) < (RET_DK // 2)
    cosv = cos_ref[...]
    sinv = sin_ref[...]

    def rot(x):
        partner = jnp.where(first_half, pltpu.roll(x, LANES - RET_DK // 2, axis=1),
                            pltpu.roll(x, RET_DK // 2, axis=1))
        return x * cosv + partner * sinv

    for hp in range(RET_HEADS // 2):
        ls = slice(hp * LANES, (hp + 1) * LANES)
        q2 = rot(q_ref[:, ls])
        k2 = rot(k_ref[:, ls]) * (RET_DK ** -0.5)
        qdec = (q2 * qd_ref[:, ls]).astype(BF16)
        kdec_t = (k2 * kd_ref[:, ls]).T.astype(BF16)
        q2b = q2.astype(BF16)
        k2b = k2.astype(BF16)
        for e in range(2):
            h = 2 * hp + e
            hs = slice(e * RET_DK, (e + 1) * RET_DK)
            vs = slice(h * RET_DV, (h + 1) * RET_DV)
            v = v_ref[:, vs]
            st = state_ref[h]
            sc = _dot_nt(q2b[:, hs], k2b[:, hs]) * dm_ref[h]
            y = _dot(sc.astype(BF16), v) + _dot(qdec[:, hs], st.astype(BF16))
            state_ref[h] = cd_ref[h] * st + _dot(kdec_t[hs, :], v)
            mu = jnp.mean(y, axis=-1, keepdims=True)
            var = jnp.mean(jnp.square(y - mu), axis=-1, keepdims=True)
            yn = (y - mu) * lax.rsqrt(var + GN_EPS)
            o_ref[:, vs] = (jax.nn.silu(g_ref[:, vs]) * yn).astype(o_ref.dtype)


def _retention(pf, pb, b, s, col_q, col_k, col_v, col_g, tabs):
    cosf, sinf, dm, qd, kd, cd = tabs
    n = s // RET_CHUNK
    t = b * s
    c = RET_CHUNK
    wqk = RET_HEADS * RET_DK
    wv = RET_HEADS * RET_DV
    row = lambda bi, ni: bi * n + ni
    full = lambda a: pl.BlockSpec(a.shape, lambda bi, ni: (0,) * a.ndim)
    return pl.pallas_call(
        _ret_kernel,
        grid=(b, n),
        in_specs=[
            pl.BlockSpec((c, wqk), lambda bi, ni: (row(bi, ni), col_q // wqk)),
            pl.BlockSpec((c, wqk), lambda bi, ni: (row(bi, ni), col_k // wqk)),
            pl.BlockSpec((c, wv), lambda bi, ni: (row(bi, ni), col_v // wv)),
            pl.BlockSpec((c, wv), lambda bi, ni: (row(bi, ni), col_g // wv)),
            pl.BlockSpec((c, LANES), lambda bi, ni: (ni, 0)),
            pl.BlockSpec((c, LANES), lambda bi, ni: (ni, 0)),
            full(dm), full(qd), full(kd), full(cd),
        ],
        out_specs=pl.BlockSpec((c, wv), lambda bi, ni: (row(bi, ni), 0)),
        out_shape=jax.ShapeDtypeStruct((t, wv), BF16),
        scratch_shapes=[pltpu.VMEM((RET_HEADS, RET_DK, RET_DV), F32)],
        compiler_params=_cparams(("arbitrary",) * 2),
        name="retention",
    )(pf, pf, pb, pf, cosf, sinf, dm, qd, kd, cd)


def _retention_tables(s):
    half = RET_DK // 2
    inv = ROPE_BASE ** (-jnp.arange(half, dtype=F32) * 2.0 / RET_DK)
    ang = jnp.arange(s, dtype=F32)[:, None] * inv[None, :]
    cos, sin = jnp.cos(ang), jnp.sin(ang)
    cosf = jnp.concatenate([cos, cos, cos, cos], axis=1)
    sinf = jnp.concatenate([-sin, sin, -sin, sin], axis=1)
    lg = jnp.log(1.0 - 2.0 ** (-5.0 - jnp.arange(RET_HEADS, dtype=F32)))
    i = jnp.arange(RET_CHUNK, dtype=F32)
    diff = i[:, None] - i[None, :]
    dm = jnp.where(diff >= 0, jnp.exp(jnp.maximum(diff, 0.0)[None] * lg[:, None, None]), 0.0)
    kdec = jnp.exp((RET_CHUNK - 1.0 - i)[None, :] * lg[:, None])
    qdec = jnp.exp((i + 1.0)[None, :] * lg[:, None])
    expand = lambda a: jnp.repeat(a.T, RET_DK, axis=1)
    cd = jnp.broadcast_to(jnp.exp(RET_CHUNK * lg)[:, None, None], (RET_HEADS, 1, LANES))
    return cosf, sinf, dm, expand(qdec), expand(kdec), cd


def _layer_norm(z, g, b):
    mu = jnp.mean(z, axis=-1, keepdims=True)
    var = jnp.mean(jnp.square(z - mu), axis=-1, keepdims=True)
    return (z - mu) * lax.rsqrt(var + LN_EPS) * g + b


def _merge_kernel(ya_ref, yb_ref, ma_ref, mb_ref, x_ref, wa_ref, wb_ref, wo_ref, g_ref, b_ref,
                  o_ref, *, alpha):
    merged = (jax.nn.sigmoid(ma_ref[...]) * _dot(ya_ref[...], wa_ref[...])
              + jax.nn.sigmoid(mb_ref[...]) * _dot(yb_ref[...], wb_ref[...]))
    z = alpha * x_ref[...] + _dot(merged.astype(BF16), wo_ref[...])
    o_ref[...] = _layer_norm(z, g_ref[...], b_ref[...])


def _merge(ya, yb, pf, col_ma, col_mb, x, wa, wb, wo, g, bb, alpha, tm):
    t, d = x.shape
    full = lambda a: pl.BlockSpec(a.shape, lambda i: (0,) * a.ndim)
    return pl.pallas_call(
        functools.partial(_merge_kernel, alpha=alpha),
        grid=(t // tm,),
        in_specs=[pl.BlockSpec((tm, ya.shape[1]), lambda i: (i, 0)),
                  pl.BlockSpec((tm, yb.shape[1]), lambda i: (i, 0)),
                  pl.BlockSpec((tm, d), lambda i: (i, col_ma // d)),
                  pl.BlockSpec((tm, d), lambda i: (i, col_mb // d)),
                  pl.BlockSpec((tm, d), lambda i: (i, 0)),
                  full(wa), full(wb), full(wo), full(g), full(bb)],
        out_specs=pl.BlockSpec((tm, d), lambda i: (i, 0)),
        out_shape=jax.ShapeDtypeStruct((t, d), F32),
        compiler_params=_cparams(("arbitrary",)),
        name="merge_ln1",
    )(ya, yb, pf, pf, x, wa, wb, wo, g, bb)


def _router_kernel(x_ref, wr_ref, rb_ref, tri_ref, eidx_ref, wts_ref, rank_ref, cnt_ref, carry_ref):
    step = pl.program_id(0)

    @pl.when(step == 0)
    def _():
        carry_ref[...] = jnp.zeros_like(carry_ref)

    logits = lax.dot_general(wr_ref[...], x_ref[...], (((1,), (1,)), ((), ())),
                             precision=lax.Precision.HIGHEST, preferred_element_type=F32)
    tm = logits.shape[1]
    mx = jnp.max(logits, axis=0, keepdims=True)
    ex = jnp.exp(logits - mx)
    probs = ex / jnp.sum(ex, axis=0, keepdims=True)
    sel = probs + rb_ref[...]
    rows = [sel[e:e + 1] for e in range(N_EXPERTS)]
    gscore = []
    for gi in range(N_GROUPS):
        r = rows[gi * EPG:(gi + 1) * EPG]
        best2 = None
        for a in range(EPG):
            for c in range(a + 1, EPG):
                pair = r[a] + r[c]
                best2 = pair if best2 is None else jnp.maximum(best2, pair)
        gscore.append(best2)
    best = jnp.zeros((1, tm), I32)
    bscore = gscore[0]
    for gi in range(1, N_GROUPS):
        better = gscore[gi] > bscore
        best = jnp.where(better, gi, best)
        bscore = jnp.where(better, gscore[gi], bscore)
    ing = []
    for a in range(EPG):
        v = rows[a]
        for gi in range(1, N_GROUPS):
            v = jnp.where(best == gi, rows[gi * EPG + a], v)
        ing.append(v)
    i1 = jnp.zeros((1, tm), I32)
    v1 = ing[0]
    for a in range(1, EPG):
        better = ing[a] > v1
        i1 = jnp.where(better, a, i1)
        v1 = jnp.where(better, ing[a], v1)
    i2 = jnp.full((1, tm), -1, I32)
    v2 = jnp.full((1, tm), -jnp.inf, F32)
    for a in range(EPG):
        better = (i1 != a) & (ing[a] > v2)
        i2 = jnp.where(better, a, i2)
        v2 = jnp.where(better, ing[a], v2)
    e0 = best * EPG + i1
    e1 = best * EPG + i2
    eio = lax.broadcasted_iota(I32, (N_EXPERTS, tm), 0)
    oh0 = (eio == e0).astype(F32)
    oh1 = (eio == e1).astype(F32)
    p0 = jnp.sum(oh0 * probs, axis=0, keepdims=True)
    p1 = jnp.sum(oh1 * probs, axis=0, keepdims=True)
    den = p0 + p1
    eidx_ref[0:1, :] = e0
    eidx_ref[1:2, :] = e1
    wts_ref[0:1, :] = p0 / den
    wts_ref[1:2, :] = p1 / den
    tot = oh0 + oh1
    prefix = _dot(tot.astype(BF16), tri_ref[...]) + carry_ref[:, 0:1]
    rank_ref[0:1, :] = jnp.sum(oh0 * prefix, axis=0, keepdims=True).astype(I32)
    rank_ref[1:2, :] = jnp.sum(oh1 * prefix, axis=0, keepdims=True).astype(I32)
    carry_ref[...] = carry_ref[...] + jnp.sum(tot, axis=1, keepdims=True)
    cnt_ref[...] = carry_ref[...]


def _router(x, wr_t, rb_col, tri, tm):
    t, d = x.shape
    out2 = lambda dt: jax.ShapeDtypeStruct((2, t), dt)
    row2 = pl.BlockSpec((2, tm), lambda i: (0, i))
    return pl.pallas_call(
        _router_kernel,
        grid=(t // tm,),
        in_specs=[pl.BlockSpec((tm, d), lambda i: (i, 0)),
                  pl.BlockSpec(wr_t.shape, lambda i: (0, 0)),
                  pl.BlockSpec(rb_col.shape, lambda i: (0, 0)),
                  pl.BlockSpec(tri.shape, lambda i: (0, 0))],
        out_specs=[row2, row2, row2, pl.BlockSpec((N_EXPERTS, LANES), lambda i: (0, 0))],
        out_shape=[out2(I32), out2(F32), out2(I32), jax.ShapeDtypeStruct((N_EXPERTS, LANES), F32)],
        scratch_shapes=[pltpu.VMEM((N_EXPERTS, LANES), F32)],
        compiler_params=_cparams(("arbitrary",)),
        name="moe_router",
    )(x, wr_t, rb_col, tri)


def _dispatch_kernel(dest_ref, zflag_ref, x_ref, xs_ref, zero_ref, sem, zsem, *, t_total):
    step = pl.program_id(0)
    tm = x_ref.shape[0]

    @pl.when(step == 0)
    def _():
        zero_ref[...] = jnp.zeros_like(zero_ref)

        def zero_block(bi, _):
            @pl.when(zflag_ref[bi] == 1)
            def _():
                r0 = pl.multiple_of(bi * MOE_BM, MOE_BM)
                cp = pltpu.make_async_copy(zero_ref, xs_ref.at[pl.ds(r0, MOE_BM)], zsem)
                cp.start()
                cp.wait()
            return 0

        lax.fori_loop(0, zflag_ref.shape[0], zero_block, 0)

    def copy(r, slot):
        dst = dest_ref[slot * t_total + step * tm + r]
        return pltpu.make_async_copy(x_ref.at[pl.ds(r, 1)], xs_ref.at[pl.ds(dst, 1)], sem)

    def issue(r, _):
        copy(r, 0).start()
        copy(r, 1).start()
        return 0

    def drain(r, _):
        copy(r, 0).wait()
        copy(r, 1).wait()
        return 0

    lax.fori_loop(0, tm, issue, 0, unroll=ROW_DMA_UNROLL)
    lax.fori_loop(0, tm, drain, 0, unroll=ROW_DMA_UNROLL)


def _dispatch(dest_flat, zflag, x, rows, tm):
    t, d = x.shape
    return pl.pallas_call(
        functools.partial(_dispatch_kernel, t_total=t),
        grid_spec=pltpu.PrefetchScalarGridSpec(
            num_scalar_prefetch=2,
            grid=(t // tm,),
            in_specs=[pl.BlockSpec((tm, d), lambda i, dr, zf: (i, 0))],
            out_specs=pl.BlockSpec(memory_space=pl.ANY),
            scratch_shapes=[pltpu.VMEM((MOE_BM, d), x.dtype), pltpu.SemaphoreType.DMA,
                            pltpu.SemaphoreType.DMA],
        ),
        out_shape=jax.ShapeDtypeStruct((rows, d), x.dtype),
        compiler_params=pltpu.CompilerParams(dimension_semantics=("arbitrary",), has_side_effects=True,
                                             vmem_limit_bytes=VMEM_LIMIT),
        name="moe_dispatch",
    )(dest_flat, zflag, x)


def _expert_kernel(be_ref, na_ref, x_ref, wg_ref, wu_ref, wd_ref, o_ref):
    active = pl.program_id(0) < na_ref[0]

    @pl.when(active)
    def _():
        xb = x_ref[...].astype(BF16)
        gate = _dot(xb, wg_ref[...])
        up = _dot(xb, wu_ref[...])
        hid = (jax.nn.silu(gate) * up).astype(BF16)
        o_ref[...] = _dot(hid, wd_ref[...])

    @pl.when(jnp.logical_not(active))
    def _():
        o_ref[...] = jnp.zeros_like(o_ref)


def _experts(blk_e, n_active, xs, wg, wu, wd):
    rows, d = xs.shape
    nb = rows // MOE_BM
    act = lambda i, be, na: jnp.minimum(i, na[0] - 1)
    return pl.pallas_call(
        _expert_kernel,
        grid_spec=pltpu.PrefetchScalarGridSpec(
            num_scalar_prefetch=2,
            grid=(nb,),
            in_specs=[pl.BlockSpec((MOE_BM, d), lambda i, be, na: (act(i, be, na), 0)),
                      pl.BlockSpec((None, d, D_EXPERT), lambda i, be, na: (be[i], 0, 0)),
                      pl.BlockSpec((None, d, D_EXPERT), lambda i, be, na: (be[i], 0, 0)),
                      pl.BlockSpec((None, D_EXPERT, d), lambda i, be, na: (be[i], 0, 0))],
            out_specs=pl.BlockSpec((MOE_BM, d), lambda i, be, na: (i, 0)),
        ),
        out_shape=jax.ShapeDtypeStruct((rows, d), F32),
        compiler_params=_cparams(("arbitrary",)),
        name="moe_experts",
    )(blk_e, n_active, xs, wg, wu, wd)


def _combine_kernel(dest_ref, x_ref, w_ref, g_ref, b_ref, ys_ref, o_ref, ob_ref, buf_ref, sem, *, alpha, t_total):
    step = pl.program_id(0)
    tm = x_ref.shape[0]

    def copy(r, slot):
        src = dest_ref[slot * t_total + step * tm + r]
        return pltpu.make_async_copy(ys_ref.at[pl.ds(src, 1)], buf_ref.at[slot, pl.ds(r, 1)], sem)

    def issue(r, _):
        copy(r, 0).start()
        copy(r, 1).start()
        return 0

    def drain(r, _):
        copy(r, 0).wait()
        copy(r, 1).wait()
        return 0

    lax.fori_loop(0, tm, issue, 0, unroll=ROW_DMA_UNROLL)
    lax.fori_loop(0, tm, drain, 0, unroll=ROW_DMA_UNROLL)
    w = w_ref[...]
    z = alpha * x_ref[...] + w[:, 0:1] * buf_ref[0] + w[:, 1:2] * buf_ref[1]
    y = _layer_norm(z, g_ref[...], b_ref[...])
    o_ref[...] = y
    ob_ref[...] = y.astype(BF16)


def _combine(dest_flat, x, wts_t, g, bb, ys, alpha, tm):
    t, d = x.shape
    return pl.pallas_call(
        functools.partial(_combine_kernel, alpha=alpha, t_total=t),
        grid_spec=pltpu.PrefetchScalarGridSpec(
            num_scalar_prefetch=1,
            grid=(t // tm,),
            in_specs=[pl.BlockSpec((tm, d), lambda i, dr: (i, 0)),
                      pl.BlockSpec((tm, 2), lambda i, dr: (i, 0)),
                      pl.BlockSpec(g.shape, lambda i, dr: (0, 0)),
                      pl.BlockSpec(bb.shape, lambda i, dr: (0, 0)),
                      pl.BlockSpec(memory_space=pl.ANY)],
            out_specs=[pl.BlockSpec((tm, d), lambda i, dr: (i, 0)), pl.BlockSpec((tm, d), lambda i, dr: (i, 0))],
            scratch_shapes=[pltpu.VMEM((2, tm, d), F32), pltpu.SemaphoreType.DMA],
        ),
        out_shape=[jax.ShapeDtypeStruct((t, d), F32), jax.ShapeDtypeStruct((t, d), BF16)],
        compiler_params=_cparams(("arbitrary",)),
        name="moe_combine_ln2",
    )(dest_flat, x, wts_t, g, bb, ys)


def _attention_tables(rel_bias, s):
    nc = s // CMP_STRIDE
    nqb = s // QB
    iq = jnp.arange(QB, dtype=I32)[None, None, :]
    x = jnp.arange(nc + (QB // CMP_STRIDE) * nqb, dtype=I32)[:, None]
    rel_c = iq[0] + CMP_STRIDE * (x - nc + 1) - (CMP_LEN - 1)
    jk = jnp.arange(QB, dtype=I32)[None, :, None]
    rel_t = jnp.arange(N_OFF, dtype=I32)[:, None, None] * QB + iq - jk
    jw = jnp.arange(WINDOW + QB, dtype=I32)[:, None]
    rel_w = iq[0] - jw + WINDOW
    rel_w = jnp.where(rel_w < WINDOW, rel_w, -1)
    flat = rel_bias.reshape(-1)
    r4 = NSA_HPG * QB
    cb = _bias_table(flat, jnp.maximum(rel_c, -1), rel_c.shape[0])
    bt = _bias_table(flat, jnp.maximum(rel_t, -1).reshape(N_OFF * QB, QB), QB, delta=True)
    bt = bt.reshape(NSA_GROUPS, N_OFF, QB, r4)
    wb = _bias_table(flat, jnp.maximum(rel_w, -1), WINDOW + QB)
    sat = jnp.repeat(rel_bias[REL_BUCKETS - 1].reshape(NSA_GROUPS, 1, NSA_HPG), QB, axis=2)
    hi = sat.astype(BF16)
    lo = (sat - hi.astype(F32)).astype(BF16)
    b31 = jnp.concatenate([hi, lo, jnp.zeros((NSA_GROUPS, AUG_ROWS - 2, r4), BF16)], axis=1)
    return cb, bt, wb, b31


def _selection_tables(s):
    nc = s // CMP_STRIDE
    ns = s // SLC_BLOCK
    c_start = np.arange(nc) * CMP_STRIDE
    s_start = np.arange(ns) * SLC_BLOCK
    cover = np.maximum(np.minimum(c_start[:, None] + CMP_LEN, s_start[None, :] + SLC_BLOCK)
                       - np.maximum(c_start[:, None], s_start[None, :]), 0).astype(np.float32) / CMP_LEN
    cover[nc - 1] = 0.0
    cover_rev_t = np.ascontiguousarray(cover[::-1].T)
    blockcols = np.where(np.arange(s)[:, None] // SLC_BLOCK == np.arange(ns)[None, :], -MASK_BIG, 0.0)
    return jnp.asarray(cover_rev_t, BF16), jnp.asarray(blockcols, BF16)


def _split_in_weights(w_in_l):
    o = np.cumsum((0, 512, 128, 128, 128, 128, 128, 128, 24, 512, 512, 1024, 1024, 1024, 1024))
    seg = lambda a: w_in_l[:, o[a]:o[a + 1]]
    wb = jnp.concatenate([seg(10), seg(0), seg(1), seg(2), seg(3), seg(4), seg(5), seg(6)], axis=1)
    gpad = jnp.pad(seg(7), ((0, 0), (0, LANES - 24)))
    wf = jnp.concatenate([seg(8), seg(9), seg(11), seg(12), seg(13), gpad], axis=1)
    return wb.astype(BF16), wf.astype(BF16)


def kernel(x, w_in, cmp_pos, cmp_w1, cmp_w2, w_br_a, w_br_b, w_o, ln1_g, ln1_b, w_router, router_bias,
           moe_w_gate, moe_w_up, moe_w_down, ln2_g, ln2_b, rel_bias):
    b, s, d = x.shape
    depth = w_in.shape[0]
    t = b * s
    alpha = (2 * depth) ** 0.25
    g, r, dh = NSA_GROUPS, NSA_HPG, NSA_DH
    nqb = s // QB
    nb16 = s // CMP_STRIDE
    n_sel = min(N_SELECT, s // SLC_BLOCK)

    cb, bt, wb, b31 = _attention_tables(rel_bias, s)
    cov_t, blockcols = _selection_tables(s)
    ret_tabs = _retention_tables(s)
    tri = jnp.asarray(np.triu(np.ones((512, 512), np.float32), 1), BF16)
    wr_t = w_router.T
    rb_col = router_bias.reshape(N_EXPERTS, 1)
    n_blk = t * 2 // MOE_BM + N_EXPERTS
    rows = n_blk * MOE_BM

    xf = x.reshape(t, d)
    xb = xf.astype(BF16)
    for l in range(depth):
        w_b, w_f = _split_in_weights(w_in[l])
        pb =_matmul(xb, w_b, BF16, 1024, 1152)
        pf = _matmul(xb, w_f, F32, 1024, 1408)
        c_qa = RET_HEADS * RET_DV
        c_kv = c_qa + NSA_HEADS * dh

        def kv_heads(slab):
            col = c_kv + slab * g * dh
            return pb[:, col:col + g * dh].reshape(b, s, g, dh).transpose(0, 2, 1, 3)
        x16 = jnp.stack([kv_heads(0), kv_heads(1)]).reshape(2, b, g, nb16, CMP_STRIDE * dh)[:, :, :, ::-1]
        w1 = cmp_w1[l].astype(BF16)
        half = CMP_STRIDE * dh
        w1ab = jnp.concatenate([w1[:, :half], w1[:, half:]], axis=2)
        pos8 = jnp.broadcast_to(cmp_pos[l].reshape(2, 1, CMP_LEN * dh), (2, 8, CMP_LEN * dh)).astype(BF16)
        kvc = _compress(x16, w1ab, _posterm(pos8, w1), cmp_w2[l].astype(BF16))
        gl = (pf[:, 4096:4096 + 24].reshape(b, nqb, QB, g, r, 3).transpose(0, 3, 1, 5, 4, 2)
              .reshape(b, g, nqb, 3, r * QB))
        ya = _nsa_attention(pb, c_qa, c_kv, kvc, gl, cb, bt, wb, cov_t, b31, blockcols, b, s, n_sel)

        yb = _retention(pf, pb, b, s, 0, 512, 0, 1024, ret_tabs)

        x1 = _merge(ya, yb, pf, 2048, 3072, xf, w_br_a[l].astype(BF16), w_br_b[l].astype(BF16),
                    w_o[l].astype(BF16), ln1_g[l].reshape(1, d), ln1_b[l].reshape(1, d), alpha, 512)

        eidx, wts, rank, cnt = _router(x1, wr_t, rb_col, tri, 512)
        counts = cnt[:, 0].astype(I32)
        padded = ((counts + MOE_BM - 1) // MOE_BM) * MOE_BM
        pad_end = jnp.cumsum(padded)
        pad_start = pad_end - padded
        e_iota = jnp.arange(N_EXPERTS, dtype=I32)[:, None, None]
        start_of = jnp.sum(jnp.where(eidx[None] == e_iota, pad_start[:, None, None], 0), axis=0)
        dest = (start_of + rank).reshape(-1)
        n_active = (pad_end[-1] // MOE_BM).astype(I32).reshape(1)
        blk = jnp.minimum(jnp.arange(n_blk, dtype=I32), n_active[0] - 1) * MOE_BM
        blk_e = jnp.minimum(jnp.sum(pad_end[None, :] <= blk[:, None], axis=1), N_EXPERTS - 1).astype(I32)
        blk_all = jnp.arange(n_blk, dtype=I32)
        last_of_expert = jnp.any(pad_end[None, :] == (blk_all[:, None] + 1) * MOE_BM, axis=1)
        zflag = (last_of_expert | (blk_all >= n_active[0])).astype(I32)
        xs = _dispatch(dest, zflag, x1, rows, 512)
        ys = _experts(blk_e, n_active, xs, moe_w_gate[l].astype(BF16), moe_w_up[l].astype(BF16),
                      moe_w_down[l].astype(BF16))
        xf, xb = _combine(dest, x1, wts.T, ln2_g[l].reshape(1, d), ln2_b[l].reshape(1, d), ys, alpha, 256)
    return xf.reshape(b, s, d)
```

```python
import functools
import math

import jax
import jax.numpy as jnp
import numpy as np
from jax import lax
from jax.experimental import pallas as pl
from jax.experimental.pallas import tpu as pltpu

F32 = jnp.float32
BF16 = jnp.bfloat16
I32 = jnp.int32

NSA_HEADS = 8
NSA_GROUPS = 2
NSA_HPG = 4
NSA_DH = 64
CMP_LEN = 32
CMP_STRIDE = 16
SLC_BLOCK = 64
N_SELECT = 16
WINDOW = 512
QB = 128
FORCE_SCORE = 1.0e4
RET_HEADS = 8
RET_DK = 64
RET_DV = 128
RET_CHUNK = 128
ROPE_BASE = 10000.0
REL_BUCKETS = 32
REL_MAX_EXACT = 16
REL_MAX_DIST = 2048
N_EXPERTS = 16
N_GROUPS = 4
EPG = 4
D_EXPERT = 512
LN_EPS = 1e-5
GN_EPS = 1e-5
NEG_INF = -1e30

LANES = 128
VMEM_LIMIT = 56 * 1024 * 1024

SEL_TK = 512
SEL_CW = 512
AUG_ROWS = 16
MASK_BIG = 2.0 ** 100
N_OFF = 14
MOE_BM = 256
ROW_DMA_UNROLL = 8


def _cparams(sem):
    return pltpu.CompilerParams(dimension_semantics=sem, vmem_limit_bytes=VMEM_LIMIT)


def _dot(a, b):
    return jnp.dot(a, b, preferred_element_type=F32)


def _dot_nt(a, b):
    return lax.dot_general(a, b, (((1,), (1,)), ((), ())), preferred_element_type=F32)


def _mm_kernel(a_ref, b_ref, o_ref):
    o_ref[...] = _dot(a_ref[...], b_ref[...]).astype(o_ref.dtype)


def _matmul(a, b, out_dtype, tm, tn):
    m, k = a.shape
    n = b.shape[1]
    return pl.pallas_call(
        _mm_kernel,
        grid=(n // tn, m // tm),
        in_specs=[pl.BlockSpec((tm, k), lambda j, i: (i, 0)),
                  pl.BlockSpec((k, tn), lambda j, i: (0, j))],
        out_specs=pl.BlockSpec((tm, tn), lambda j, i: (i, j)),
        out_shape=jax.ShapeDtypeStruct((m, n), out_dtype),
        compiler_params=_cparams(("arbitrary", "arbitrary")),
        name="proj_matmul",
    )(a, b)


def _bias_kernel(tab_ref, rel_ref, o_ref, *, delta):
    rel = rel_ref[...]
    n = jnp.maximum(rel, 0)
    nf = jnp.maximum(n, 1).astype(F32)
    large = REL_MAX_EXACT + (jnp.log(nf / REL_MAX_EXACT) / math.log(REL_MAX_DIST / REL_MAX_EXACT)
                             * (REL_BUCKETS - REL_MAX_EXACT)).astype(I32)
    large = jnp.minimum(large, REL_BUCKETS - 1)
    bucket = jnp.where(n < REL_MAX_EXACT, n, large)
    eqs = [bucket == k for k in range(REL_BUCKETS)]
    for h in range(NSA_HEADS):
        acc = jnp.full(rel.shape, NEG_INF, F32)
        for k in range(REL_BUCKETS):
            acc = jnp.where(eqs[k], tab_ref[k * NSA_HEADS + h], acc)
        if delta:
            acc = acc - tab_ref[(REL_BUCKETS - 1) * NSA_HEADS + h]
        hh = h % NSA_HPG
        o_ref[h // NSA_HPG, :, hh * QB:(hh + 1) * QB] = jnp.where(rel >= 0, acc, NEG_INF)


def _bias_table(rel_bias_flat, rel, tr, delta=False):
    r, c = rel.shape
    return pl.pallas_call(
        functools.partial(_bias_kernel, delta=delta),
        grid_spec=pltpu.PrefetchScalarGridSpec(
            num_scalar_prefetch=1,
            grid=(r // tr,),
            in_specs=[pl.BlockSpec((tr, c), lambda i, tab: (i, 0))],
            out_specs=pl.BlockSpec((NSA_GROUPS, tr, NSA_HPG * c), lambda i, tab: (0, i, 0)),
        ),
        out_shape=jax.ShapeDtypeStruct((NSA_GROUPS, r, NSA_HPG * c), F32),
        compiler_params=_cparams(("arbitrary",)),
        name="t5_bias_table",
    )(rel_bias_flat, rel)


def _compress_kernel(x_ref, w1_ref, pos_ref, w2_ref, o_ref):
    ab = _dot(x_ref[...], w1_ref[...])
    nb = ab.shape[0]
    nxt = pltpu.roll(ab, nb - 1, axis=0)
    nxt = pltpu.roll(nxt, NSA_DH, axis=1)
    pre = (ab + nxt)[:, :NSA_DH] + pos_ref[...]
    hid = jax.nn.gelu(pre)
    out = _dot(hid.astype(BF16), w2_ref[...]).astype(BF16)
    rows = lax.broadcasted_iota(I32, (nb, nb), 0)
    cols = lax.broadcasted_iota(I32, (nb, nb), 1)
    flip = jnp.where(rows + cols == nb - 1, 1.0, 0.0).astype(BF16)
    o_ref[...] = _dot(flip, out).astype(o_ref.dtype)


def _compress(x16, w1ab, posterm, w2):
    _, b, g, nb, wd = x16.shape
    return pl.pallas_call(
        _compress_kernel,
        grid=(2, b, g),
        in_specs=[pl.BlockSpec((None, None, None, nb, wd), lambda s, bi, gi: (s, bi, gi, 0, 0)),
                  pl.BlockSpec((None, wd, 2 * NSA_DH), lambda s, bi, gi: (s, 0, 0)),
                  pl.BlockSpec((None, 1, NSA_DH), lambda s, bi, gi: (s, 0, 0)),
                  pl.BlockSpec((None, NSA_DH, NSA_DH), lambda s, bi, gi: (s, 0, 0))],
        out_specs=pl.BlockSpec((None, None, None, nb, NSA_DH), lambda s, bi, gi: (s, bi, gi, 0, 0)),
        out_shape=jax.ShapeDtypeStruct((2, b, g, nb, NSA_DH), BF16),
        compiler_params=_cparams(("arbitrary",) * 3),
        name="nsa_compress",
    )(x16, w1ab, posterm, w2)


def _posterm_kernel(p_ref, w_ref, o_ref):
    o_ref[...] = _dot(p_ref[...], w_ref[...])[0:1]


def _posterm(pos8, w1):
    return pl.pallas_call(
        _posterm_kernel,
        grid=(2,),
        in_specs=[pl.BlockSpec((None, 8, pos8.shape[2]), lambda s: (s, 0, 0)),
                  pl.BlockSpec((None, w1.shape[1], NSA_DH), lambda s: (s, 0, 0))],
        out_specs=pl.BlockSpec((None, 1, NSA_DH), lambda s: (s, 0, 0)),
        out_shape=jax.ShapeDtypeStruct((2, 1, NSA_DH), F32),
        name="nsa_posterm",
    )(pos8, w1)


def _eye(n, scale=1.0):
    rows = lax.broadcasted_iota(I32, (n, n), 0)
    cols = lax.broadcasted_iota(I32, (n, n), 1)
    return jnp.where(rows == cols, scale, 0.0).astype(BF16)


def _nsa_kernel(q_ref, qn_ref, kc_ref, vcin_ref, ksin_ref, vsin_ref, kwin_ref, vwin_ref, gate_ref,
                cb_ref, bt_ref, wb_ref, cov_ref, b31_ref, bcols_ref, o_ref,
                s_ref, ks_ref, vs_ref, kw_ref, vw_ref, vc_ref, qt_ref, *, n_sel):
    gi = pl.program_id(1)
    i = pl.program_id(2)
    q0 = i * QB
    r4 = NSA_HPG * QB
    ns = cov_ref.shape[0]
    eye_dh = _eye(NSA_DH)

    def group_lanes(x):
        return jnp.where(gi == 0, x[:, 0:NSA_DH], x[:, NSA_DH:2 * NSA_DH])

    def q_transposed(ref):
        eye_q = _eye(NSA_DH, NSA_DH ** -0.5)
        qb = ref[...]
        return jnp.concatenate([_dot_nt(eye_q, qb[:, h * NSA_DH:(h + 1) * NSA_DH]) for h in range(NSA_HPG)],
                               axis=1).astype(BF16)

    @pl.when(i == 0)
    def _():
        s_len = ksin_ref.shape[0]
        ones_rows = jnp.where(lax.broadcasted_iota(I32, (AUG_ROWS, SEL_TK), 0) == 0, 1.0, 0.0).astype(BF16)
        tail = jnp.where(lax.broadcasted_iota(I32, (SEL_TK, NSA_DH), 1) < 2, 1.0, 0.0).astype(BF16)
        depth = ks_ref.shape[-1]
        vc_ref[...] = _dot_nt(eye_dh, vcin_ref[...]).astype(BF16)
        qt_ref[...] = q_transposed(q_ref)
        kw_ref[0:WINDOW, :] = jnp.zeros((WINDOW, NSA_DH), BF16)
        for u in range(WINDOW // QB):
            vw_ref[u] = jnp.zeros(vw_ref.shape[1:], BF16)

        def chunk(c, _):
            r0 = pl.multiple_of(c * SEL_TK, SEL_TK)
            rows = pl.ds(r0, SEL_TK)
            k_aug = jnp.concatenate([group_lanes(ksin_ref[rows, :]), tail], axis=1)
            ks_ref[rows, 0:ns] = bcols_ref[rows, :]
            ks_ref[rows, ns:ns + 2 * NSA_DH] = k_aug
            if depth > ns + 2 * NSA_DH:
                ks_ref[rows, ns + 2 * NSA_DH:depth] = jnp.zeros((SEL_TK, depth - ns - 2 * NSA_DH), BF16)
            v_t = _dot_nt(eye_dh, group_lanes(vsin_ref[rows, :])).astype(BF16)
            vs_ref[c] = jnp.concatenate([v_t, ones_rows], axis=0)
            kw_ref[pl.ds(WINDOW + r0, SEL_TK), :] = group_lanes(kwin_ref[rows, :])
            vw_t = _dot_nt(eye_dh, group_lanes(vwin_ref[rows, :])).astype(BF16)
            for u in range(SEL_TK // QB):
                vw_ref[WINDOW // QB + c * (SEL_TK // QB) + u] = jnp.concatenate(
                    [vw_t[:, u * QB:(u + 1) * QB], ones_rows[:, 0:QB]], axis=0)
            return 0

        lax.fori_loop(0, s_len // SEL_TK, chunk, 0)

    q = qt_ref[...]

    step_rows = QB // CMP_STRIDE
    cb = cb_ref[pl.ds(pl.multiple_of(i * step_rows, step_rows), kc_ref.shape[0]), :]
    s_c = _dot(kc_ref[...], q) + cb
    m_c = jnp.max(s_c, axis=0, keepdims=True)
    p_c = jnp.where(cb > 0.5 * NEG_INF, jnp.exp(s_c - m_c), 0.0)
    l_c = jnp.sum(p_c, axis=0, keepdims=True)
    p_c = p_c / jnp.maximum(l_c, 1e-30)
    o_c = _dot(vc_ref[...], p_c.astype(BF16))

    psum = p_c[:, 0:QB] + p_c[:, QB:2 * QB] + p_c[:, 2 * QB:3 * QB] + p_c[:, 3 * QB:4 * QB]
    p_hi = psum.astype(BF16)
    p_lo = (psum - p_hi.astype(F32)).astype(BF16)
    cov = cov_ref[...]
    imp_t = _dot(cov, p_hi) + _dot(cov, p_lo)
    ns = imp_t.shape[0]

    wk = WINDOW + QB
    kwin = kw_ref[pl.ds(pl.multiple_of(q0, QB), wk), :]
    s_w = _dot(kwin, q) + wb_ref[...]
    kpos = q0 - WINDOW + lax.broadcasted_iota(I32, (wk, r4), 0)
    s_w = jnp.where(kpos >= 0, s_w, NEG_INF)
    m_w = jnp.max(s_w, axis=0, keepdims=True)
    p_w = jnp.exp(s_w - m_w).astype(BF16)
    acc_w = jnp.zeros((NSA_DH + AUG_ROWS, r4), F32)
    for u in range(wk // QB):
        acc_w = acc_w + _dot(vw_ref[i + u], p_w[u * QB:(u + 1) * QB])
    o_w = acc_w[0:NSA_DH] / acc_w[NSA_DH:NSA_DH + 1]

    jidx = lax.broadcasted_iota(I32, (ns, QB), 0)
    t = q0 + lax.broadcasted_iota(I32, (ns, QB), 1)
    cur = jnp.right_shift(t, 6)
    eligible = jidx <= cur
    forced = (jidx == 0) | (jidx == cur) | (jidx == cur - 1)
    score = jnp.where(eligible, jnp.where(forced, -2.0, imp_t), -1.0)
    sel = jnp.where(forced, 1.0, 0.0)
    for _ in range(n_sel - 3):
        mx = jnp.max(score, axis=0, keepdims=True)
        first = jnp.min(jnp.where(score == mx, jidx, ns), axis=0, keepdims=True)
        pick = jidx == first
        sel = jnp.where(pick, 1.0, sel)
        score = jnp.where(pick, -2.0, score)
    notsel = jnp.where(eligible, 1.0 - sel, 1.0).astype(BF16)

    depth = ks_ref.shape[-1]
    pad_rows = depth - ns - NSA_DH - AUG_ROWS
    qa = jnp.concatenate([jnp.concatenate([notsel] * NSA_HPG, axis=1), q, b31_ref[...],
                          jnp.zeros((pad_rows, r4), BF16)], axis=0)
    sub = SEL_TK // QB
    n_tiles = i // sub + 1
    n_far = jnp.maximum(i - N_OFF + 2, 0) // sub

    def put_scores(jt, slot):
        k0 = pl.multiple_of(jt * SEL_TK, SEL_TK)
        s_ref[slot] = _dot(ks_ref[pl.ds(k0, SEL_TK), :], qa)

    def absorb(jt, slot, carry, near):
        m_prev, acc = carry
        s = s_ref[slot]
        if near:
            offs = [jnp.clip(i - sub * jt - u, 0, N_OFF - 1) for u in range(sub)]
            s = s + jnp.concatenate([bt_ref[o] for o in offs], axis=0)
        m_new = jnp.maximum(m_prev, jnp.max(s, axis=0, keepdims=True))
        alpha = jnp.exp(m_prev - m_new)
        p = jnp.exp(s - m_new).astype(BF16)
        return m_new, alpha * acc + _dot(vs_ref[jt], p)

    def pair_body(pp, carry, near):
        ja = 2 * pp
        put_scores(ja + 1, 1)
        carry = absorb(ja, 0, carry, near)
        put_scores(jnp.minimum(ja + 2, last_tile), 0)
        return absorb(ja + 1, 1, carry, near)

    n_pairs = (n_tiles + 1) // 2
    last_tile = 2 * n_pairs - 1
    far_pairs = n_far // 2
    put_scores(0, 0)
    carry = (jnp.full((1, r4), -1e38, F32), jnp.zeros((NSA_DH + AUG_ROWS, r4), F32))
    carry = lax.fori_loop(0, far_pairs, functools.partial(pair_body, near=False), carry)
    _, acc_s = lax.fori_loop(far_pairs, n_pairs, functools.partial(pair_body, near=True), carry)
    o_s = acc_s[0:NSA_DH] / acc_s[NSA_DH:NSA_DH + 1]

    gates = jax.nn.sigmoid(gate_ref[...])
    o = (gates[0:1] * o_c + gates[1:2] * o_s + gates[2:3] * o_w).astype(BF16)
    qt_ref[...] = q_transposed(qn_ref)
    eye_qb = _eye(QB)
    o_ref[...] = jnp.concatenate([_dot_nt(eye_qb, o[:, h * QB:(h + 1) * QB]) for h in range(NSA_HPG)],
                                 axis=1).astype(o_ref.dtype)


def _nsa_attention(pb, c_qa, c_kv, kvc, gates, cb, bt, wb, cov_t, b31, blockcols, b, s, n_sel):
    g, dh = NSA_GROUPS, NSA_DH
    r4 = NSA_HPG * QB
    nqb = s // QB
    nc = kvc.shape[3]
    ns = cov_t.shape[0]
    depth = -(-(ns + 2 * dh) // LANES) * LANES
    n_tiles = s // SEL_TK
    assert n_tiles % 2 == 0, "the selected sweep consumes key tiles in pairs"
    qw = NSA_HPG * dh
    slab = lambda k: pl.BlockSpec((s, 2 * dh), lambda bi, gi, i: (bi, c_kv // (2 * dh) + k))
    per_g = lambda a: pl.BlockSpec((None,) + a.shape[1:], lambda bi, gi, i: (gi,) + (0,) * (a.ndim - 1))
    const = lambda a: pl.BlockSpec(a.shape, lambda bi, gi, i: (0,) * a.ndim)
    return pl.pallas_call(
        functools.partial(_nsa_kernel, n_sel=n_sel),
        grid=(b, g, nqb),
        in_specs=[
            pl.BlockSpec((QB, qw), lambda bi, gi, i: (bi * nqb + i, c_qa // qw + gi)),
            pl.BlockSpec((QB, qw), lambda bi, gi, i: (bi * nqb + jnp.minimum(i + 1, nqb - 1), c_qa // qw + gi)),
            pl.BlockSpec((None, None, None, nc, dh), lambda bi, gi, i: (0, bi, gi, 0, 0)),
            pl.BlockSpec((None, None, None, nc, dh), lambda bi, gi, i: (1, bi, gi, 0, 0)),
            slab(2), slab(3), slab(4), slab(5),
            pl.BlockSpec((None, None, None, 3, r4), lambda bi, gi, i: (bi, gi, i, 0, 0)),
            per_g(cb), per_g(bt), per_g(wb), const(cov_t), per_g(b31), const(blockcols),
        ],
        out_specs=pl.BlockSpec((QB, qw), lambda bi, gi, i: (bi * nqb + i, gi)),
        out_shape=jax.ShapeDtypeStruct((b * s, g * qw), BF16),
        scratch_shapes=[pltpu.VMEM((2, SEL_TK, r4), F32),
                        pltpu.VMEM((s, depth), BF16),
                        pltpu.VMEM((n_tiles, dh + AUG_ROWS, SEL_TK), BF16),
                        pltpu.VMEM((s + WINDOW, dh), BF16),
                        pltpu.VMEM(((s + WINDOW) // QB, dh + AUG_ROWS, QB), BF16),
                        pltpu.VMEM((dh, nc), BF16),
                        pltpu.VMEM((dh, r4), BF16)],
        compiler_params=_cparams(("arbitrary",) * 3),
        name="nsa_attention",
    )(pb, pb, kvc, kvc, pb, pb, pb, pb, gates, cb, bt, wb, cov_t, b31, blockcols)


def _ret_kernel(q_ref, k_ref, v_ref, g_ref, cos_ref, sin_ref, dm_ref, qd_ref, kd_ref, cd_ref,
                o_ref, state_ref):
    n = pl.program_id(1)

    @pl.when(n == 0)
    def _():
        state_ref[...] = jnp.zeros_like(state_ref)

    lane = lax.broadcasted_iota(I32, (RET_CHUNK, LANES), 1)
    first_half = (lane & (RET_DK - 1)) < (RET_DK // 2)
    cosv = cos_ref[...]
    sinv = sin_ref[...]

    def rot(x):
        partner = jnp.where(first_half, pltpu.roll(x, LANES - RET_DK // 2, axis=1),
                            pltpu.roll(x, RET_DK // 2, axis=1))
        return x * cosv + partner * sinv

    for hp in range(RET_HEADS // 2):
        ls = slice(hp * LANES, (hp + 1) * LANES)
        q2 = rot(q_ref[:, ls])
        k2 = rot(k_ref[:, ls]) * (RET_DK ** -0.5)
        qdec = (q2 * qd_ref[:, ls]).astype(BF16)
        kdec_t = (k2 * kd_ref[:, ls]).T.astype(BF16)
        q2b = q2.astype(BF16)
        k2b = k2.astype(BF16)
        for e in range(2):
            h = 2 * hp + e
            hs = slice(e * RET_DK, (e + 1) * RET_DK)
            vs = slice(h * RET_DV, (h + 1) * RET_DV)
            v = v_ref[:, vs]
            st = state_ref[h]
            sc = _dot_nt(q2b[:, hs], k2b[:, hs]) * dm_ref[h]
            y = _dot(sc.astype(BF16), v) + _dot(qdec[:, hs], st.astype(BF16))
            state_ref[h] = cd_ref[h] * st + _dot(kdec_t[hs, :], v)
            mu = jnp.mean(y, axis=-1, keepdims=True)
            var = jnp.mean(jnp.square(y - mu), axis=-1, keepdims=True)
            yn = (y - mu) * lax.rsqrt(var + GN_EPS)
            o_ref[:, vs] = (jax.nn.silu(g_ref[:, vs]) * yn).astype(o_ref.dtype)


def _retention(pf, pb, b, s, col_q, col_k, col_v, col_g, tabs):
    cosf, sinf, dm, qd, kd, cd = tabs
    n = s // RET_CHUNK
    t = b * s
    c = RET_CHUNK
    wqk = RET_HEADS * RET_DK
    wv = RET_HEADS * RET_DV
    row = lambda bi, ni: bi * n + ni
    full = lambda a: pl.BlockSpec(a.shape, lambda bi, ni: (0,) * a.ndim)
    return pl.pallas_call(
        _ret_kernel,
        grid=(b, n),
        in_specs=[
            pl.BlockSpec((c, wqk), lambda bi, ni: (row(bi, ni), col_q // wqk)),
            pl.BlockSpec((c, wqk), lambda bi, ni: (row(bi, ni), col_k // wqk)),
            pl.BlockSpec((c, wv), lambda bi, ni: (row(bi, ni), col_v // wv)),
            pl.BlockSpec((c, wv), lambda bi, ni: (row(bi, ni), col_g // wv)),
            pl.BlockSpec((c, LANES), lambda bi, ni: (ni, 0)),
            pl.BlockSpec((c, LANES), lambda bi, ni: (ni, 0)),
            full(dm), full(qd), full(kd), full(cd),
        ],
        out_specs=pl.BlockSpec((c, wv), lambda bi, ni: (row(bi, ni), 0)),
        out_shape=jax.ShapeDtypeStruct((t, wv), BF16),
        scratch_shapes=[pltpu.VMEM((RET_HEADS, RET_DK, RET_DV), F32)],
        compiler_params=_cparams(("arbitrary",) * 2),
        name="retention",
    )(pf, pf, pb, pf, cosf, sinf, dm, qd, kd, cd)


def _retention_tables(s):
    half = RET_DK // 2
    inv = ROPE_BASE ** (-jnp.arange(half, dtype=F32) * 2.0 / RET_DK)
    ang = jnp.arange(s, dtype=F32)[:, None] * inv[None, :]
    cos, sin = jnp.cos(ang), jnp.sin(ang)
    cosf = jnp.concatenate([cos, cos, cos, cos], axis=1)
    sinf = jnp.concatenate([-sin, sin, -sin, sin], axis=1)
    lg = jnp.log(1.0 - 2.0 ** (-5.0 - jnp.arange(RET_HEADS, dtype=F32)))
    i = jnp.arange(RET_CHUNK, dtype=F32)
    diff = i[:, None] - i[None, :]
    dm = jnp.where(diff >= 0, jnp.exp(jnp.maximum(diff, 0.0)[None] * lg[:, None, None]), 0.0)
    kdec = jnp.exp((RET_CHUNK - 1.0 - i)[None, :] * lg[:, None])
    qdec = jnp.exp((i + 1.0)[None, :] * lg[:, None])
    expand = lambda a: jnp.repeat(a.T, RET_DK, axis=1)
    cd = jnp.broadcast_to(jnp.exp(RET_CHUNK * lg)[:, None, None], (RET_HEADS, 1, LANES))
    return cosf, sinf, dm, expand(qdec), expand(kdec), cd


def _layer_norm(z, g, b):
    mu = jnp.mean(z, axis=-1, keepdims=True)
    var = jnp.mean(jnp.square(z - mu), axis=-1, keepdims=True)
    return (z - mu) * lax.rsqrt(var + LN_EPS) * g + b


def _merge_kernel(ya_ref, yb_ref, ma_ref, mb_ref, x_ref, wa_ref, wb_ref, wo_ref, g_ref, b_ref,
                  o_ref, *, alpha):
    merged = (jax.nn.sigmoid(ma_ref[...]) * _dot(ya_ref[...], wa_ref[...])
              + jax.nn.sigmoid(mb_ref[...]) * _dot(yb_ref[...], wb_ref[...]))
    z = alpha * x_ref[...] + _dot(merged.astype(BF16), wo_ref[...])
    o_ref[...] = _layer_norm(z, g_ref[...], b_ref[...])


def _merge(ya, yb, pf, col_ma, col_mb, x, wa, wb, wo, g, bb, alpha, tm):
    t, d = x.shape
    full = lambda a: pl.BlockSpec(a.shape, lambda i: (0,) * a.ndim)
    return pl.pallas_call(
        functools.partial(_merge_kernel, alpha=alpha),
        grid=(t // tm,),
        in_specs=[pl.BlockSpec((tm, ya.shape[1]), lambda i: (i, 0)),
                  pl.BlockSpec((tm, yb.shape[1]), lambda i: (i, 0)),
                  pl.BlockSpec((tm, d), lambda i: (i, col_ma // d)),
                  pl.BlockSpec((tm, d), lambda i: (i, col_mb // d)),
                  pl.BlockSpec((tm, d), lambda i: (i, 0)),
                  full(wa), full(wb), full(wo), full(g), full(bb)],
        out_specs=pl.BlockSpec((tm, d), lambda i: (i, 0)),
        out_shape=jax.ShapeDtypeStruct((t, d), F32),
        compiler_params=_cparams(("arbitrary",)),
        name="merge_ln1",
    )(ya, yb, pf, pf, x, wa, wb, wo, g, bb)


def _router_kernel(x_ref, wr_ref, rb_ref, tri_ref, eidx_ref, wts_ref, rank_ref, cnt_ref, carry_ref):
    step = pl.program_id(0)

    @pl.when(step == 0)
    def _():
        carry_ref[...] = jnp.zeros_like(carry_ref)

    logits = lax.dot_general(wr_ref[...], x_ref[...], (((1,), (1,)), ((), ())),
                             precision=lax.Precision.HIGHEST, preferred_element_type=F32)
    tm = logits.shape[1]
    mx = jnp.max(logits, axis=0, keepdims=True)
    ex = jnp.exp(logits - mx)
    probs = ex / jnp.sum(ex, axis=0, keepdims=True)
    sel = probs + rb_ref[...]
    rows = [sel[e:e + 1] for e in range(N_EXPERTS)]
    gscore = []
    for gi in range(N_GROUPS):
        r = rows[gi * EPG:(gi + 1) * EPG]
        best2 = None
        for a in range(EPG):
            for c in range(a + 1, EPG):
                pair = r[a] + r[c]
                best2 = pair if best2 is None else jnp.maximum(best2, pair)
        gscore.append(best2)
    best = jnp.zeros((1, tm), I32)
    bscore = gscore[0]
    for gi in range(1, N_GROUPS):
        better = gscore[gi] > bscore
        best = jnp.where(better, gi, best)
        bscore = jnp.where(better, gscore[gi], bscore)
    ing = []
    for a in range(EPG):
        v = rows[a]
        for gi in range(1, N_GROUPS):
            v = jnp.where(best == gi, rows[gi * EPG + a], v)
        ing.append(v)
    i1 = jnp.zeros((1, tm), I32)
    v1 = ing[0]
    for a in range(1, EPG):
        better = ing[a] > v1
        i1 = jnp.where(better, a, i1)
        v1 = jnp.where(better, ing[a], v1)
    i2 = jnp.full((1, tm), -1, I32)
    v2 = jnp.full((1, tm), -jnp.inf, F32)
    for a in range(EPG):
        better = (i1 != a) & (ing[a] > v2)
        i2 = jnp.where(better, a, i2)
        v2 = jnp.where(better, ing[a], v2)
    e0 = best * EPG + i1
    e1 = best * EPG + i2
    eio = lax.broadcasted_iota(I32, (N_EXPERTS, tm), 0)
    oh0 = (eio == e0).astype(F32)
    oh1 = (eio == e1).astype(F32)
    p0 = jnp.sum(oh0 * probs, axis=0, keepdims=True)
    p1 = jnp.sum(oh1 * probs, axis=0, keepdims=True)
    den = p0 + p1
    eidx_ref[0:1, :] = e0
    eidx_ref[1:2, :] = e1
    wts_ref[0:1, :] = p0 / den
    wts_ref[1:2, :] = p1 / den
    tot = oh0 + oh1
    prefix = _dot(tot.astype(BF16), tri_ref[...]) + carry_ref[:, 0:1]
    rank_ref[0:1, :] = jnp.sum(oh0 * prefix, axis=0, keepdims=True).astype(I32)
    rank_ref[1:2, :] = jnp.sum(oh1 * prefix, axis=0, keepdims=True).astype(I32)
    carry_ref[...] = carry_ref[...] + jnp.sum(tot, axis=1, keepdims=True)
    cnt_ref[...] = carry_ref[...]


def _router(x, wr_t, rb_col, tri, tm):
    t, d = x.shape
    out2 = lambda dt: jax.ShapeDtypeStruct((2, t), dt)
    row2 = pl.BlockSpec((2, tm), lambda i: (0, i))
    return pl.pallas_call(
        _router_kernel,
        grid=(t // tm,),
        in_specs=[pl.BlockSpec((tm, d), lambda i: (i, 0)),
                  pl.BlockSpec(wr_t.shape, lambda i: (0, 0)),
                  pl.BlockSpec(rb_col.shape, lambda i: (0, 0)),
                  pl.BlockSpec(tri.shape, lambda i: (0, 0))],
        out_specs=[row2, row2, row2, pl.BlockSpec((N_EXPERTS, LANES), lambda i: (0, 0))],
        out_shape=[out2(I32), out2(F32), out2(I32), jax.ShapeDtypeStruct((N_EXPERTS, LANES), F32)],
        scratch_shapes=[pltpu.VMEM((N_EXPERTS, LANES), F32)],
        compiler_params=_cparams(("arbitrary",)),
        name="moe_router",
    )(x, wr_t, rb_col, tri)


def _dispatch_kernel(dest_ref, zflag_ref, x_ref, xs_ref, zero_ref, sem, zsem, *, t_total):
    step = pl.program_id(0)
    tm = x_ref.shape[0]

    @pl.when(step == 0)
    def _():
        zero_ref[...] = jnp.zeros_like(zero_ref)

        def zero_block(bi, _):
            @pl.when(zflag_ref[bi] == 1)
            def _():
                r0 = pl.multiple_of(bi * MOE_BM, MOE_BM)
                cp = pltpu.make_async_copy(zero_ref, xs_ref.at[pl.ds(r0, MOE_BM)], zsem)
                cp.start()
                cp.wait()
            return 0

        lax.fori_loop(0, zflag_ref.shape[0], zero_block, 0)

    def copy(r, slot):
        dst = dest_ref[slot * t_total + step * tm + r]
        return pltpu.make_async_copy(x_ref.at[pl.ds(r, 1)], xs_ref.at[pl.ds(dst, 1)], sem)

    def issue(r, _):
        copy(r, 0).start()
        copy(r, 1).start()
        return 0

    def drain(r, _):
        copy(r, 0).wait()
        copy(r, 1).wait()
        return 0

    lax.fori_loop(0, tm, issue, 0, unroll=ROW_DMA_UNROLL)
    lax.fori_loop(0, tm, drain, 0, unroll=ROW_DMA_UNROLL)


def _dispatch(dest_flat, zflag, x, rows, tm):
    t, d = x.shape
    return pl.pallas_call(
        functools.partial(_dispatch_kernel, t_total=t),
        grid_spec=pltpu.PrefetchScalarGridSpec(
            num_scalar_prefetch=2,
            grid=(t // tm,),
            in_specs=[pl.BlockSpec((tm, d), lambda i, dr, zf: (i, 0))],
            out_specs=pl.BlockSpec(memory_space=pl.ANY),
            scratch_shapes=[pltpu.VMEM((MOE_BM, d), x.dtype), pltpu.SemaphoreType.DMA,
                            pltpu.SemaphoreType.DMA],
        ),
        out_shape=jax.ShapeDtypeStruct((rows, d), x.dtype),
        compiler_params=pltpu.CompilerParams(dimension_semantics=("arbitrary",), has_side_effects=True,
                                             vmem_limit_bytes=VMEM_LIMIT),
        name="moe_dispatch",
    )(dest_flat, zflag, x)


def _expert_kernel(be_ref, na_ref, x_ref, wg_ref, wu_ref, wd_ref, o_ref):
    active = pl.program_id(0) < na_ref[0]

    @pl.when(active)
    def _():
        xb = x_ref[...].astype(BF16)
        gate = _dot(xb, wg_ref[...])
        up = _dot(xb, wu_ref[...])
        hid = (jax.nn.silu(gate) * up).astype(BF16)
        o_ref[...] = _dot(hid, wd_ref[...])

    @pl.when(jnp.logical_not(active))
    def _():
        o_ref[...] = jnp.zeros_like(o_ref)


def _experts(blk_e, n_active, xs, wg, wu, wd):
    rows, d = xs.shape
    nb = rows // MOE_BM
    act = lambda i, be, na: jnp.minimum(i, na[0] - 1)
    return pl.pallas_call(
        _expert_kernel,
        grid_spec=pltpu.PrefetchScalarGridSpec(
            num_scalar_prefetch=2,
            grid=(nb,),
            in_specs=[pl.BlockSpec((MOE_BM, d), lambda i, be, na: (act(i, be, na), 0)),
                      pl.BlockSpec((None, d, D_EXPERT), lambda i, be, na: (be[i], 0, 0)),
                      pl.BlockSpec((None, d, D_EXPERT), lambda i, be, na: (be[i], 0, 0)),
                      pl.BlockSpec((None, D_EXPERT, d), lambda i, be, na: (be[i], 0, 0))],
            out_specs=pl.BlockSpec((MOE_BM, d), lambda i, be, na: (i, 0)),
        ),
        out_shape=jax.ShapeDtypeStruct((rows, d), F32),
        compiler_params=_cparams(("arbitrary",)),
        name="moe_experts",
    )(blk_e, n_active, xs, wg, wu, wd)


def _combine_kernel(dest_ref, x_ref, w_ref, g_ref, b_ref, ys_ref, o_ref, ob_ref, buf_ref, sem, *, alpha, t_total):
    step = pl.program_id(0)
    tm = x_ref.shape[0]

    def copy(r, slot):
        src = dest_ref[slot * t_total + step * tm + r]
        return pltpu.make_async_copy(ys_ref.at[pl.ds(src, 1)], buf_ref.at[slot, pl.ds(r, 1)], sem)

    def issue(r, _):
        copy(r, 0).start()
        copy(r, 1).start()
        return 0

    def drain(r, _):
        copy(r, 0).wait()
        copy(r, 1).wait()
        return 0

    lax.fori_loop(0, tm, issue, 0, unroll=ROW_DMA_UNROLL)
    lax.fori_loop(0, tm, drain, 0, unroll=ROW_DMA_UNROLL)
    w = w_ref[...]
    z = alpha * x_ref[...] + w[:, 0:1] * buf_ref[0] + w[:, 1:2] * buf_ref[1]
    y = _layer_norm(z, g_ref[...], b_ref[...])
    o_ref[...] = y
    ob_ref[...] = y.astype(BF16)


def _combine(dest_flat, x, wts_t, g, bb, ys, alpha, tm):
    t, d = x.shape
    return pl.pallas_call(
        functools.partial(_combine_kernel, alpha=alpha, t_total=t),
        grid_spec=pltpu.PrefetchScalarGridSpec(
            num_scalar_prefetch=1,
            grid=(t // tm,),
            in_specs=[pl.BlockSpec((tm, d), lambda i, dr: (i, 0)),
                      pl.BlockSpec((tm, 2), lambda i, dr: (i, 0)),
                      pl.BlockSpec(g.shape, lambda i, dr: (0, 0)),
                      pl.BlockSpec(bb.shape, lambda i, dr: (0, 0)),
                      pl.BlockSpec(memory_space=pl.ANY)],
            out_specs=[pl.BlockSpec((tm, d), lambda i, dr: (i, 0)), pl.BlockSpec((tm, d), lambda i, dr: (i, 0))],
            scratch_shapes=[pltpu.VMEM((2, tm, d), F32), pltpu.SemaphoreType.DMA],
        ),
        out_shape=[jax.ShapeDtypeStruct((t, d), F32), jax.ShapeDtypeStruct((t, d), BF16)],
        compiler_params=_cparams(("arbitrary",)),
        name="moe_combine_ln2",
    )(dest_flat, x, wts_t, g, bb, ys)


def _attention_tables(rel_bias, s):
    nc = s // CMP_STRIDE
    nqb = s // QB
    iq = jnp.arange(QB, dtype=I32)[None, None, :]
    x = jnp.arange(nc + (QB // CMP_STRIDE) * nqb, dtype=I32)[:, None]
    rel_c = iq[0] + CMP_STRIDE * (x - nc + 1) - (CMP_LEN - 1)
    jk = jnp.arange(QB, dtype=I32)[None, :, None]
    rel_t = jnp.arange(N_OFF, dtype=I32)[:, None, None] * QB + iq - jk
    jw = jnp.arange(WINDOW + QB, dtype=I32)[:, None]
    rel_w = iq[0] - jw + WINDOW
    rel_w = jnp.where(rel_w < WINDOW, rel_w, -1)
    flat = rel_bias.reshape(-1)
    r4 = NSA_HPG * QB
    cb = _bias_table(flat, jnp.maximum(rel_c, -1), rel_c.shape[0])
    bt = _bias_table(flat, jnp.maximum(rel_t, -1).reshape(N_OFF * QB, QB), QB, delta=True)
    bt = bt.reshape(NSA_GROUPS, N_OFF, QB, r4)
    wb = _bias_table(flat, jnp.maximum(rel_w, -1), WINDOW + QB)
    sat = jnp.repeat(rel_bias[REL_BUCKETS - 1].reshape(NSA_GROUPS, 1, NSA_HPG), QB, axis=2)
    hi = sat.astype(BF16)
    lo = (sat - hi.astype(F32)).astype(BF16)
    b31 = jnp.concatenate([hi, lo, jnp.zeros((NSA_GROUPS, AUG_ROWS - 2, r4), BF16)], axis=1)
    return cb, bt, wb, b31


def _selection_tables(s):
    nc = s // CMP_STRIDE
    ns = s // SLC_BLOCK
    c_start = np.arange(nc) * CMP_STRIDE
    s_start = np.arange(ns) * SLC_BLOCK
    cover = np.maximum(np.minimum(c_start[:, None] + CMP_LEN, s_start[None, :] + SLC_BLOCK)
                       - np.maximum(c_start[:, None], s_start[None, :]), 0).astype(np.float32) / CMP_LEN
    cover[nc - 1] = 0.0
    cover_rev_t = np.ascontiguousarray(cover[::-1].T)
    blockcols = np.where(np.arange(s)[:, None] // SLC_BLOCK == np.arange(ns)[None, :], -MASK_BIG, 0.0)
    return jnp.asarray(cover_rev_t, BF16), jnp.asarray(blockcols, BF16)


def _split_in_weights(w_in_l):
    o = np.cumsum((0, 512, 128, 128, 128, 128, 128, 128, 24, 512, 512, 1024, 1024, 1024, 1024))
    seg = lambda a: w_in_l[:, o[a]:o[a + 1]]
    wb = jnp.concatenate([seg(10), seg(0), seg(1), seg(2), seg(3), seg(4), seg(5), seg(6)], axis=1)
    gpad = jnp.pad(seg(7), ((0, 0), (0, LANES - 24)))
    wf = jnp.concatenate([seg(8), seg(9), seg(11), seg(12), seg(13), gpad], axis=1)
    return wb.astype(BF16), wf.astype(BF16)


def kernel(x, w_in, cmp_pos, cmp_w1, cmp_w2, w_br_a, w_br_b, w_o, ln1_g, ln1_b, w_router, router_bias,
           moe_w_gate, moe_w_up, moe_w_down, ln2_g, ln2_b, rel_bias):
    b, s, d = x.shape
    depth = w_in.shape[0]
    t = b * s
    alpha = (2 * depth) ** 0.25
    g, r, dh = NSA_GROUPS, NSA_HPG, NSA_DH
    nqb = s // QB
    nb16 = s // CMP_STRIDE
    n_sel = min(N_SELECT, s // SLC_BLOCK)

    cb, bt, wb, b31 = _attention_tables(rel_bias, s)
    cov_t, blockcols = _selection_tables(s)
    ret_tabs = _retention_tables(s)
    tri = jnp.asarray(np.triu(np.ones((512, 512), np.float32), 1), BF16)
    wr_t = w_router.T
    rb_col = router_bias.reshape(N_EXPERTS, 1)
    n_blk = t * 2 // MOE_BM + N_EXPERTS
    rows = n_blk * MOE_BM

    xf = x.reshape(t, d)
    xb = xf.astype(BF16)
    for l in range(depth):
        w_b, w_f = _split_in_weights(w_in[l])
        pb =_matmul(xb, w_b, BF16, 1024, 1152)
        pf = _matmul(xb, w_f, F32, 1024, 1408)
        c_qa = RET_HEADS * RET_DV
        c_kv = c_qa + NSA_HEADS * dh

        def kv_heads(slab):
            col = c_kv + slab * g * dh
            return pb[:, col:col + g * dh].reshape(b, s, g, dh).transpose(0, 2, 1, 3)
        x16 = jnp.stack([kv_heads(0), kv_heads(1)]).reshape(2, b, g, nb16, CMP_STRIDE * dh)
        w1 = cmp_w1[l].astype(BF16)
        half = CMP_STRIDE * dh
        w1ab = jnp.concatenate([w1[:, :half], w1[:, half:]], axis=2)
        pos8 = jnp.broadcast_to(cmp_pos[l].reshape(2, 1, CMP_LEN * dh), (2, 8, CMP_LEN * dh)).astype(BF16)
        kvc = _compress(x16, w1ab, _posterm(pos8, w1), cmp_w2[l].astype(BF16))
        gl = (pf[:, 4096:4096 + 24].reshape(b, nqb, QB, g, r, 3).transpose(0, 3, 1, 5, 4, 2)
              .reshape(b, g, nqb, 3, r * QB))
        ya = _nsa_attention(pb, c_qa, c_kv, kvc, gl, cb, bt, wb, cov_t, b31, blockcols, b, s, n_sel)

        yb = _retention(pf, pb, b, s, 0, 512, 0, 1024, ret_tabs)

        x1 = _merge(ya, yb, pf, 2048, 3072, xf, w_br_a[l].astype(BF16), w_br_b[l].astype(BF16),
                    w_o[l].astype(BF16), ln1_g[l].reshape(1, d), ln1_b[l].reshape(1, d), alpha, 512)

        eidx, wts, rank, cnt = _router(x1, wr_t, rb_col, tri, 512)
        counts = cnt[:, 0].astype(I32)
        padded = ((counts + MOE_BM - 1) // MOE_BM) * MOE_BM
        pad_end = jnp.cumsum(padded)
        pad_start = pad_end - padded
        e_iota = jnp.arange(N_EXPERTS, dtype=I32)[:, None, None]
        start_of = jnp.sum(jnp.where(eidx[None] == e_iota, pad_start[:, None, None], 0), axis=0)
        dest = (start_of + rank).reshape(-1)
        n_active = (pad_end[-1] // MOE_BM).astype(I32).reshape(1)
        blk = jnp.minimum(jnp.arange(n_blk, dtype=I32), n_active[0] - 1) * MOE_BM
        blk_e = jnp.minimum(jnp.sum(pad_end[None, :] <= blk[:, None], axis=1), N_EXPERTS - 1).astype(I32)
        blk_all = jnp.arange(n_blk, dtype=I32)
        last_of_expert = jnp.any(pad_end[None, :] == (blk_all[:, None] + 1) * MOE_BM, axis=1)
        zflag = (last_of_expert | (blk_all >= n_active[0])).astype(I32)
        xs = _dispatch(dest, zflag, x1, rows, 512)
        ys = _experts(blk_e, n_active, xs, moe_w_gate[l].astype(BF16), moe_w_up[l].astype(BF16),
                      moe_w_down[l].astype(BF16))
        xf, xb = _combine(dest, x1, wts.T, ln2_g[l].reshape(1, d), ln2_b[l].reshape(1, d), ys, alpha, 256)
    return xf.reshape(b, s, d)
```

```python
import functools
import math

import jax
import jax.numpy as jnp
import numpy as np
from jax import lax
from jax.experimental import pallas as pl
from jax.experimental.pallas import tpu as pltpu

F32 = jnp.float32
BF16 = jnp.bfloat16
I32 = jnp.int32

NSA_HEADS = 8
NSA_GROUPS = 2
NSA_HPG = 4
NSA_DH = 64
CMP_LEN = 32
CMP_STRIDE = 16
SLC_BLOCK = 64
N_SELECT = 16
WINDOW = 512
QB = 256
FORCE_SCORE = 1.0e4
RET_HEADS = 8
RET_DK = 64
RET_DV = 128
RET_CHUNK = 128
ROPE_BASE = 10000.0
REL_BUCKETS = 32
REL_MAX_EXACT = 16
REL_MAX_DIST = 2048
N_EXPERTS = 16
N_GROUPS = 4
EPG = 4
D_EXPERT = 512
LN_EPS = 1e-5
GN_EPS = 1e-5
NEG_INF = -1e30

LANES = 128
VMEM_LIMIT = 56 * 1024 * 1024

SEL_TK = 512
SEL_CW = 512
AUG_ROWS = 16
MASK_BIG = 2.0 ** 100
SAT_REL = 1513
N_OFF = -(-(SAT_REL + QB - 1) // QB) + 1
MOE_BM = 256
ROW_DMA_UNROLL = 8


def _cparams(sem):
    return pltpu.CompilerParams(dimension_semantics=sem, vmem_limit_bytes=VMEM_LIMIT)


def _dot(a, b):
    return jnp.dot(a, b, preferred_element_type=F32)


def _dot_nt(a, b):
    return lax.dot_general(a, b, (((1,), (1,)), ((), ())), preferred_element_type=F32)


def _mm_kernel(a_ref, b_ref, o_ref):
    o_ref[...] = _dot(a_ref[...], b_ref[...]).astype(o_ref.dtype)


def _matmul(a, b, out_dtype, tm, tn):
    m, k = a.shape
    n = b.shape[1]
    return pl.pallas_call(
        _mm_kernel,
        grid=(n // tn, m // tm),
        in_specs=[pl.BlockSpec((tm, k), lambda j, i: (i, 0)),
                  pl.BlockSpec((k, tn), lambda j, i: (0, j))],
        out_specs=pl.BlockSpec((tm, tn), lambda j, i: (i, j)),
        out_shape=jax.ShapeDtypeStruct((m, n), out_dtype),
        compiler_params=_cparams(("arbitrary", "arbitrary")),
        name="proj_matmul",
    )(a, b)


def _bias_kernel(tab_ref, rel_ref, o_ref, *, delta):
    rel = rel_ref[...]
    n = jnp.maximum(rel, 0)
    nf = jnp.maximum(n, 1).astype(F32)
    large = REL_MAX_EXACT + (jnp.log(nf / REL_MAX_EXACT) / math.log(REL_MAX_DIST / REL_MAX_EXACT)
                             * (REL_BUCKETS - REL_MAX_EXACT)).astype(I32)
    large = jnp.minimum(large, REL_BUCKETS - 1)
    bucket = jnp.where(n < REL_MAX_EXACT, n, large)
    eqs = [bucket == k for k in range(REL_BUCKETS)]
    for h in range(NSA_HEADS):
        acc = jnp.full(rel.shape, NEG_INF, F32)
        for k in range(REL_BUCKETS):
            acc = jnp.where(eqs[k], tab_ref[k * NSA_HEADS + h], acc)
        if delta:
            acc = acc - tab_ref[(REL_BUCKETS - 1) * NSA_HEADS + h]
        hh = h % NSA_HPG
        o_ref[h // NSA_HPG, :, hh * QB:(hh + 1) * QB] = jnp.where(rel >= 0, acc, NEG_INF)


def _bias_table(rel_bias_flat, rel, tr, delta=False):
    r, c = rel.shape
    return pl.pallas_call(
        functools.partial(_bias_kernel, delta=delta),
        grid_spec=pltpu.PrefetchScalarGridSpec(
            num_scalar_prefetch=1,
            grid=(r // tr,),
            in_specs=[pl.BlockSpec((tr, c), lambda i, tab: (i, 0))],
            out_specs=pl.BlockSpec((NSA_GROUPS, tr, NSA_HPG * c), lambda i, tab: (0, i, 0)),
        ),
        out_shape=jax.ShapeDtypeStruct((NSA_GROUPS, r, NSA_HPG * c), F32),
        compiler_params=_cparams(("arbitrary",)),
        name="t5_bias_table",
    )(rel_bias_flat, rel)


def _compress_kernel(x_ref, w1_ref, pos_ref, w2_ref, o_ref):
    ab = _dot(x_ref[...], w1_ref[...])
    nb = ab.shape[0]
    nxt = pltpu.roll(ab, nb - 1, axis=0)
    nxt = pltpu.roll(nxt, NSA_DH, axis=1)
    pre = (ab + nxt)[:, :NSA_DH] + pos_ref[...]
    hid = jax.nn.gelu(pre)
    out = _dot(hid.astype(BF16), w2_ref[...]).astype(BF16)
    rows = lax.broadcasted_iota(I32, (nb, nb), 0)
    cols = lax.broadcasted_iota(I32, (nb, nb), 1)
    flip = jnp.where(rows + cols == nb - 1, 1.0, 0.0).astype(BF16)
    o_ref[...] = _dot(flip, out).astype(o_ref.dtype)


def _compress(x16, w1ab, posterm, w2):
    _, b, g, nb, wd = x16.shape
    return pl.pallas_call(
        _compress_kernel,
        grid=(2, b, g),
        in_specs=[pl.BlockSpec((None, None, None, nb, wd), lambda s, bi, gi: (s, bi, gi, 0, 0)),
                  pl.BlockSpec((None, wd, 2 * NSA_DH), lambda s, bi, gi: (s, 0, 0)),
                  pl.BlockSpec((None, 1, NSA_DH), lambda s, bi, gi: (s, 0, 0)),
                  pl.BlockSpec((None, NSA_DH, NSA_DH), lambda s, bi, gi: (s, 0, 0))],
        out_specs=pl.BlockSpec((None, None, None, nb, NSA_DH), lambda s, bi, gi: (s, bi, gi, 0, 0)),
        out_shape=jax.ShapeDtypeStruct((2, b, g, nb, NSA_DH), BF16),
        compiler_params=_cparams(("arbitrary",) * 3),
        name="nsa_compress",
    )(x16, w1ab, posterm, w2)


def _posterm_kernel(p_ref, w_ref, o_ref):
    o_ref[...] = _dot(p_ref[...], w_ref[...])[0:1]


def _posterm(pos8, w1):
    return pl.pallas_call(
        _posterm_kernel,
        grid=(2,),
        in_specs=[pl.BlockSpec((None, 8, pos8.shape[2]), lambda s: (s, 0, 0)),
                  pl.BlockSpec((None, w1.shape[1], NSA_DH), lambda s: (s, 0, 0))],
        out_specs=pl.BlockSpec((None, 1, NSA_DH), lambda s: (s, 0, 0)),
        out_shape=jax.ShapeDtypeStruct((2, 1, NSA_DH), F32),
        name="nsa_posterm",
    )(pos8, w1)


def _eye(n, scale=1.0):
    rows = lax.broadcasted_iota(I32, (n, n), 0)
    cols = lax.broadcasted_iota(I32, (n, n), 1)
    return jnp.where(rows == cols, scale, 0.0).astype(BF16)


def _nsa_kernel(q_ref, qn_ref, kc_ref, vcin_ref, ksin_ref, vsin_ref, kwin_ref, vwin_ref, gate_ref,
                cb_ref, bt_ref, wb_ref, cov_ref, b31_ref, bcols_ref, o_ref,
                s_ref, ks_ref, vs_ref, kw_ref, vw_ref, vc_ref, qt_ref, *, n_sel):
    gi = pl.program_id(1)
    i = pl.program_id(2)
    q0 = i * QB
    r4 = NSA_HPG * QB
    ns = cov_ref.shape[0]
    eye_dh = _eye(NSA_DH)

    def group_lanes(x):
        return jnp.where(gi == 0, x[:, 0:NSA_DH], x[:, NSA_DH:2 * NSA_DH])

    def q_transposed(ref):
        eye_q = _eye(NSA_DH, NSA_DH ** -0.5)
        qb = ref[...]
        return jnp.concatenate([_dot_nt(eye_q, qb[:, h * NSA_DH:(h + 1) * NSA_DH]) for h in range(NSA_HPG)],
                               axis=1).astype(BF16)

    @pl.when(i == 0)
    def _():
        s_len = ksin_ref.shape[0]
        ones_rows = jnp.where(lax.broadcasted_iota(I32, (AUG_ROWS, SEL_TK), 0) == 0, 1.0, 0.0).astype(BF16)
        ones_rows_qb = jnp.where(lax.broadcasted_iota(I32, (AUG_ROWS, QB), 0) == 0, 1.0, 0.0).astype(BF16)
        tail = jnp.where(lax.broadcasted_iota(I32, (SEL_TK, NSA_DH), 1) < 2, 1.0, 0.0).astype(BF16)
        depth = ks_ref.shape[-1]
        vc_ref[...] = _dot_nt(eye_dh, vcin_ref[...]).astype(BF16)
        qt_ref[...] = q_transposed(q_ref)
        kw_ref[0:WINDOW, :] = jnp.zeros((WINDOW, NSA_DH), BF16)
        for u in range(WINDOW // QB):
            vw_ref[u] = jnp.zeros(vw_ref.shape[1:], BF16)

        def chunk(c, _):
            r0 = pl.multiple_of(c * SEL_TK, SEL_TK)
            rows = pl.ds(r0, SEL_TK)
            k_aug = jnp.concatenate([group_lanes(ksin_ref[rows, :]), tail], axis=1)
            ks_ref[rows, 0:ns] = bcols_ref[rows, :]
            ks_ref[rows, ns:ns + 2 * NSA_DH] = k_aug
            if depth > ns + 2 * NSA_DH:
                ks_ref[rows, ns + 2 * NSA_DH:depth] = jnp.zeros((SEL_TK, depth - ns - 2 * NSA_DH), BF16)
            v_t = _dot_nt(eye_dh, group_lanes(vsin_ref[rows, :])).astype(BF16)
            vs_ref[c] = jnp.concatenate([v_t, ones_rows], axis=0)
            kw_ref[pl.ds(WINDOW + r0, SEL_TK), :] = group_lanes(kwin_ref[rows, :])
            vw_t = _dot_nt(eye_dh, group_lanes(vwin_ref[rows, :])).astype(BF16)
            for u in range(SEL_TK // QB):
                vw_ref[WINDOW // QB + c * (SEL_TK // QB) + u] = jnp.concatenate(
                    [vw_t[:, u * QB:(u + 1) * QB], ones_rows_qb], axis=0)
            return 0

        lax.fori_loop(0, s_len // SEL_TK, chunk, 0)

    q = qt_ref[...]

    step_rows = QB // CMP_STRIDE
    cb = cb_ref[pl.ds(pl.multiple_of(i * step_rows, step_rows), kc_ref.shape[0]), :]
    s_c = _dot(kc_ref[...], q) + cb
    m_c = jnp.max(s_c, axis=0, keepdims=True)
    p_c = jnp.where(cb > 0.5 * NEG_INF, jnp.exp(s_c - m_c), 0.0)
    l_c = jnp.sum(p_c, axis=0, keepdims=True)
    p_c = p_c / jnp.maximum(l_c, 1e-30)
    o_c = _dot(vc_ref[...], p_c.astype(BF16))

    psum = p_c[:, 0:QB] + p_c[:, QB:2 * QB] + p_c[:, 2 * QB:3 * QB] + p_c[:, 3 * QB:4 * QB]
    p_hi = psum.astype(BF16)
    p_lo = (psum - p_hi.astype(F32)).astype(BF16)
    cov = cov_ref[...]
    imp_t = _dot(cov, p_hi) + _dot(cov, p_lo)
    ns = imp_t.shape[0]

    wk = WINDOW + QB
    kwin = kw_ref[pl.ds(pl.multiple_of(q0, QB), wk), :]
    s_w = _dot(kwin, q) + wb_ref[...]
    kpos = q0 - WINDOW + lax.broadcasted_iota(I32, (wk, r4), 0)
    s_w = jnp.where(kpos >= 0, s_w, NEG_INF)
    m_w = jnp.max(s_w, axis=0, keepdims=True)
    p_w = jnp.exp(s_w - m_w).astype(BF16)
    acc_w = jnp.zeros((NSA_DH + AUG_ROWS, r4), F32)
    for u in range(wk // QB):
        acc_w = acc_w + _dot(vw_ref[i + u], p_w[u * QB:(u + 1) * QB])
    o_w = acc_w[0:NSA_DH] / acc_w[NSA_DH:NSA_DH + 1]

    jidx = lax.broadcasted_iota(I32, (ns, QB), 0)
    t = q0 + lax.broadcasted_iota(I32, (ns, QB), 1)
    cur = jnp.right_shift(t, 6)
    eligible = jidx <= cur
    forced = (jidx == 0) | (jidx == cur) | (jidx == cur - 1)
    score = jnp.where(eligible, jnp.where(forced, -2.0, imp_t), -1.0)
    sel = jnp.where(forced, 1.0, 0.0)
    for _ in range(n_sel - 3):
        mx = jnp.max(score, axis=0, keepdims=True)
        first = jnp.min(jnp.where(score == mx, jidx, ns), axis=0, keepdims=True)
        pick = jidx == first
        sel = jnp.where(pick, 1.0, sel)
        score = jnp.where(pick, -2.0, score)
    notsel = jnp.where(eligible, 1.0 - sel, 1.0).astype(BF16)

    depth = ks_ref.shape[-1]
    pad_rows = depth - ns - NSA_DH - AUG_ROWS
    qa = jnp.concatenate([jnp.concatenate([notsel] * NSA_HPG, axis=1), q, b31_ref[...],
                          jnp.zeros((pad_rows, r4), BF16)], axis=0)
    sub = SEL_TK // QB
    n_tiles = i // sub + 1
    n_far = jnp.maximum(i - N_OFF + 2, 0) // sub

    def put_scores(jt, slot):
        k0 = pl.multiple_of(jt * SEL_TK, SEL_TK)
        s_ref[slot] = _dot(ks_ref[pl.ds(k0, SEL_TK), :], qa)

    def absorb(jt, slot, carry, near):
        m_prev, acc = carry
        s = s_ref[slot]
        if near:
            offs = [jnp.clip(i - sub * jt - u, 0, N_OFF - 1) for u in range(sub)]
            s = s + jnp.concatenate([bt_ref[o] for o in offs], axis=0)
        m_new = jnp.maximum(m_prev, jnp.max(s, axis=0, keepdims=True))
        alpha = jnp.exp(m_prev - m_new)
        p = jnp.exp(s - m_new).astype(BF16)
        return m_new, alpha * acc + _dot(vs_ref[jt], p)

    def pair_body(pp, carry, near):
        ja = 2 * pp
        put_scores(ja + 1, 1)
        carry = absorb(ja, 0, carry, near)
        put_scores(jnp.minimum(ja + 2, last_tile), 0)
        return absorb(ja + 1, 1, carry, near)

    n_pairs = (n_tiles + 1) // 2
    last_tile = 2 * n_pairs - 1
    far_pairs = n_far // 2
    put_scores(0, 0)
    carry = (jnp.full((1, r4), -1e38, F32), jnp.zeros((NSA_DH + AUG_ROWS, r4), F32))
    carry = lax.fori_loop(0, far_pairs, functools.partial(pair_body, near=False), carry)
    _, acc_s = lax.fori_loop(far_pairs, n_pairs, functools.partial(pair_body, near=True), carry)
    o_s = acc_s[0:NSA_DH] / acc_s[NSA_DH:NSA_DH + 1]

    gates = jax.nn.sigmoid(gate_ref[...])
    o = (gates[0:1] * o_c + gates[1:2] * o_s + gates[2:3] * o_w).astype(BF16)
    qt_ref[...] = q_transposed(qn_ref)
    eye_l = _eye(LANES)
    o_ref[...] = jnp.concatenate(
        [jnp.concatenate([_dot_nt(eye_l, o[:, h * QB + u * LANES:h * QB + (u + 1) * LANES])
                          for u in range(QB // LANES)], axis=0) for h in range(NSA_HPG)],
        axis=1).astype(o_ref.dtype)


def _nsa_attention(pb, c_qa, c_kv, kvc, gates, cb, bt, wb, cov_t, b31, blockcols, b, s, n_sel):
    g, dh = NSA_GROUPS, NSA_DH
    r4 = NSA_HPG * QB
    nqb = s // QB
    nc = kvc.shape[3]
    ns = cov_t.shape[0]
    depth = -(-(ns + 2 * dh) // LANES) * LANES
    n_tiles = s // SEL_TK
    assert n_tiles % 2 == 0, "the selected sweep consumes key tiles in pairs"
    qw = NSA_HPG * dh
    once = pl.Buffered(1)
    slab = lambda k: pl.BlockSpec((s, 2 * dh), lambda bi, gi, i: (bi, c_kv // (2 * dh) + k), pipeline_mode=once)
    per_g = lambda a: pl.BlockSpec((None,) + a.shape[1:], lambda bi, gi, i: (gi,) + (0,) * (a.ndim - 1),
                                   pipeline_mode=once)
    const = lambda a: pl.BlockSpec(a.shape, lambda bi, gi, i: (0,) * a.ndim, pipeline_mode=once)
    return pl.pallas_call(
        functools.partial(_nsa_kernel, n_sel=n_sel),
        grid=(b, g, nqb),
        in_specs=[
            pl.BlockSpec((QB, qw), lambda bi, gi, i: (bi * nqb + i, c_qa // qw + gi)),
            pl.BlockSpec((QB, qw), lambda bi, gi, i: (bi * nqb + jnp.minimum(i + 1, nqb - 1), c_qa // qw + gi)),
            pl.BlockSpec((None, None, None, nc, dh), lambda bi, gi, i: (0, bi, gi, 0, 0)),
            pl.BlockSpec((None, None, None, nc, dh), lambda bi, gi, i: (1, bi, gi, 0, 0)),
            slab(2), slab(3), slab(4), slab(5),
            pl.BlockSpec((None, None, None, 3, r4), lambda bi, gi, i: (bi, gi, i, 0, 0)),
            per_g(cb), per_g(bt), per_g(wb), const(cov_t), per_g(b31), const(blockcols),
        ],
        out_specs=pl.BlockSpec((QB, qw), lambda bi, gi, i: (bi * nqb + i, gi)),
        out_shape=jax.ShapeDtypeStruct((b * s, g * qw), BF16),
        scratch_shapes=[pltpu.VMEM((2, SEL_TK, r4), F32),
                        pltpu.VMEM((s, depth), BF16),
                        pltpu.VMEM((n_tiles, dh + AUG_ROWS, SEL_TK), BF16),
                        pltpu.VMEM((s + WINDOW, dh), BF16),
                        pltpu.VMEM(((s + WINDOW) // QB, dh + AUG_ROWS, QB), BF16),
                        pltpu.VMEM((dh, nc), BF16),
                        pltpu.VMEM((dh, r4), BF16)],
        compiler_params=_cparams(("arbitrary",) * 3),
        name="nsa_attention",
    )(pb, pb, kvc, kvc, pb, pb, pb, pb, gates, cb, bt, wb, cov_t, b31, blockcols)


def _ret_kernel(q_ref, k_ref, v_ref, g_ref, cos_ref, sin_ref, dm_ref, qd_ref, kd_ref, cd_ref,
                o_ref, state_ref):
    n = pl.program_id(1)

    @pl.when(n == 0)
    def _():
        state_ref[...] = jnp.zeros_like(state_ref)

    lane = lax.broadcasted_iota(I32, (RET_CHUNK, LANES), 1)
    first_half = (lane & (RET_DK - 1)) < (RET_DK // 2)
    cosv = cos_ref[...]
    sinv = sin_ref[...]

    def rot(x):
        partner = jnp.where(first_half, pltpu.roll(x, LANES - RET_DK // 2, axis=1),
                            pltpu.roll(x, RET_DK // 2, axis=1))
        return x * cosv + partner * sinv

    for hp in range(RET_HEADS // 2):
        ls = slice(hp * LANES, (hp + 1) * LANES)
        q2 = rot(q_ref[:, ls])
        k2 = rot(k_ref[:, ls]) * (RET_DK ** -0.5)
        qdec = (q2 * qd_ref[:, ls]).astype(BF16)
        kdec_t = (k2 * kd_ref[:, ls]).T.astype(BF16)
        q2b = q2.astype(BF16)
        k2b = k2.astype(BF16)
        for e in range(2):
            h = 2 * hp + e
            hs = slice(e * RET_DK, (e + 1) * RET_DK)
            vs = slice(h * RET_DV, (h + 1) * RET_DV)
            v = v_ref[:, vs]
            st = state_ref[h]
            sc = _dot_nt(q2b[:, hs], k2b[:, hs]) * dm_ref[h]
            y = _dot(sc.astype(BF16), v) + _dot(qdec[:, hs], st.astype(BF16))
            state_ref[h] = cd_ref[h] * st + _dot(kdec_t[hs, :], v)
            mu = jnp.mean(y, axis=-1, keepdims=True)
            var = jnp.mean(jnp.square(y - mu), axis=-1, keepdims=True)
            yn = (y - mu) * lax.rsqrt(var + GN_EPS)
            o_ref[:, vs] = (jax.nn.silu(g_ref[:, vs]) * yn).astype(o_ref.dtype)


def _retention(pf, pb, b, s, col_q, col_k, col_v, col_g, tabs):
    cosf, sinf, dm, qd, kd, cd = tabs
    n = s // RET_CHUNK
    t = b * s
    c = RET_CHUNK
    wqk = RET_HEADS * RET_DK
    wv = RET_HEADS * RET_DV
    row = lambda bi, ni: bi * n + ni
    full = lambda a: pl.BlockSpec(a.shape, lambda bi, ni: (0,) * a.ndim)
    return pl.pallas_call(
        _ret_kernel,
        grid=(b, n),
        in_specs=[
            pl.BlockSpec((c, wqk), lambda bi, ni: (row(bi, ni), col_q // wqk)),
            pl.BlockSpec((c, wqk), lambda bi, ni: (row(bi, ni), col_k // wqk)),
            pl.BlockSpec((c, wv), lambda bi, ni: (row(bi, ni), col_v // wv)),
            pl.BlockSpec((c, wv), lambda bi, ni: (row(bi, ni), col_g // wv)),
            pl.BlockSpec((c, LANES), lambda bi, ni: (ni, 0)),
            pl.BlockSpec((c, LANES), lambda bi, ni: (ni, 0)),
            full(dm), full(qd), full(kd), full(cd),
        ],
        out_specs=pl.BlockSpec((c, wv), lambda bi, ni: (row(bi, ni), 0)),
        out_shape=jax.ShapeDtypeStruct((t, wv), BF16),
        scratch_shapes=[pltpu.VMEM((RET_HEADS, RET_DK, RET_DV), F32)],
        compiler_params=_cparams(("arbitrary",) * 2),
        name="retention",
    )(pf, pf, pb, pf, cosf, sinf, dm, qd, kd, cd)


def _retention_tables(s):
    half = RET_DK // 2
    inv = ROPE_BASE ** (-jnp.arange(half, dtype=F32) * 2.0 / RET_DK)
    ang = jnp.arange(s, dtype=F32)[:, None] * inv[None, :]
    cos, sin = jnp.cos(ang), jnp.sin(ang)
    cosf = jnp.concatenate([cos, cos, cos, cos], axis=1)
    sinf = jnp.concatenate([-sin, sin, -sin, sin], axis=1)
    lg = jnp.log(1.0 - 2.0 ** (-5.0 - jnp.arange(RET_HEADS, dtype=F32)))
    i = jnp.arange(RET_CHUNK, dtype=F32)
    diff = i[:, None] - i[None, :]
    dm = jnp.where(diff >= 0, jnp.exp(jnp.maximum(diff, 0.0)[None] * lg[:, None, None]), 0.0)
    kdec = jnp.exp((RET_CHUNK - 1.0 - i)[None, :] * lg[:, None])
    qdec = jnp.exp((i + 1.0)[None, :] * lg[:, None])
    expand = lambda a: jnp.repeat(a.T, RET_DK, axis=1)
    cd = jnp.broadcast_to(jnp.exp(RET_CHUNK * lg)[:, None, None], (RET_HEADS, 1, LANES))
    return cosf, sinf, dm, expand(qdec), expand(kdec), cd


def _layer_norm(z, g, b):
    mu = jnp.mean(z, axis=-1, keepdims=True)
    var = jnp.mean(jnp.square(z - mu), axis=-1, keepdims=True)
    return (z - mu) * lax.rsqrt(var + LN_EPS) * g + b


def _merge_kernel(ya_ref, yb_ref, ma_ref, mb_ref, x_ref, wa_ref, wb_ref, wo_ref, g_ref, b_ref,
                  o_ref, *, alpha):
    merged = (jax.nn.sigmoid(ma_ref[...]) * _dot(ya_ref[...], wa_ref[...])
              + jax.nn.sigmoid(mb_ref[...]) * _dot(yb_ref[...], wb_ref[...]))
    z = alpha * x_ref[...] + _dot(merged.astype(BF16), wo_ref[...])
    o_ref[...] = _layer_norm(z, g_ref[...], b_ref[...])


def _merge(ya, yb, pf, col_ma, col_mb, x, wa, wb, wo, g, bb, alpha, tm):
    t, d = x.shape
    full = lambda a: pl.BlockSpec(a.shape, lambda i: (0,) * a.ndim)
    return pl.pallas_call(
        functools.partial(_merge_kernel, alpha=alpha),
        grid=(t // tm,),
        in_specs=[pl.BlockSpec((tm, ya.shape[1]), lambda i: (i, 0)),
                  pl.BlockSpec((tm, yb.shape[1]), lambda i: (i, 0)),
                  pl.BlockSpec((tm, d), lambda i: (i, col_ma // d)),
                  pl.BlockSpec((tm, d), lambda i: (i, col_mb // d)),
                  pl.BlockSpec((tm, d), lambda i: (i, 0)),
                  full(wa), full(wb), full(wo), full(g), full(bb)],
        out_specs=pl.BlockSpec((tm, d), lambda i: (i, 0)),
        out_shape=jax.ShapeDtypeStruct((t, d), F32),
        compiler_params=_cparams(("arbitrary",)),
        name="merge_ln1",
    )(ya, yb, pf, pf, x, wa, wb, wo, g, bb)


def _router_kernel(x_ref, wr_ref, rb_ref, tri_ref, eidx_ref, wts_ref, rank_ref, cnt_ref, carry_ref):
    step = pl.program_id(0)

    @pl.when(step == 0)
    def _():
        carry_ref[...] = jnp.zeros_like(carry_ref)

    logits = lax.dot_general(wr_ref[...], x_ref[...], (((1,), (1,)), ((), ())),
                             precision=lax.Precision.HIGHEST, preferred_element_type=F32)
    tm = logits.shape[1]
    mx = jnp.max(logits, axis=0, keepdims=True)
    ex = jnp.exp(logits - mx)
    probs = ex / jnp.sum(ex, axis=0, keepdims=True)
    sel = probs + rb_ref[...]
    rows = [sel[e:e + 1] for e in range(N_EXPERTS)]
    gscore = []
    for gi in range(N_GROUPS):
        r = rows[gi * EPG:(gi + 1) * EPG]
        best2 = None
        for a in range(EPG):
            for c in range(a + 1, EPG):
                pair = r[a] + r[c]
                best2 = pair if best2 is None else jnp.maximum(best2, pair)
        gscore.append(best2)
    best = jnp.zeros((1, tm), I32)
    bscore = gscore[0]
    for gi in range(1, N_GROUPS):
        better = gscore[gi] > bscore
        best = jnp.where(better, gi, best)
        bscore = jnp.where(better, gscore[gi], bscore)
    ing = []
    for a in range(EPG):
        v = rows[a]
        for gi in range(1, N_GROUPS):
            v = jnp.where(best == gi, rows[gi * EPG + a], v)
        ing.append(v)
    i1 = jnp.zeros((1, tm), I32)
    v1 = ing[0]
    for a in range(1, EPG):
        better = ing[a] > v1
        i1 = jnp.where(better, a, i1)
        v1 = jnp.where(better, ing[a], v1)
    i2 = jnp.full((1, tm), -1, I32)
    v2 = jnp.full((1, tm), -jnp.inf, F32)
    for a in range(EPG):
        better = (i1 != a) & (ing[a] > v2)
        i2 = jnp.where(better, a, i2)
        v2 = jnp.where(better, ing[a], v2)
    e0 = best * EPG + i1
    e1 = best * EPG + i2
    eio = lax.broadcasted_iota(I32, (N_EXPERTS, tm), 0)
    oh0 = (eio == e0).astype(F32)
    oh1 = (eio == e1).astype(F32)
    p0 = jnp.sum(oh0 * probs, axis=0, keepdims=True)
    p1 = jnp.sum(oh1 * probs, axis=0, keepdims=True)
    den = p0 + p1
    eidx_ref[0:1, :] = e0
    eidx_ref[1:2, :] = e1
    wts_ref[0:1, :] = p0 / den
    wts_ref[1:2, :] = p1 / den
    tot = oh0 + oh1
    prefix = _dot(tot.astype(BF16), tri_ref[...]) + carry_ref[:, 0:1]
    rank_ref[0:1, :] = jnp.sum(oh0 * prefix, axis=0, keepdims=True).astype(I32)
    rank_ref[1:2, :] = jnp.sum(oh1 * prefix, axis=0, keepdims=True).astype(I32)
    carry_ref[...] = carry_ref[...] + jnp.sum(tot, axis=1, keepdims=True)
    cnt_ref[...] = carry_ref[...]


def _router(x, wr_t, rb_col, tri, tm):
    t, d = x.shape
    out2 = lambda dt: jax.ShapeDtypeStruct((2, t), dt)
    row2 = pl.BlockSpec((2, tm), lambda i: (0, i))
    return pl.pallas_call(
        _router_kernel,
        grid=(t // tm,),
        in_specs=[pl.BlockSpec((tm, d), lambda i: (i, 0)),
                  pl.BlockSpec(wr_t.shape, lambda i: (0, 0)),
                  pl.BlockSpec(rb_col.shape, lambda i: (0, 0)),
                  pl.BlockSpec(tri.shape, lambda i: (0, 0))],
        out_specs=[row2, row2, row2, pl.BlockSpec((N_EXPERTS, LANES), lambda i: (0, 0))],
        out_shape=[out2(I32), out2(F32), out2(I32), jax.ShapeDtypeStruct((N_EXPERTS, LANES), F32)],
        scratch_shapes=[pltpu.VMEM((N_EXPERTS, LANES), F32)],
        compiler_params=_cparams(("arbitrary",)),
        name="moe_router",
    )(x, wr_t, rb_col, tri)


def _dispatch_kernel(dest_ref, zflag_ref, x_ref, xs_ref, zero_ref, sem, zsem, *, t_total):
    step = pl.program_id(0)
    tm = x_ref.shape[0]

    @pl.when(step == 0)
    def _():
        zero_ref[...] = jnp.zeros_like(zero_ref)

        def zero_block(bi, _):
            @pl.when(zflag_ref[bi] == 1)
            def _():
                r0 = pl.multiple_of(bi * MOE_BM, MOE_BM)
                cp = pltpu.make_async_copy(zero_ref, xs_ref.at[pl.ds(r0, MOE_BM)], zsem)
                cp.start()
                cp.wait()
            return 0

        lax.fori_loop(0, zflag_ref.shape[0], zero_block, 0)

    def copy(r, slot):
        dst = dest_ref[slot * t_total + step * tm + r]
        return pltpu.make_async_copy(x_ref.at[pl.ds(r, 1)], xs_ref.at[pl.ds(dst, 1)], sem)

    def issue(r, _):
        copy(r, 0).start()
        copy(r, 1).start()
        return 0

    def drain(r, _):
        copy(r, 0).wait()
        copy(r, 1).wait()
        return 0

    lax.fori_loop(0, tm, issue, 0, unroll=ROW_DMA_UNROLL)
    lax.fori_loop(0, tm, drain, 0, unroll=ROW_DMA_UNROLL)


def _dispatch(dest_flat, zflag, x, rows, tm):
    t, d = x.shape
    return pl.pallas_call(
        functools.partial(_dispatch_kernel, t_total=t),
        grid_spec=pltpu.PrefetchScalarGridSpec(
            num_scalar_prefetch=2,
            grid=(t // tm,),
            in_specs=[pl.BlockSpec((tm, d), lambda i, dr, zf: (i, 0))],
            out_specs=pl.BlockSpec(memory_space=pl.ANY),
            scratch_shapes=[pltpu.VMEM((MOE_BM, d), x.dtype), pltpu.SemaphoreType.DMA,
                            pltpu.SemaphoreType.DMA],
        ),
        out_shape=jax.ShapeDtypeStruct((rows, d), x.dtype),
        compiler_params=pltpu.CompilerParams(dimension_semantics=("arbitrary",), has_side_effects=True,
                                             vmem_limit_bytes=VMEM_LIMIT),
        name="moe_dispatch",
    )(dest_flat, zflag, x)


def _expert_kernel(be_ref, na_ref, x_ref, wg_ref, wu_ref, wd_ref, o_ref):
    active = pl.program_id(0) < na_ref[0]

    @pl.when(active)
    def _():
        xb = x_ref[...].astype(BF16)
        gate = _dot(xb, wg_ref[...])
        up = _dot(xb, wu_ref[...])
        hid = (jax.nn.silu(gate) * up).astype(BF16)
        o_ref[...] = _dot(hid, wd_ref[...])

    @pl.when(jnp.logical_not(active))
    def _():
        o_ref[...] = jnp.zeros_like(o_ref)


def _experts(blk_e, n_active, xs, wg, wu, wd):
    rows, d = xs.shape
    nb = rows // MOE_BM
    act = lambda i, be, na: jnp.minimum(i, na[0] - 1)
    return pl.pallas_call(
        _expert_kernel,
        grid_spec=pltpu.PrefetchScalarGridSpec(
            num_scalar_prefetch=2,
            grid=(nb,),
            in_specs=[pl.BlockSpec((MOE_BM, d), lambda i, be, na: (act(i, be, na), 0)),
                      pl.BlockSpec((None, d, D_EXPERT), lambda i, be, na: (be[i], 0, 0)),
                      pl.BlockSpec((None, d, D_EXPERT), lambda i, be, na: (be[i], 0, 0)),
                      pl.BlockSpec((None, D_EXPERT, d), lambda i, be, na: (be[i], 0, 0))],
            out_specs=pl.BlockSpec((MOE_BM, d), lambda i, be, na: (i, 0)),
        ),
        out_shape=jax.ShapeDtypeStruct((rows, d), F32),
        compiler_params=_cparams(("arbitrary",)),
        name="moe_experts",
    )(blk_e, n_active, xs, wg, wu, wd)


def _combine_kernel(dest_ref, x_ref, w_ref, g_ref, b_ref, ys_ref, o_ref, ob_ref, buf_ref, sem, *, alpha, t_total):
    step = pl.program_id(0)
    tm = x_ref.shape[0]

    def copy(r, slot):
        src = dest_ref[slot * t_total + step * tm + r]
        return pltpu.make_async_copy(ys_ref.at[pl.ds(src, 1)], buf_ref.at[slot, pl.ds(r, 1)], sem)

    def issue(r, _):
        copy(r, 0).start()
        copy(r, 1).start()
        return 0

    def drain(r, _):
        copy(r, 0).wait()
        copy(r, 1).wait()
        return 0

    lax.fori_loop(0, tm, issue, 0, unroll=ROW_DMA_UNROLL)
    lax.fori_loop(0, tm, drain, 0, unroll=ROW_DMA_UNROLL)
    w = w_ref[...]
    z = alpha * x_ref[...] + w[:, 0:1] * buf_ref[0] + w[:, 1:2] * buf_ref[1]
    y = _layer_norm(z, g_ref[...], b_ref[...])
    o_ref[...] = y
    ob_ref[...] = y.astype(BF16)


def _combine(dest_flat, x, wts_t, g, bb, ys, alpha, tm):
    t, d = x.shape
    return pl.pallas_call(
        functools.partial(_combine_kernel, alpha=alpha, t_total=t),
        grid_spec=pltpu.PrefetchScalarGridSpec(
            num_scalar_prefetch=1,
            grid=(t // tm,),
            in_specs=[pl.BlockSpec((tm, d), lambda i, dr: (i, 0)),
                      pl.BlockSpec((tm, 2), lambda i, dr: (i, 0)),
                      pl.BlockSpec(g.shape, lambda i, dr: (0, 0)),
                      pl.BlockSpec(bb.shape, lambda i, dr: (0, 0)),
                      pl.BlockSpec(memory_space=pl.ANY)],
            out_specs=[pl.BlockSpec((tm, d), lambda i, dr: (i, 0)), pl.BlockSpec((tm, d), lambda i, dr: (i, 0))],
            scratch_shapes=[pltpu.VMEM((2, tm, d), F32), pltpu.SemaphoreType.DMA],
        ),
        out_shape=[jax.ShapeDtypeStruct((t, d), F32), jax.ShapeDtypeStruct((t, d), BF16)],
        compiler_params=_cparams(("arbitrary",)),
        name="moe_combine_ln2",
    )(dest_flat, x, wts_t, g, bb, ys)


def _attention_tables(rel_bias, s):
    nc = s // CMP_STRIDE
    nqb = s // QB
    iq = jnp.arange(QB, dtype=I32)[None, None, :]
    x = jnp.arange(nc + (QB // CMP_STRIDE) * nqb, dtype=I32)[:, None]
    rel_c = iq[0] + CMP_STRIDE * (x - nc + 1) - (CMP_LEN - 1)
    jk = jnp.arange(QB, dtype=I32)[None, :, None]
    rel_t = jnp.arange(N_OFF, dtype=I32)[:, None, None] * QB + iq - jk
    jw = jnp.arange(WINDOW + QB, dtype=I32)[:, None]
    rel_w = iq[0] - jw + WINDOW
    rel_w = jnp.where(rel_w < WINDOW, rel_w, -1)
    flat = rel_bias.reshape(-1)
    r4 = NSA_HPG * QB
    cb = _bias_table(flat, jnp.maximum(rel_c, -1), rel_c.shape[0])
    bt = _bias_table(flat, jnp.maximum(rel_t, -1).reshape(N_OFF * QB, QB), QB, delta=True)
    bt = bt.reshape(NSA_GROUPS, N_OFF, QB, r4)
    wb = _bias_table(flat, jnp.maximum(rel_w, -1), WINDOW + QB)
    sat = jnp.repeat(rel_bias[REL_BUCKETS - 1].reshape(NSA_GROUPS, 1, NSA_HPG), QB, axis=2)
    hi = sat.astype(BF16)
    lo = (sat - hi.astype(F32)).astype(BF16)
    b31 = jnp.concatenate([hi, lo, jnp.zeros((NSA_GROUPS, AUG_ROWS - 2, r4), BF16)], axis=1)
    return cb, bt, wb, b31


def _selection_tables(s):
    nc = s // CMP_STRIDE
    ns = s // SLC_BLOCK
    c_start = np.arange(nc) * CMP_STRIDE
    s_start = np.arange(ns) * SLC_BLOCK
    cover = np.maximum(np.minimum(c_start[:, None] + CMP_LEN, s_start[None, :] + SLC_BLOCK)
                       - np.maximum(c_start[:, None], s_start[None, :]), 0).astype(np.float32) / CMP_LEN
    cover[nc - 1] = 0.0
    cover_rev_t = np.ascontiguousarray(cover[::-1].T)
    blockcols = np.where(np.arange(s)[:, None] // SLC_BLOCK == np.arange(ns)[None, :], -MASK_BIG, 0.0)
    return jnp.asarray(cover_rev_t, BF16), jnp.asarray(blockcols, BF16)


def _split_in_weights(w_in_l):
    o = np.cumsum((0, 512, 128, 128, 128, 128, 128, 128, 24, 512, 512, 1024, 1024, 1024, 1024))
    seg = lambda a: w_in_l[:, o[a]:o[a + 1]]
    wb = jnp.concatenate([seg(10), seg(0), seg(1), seg(2), seg(3), seg(4), seg(5), seg(6)], axis=1)
    gpad = jnp.pad(seg(7), ((0, 0), (0, LANES - 24)))
    wf = jnp.concatenate([seg(8), seg(9), seg(11), seg(12), seg(13), gpad], axis=1)
    return wb.astype(BF16), wf.astype(BF16)


def kernel(x, w_in, cmp_pos, cmp_w1, cmp_w2, w_br_a, w_br_b, w_o, ln1_g, ln1_b, w_router, router_bias,
           moe_w_gate, moe_w_up, moe_w_down, ln2_g, ln2_b, rel_bias):
    b, s, d = x.shape
    depth = w_in.shape[0]
    t = b * s
    alpha = (2 * depth) ** 0.25
    g, r, dh = NSA_GROUPS, NSA_HPG, NSA_DH
    nqb = s // QB
    nb16 = s // CMP_STRIDE
    n_sel = min(N_SELECT, s // SLC_BLOCK)

    cb, bt, wb, b31 = _attention_tables(rel_bias, s)
    cov_t, blockcols = _selection_tables(s)
    ret_tabs = _retention_tables(s)
    tri = jnp.asarray(np.triu(np.ones((512, 512), np.float32), 1), BF16)
    wr_t = w_router.T
    rb_col = router_bias.reshape(N_EXPERTS, 1)
    n_blk = t * 2 // MOE_BM + N_EXPERTS
    rows = n_blk * MOE_BM

    xf = x.reshape(t, d)
    xb = xf.astype(BF16)
    for l in range(depth):
        w_b, w_f = _split_in_weights(w_in[l])
        pb =_matmul(xb, w_b, BF16, 1024, 1152)
        pf = _matmul(xb, w_f, F32, 1024, 1408)
        c_qa = RET_HEADS * RET_DV
        c_kv = c_qa + NSA_HEADS * dh

        def kv_heads(slab):
            col = c_kv + slab * g * dh
            return pb[:, col:col + g * dh].reshape(b, s, g, dh).transpose(0, 2, 1, 3)
        x16 = jnp.stack([kv_heads(0), kv_heads(1)]).reshape(2, b, g, nb16, CMP_STRIDE * dh)
        w1 = cmp_w1[l].astype(BF16)
        half = CMP_STRIDE * dh
        w1ab = jnp.concatenate([w1[:, :half], w1[:, half:]], axis=2)
        pos8 = jnp.broadcast_to(cmp_pos[l].reshape(2, 1, CMP_LEN * dh), (2, 8, CMP_LEN * dh)).astype(BF16)
        kvc = _compress(x16, w1ab, _posterm(pos8, w1), cmp_w2[l].astype(BF16))
        gl = (pf[:, 4096:4096 + 24].reshape(b, nqb, QB, g, r, 3).transpose(0, 3, 1, 5, 4, 2)
              .reshape(b, g, nqb, 3, r * QB))
        ya = _nsa_attention(pb, c_qa, c_kv, kvc, gl, cb, bt, wb, cov_t, b31, blockcols, b, s, n_sel)

        yb = _retention(pf, pb, b, s, 0, 512, 0, 1024, ret_tabs)

        x1 = _merge(ya, yb, pf, 2048, 3072, xf, w_br_a[l].astype(BF16), w_br_b[l].astype(BF16),
                    w_o[l].astype(BF16), ln1_g[l].reshape(1, d), ln1_b[l].reshape(1, d), alpha, 512)

        eidx, wts, rank, cnt = _router(x1, wr_t, rb_col, tri, 512)
        counts = cnt[:, 0].astype(I32)
        padded = ((counts + MOE_BM - 1) // MOE_BM) * MOE_BM
        pad_end = jnp.cumsum(padded)
        pad_start = pad_end - padded
        e_iota = jnp.arange(N_EXPERTS, dtype=I32)[:, None, None]
        start_of = jnp.sum(jnp.where(eidx[None] == e_iota, pad_start[:, None, None], 0), axis=0)
        dest = (start_of + rank).reshape(-1)
        n_active = (pad_end[-1] // MOE_BM).astype(I32).reshape(1)
        blk = jnp.minimum(jnp.arange(n_blk, dtype=I32), n_active[0] - 1) * MOE_BM
        blk_e = jnp.minimum(jnp.sum(pad_end[None, :] <= blk[:, None], axis=1), N_EXPERTS - 1).astype(I32)
        blk_all = jnp.arange(n_blk, dtype=I32)
        last_of_expert = jnp.any(pad_end[None, :] == (blk_all[:, None] + 1) * MOE_BM, axis=1)
        zflag = (last_of_expert | (blk_all >= n_active[0])).astype(I32)
        xs = _dispatch(dest, zflag, x1, rows, 512)
        ys = _experts(blk_e, n_active, xs, moe_w_gate[l].astype(BF16), moe_w_up[l].astype(BF16),
                      moe_w_down[l].astype(BF16))
        xf, xb = _combine(dest, x1, wts.T, ln2_g[l].reshape(1, d), ln2_b[l].reshape(1, d), ys, alpha, 256)
    return xf.reshape(b, s, d)
```

```python
import functools
import math

import jax
import jax.numpy as jnp
import numpy as np
from jax import lax
from jax.experimental import pallas as pl
from jax.experimental.pallas import tpu as pltpu

F32 = jnp.float32
BF16 = jnp.bfloat16
I32 = jnp.int32

NSA_HEADS = 8
NSA_GROUPS = 2
NSA_HPG = 4
NSA_DH = 64
CMP_LEN = 32
CMP_STRIDE = 16
SLC_BLOCK = 64
N_SELECT = 16
WINDOW = 512
QB = 256
FORCE_SCORE = 1.0e4
RET_HEADS = 8
RET_DK = 64
RET_DV = 128
RET_CHUNK = 128
ROPE_BASE = 10000.0
REL_BUCKETS = 32
REL_MAX_EXACT = 16
REL_MAX_DIST = 2048
N_EXPERTS = 16
N_GROUPS = 4
EPG = 4
D_EXPERT = 512
LN_EPS = 1e-5
GN_EPS = 1e-5
NEG_INF = -1e30

LANES = 128
VMEM_LIMIT = 56 * 1024 * 1024

SEL_TK = 512
SEL_CW = 512
AUG_ROWS = 16
MASK_BIG = 2.0 ** 100
SAT_REL = 1513
N_OFF = -(-(SAT_REL + QB - 1) // QB) + 1
MOE_BM = 256
ROW_DMA_UNROLL = 8


def _cparams(sem):
    return pltpu.CompilerParams(dimension_semantics=sem, vmem_limit_bytes=VMEM_LIMIT)


def _dot(a, b):
    return jnp.dot(a, b, preferred_element_type=F32)


def _dot_nt(a, b):
    return lax.dot_general(a, b, (((1,), (1,)), ((), ())), preferred_element_type=F32)


def _mm_kernel(a_ref, b_ref, o_ref):
    o_ref[...] = _dot(a_ref[...], b_ref[...]).astype(o_ref.dtype)


def _matmul(a, b, out_dtype, tm, tn):
    m, k = a.shape
    n = b.shape[1]
    return pl.pallas_call(
        _mm_kernel,
        grid=(n // tn, m // tm),
        in_specs=[pl.BlockSpec((tm, k), lambda j, i: (i, 0)),
                  pl.BlockSpec((k, tn), lambda j, i: (0, j))],
        out_specs=pl.BlockSpec((tm, tn), lambda j, i: (i, j)),
        out_shape=jax.ShapeDtypeStruct((m, n), out_dtype),
        compiler_params=_cparams(("arbitrary", "arbitrary")),
        name="proj_matmul",
    )(a, b)


def _bias_kernel(tab_ref, rel_ref, o_ref, *, delta):
    rel = rel_ref[...]
    n = jnp.maximum(rel, 0)
    nf = jnp.maximum(n, 1).astype(F32)
    large = REL_MAX_EXACT + (jnp.log(nf / REL_MAX_EXACT) / math.log(REL_MAX_DIST / REL_MAX_EXACT)
                             * (REL_BUCKETS - REL_MAX_EXACT)).astype(I32)
    large = jnp.minimum(large, REL_BUCKETS - 1)
    bucket = jnp.where(n < REL_MAX_EXACT, n, large)
    eqs = [bucket == k for k in range(REL_BUCKETS)]
    for h in range(NSA_HEADS):
        acc = jnp.full(rel.shape, NEG_INF, F32)
        for k in range(REL_BUCKETS):
            acc = jnp.where(eqs[k], tab_ref[k * NSA_HEADS + h], acc)
        if delta:
            acc = acc - tab_ref[(REL_BUCKETS - 1) * NSA_HEADS + h]
        hh = h % NSA_HPG
        o_ref[h // NSA_HPG, :, hh * QB:(hh + 1) * QB] = jnp.where(rel >= 0, acc, NEG_INF)


def _bias_table(rel_bias_flat, rel, tr, delta=False):
    r, c = rel.shape
    return pl.pallas_call(
        functools.partial(_bias_kernel, delta=delta),
        grid_spec=pltpu.PrefetchScalarGridSpec(
            num_scalar_prefetch=1,
            grid=(r // tr,),
            in_specs=[pl.BlockSpec((tr, c), lambda i, tab: (i, 0))],
            out_specs=pl.BlockSpec((NSA_GROUPS, tr, NSA_HPG * c), lambda i, tab: (0, i, 0)),
        ),
        out_shape=jax.ShapeDtypeStruct((NSA_GROUPS, r, NSA_HPG * c), F32),
        compiler_params=_cparams(("arbitrary",)),
        name="t5_bias_table",
    )(rel_bias_flat, rel)


def _compress_kernel(x_ref, w1_ref, pos_ref, w2_ref, o_ref):
    ab = _dot(x_ref[...], w1_ref[...])
    nb = ab.shape[0]
    nxt = pltpu.roll(ab, nb - 1, axis=0)
    nxt = pltpu.roll(nxt, NSA_DH, axis=1)
    pre = (ab + nxt)[:, :NSA_DH] + pos_ref[...]
    hid = jax.nn.gelu(pre)
    out = _dot(hid.astype(BF16), w2_ref[...]).astype(BF16)
    rows = lax.broadcasted_iota(I32, (nb, nb), 0)
    cols = lax.broadcasted_iota(I32, (nb, nb), 1)
    flip = jnp.where(rows + cols == nb - 1, 1.0, 0.0).astype(BF16)
    o_ref[...] = _dot(flip, out).astype(o_ref.dtype)


def _compress(x16, w1ab, posterm, w2):
    _, b, g, nb, wd = x16.shape
    return pl.pallas_call(
        _compress_kernel,
        grid=(2, b, g),
        in_specs=[pl.BlockSpec((None, None, None, nb, wd), lambda s, bi, gi: (s, bi, gi, 0, 0)),
                  pl.BlockSpec((None, wd, 2 * NSA_DH), lambda s, bi, gi: (s, 0, 0)),
                  pl.BlockSpec((None, 1, NSA_DH), lambda s, bi, gi: (s, 0, 0)),
                  pl.BlockSpec((None, NSA_DH, NSA_DH), lambda s, bi, gi: (s, 0, 0))],
        out_specs=pl.BlockSpec((None, None, None, nb, NSA_DH), lambda s, bi, gi: (s, bi, gi, 0, 0)),
        out_shape=jax.ShapeDtypeStruct((2, b, g, nb, NSA_DH), BF16),
        compiler_params=_cparams(("arbitrary",) * 3),
        name="nsa_compress",
    )(x16, w1ab, posterm, w2)


def _posterm_kernel(p_ref, w_ref, o_ref):
    o_ref[...] = _dot(p_ref[...], w_ref[...])[0:1]


def _posterm(pos8, w1):
    return pl.pallas_call(
        _posterm_kernel,
        grid=(2,),
        in_specs=[pl.BlockSpec((None, 8, pos8.shape[2]), lambda s: (s, 0, 0)),
                  pl.BlockSpec((None, w1.shape[1], NSA_DH), lambda s: (s, 0, 0))],
        out_specs=pl.BlockSpec((None, 1, NSA_DH), lambda s: (s, 0, 0)),
        out_shape=jax.ShapeDtypeStruct((2, 1, NSA_DH), F32),
        name="nsa_posterm",
    )(pos8, w1)


def _eye(n, scale=1.0):
    rows = lax.broadcasted_iota(I32, (n, n), 0)
    cols = lax.broadcasted_iota(I32, (n, n), 1)
    return jnp.where(rows == cols, scale, 0.0).astype(BF16)


def _nsa_kernel(q_ref, qn_ref, kc_ref, vcin_ref, ksin_ref, vsin_ref, kwin_ref, vwin_ref, gate_ref,
                cb_ref, bt_ref, wb_ref, cov_ref, b31_ref, bcols_ref, o_ref,
                s_ref, ks_ref, vs_ref, kw_ref, vw_ref, vc_ref, qt_ref, *, n_sel):
    gi = pl.program_id(1)
    i = pl.program_id(2)
    q0 = i * QB
    r4 = NSA_HPG * QB
    ns = cov_ref.shape[0]
    eye_dh = _eye(NSA_DH)

    def group_lanes(x):
        return jnp.where(gi == 0, x[:, 0:NSA_DH], x[:, NSA_DH:2 * NSA_DH])

    def q_transposed(ref):
        eye_q = _eye(NSA_DH, NSA_DH ** -0.5)
        qb = ref[...]
        return jnp.concatenate([_dot_nt(eye_q, qb[:, h * NSA_DH:(h + 1) * NSA_DH]) for h in range(NSA_HPG)],
                               axis=1).astype(BF16)

    @pl.when(i == 0)
    def _():
        s_len = ksin_ref.shape[0]
        ones_rows = jnp.where(lax.broadcasted_iota(I32, (AUG_ROWS, SEL_TK), 0) == 0, 1.0, 0.0).astype(BF16)
        ones_rows_qb = jnp.where(lax.broadcasted_iota(I32, (AUG_ROWS, QB), 0) == 0, 1.0, 0.0).astype(BF16)
        tail = jnp.where(lax.broadcasted_iota(I32, (SEL_TK, NSA_DH), 1) < 2, 1.0, 0.0).astype(BF16)
        depth = ks_ref.shape[-1]
        vc_ref[...] = _dot_nt(eye_dh, vcin_ref[...]).astype(BF16)
        qt_ref[...] = q_transposed(q_ref)
        kw_ref[0:WINDOW, :] = jnp.zeros((WINDOW, NSA_DH), BF16)
        for u in range(WINDOW // QB):
            vw_ref[u] = jnp.zeros(vw_ref.shape[1:], BF16)

        def chunk(c, _):
            r0 = pl.multiple_of(c * SEL_TK, SEL_TK)
            rows = pl.ds(r0, SEL_TK)
            k_aug = jnp.concatenate([group_lanes(ksin_ref[rows, :]), tail], axis=1)
            ks_ref[rows, 0:ns] = bcols_ref[rows, :]
            ks_ref[rows, ns:ns + 2 * NSA_DH] = k_aug
            if depth > ns + 2 * NSA_DH:
                ks_ref[rows, ns + 2 * NSA_DH:depth] = jnp.zeros((SEL_TK, depth - ns - 2 * NSA_DH), BF16)
            v_t = _dot_nt(eye_dh, group_lanes(vsin_ref[rows, :])).astype(BF16)
            vs_ref[c] = jnp.concatenate([v_t, ones_rows], axis=0)
            kw_ref[pl.ds(WINDOW + r0, SEL_TK), :] = group_lanes(kwin_ref[rows, :])
            vw_t = _dot_nt(eye_dh, group_lanes(vwin_ref[rows, :])).astype(BF16)
            for u in range(SEL_TK // QB):
                vw_ref[WINDOW // QB + c * (SEL_TK // QB) + u] = jnp.concatenate(
                    [vw_t[:, u * QB:(u + 1) * QB], ones_rows_qb], axis=0)
            return 0

        lax.fori_loop(0, s_len // SEL_TK, chunk, 0)

    q = qt_ref[...]

    step_rows = QB // CMP_STRIDE
    cb = cb_ref[pl.ds(pl.multiple_of(i * step_rows, step_rows), kc_ref.shape[0]), :]
    s_c = _dot(kc_ref[...], q) + cb
    m_c = jnp.max(s_c, axis=0, keepdims=True)
    p_c = jnp.where(cb > 0.5 * NEG_INF, jnp.exp(s_c - m_c), 0.0)
    l_c = jnp.sum(p_c, axis=0, keepdims=True)
    p_c = p_c / jnp.maximum(l_c, 1e-30)
    o_c = _dot(vc_ref[...], p_c.astype(BF16))

    psum = p_c[:, 0:QB] + p_c[:, QB:2 * QB] + p_c[:, 2 * QB:3 * QB] + p_c[:, 3 * QB:4 * QB]
    p_hi = psum.astype(BF16)
    p_lo = (psum - p_hi.astype(F32)).astype(BF16)
    cov = cov_ref[...]
    imp_t = _dot(cov, p_hi) + _dot(cov, p_lo)
    ns = imp_t.shape[0]

    wk = WINDOW + QB
    kwin = kw_ref[pl.ds(pl.multiple_of(q0, QB), wk), :]
    s_w = _dot(kwin, q) + wb_ref[...]
    kpos = q0 - WINDOW + lax.broadcasted_iota(I32, (wk, r4), 0)
    s_w = jnp.where(kpos >= 0, s_w, NEG_INF)
    m_w = jnp.max(s_w, axis=0, keepdims=True)
    p_w = jnp.exp(s_w - m_w).astype(BF16)
    acc_w = jnp.zeros((NSA_DH + AUG_ROWS, r4), F32)
    for u in range(wk // QB):
        acc_w = acc_w + _dot(vw_ref[i + u], p_w[u * QB:(u + 1) * QB])
    o_w = acc_w[0:NSA_DH] / acc_w[NSA_DH:NSA_DH + 1]

    jidx = lax.broadcasted_iota(I32, (ns, QB), 0)
    t = q0 + lax.broadcasted_iota(I32, (ns, QB), 1)
    cur = jnp.right_shift(t, 6)
    eligible = jidx <= cur
    forced = (jidx == 0) | (jidx == cur) | (jidx == cur - 1)
    score = jnp.where(eligible, jnp.where(forced, -2.0, imp_t), -1.0)
    sel = jnp.where(forced, 1.0, 0.0)
    for _ in range(n_sel - 3):
        mx = jnp.max(score, axis=0, keepdims=True)
        first = jnp.min(jnp.where(score == mx, jidx, ns), axis=0, keepdims=True)
        pick = jidx == first
        sel = jnp.where(pick, 1.0, sel)
        score = jnp.where(pick, -2.0, score)
    notsel = jnp.where(eligible, 1.0 - sel, 1.0).astype(BF16)

    depth = ks_ref.shape[-1]
    pad_rows = depth - ns - NSA_DH - AUG_ROWS
    qa = jnp.concatenate([jnp.concatenate([notsel] * NSA_HPG, axis=1), q, b31_ref[...],
                          jnp.zeros((pad_rows, r4), BF16)], axis=0)
    sub = SEL_TK // QB
    n_tiles = i // sub + 1
    n_far = jnp.maximum(i - N_OFF + 2, 0) // sub

    def put_scores(jt, slot):
        k0 = pl.multiple_of(jt * SEL_TK, SEL_TK)
        s_ref[slot] = _dot(ks_ref[pl.ds(k0, SEL_TK), :], qa)

    def absorb(jt, slot, carry, near):
        m_prev, acc = carry
        s = s_ref[slot]
        if near:
            offs = [jnp.clip(i - sub * jt - u, 0, N_OFF - 1) for u in range(sub)]
            s = s + jnp.concatenate([bt_ref[o] for o in offs], axis=0)
        m_new = jnp.maximum(m_prev, jnp.max(s, axis=0, keepdims=True))
        alpha = jnp.exp(m_prev - m_new)
        p = jnp.exp(s - m_new).astype(BF16)
        return m_new, alpha * acc + _dot(vs_ref[jt], p)

    def pair_body(pp, carry, near):
        ja = 2 * pp
        put_scores(ja + 1, 1)
        carry = absorb(ja, 0, carry, near)
        put_scores(jnp.minimum(ja + 2, last_tile), 0)
        return absorb(ja + 1, 1, carry, near)

    n_pairs = (n_tiles + 1) // 2
    last_tile = 2 * n_pairs - 1
    far_pairs = n_far // 2
    put_scores(0, 0)
    carry = (jnp.full((1, r4), -1e38, F32), jnp.zeros((NSA_DH + AUG_ROWS, r4), F32))
    carry = lax.fori_loop(0, far_pairs, functools.partial(pair_body, near=False), carry)
    _, acc_s = lax.fori_loop(far_pairs, n_pairs, functools.partial(pair_body, near=True), carry)
    o_s = acc_s[0:NSA_DH] / acc_s[NSA_DH:NSA_DH + 1]

    gates = jax.nn.sigmoid(gate_ref[...])
    o = (gates[0:1] * o_c + gates[1:2] * o_s + gates[2:3] * o_w).astype(BF16)
    qt_ref[...] = q_transposed(qn_ref)
    eye_l = _eye(LANES)
    o_ref[...] = jnp.concatenate(
        [jnp.concatenate([_dot_nt(eye_l, o[:, h * QB + u * LANES:h * QB + (u + 1) * LANES])
                          for u in range(QB // LANES)], axis=0) for h in range(NSA_HPG)],
        axis=1).astype(o_ref.dtype)


def _nsa_attention(pb, c_qa, c_kv, kvc, gates, cb, bt, wb, cov_t, b31, blockcols, b, s, n_sel):
    g, dh = NSA_GROUPS, NSA_DH
    r4 = NSA_HPG * QB
    nqb = s // QB
    nc = kvc.shape[3]
    ns = cov_t.shape[0]
    depth = -(-(ns + 2 * dh) // LANES) * LANES
    n_tiles = s // SEL_TK
    assert n_tiles % 2 == 0, "the selected sweep consumes key tiles in pairs"
    qw = NSA_HPG * dh
    once = pl.Buffered(1)
    slab = lambda k: pl.BlockSpec((s, 2 * dh), lambda bi, gi, i: (bi, c_kv // (2 * dh) + k), pipeline_mode=once)
    per_g = lambda a: pl.BlockSpec((None,) + a.shape[1:], lambda bi, gi, i: (gi,) + (0,) * (a.ndim - 1),
                                   pipeline_mode=once)
    const = lambda a: pl.BlockSpec(a.shape, lambda bi, gi, i: (0,) * a.ndim, pipeline_mode=once)
    return pl.pallas_call(
        functools.partial(_nsa_kernel, n_sel=n_sel),
        grid=(b, g, nqb),
        in_specs=[
            pl.BlockSpec((QB, qw), lambda bi, gi, i: (bi * nqb + i, c_qa // qw + gi)),
            pl.BlockSpec((QB, qw), lambda bi, gi, i: (bi * nqb + jnp.minimum(i + 1, nqb - 1), c_qa // qw + gi)),
            pl.BlockSpec((None, None, None, nc, dh), lambda bi, gi, i: (0, bi, gi, 0, 0)),
            pl.BlockSpec((None, None, None, nc, dh), lambda bi, gi, i: (1, bi, gi, 0, 0)),
            slab(2), slab(3), slab(4), slab(5),
            pl.BlockSpec((None, None, None, 3, r4), lambda bi, gi, i: (bi, gi, i, 0, 0)),
            per_g(cb), per_g(bt), per_g(wb), const(cov_t), per_g(b31), const(blockcols),
        ],
        out_specs=pl.BlockSpec((QB, qw), lambda bi, gi, i: (bi * nqb + i, gi)),
        out_shape=jax.ShapeDtypeStruct((b * s, g * qw), BF16),
        scratch_shapes=[pltpu.VMEM((2, SEL_TK, r4), F32),
                        pltpu.VMEM((s, depth), BF16),
                        pltpu.VMEM((n_tiles, dh + AUG_ROWS, SEL_TK), BF16),
                        pltpu.VMEM((s + WINDOW, dh), BF16),
                        pltpu.VMEM(((s + WINDOW) // QB, dh + AUG_ROWS, QB), BF16),
                        pltpu.VMEM((dh, nc), BF16),
                        pltpu.VMEM((dh, r4), BF16)],
        compiler_params=_cparams(("arbitrary",) * 3),
        name="nsa_attention",
    )(pb, pb, kvc, kvc, pb, pb, pb, pb, gates, cb, bt, wb, cov_t, b31, blockcols)


def _ret_kernel(q_ref, k_ref, v_ref, g_ref, cos_ref, sin_ref, dm_ref, qd_ref, kd_ref, cd_ref,
                o_ref, state_ref):
    n = pl.program_id(1)

    @pl.when(n == 0)
    def _():
        state_ref[...] = jnp.zeros_like(state_ref)

    lane = lax.broadcasted_iota(I32, (RET_CHUNK, LANES), 1)
    first_half = (lane & (RET_DK - 1)) < (RET_DK // 2)
    cosv = cos_ref[...]
    sinv = sin_ref[...]

    def rot(x):
        partner = jnp.where(first_half, pltpu.roll(x, LANES - RET_DK // 2, axis=1),
                            pltpu.roll(x, RET_DK // 2, axis=1))
        return x * cosv + partner * sinv

    for hp in range(RET_HEADS // 2):
        ls = slice(hp * LANES, (hp + 1) * LANES)
        q2 = rot(q_ref[:, ls])
        k2 = rot(k_ref[:, ls]) * (RET_DK ** -0.5)
        qdec = (q2 * qd_ref[:, ls]).astype(BF16)
        kdec_t = (k2 * kd_ref[:, ls]).T.astype(BF16)
        q2b = q2.astype(BF16)
        k2b = k2.astype(BF16)
        for e in range(2):
            h = 2 * hp + e
            hs = slice(e * RET_DK, (e + 1) * RET_DK)
            vs = slice(h * RET_DV, (h + 1) * RET_DV)
            v = v_ref[:, vs]
            st = state_ref[h]
            sc = _dot_nt(q2b[:, hs], k2b[:, hs]) * dm_ref[h]
            y = _dot(sc.astype(BF16), v) + _dot(qdec[:, hs], st.astype(BF16))
            state_ref[h] = cd_ref[h] * st + _dot(kdec_t[hs, :], v)
            mu = jnp.mean(y, axis=-1, keepdims=True)
            var = jnp.mean(jnp.square(y - mu), axis=-1, keepdims=True)
            yn = (y - mu) * lax.rsqrt(var + GN_EPS)
            o_ref[:, vs] = (jax.nn.silu(g_ref[:, vs]) * yn).astype(o_ref.dtype)


def _retention(pf, pb, b, s, col_q, col_k, col_v, col_g, tabs):
    cosf, sinf, dm, qd, kd, cd = tabs
    n = s // RET_CHUNK
    t = b * s
    c = RET_CHUNK
    wqk = RET_HEADS * RET_DK
    wv = RET_HEADS * RET_DV
    row = lambda bi, ni: bi * n + ni
    full = lambda a: pl.BlockSpec(a.shape, lambda bi, ni: (0,) * a.ndim)
    return pl.pallas_call(
        _ret_kernel,
        grid=(b, n),
        in_specs=[
            pl.BlockSpec((c, wqk), lambda bi, ni: (row(bi, ni), col_q // wqk)),
            pl.BlockSpec((c, wqk), lambda bi, ni: (row(bi, ni), col_k // wqk)),
            pl.BlockSpec((c, wv), lambda bi, ni: (row(bi, ni), col_v // wv)),
            pl.BlockSpec((c, wv), lambda bi, ni: (row(bi, ni), col_g // wv)),
            pl.BlockSpec((c, LANES), lambda bi, ni: (ni, 0)),
            pl.BlockSpec((c, LANES), lambda bi, ni: (ni, 0)),
            full(dm), full(qd), full(kd), full(cd),
        ],
        out_specs=pl.BlockSpec((c, wv), lambda bi, ni: (row(bi, ni), 0)),
        out_shape=jax.ShapeDtypeStruct((t, wv), BF16),
        scratch_shapes=[pltpu.VMEM((RET_HEADS, RET_DK, RET_DV), F32)],
        compiler_params=_cparams(("arbitrary",) * 2),
        name="retention",
    )(pf, pf, pb, pf, cosf, sinf, dm, qd, kd, cd)


def _retention_tables(s):
    half = RET_DK // 2
    inv = ROPE_BASE ** (-jnp.arange(half, dtype=F32) * 2.0 / RET_DK)
    ang = jnp.arange(s, dtype=F32)[:, None] * inv[None, :]
    cos, sin = jnp.cos(ang), jnp.sin(ang)
    cosf = jnp.concatenate([cos, cos, cos, cos], axis=1)
    sinf = jnp.concatenate([-sin, sin, -sin, sin], axis=1)
    lg = jnp.log(1.0 - 2.0 ** (-5.0 - jnp.arange(RET_HEADS, dtype=F32)))
    i = jnp.arange(RET_CHUNK, dtype=F32)
    diff = i[:, None] - i[None, :]
    dm = jnp.where(diff >= 0, jnp.exp(jnp.maximum(diff, 0.0)[None] * lg[:, None, None]), 0.0)
    kdec = jnp.exp((RET_CHUNK - 1.0 - i)[None, :] * lg[:, None])
    qdec = jnp.exp((i + 1.0)[None, :] * lg[:, None])
    expand = lambda a: jnp.repeat(a.T, RET_DK, axis=1)
    cd = jnp.broadcast_to(jnp.exp(RET_CHUNK * lg)[:, None, None], (RET_HEADS, 1, LANES))
    return cosf, sinf, dm, expand(qdec), expand(kdec), cd


def _layer_norm(z, g, b):
    mu = jnp.mean(z, axis=-1, keepdims=True)
    var = jnp.mean(jnp.square(z - mu), axis=-1, keepdims=True)
    return (z - mu) * lax.rsqrt(var + LN_EPS) * g + b


def _merge_kernel(ya_ref, yb_ref, ma_ref, mb_ref, x_ref, wa_ref, wb_ref, wo_ref, g_ref, b_ref,
                  wr_ref, rb_ref, tri_ref, o_ref, eidx_ref, wts_ref, rank_ref, cnt_ref, carry_ref, *, alpha):
    merged = (jax.nn.sigmoid(ma_ref[...]) * _dot(ya_ref[...], wa_ref[...])
              + jax.nn.sigmoid(mb_ref[...]) * _dot(yb_ref[...], wb_ref[...]))
    z = alpha * x_ref[...] + _dot(merged.astype(BF16), wo_ref[...])
    x1 = _layer_norm(z, g_ref[...], b_ref[...])
    o_ref[...] = x1
    _route(x1, wr_ref, rb_ref, tri_ref, eidx_ref, wts_ref, rank_ref, cnt_ref, carry_ref)


def _merge(ya, yb, pf, col_ma, col_mb, x, wa, wb, wo, g, bb, wr_t, rb_col, tri, alpha):
    t, d = x.shape
    tm = tri.shape[0]
    full = lambda a: pl.BlockSpec(a.shape, lambda i: (0,) * a.ndim)
    out2 = lambda dt: jax.ShapeDtypeStruct((2, t), dt)
    row2 = pl.BlockSpec((2, tm), lambda i: (0, i))
    return pl.pallas_call(
        functools.partial(_merge_kernel, alpha=alpha),
        grid=(t // tm,),
        in_specs=[pl.BlockSpec((tm, ya.shape[1]), lambda i: (i, 0)),
                  pl.BlockSpec((tm, yb.shape[1]), lambda i: (i, 0)),
                  pl.BlockSpec((tm, d), lambda i: (i, col_ma // d)),
                  pl.BlockSpec((tm, d), lambda i: (i, col_mb // d)),
                  pl.BlockSpec((tm, d), lambda i: (i, 0)),
                  full(wa), full(wb), full(wo), full(g), full(bb), full(wr_t), full(rb_col), full(tri)],
        out_specs=[pl.BlockSpec((tm, d), lambda i: (i, 0)), row2, row2, row2,
                   pl.BlockSpec((N_EXPERTS, LANES), lambda i: (0, 0))],
        out_shape=[jax.ShapeDtypeStruct((t, d), F32), out2(I32), out2(F32), out2(I32),
                   jax.ShapeDtypeStruct((N_EXPERTS, LANES), F32)],
        scratch_shapes=[pltpu.VMEM((N_EXPERTS, LANES), F32)],
        compiler_params=_cparams(("arbitrary",)),
        name="merge_ln1_router",
    )(ya, yb, pf, pf, x, wa, wb, wo, g, bb, wr_t, rb_col, tri)


def _route(x, wr_ref, rb_ref, tri_ref, eidx_ref, wts_ref, rank_ref, cnt_ref, carry_ref):
    step = pl.program_id(0)

    @pl.when(step == 0)
    def _():
        carry_ref[...] = jnp.zeros_like(carry_ref)

    logits = lax.dot_general(wr_ref[...], x, (((1,), (1,)), ((), ())),
                             precision=lax.Precision.HIGHEST, preferred_element_type=F32)
    tm = logits.shape[1]
    mx = jnp.max(logits, axis=0, keepdims=True)
    ex = jnp.exp(logits - mx)
    probs = ex / jnp.sum(ex, axis=0, keepdims=True)
    sel = probs + rb_ref[...]
    rows = [sel[e:e + 1] for e in range(N_EXPERTS)]
    gscore = []
    for gi in range(N_GROUPS):
        r = rows[gi * EPG:(gi + 1) * EPG]
        best2 = None
        for a in range(EPG):
            for c in range(a + 1, EPG):
                pair = r[a] + r[c]
                best2 = pair if best2 is None else jnp.maximum(best2, pair)
        gscore.append(best2)
    best = jnp.zeros((1, tm), I32)
    bscore = gscore[0]
    for gi in range(1, N_GROUPS):
        better = gscore[gi] > bscore
        best = jnp.where(better, gi, best)
        bscore = jnp.where(better, gscore[gi], bscore)
    ing = []
    for a in range(EPG):
        v = rows[a]
        for gi in range(1, N_GROUPS):
            v = jnp.where(best == gi, rows[gi * EPG + a], v)
        ing.append(v)
    i1 = jnp.zeros((1, tm), I32)
    v1 = ing[0]
    for a in range(1, EPG):
        better = ing[a] > v1
        i1 = jnp.where(better, a, i1)
        v1 = jnp.where(better, ing[a], v1)
    i2 = jnp.full((1, tm), -1, I32)
    v2 = jnp.full((1, tm), -jnp.inf, F32)
    for a in range(EPG):
        better = (i1 != a) & (ing[a] > v2)
        i2 = jnp.where(better, a, i2)
        v2 = jnp.where(better, ing[a], v2)
    e0 = best * EPG + i1
    e1 = best * EPG + i2
    eio = lax.broadcasted_iota(I32, (N_EXPERTS, tm), 0)
    oh0 = (eio == e0).astype(F32)
    oh1 = (eio == e1).astype(F32)
    p0 = jnp.sum(oh0 * probs, axis=0, keepdims=True)
    p1 = jnp.sum(oh1 * probs, axis=0, keepdims=True)
    den = p0 + p1
    eidx_ref[0:1, :] = e0
    eidx_ref[1:2, :] = e1
    wts_ref[0:1, :] = p0 / den
    wts_ref[1:2, :] = p1 / den
    tot = oh0 + oh1
    prefix = _dot(tot.astype(BF16), tri_ref[...]) + carry_ref[:, 0:1]
    rank_ref[0:1, :] = jnp.sum(oh0 * prefix, axis=0, keepdims=True).astype(I32)
    rank_ref[1:2, :] = jnp.sum(oh1 * prefix, axis=0, keepdims=True).astype(I32)
    carry_ref[...] = carry_ref[...] + jnp.sum(tot, axis=1, keepdims=True)
    cnt_ref[...] = carry_ref[...]


def _dispatch_kernel(dest_ref, zflag_ref, x_ref, xs_ref, zero_ref, sem, zsem, *, t_total):
    step = pl.program_id(0)
    tm = x_ref.shape[0]

    @pl.when(step == 0)
    def _():
        zero_ref[...] = jnp.zeros_like(zero_ref)

        def zero_block(bi, _):
            @pl.when(zflag_ref[bi] == 1)
            def _():
                r0 = pl.multiple_of(bi * MOE_BM, MOE_BM)
                cp = pltpu.make_async_copy(zero_ref, xs_ref.at[pl.ds(r0, MOE_BM)], zsem)
                cp.start()
                cp.wait()
            return 0

        lax.fori_loop(0, zflag_ref.shape[0], zero_block, 0)

    def copy(r, slot):
        dst = dest_ref[slot * t_total + step * tm + r]
        return pltpu.make_async_copy(x_ref.at[pl.ds(r, 1)], xs_ref.at[pl.ds(dst, 1)], sem)

    def issue(r, _):
        copy(r, 0).start()
        copy(r, 1).start()
        return 0

    def drain(r, _):
        copy(r, 0).wait()
        copy(r, 1).wait()
        return 0

    lax.fori_loop(0, tm, issue, 0, unroll=ROW_DMA_UNROLL)
    lax.fori_loop(0, tm, drain, 0, unroll=ROW_DMA_UNROLL)


def _dispatch(dest_flat, zflag, x, rows, tm):
    t, d = x.shape
    return pl.pallas_call(
        functools.partial(_dispatch_kernel, t_total=t),
        grid_spec=pltpu.PrefetchScalarGridSpec(
            num_scalar_prefetch=2,
            grid=(t // tm,),
            in_specs=[pl.BlockSpec((tm, d), lambda i, dr, zf: (i, 0))],
            out_specs=pl.BlockSpec(memory_space=pl.ANY),
            scratch_shapes=[pltpu.VMEM((MOE_BM, d), x.dtype), pltpu.SemaphoreType.DMA,
                            pltpu.SemaphoreType.DMA],
        ),
        out_shape=jax.ShapeDtypeStruct((rows, d), x.dtype),
        compiler_params=pltpu.CompilerParams(dimension_semantics=("arbitrary",), has_side_effects=True,
                                             vmem_limit_bytes=VMEM_LIMIT),
        name="moe_dispatch",
    )(dest_flat, zflag, x)


def _expert_kernel(be_ref, na_ref, x_ref, wg_ref, wu_ref, wd_ref, o_ref, wg_s, wu_s, wd_s):
    i = pl.program_id(0)
    active = i < na_ref[0]

    @pl.when((i == 0) | (be_ref[i] != be_ref[jnp.maximum(i - 1, 0)]))
    def _():
        wg_s[...] = wg_ref[...].astype(BF16)
        wu_s[...] = wu_ref[...].astype(BF16)
        wd_s[...] = wd_ref[...].astype(BF16)

    @pl.when(active)
    def _():
        xb = x_ref[...].astype(BF16)
        gate = _dot(xb, wg_s[...])
        up = _dot(xb, wu_s[...])
        hid = (jax.nn.silu(gate) * up).astype(BF16)
        o_ref[...] = _dot(hid, wd_s[...])

    @pl.when(jnp.logical_not(active))
    def _():
        o_ref[...] = jnp.zeros_like(o_ref)


def _experts(blk_e, n_active, xs, wg, wu, wd, layer):
    rows, d = xs.shape
    nb = rows // MOE_BM
    act = lambda i, be, na: jnp.minimum(i, na[0] - 1)
    wspec = lambda a: pl.BlockSpec((None, None) + a.shape[2:], lambda i, be, na: (layer, be[i], 0, 0))
    return pl.pallas_call(
        _expert_kernel,
        grid_spec=pltpu.PrefetchScalarGridSpec(
            num_scalar_prefetch=2,
            grid=(nb,),
            in_specs=[pl.BlockSpec((MOE_BM, d), lambda i, be, na: (act(i, be, na), 0)),
                      wspec(wg), wspec(wu), wspec(wd)],
            out_specs=pl.BlockSpec((MOE_BM, d), lambda i, be, na: (i, 0)),
            scratch_shapes=[pltpu.VMEM(wg.shape[2:], BF16), pltpu.VMEM(wu.shape[2:], BF16),
                            pltpu.VMEM(wd.shape[2:], BF16)],
        ),
        out_shape=jax.ShapeDtypeStruct((rows, d), F32),
        compiler_params=_cparams(("arbitrary",)),
        name="moe_experts",
    )(blk_e, n_active, xs, wg, wu, wd)


def _combine_kernel(dest_ref, x_ref, w_ref, g_ref, b_ref, ys_ref, o_ref, ob_ref, buf_ref, sem, *, alpha, t_total):
    step = pl.program_id(0)
    tm = x_ref.shape[0]

    def copy(r, slot):
        src = dest_ref[slot * t_total + step * tm + r]
        return pltpu.make_async_copy(ys_ref.at[pl.ds(src, 1)], buf_ref.at[slot, pl.ds(r, 1)], sem)

    def issue(r, _):
        copy(r, 0).start()
        copy(r, 1).start()
        return 0

    def drain(r, _):
        copy(r, 0).wait()
        copy(r, 1).wait()
        return 0

    lax.fori_loop(0, tm, issue, 0, unroll=ROW_DMA_UNROLL)
    lax.fori_loop(0, tm, drain, 0, unroll=ROW_DMA_UNROLL)
    w = w_ref[...]
    z = alpha * x_ref[...] + w[:, 0:1] * buf_ref[0] + w[:, 1:2] * buf_ref[1]
    y = _layer_norm(z, g_ref[...], b_ref[...])
    o_ref[...] = y
    ob_ref[...] = y.astype(BF16)


def _combine(dest_flat, x, wts_t, g, bb, ys, alpha, tm):
    t, d = x.shape
    return pl.pallas_call(
        functools.partial(_combine_kernel, alpha=alpha, t_total=t),
        grid_spec=pltpu.PrefetchScalarGridSpec(
            num_scalar_prefetch=1,
            grid=(t // tm,),
            in_specs=[pl.BlockSpec((tm, d), lambda i, dr: (i, 0)),
                      pl.BlockSpec((tm, 2), lambda i, dr: (i, 0)),
                      pl.BlockSpec(g.shape, lambda i, dr: (0, 0)),
                      pl.BlockSpec(bb.shape, lambda i, dr: (0, 0)),
                      pl.BlockSpec(memory_space=pl.ANY)],
            out_specs=[pl.BlockSpec((tm, d), lambda i, dr: (i, 0)), pl.BlockSpec((tm, d), lambda i, dr: (i, 0))],
            scratch_shapes=[pltpu.VMEM((2, tm, d), F32), pltpu.SemaphoreType.DMA],
        ),
        out_shape=[jax.ShapeDtypeStruct((t, d), F32), jax.ShapeDtypeStruct((t, d), BF16)],
        compiler_params=_cparams(("arbitrary",)),
        name="moe_combine_ln2",
    )(dest_flat, x, wts_t, g, bb, ys)


def _attention_tables(rel_bias, s):
    nc = s // CMP_STRIDE
    nqb = s // QB
    iq = jnp.arange(QB, dtype=I32)[None, None, :]
    x = jnp.arange(nc + (QB // CMP_STRIDE) * nqb, dtype=I32)[:, None]
    rel_c = iq[0] + CMP_STRIDE * (x - nc + 1) - (CMP_LEN - 1)
    jk = jnp.arange(QB, dtype=I32)[None, :, None]
    rel_t = jnp.arange(N_OFF, dtype=I32)[:, None, None] * QB + iq - jk
    jw = jnp.arange(WINDOW + QB, dtype=I32)[:, None]
    rel_w = iq[0] - jw + WINDOW
    rel_w = jnp.where(rel_w < WINDOW, rel_w, -1)
    flat = rel_bias.reshape(-1)
    r4 = NSA_HPG * QB
    cb = _bias_table(flat, jnp.maximum(rel_c, -1), rel_c.shape[0])
    bt = _bias_table(flat, jnp.maximum(rel_t, -1).reshape(N_OFF * QB, QB), QB, delta=True)
    bt = bt.reshape(NSA_GROUPS, N_OFF, QB, r4)
    wb = _bias_table(flat, jnp.maximum(rel_w, -1), WINDOW + QB)
    sat = jnp.repeat(rel_bias[REL_BUCKETS - 1].reshape(NSA_GROUPS, 1, NSA_HPG), QB, axis=2)
    hi = sat.astype(BF16)
    lo = (sat - hi.astype(F32)).astype(BF16)
    b31 = jnp.concatenate([hi, lo, jnp.zeros((NSA_GROUPS, AUG_ROWS - 2, r4), BF16)], axis=1)
    return cb, bt, wb, b31


def _selection_tables(s):
    nc = s // CMP_STRIDE
    ns = s // SLC_BLOCK
    c_start = np.arange(nc) * CMP_STRIDE
    s_start = np.arange(ns) * SLC_BLOCK
    cover = np.maximum(np.minimum(c_start[:, None] + CMP_LEN, s_start[None, :] + SLC_BLOCK)
                       - np.maximum(c_start[:, None], s_start[None, :]), 0).astype(np.float32) / CMP_LEN
    cover[nc - 1] = 0.0
    cover_rev_t = np.ascontiguousarray(cover[::-1].T)
    blockcols = np.where(np.arange(s)[:, None] // SLC_BLOCK == np.arange(ns)[None, :], -MASK_BIG, 0.0)
    return jnp.asarray(cover_rev_t, BF16), jnp.asarray(blockcols, BF16)


def _split_in_weights(w_in_l):
    o = np.cumsum((0, 512, 128, 128, 128, 128, 128, 128, 24, 512, 512, 1024, 1024, 1024, 1024))
    seg = lambda a: w_in_l[:, o[a]:o[a + 1]]
    wb = jnp.concatenate([seg(10), seg(0), seg(1), seg(2), seg(3), seg(4), seg(5), seg(6)], axis=1)
    gpad = jnp.pad(seg(7), ((0, 0), (0, LANES - 24)))
    wf = jnp.concatenate([seg(8), seg(9), seg(11), seg(12), seg(13), gpad], axis=1)
    return wb.astype(BF16), wf.astype(BF16)


def kernel(x, w_in, cmp_pos, cmp_w1, cmp_w2, w_br_a, w_br_b, w_o, ln1_g, ln1_b, w_router, router_bias,
           moe_w_gate, moe_w_up, moe_w_down, ln2_g, ln2_b, rel_bias):
    b, s, d = x.shape
    depth = w_in.shape[0]
    t = b * s
    alpha = (2 * depth) ** 0.25
    g, r, dh = NSA_GROUPS, NSA_HPG, NSA_DH
    nqb = s // QB
    nb16 = s // CMP_STRIDE
    n_sel = min(N_SELECT, s // SLC_BLOCK)

    cb, bt, wb, b31 = _attention_tables(rel_bias, s)
    cov_t, blockcols = _selection_tables(s)
    ret_tabs = _retention_tables(s)
    tri = jnp.asarray(np.triu(np.ones((512, 512), np.float32), 1), BF16)
    wr_t = w_router.T
    rb_col = router_bias.reshape(N_EXPERTS, 1)
    n_blk = t * 2 // MOE_BM + N_EXPERTS
    rows = n_blk * MOE_BM

    xf = x.reshape(t, d)
    xb = xf.astype(BF16)
    for l in range(depth):
        w_b, w_f = _split_in_weights(w_in[l])
        pb =_matmul(xb, w_b, BF16, 1024, 1152)
        pf = _matmul(xb, w_f, F32, 1024, 1408)
        c_qa = RET_HEADS * RET_DV
        c_kv = c_qa + NSA_HEADS * dh

        def kv_heads(slab):
            col = c_kv + slab * g * dh
            return pb[:, col:col + g * dh].reshape(b, s, g, dh).transpose(0, 2, 1, 3)
        x16 = jnp.stack([kv_heads(0), kv_heads(1)]).reshape(2, b, g, nb16, CMP_STRIDE * dh)
        w1 = cmp_w1[l].astype(BF16)
        half = CMP_STRIDE * dh
        w1ab = jnp.concatenate([w1[:, :half], w1[:, half:]], axis=2)
        pos8 = jnp.broadcast_to(cmp_pos[l].reshape(2, 1, CMP_LEN * dh), (2, 8, CMP_LEN * dh)).astype(BF16)
        kvc = _compress(x16, w1ab, _posterm(pos8, w1), cmp_w2[l].astype(BF16))
        gl = (pf[:, 4096:4096 + 24].reshape(b, nqb, QB, g, r, 3).transpose(0, 3, 1, 5, 4, 2)
              .reshape(b, g, nqb, 3, r * QB))
        ya = _nsa_attention(pb, c_qa, c_kv, kvc, gl, cb, bt, wb, cov_t, b31, blockcols, b, s, n_sel)

        yb = _retention(pf, pb, b, s, 0, 512, 0, 1024, ret_tabs)

        x1, eidx, wts, rank, cnt = _merge(ya, yb, pf, 2048, 3072, xf, w_br_a[l].astype(BF16),
                                          w_br_b[l].astype(BF16), w_o[l].astype(BF16), ln1_g[l].reshape(1, d),
                                          ln1_b[l].reshape(1, d), wr_t, rb_col, tri, alpha)

        counts = cnt[:, 0].astype(I32)
        padded = ((counts + MOE_BM - 1) // MOE_BM) * MOE_BM
        pad_end = jnp.cumsum(padded)
        pad_start = pad_end - padded
        e_iota = jnp.arange(N_EXPERTS, dtype=I32)[:, None, None]
        start_of = jnp.sum(jnp.where(eidx[None] == e_iota, pad_start[:, None, None], 0), axis=0)
        dest = (start_of + rank).reshape(-1)
        n_active = (pad_end[-1] // MOE_BM).astype(I32).reshape(1)
        blk = jnp.minimum(jnp.arange(n_blk, dtype=I32), n_active[0] - 1) * MOE_BM
        blk_e = jnp.minimum(jnp.sum(pad_end[None, :] <= blk[:, None], axis=1), N_EXPERTS - 1).astype(I32)
        blk_all = jnp.arange(n_blk, dtype=I32)
        last_of_expert = jnp.any(pad_end[None, :] == (blk_all[:, None] + 1) * MOE_BM, axis=1)
        zflag = (last_of_expert | (blk_all >= n_active[0])).astype(I32)
        xs = _dispatch(dest, zflag, x1, rows, 512)
        ys = _experts(blk_e, n_active, xs, moe_w_gate, moe_w_up, moe_w_down, l)
        xf, xb = _combine(dest, x1, wts.T, ln2_g[l].reshape(1, d), ln2_b[l].reshape(1, d), ys, alpha, 256)
    return xf.reshape(b, s, d)
```

```python
import functools
import math

import jax
import jax.numpy as jnp
import numpy as np
from jax import lax
from jax.experimental import pallas as pl
from jax.experimental.pallas import tpu as pltpu

F32 = jnp.float32
BF16 = jnp.bfloat16
I32 = jnp.int32

NSA_HEADS = 8
NSA_GROUPS = 2
NSA_HPG = 4
NSA_DH = 64
CMP_LEN = 32
CMP_STRIDE = 16
SLC_BLOCK = 64
N_SELECT = 16
WINDOW = 512
QB = 256
FORCE_SCORE = 1.0e4
RET_HEADS = 8
RET_DK = 64
RET_DV = 128
RET_CHUNK = 128
ROPE_BASE = 10000.0
REL_BUCKETS = 32
REL_MAX_EXACT = 16
REL_MAX_DIST = 2048
N_EXPERTS = 16
N_GROUPS = 4
EPG = 4
D_EXPERT = 512
LN_EPS = 1e-5
GN_EPS = 1e-5
NEG_INF = -1e30

LANES = 128
VMEM_LIMIT = 56 * 1024 * 1024

SEL_TK = 512
AUG_ROWS = 16
MASK_BIG = 2.0 ** 100
SAT_REL = math.ceil(REL_MAX_EXACT * (REL_MAX_DIST / REL_MAX_EXACT)
                    ** ((REL_BUCKETS - 1 - REL_MAX_EXACT) / (REL_BUCKETS - REL_MAX_EXACT)))
N_OFF = -(-(SAT_REL + QB - 1) // QB) + 1
MOE_BM = 256
ROW_DMA_UNROLL = 8


def _cparams(sem):
    return pltpu.CompilerParams(dimension_semantics=sem, vmem_limit_bytes=VMEM_LIMIT)


def _dot(a, b):
    return jnp.dot(a, b, preferred_element_type=F32)


def _dot_nt(a, b):
    return lax.dot_general(a, b, (((1,), (1,)), ((), ())), preferred_element_type=F32)


def _mm_kernel(a_ref, b_ref, o_ref):
    o_ref[...] = _dot(a_ref[...], b_ref[...]).astype(o_ref.dtype)


def _matmul(a, b, out_dtype, tm, tn):
    m, k = a.shape
    n = b.shape[1]
    return pl.pallas_call(
        _mm_kernel,
        grid=(n // tn, m // tm),
        in_specs=[pl.BlockSpec((tm, k), lambda j, i: (i, 0)),
                  pl.BlockSpec((k, tn), lambda j, i: (0, j))],
        out_specs=pl.BlockSpec((tm, tn), lambda j, i: (i, j)),
        out_shape=jax.ShapeDtypeStruct((m, n), out_dtype),
        compiler_params=_cparams(("arbitrary", "arbitrary")),
        name="proj_matmul",
    )(a, b)


def _bias_kernel(tab_ref, rel_ref, o_ref, *, delta):
    rel = rel_ref[...]
    n = jnp.maximum(rel, 0)
    nf = jnp.maximum(n, 1).astype(F32)
    large = REL_MAX_EXACT + (jnp.log(nf / REL_MAX_EXACT) / math.log(REL_MAX_DIST / REL_MAX_EXACT)
                             * (REL_BUCKETS - REL_MAX_EXACT)).astype(I32)
    large = jnp.minimum(large, REL_BUCKETS - 1)
    bucket = jnp.where(n < REL_MAX_EXACT, n, large)
    eqs = [bucket == k for k in range(REL_BUCKETS)]
    for h in range(NSA_HEADS):
        acc = jnp.full(rel.shape, NEG_INF, F32)
        for k in range(REL_BUCKETS):
            acc = jnp.where(eqs[k], tab_ref[k * NSA_HEADS + h], acc)
        if delta:
            acc = acc - tab_ref[(REL_BUCKETS - 1) * NSA_HEADS + h]
        hh = h % NSA_HPG
        o_ref[h // NSA_HPG, :, hh * QB:(hh + 1) * QB] = jnp.where(rel >= 0, acc, NEG_INF)


def _bias_table(rel_bias_flat, rel, tr, delta=False):
    r, c = rel.shape
    return pl.pallas_call(
        functools.partial(_bias_kernel, delta=delta),
        grid_spec=pltpu.PrefetchScalarGridSpec(
            num_scalar_prefetch=1,
            grid=(r // tr,),
            in_specs=[pl.BlockSpec((tr, c), lambda i, tab: (i, 0))],
            out_specs=pl.BlockSpec((NSA_GROUPS, tr, NSA_HPG * c), lambda i, tab: (0, i, 0)),
        ),
        out_shape=jax.ShapeDtypeStruct((NSA_GROUPS, r, NSA_HPG * c), F32),
        compiler_params=_cparams(("arbitrary",)),
        name="t5_bias_table",
    )(rel_bias_flat, rel)


def _compress_kernel(x_ref, w1_ref, pos_ref, w2_ref, o_ref):
    ab = _dot(x_ref[...], w1_ref[...])
    nb = ab.shape[0]
    nxt = pltpu.roll(ab, nb - 1, axis=0)
    nxt = pltpu.roll(nxt, NSA_DH, axis=1)
    pre = (ab + nxt)[:, :NSA_DH] + pos_ref[...]
    hid = jax.nn.gelu(pre)
    out = _dot(hid.astype(BF16), w2_ref[...]).astype(BF16)
    rows = lax.broadcasted_iota(I32, (nb, nb), 0)
    cols = lax.broadcasted_iota(I32, (nb, nb), 1)
    flip = jnp.where(rows + cols == nb - 1, 1.0, 0.0).astype(BF16)
    o_ref[...] = _dot(flip, out).astype(o_ref.dtype)


def _compress(x16, w1ab, posterm, w2):
    _, b, g, nb, wd = x16.shape
    return pl.pallas_call(
        _compress_kernel,
        grid=(2, b, g),
        in_specs=[pl.BlockSpec((None, None, None, nb, wd), lambda s, bi, gi: (s, bi, gi, 0, 0)),
                  pl.BlockSpec((None, wd, 2 * NSA_DH), lambda s, bi, gi: (s, 0, 0)),
                  pl.BlockSpec((None, 1, NSA_DH), lambda s, bi, gi: (s, 0, 0)),
                  pl.BlockSpec((None, NSA_DH, NSA_DH), lambda s, bi, gi: (s, 0, 0))],
        out_specs=pl.BlockSpec((None, None, None, nb, NSA_DH), lambda s, bi, gi: (s, bi, gi, 0, 0)),
        out_shape=jax.ShapeDtypeStruct((2, b, g, nb, NSA_DH), BF16),
        compiler_params=_cparams(("arbitrary",) * 3),
        name="nsa_compress",
    )(x16, w1ab, posterm, w2)


def _posterm_kernel(p_ref, w_ref, o_ref):
    o_ref[...] = _dot(p_ref[...], w_ref[...])[0:1]


def _posterm(pos8, w1):
    return pl.pallas_call(
        _posterm_kernel,
        grid=(2,),
        in_specs=[pl.BlockSpec((None, 8, pos8.shape[2]), lambda s: (s, 0, 0)),
                  pl.BlockSpec((None, w1.shape[1], NSA_DH), lambda s: (s, 0, 0))],
        out_specs=pl.BlockSpec((None, 1, NSA_DH), lambda s: (s, 0, 0)),
        out_shape=jax.ShapeDtypeStruct((2, 1, NSA_DH), F32),
        name="nsa_posterm",
    )(pos8, w1)


def _eye(n, scale=1.0):
    rows = lax.broadcasted_iota(I32, (n, n), 0)
    cols = lax.broadcasted_iota(I32, (n, n), 1)
    return jnp.where(rows == cols, scale, 0.0).astype(BF16)


def _nsa_kernel(q_ref, qn_ref, kc_ref, vcin_ref, ksin_ref, vsin_ref, kwin_ref, vwin_ref, gate_ref,
                cb_ref, bt_ref, wb_ref, cov_ref, b31_ref, bcols_ref, o_ref,
                s_ref, ks_ref, vs_ref, kw_ref, vw_ref, vc_ref, qt_ref, *, n_sel):
    gi = pl.program_id(1)
    i = pl.program_id(2)
    q0 = i * QB
    r4 = NSA_HPG * QB
    ns = cov_ref.shape[0]
    eye_dh = _eye(NSA_DH)

    def group_lanes(x):
        return jnp.where(gi == 0, x[:, 0:NSA_DH], x[:, NSA_DH:2 * NSA_DH])

    def q_transposed(ref):
        eye_q = _eye(NSA_DH, NSA_DH ** -0.5)
        qb = ref[...]
        return jnp.concatenate([_dot_nt(eye_q, qb[:, h * NSA_DH:(h + 1) * NSA_DH]) for h in range(NSA_HPG)],
                               axis=1).astype(BF16)

    @pl.when(i == 0)
    def _():
        s_len = ksin_ref.shape[0]
        ones_rows = jnp.where(lax.broadcasted_iota(I32, (AUG_ROWS, SEL_TK), 0) == 0, 1.0, 0.0).astype(BF16)
        ones_rows_qb = jnp.where(lax.broadcasted_iota(I32, (AUG_ROWS, QB), 0) == 0, 1.0, 0.0).astype(BF16)
        tail = jnp.where(lax.broadcasted_iota(I32, (SEL_TK, NSA_DH), 1) < 2, 1.0, 0.0).astype(BF16)
        depth = ks_ref.shape[-1]
        vc_ref[...] = _dot_nt(eye_dh, vcin_ref[...]).astype(BF16)
        qt_ref[...] = q_transposed(q_ref)
        pad_flag = jnp.where(lax.broadcasted_iota(I32, (WINDOW, 2 * NSA_DH), 1) == NSA_DH, 1.0, 0.0)
        kw_ref[0:WINDOW, :] = pad_flag.astype(BF16)
        for u in range(WINDOW // QB):
            vw_ref[u] = jnp.zeros(vw_ref.shape[1:], BF16)

        def chunk(c, _):
            r0 = pl.multiple_of(c * SEL_TK, SEL_TK)
            rows = pl.ds(r0, SEL_TK)
            k_aug = jnp.concatenate([group_lanes(ksin_ref[rows, :]), tail], axis=1)
            ks_ref[rows, 0:ns] = bcols_ref[rows, :]
            ks_ref[rows, ns:ns + 2 * NSA_DH] = k_aug
            if depth > ns + 2 * NSA_DH:
                ks_ref[rows, ns + 2 * NSA_DH:depth] = jnp.zeros((SEL_TK, depth - ns - 2 * NSA_DH), BF16)
            v_t = _dot_nt(eye_dh, group_lanes(vsin_ref[rows, :])).astype(BF16)
            vs_ref[c] = jnp.concatenate([v_t, ones_rows], axis=0)
            kw_ref[pl.ds(WINDOW + r0, SEL_TK), :] = jnp.concatenate(
                [group_lanes(kwin_ref[rows, :]), jnp.zeros((SEL_TK, NSA_DH), BF16)], axis=1)
            vw_t = _dot_nt(eye_dh, group_lanes(vwin_ref[rows, :])).astype(BF16)
            for u in range(SEL_TK // QB):
                vw_ref[WINDOW // QB + c * (SEL_TK // QB) + u] = jnp.concatenate(
                    [vw_t[:, u * QB:(u + 1) * QB], ones_rows_qb], axis=0)
            return 0

        lax.fori_loop(0, s_len // SEL_TK, chunk, 0)

    q = qt_ref[...]

    step_rows = QB // CMP_STRIDE
    cb = cb_ref[pl.ds(pl.multiple_of(i * step_rows, step_rows), kc_ref.shape[0]), :]
    s_c = _dot(kc_ref[...], q) + cb
    m_c = jnp.maximum(jnp.max(s_c, axis=0, keepdims=True), 0.1 * NEG_INF)
    p_c = jnp.exp(s_c - m_c)
    l_c = jnp.sum(p_c, axis=0, keepdims=True)
    p_c = p_c / jnp.maximum(l_c, 1e-30)
    o_c = _dot(vc_ref[...], p_c.astype(BF16))

    psum = p_c[:, 0:QB] + p_c[:, QB:2 * QB] + p_c[:, 2 * QB:3 * QB] + p_c[:, 3 * QB:4 * QB]
    p_hi = psum.astype(BF16)
    p_lo = (psum - p_hi.astype(F32)).astype(BF16)
    cov = cov_ref[...]
    imp_t = _dot(cov, p_hi) + _dot(cov, p_lo)
    ns = imp_t.shape[0]

    wk = WINDOW + QB
    kwin = kw_ref[pl.ds(pl.multiple_of(q0, QB), wk), :]
    pad_mask_rows = jnp.where(lax.broadcasted_iota(I32, (NSA_DH, r4), 0) == 0, -MASK_BIG, 0.0).astype(BF16)
    s_w = _dot(kwin, jnp.concatenate([q, pad_mask_rows], axis=0)) + wb_ref[...]
    m_w = jnp.max(s_w, axis=0, keepdims=True)
    p_w = jnp.exp(s_w - m_w).astype(BF16)
    acc_w = jnp.zeros((NSA_DH + AUG_ROWS, r4), F32)
    for u in range(wk // QB):
        acc_w = acc_w + _dot(vw_ref[i + u], p_w[u * QB:(u + 1) * QB])
    o_w = acc_w[0:NSA_DH] / acc_w[NSA_DH:NSA_DH + 1]

    jidx = lax.broadcasted_iota(I32, (ns, QB), 0)
    t = q0 + lax.broadcasted_iota(I32, (ns, QB), 1)
    cur = jnp.right_shift(t, 6)
    eligible = jidx <= cur
    forced = (jidx == 0) | (jidx == cur) | (jidx == cur - 1)
    score = jnp.where(eligible, jnp.where(forced, -2.0, imp_t), -1.0)
    sel = jnp.where(forced, 1.0, 0.0)
    for _ in range(n_sel - 3):
        mx = jnp.max(score, axis=0, keepdims=True)
        first = jnp.min(jnp.where(score == mx, jidx, ns), axis=0, keepdims=True)
        pick = jidx == first
        sel = jnp.where(pick, 1.0, sel)
        score = jnp.where(pick, -2.0, score)
    notsel = jnp.where(eligible, 1.0 - sel, 1.0).astype(BF16)

    depth = ks_ref.shape[-1]
    pad_rows = depth - ns - NSA_DH - AUG_ROWS
    qa = jnp.concatenate([jnp.concatenate([notsel] * NSA_HPG, axis=1), q, b31_ref[...],
                          jnp.zeros((pad_rows, r4), BF16)], axis=0)
    sub = SEL_TK // QB
    n_tiles = i // sub + 1
    n_far = jnp.maximum(i - N_OFF + 2, 0) // sub

    def put_scores(jt, slot):
        k0 = pl.multiple_of(jt * SEL_TK, SEL_TK)
        s_ref[slot] = _dot(ks_ref[pl.ds(k0, SEL_TK), :], qa)

    def absorb(jt, slot, carry, near):
        m_prev, acc = carry
        s = s_ref[slot]
        if near:
            offs = [jnp.clip(i - sub * jt - u, 0, N_OFF - 1) for u in range(sub)]
            s = s + jnp.concatenate([bt_ref[o] for o in offs], axis=0)
        m_new = jnp.maximum(m_prev, jnp.max(s, axis=0, keepdims=True))
        alpha = jnp.exp(m_prev - m_new)
        p = jnp.exp(s - m_new).astype(BF16)
        return m_new, alpha * acc + _dot(vs_ref[jt], p)

    def pair_body(pp, carry, near):
        ja = 2 * pp
        put_scores(ja + 1, 1)
        carry = absorb(ja, 0, carry, near)
        put_scores(jnp.minimum(ja + 2, last_tile), 0)
        return absorb(ja + 1, 1, carry, near)

    n_pairs = (n_tiles + 1) // 2
    last_tile = 2 * n_pairs - 1
    far_pairs = n_far // 2
    put_scores(0, 0)
    carry = (jnp.full((1, r4), -1e38, F32), jnp.zeros((NSA_DH + AUG_ROWS, r4), F32))
    carry = lax.fori_loop(0, far_pairs, functools.partial(pair_body, near=False), carry)
    _, acc_s = lax.fori_loop(far_pairs, n_pairs, functools.partial(pair_body, near=True), carry)
    o_s = acc_s[0:NSA_DH] / acc_s[NSA_DH:NSA_DH + 1]

    gates = jax.nn.sigmoid(gate_ref[...])
    o = (gates[0:1] * o_c + gates[1:2] * o_s + gates[2:3] * o_w).astype(BF16)
    qt_ref[...] = q_transposed(qn_ref)
    eye_l = _eye(LANES)
    o_ref[...] = jnp.concatenate(
        [jnp.concatenate([_dot_nt(eye_l, o[:, h * QB + u * LANES:h * QB + (u + 1) * LANES])
                          for u in range(QB // LANES)], axis=0) for h in range(NSA_HPG)],
        axis=1).astype(o_ref.dtype)


def _nsa_attention(pb, c_qa, c_kv, kvc, gates, cb, bt, wb, cov_t, b31, blockcols, b, s, n_sel):
    g, dh = NSA_GROUPS, NSA_DH
    r4 = NSA_HPG * QB
    nqb = s // QB
    nc = kvc.shape[3]
    ns = cov_t.shape[0]
    depth = -(-(ns + 2 * dh) // LANES) * LANES
    n_tiles = s // SEL_TK
    assert n_tiles % 2 == 0, "the selected sweep consumes key tiles in pairs"
    qw = NSA_HPG * dh
    once = pl.Buffered(1)
    slab = lambda k: pl.BlockSpec((s, 2 * dh), lambda bi, gi, i: (bi, c_kv // (2 * dh) + k), pipeline_mode=once)
    per_g = lambda a: pl.BlockSpec((None,) + a.shape[1:], lambda bi, gi, i: (gi,) + (0,) * (a.ndim - 1),
                                   pipeline_mode=once)
    const = lambda a: pl.BlockSpec(a.shape, lambda bi, gi, i: (0,) * a.ndim, pipeline_mode=once)
    return pl.pallas_call(
        functools.partial(_nsa_kernel, n_sel=n_sel),
        grid=(b, g, nqb),
        in_specs=[
            pl.BlockSpec((QB, qw), lambda bi, gi, i: (bi * nqb + i, c_qa // qw + gi)),
            pl.BlockSpec((QB, qw), lambda bi, gi, i: (bi * nqb + jnp.minimum(i + 1, nqb - 1), c_qa // qw + gi)),
            pl.BlockSpec((None, None, None, nc, dh), lambda bi, gi, i: (0, bi, gi, 0, 0)),
            pl.BlockSpec((None, None, None, nc, dh), lambda bi, gi, i: (1, bi, gi, 0, 0)),
            slab(2), slab(3), slab(4), slab(5),
            pl.BlockSpec((None, None, None, 3, r4), lambda bi, gi, i: (bi, gi, i, 0, 0)),
            per_g(cb), per_g(bt), per_g(wb), const(cov_t), per_g(b31), const(blockcols),
        ],
        out_specs=pl.BlockSpec((QB, qw), lambda bi, gi, i: (bi * nqb + i, gi)),
        out_shape=jax.ShapeDtypeStruct((b * s, g * qw), BF16),
        scratch_shapes=[pltpu.VMEM((2, SEL_TK, r4), F32),
                        pltpu.VMEM((s, depth), BF16),
                        pltpu.VMEM((n_tiles, dh + AUG_ROWS, SEL_TK), BF16),
                        pltpu.VMEM((s + WINDOW, 2 * dh), BF16),
                        pltpu.VMEM(((s + WINDOW) // QB, dh + AUG_ROWS, QB), BF16),
                        pltpu.VMEM((dh, nc), BF16),
                        pltpu.VMEM((dh, r4), BF16)],
        compiler_params=_cparams(("arbitrary",) * 3),
        name="nsa_attention",
    )(pb, pb, kvc, kvc, pb, pb, pb, pb, gates, cb, bt, wb, cov_t, b31, blockcols)


def _ret_kernel(q_ref, k_ref, v_ref, g_ref, cos_ref, sin_ref, dm_ref, qd_ref, kd_ref, cd_ref,
                o_ref, state_ref):
    n = pl.program_id(1)

    @pl.when(n == 0)
    def _():
        state_ref[...] = jnp.zeros_like(state_ref)

    lane = lax.broadcasted_iota(I32, (RET_CHUNK, LANES), 1)
    first_half = (lane & (RET_DK - 1)) < (RET_DK // 2)
    cosv = cos_ref[...]
    sinv = sin_ref[...]

    def rot(x):
        partner = jnp.where(first_half, pltpu.roll(x, LANES - RET_DK // 2, axis=1),
                            pltpu.roll(x, RET_DK // 2, axis=1))
        return x * cosv + partner * sinv

    for hp in range(RET_HEADS // 2):
        ls = slice(hp * LANES, (hp + 1) * LANES)
        q2 = rot(q_ref[:, ls])
        k2 = rot(k_ref[:, ls]) * (RET_DK ** -0.5)
        qdec = (q2 * qd_ref[:, ls]).astype(BF16)
        kdec_t = (k2 * kd_ref[:, ls]).T.astype(BF16)
        q2b = q2.astype(BF16)
        k2b = k2.astype(BF16)
        for e in range(2):
            h = 2 * hp + e
            hs = slice(e * RET_DK, (e + 1) * RET_DK)
            vs = slice(h * RET_DV, (h + 1) * RET_DV)
            v = v_ref[:, vs]
            st = state_ref[h]
            sc = _dot_nt(q2b[:, hs], k2b[:, hs]) * dm_ref[h]
            y = _dot(sc.astype(BF16), v) + _dot(qdec[:, hs], st.astype(BF16))
            state_ref[h] = cd_ref[h] * st + _dot(kdec_t[hs, :], v)
            mu = jnp.mean(y, axis=-1, keepdims=True)
            var = jnp.mean(jnp.square(y - mu), axis=-1, keepdims=True)
            yn = (y - mu) * lax.rsqrt(var + GN_EPS)
            o_ref[:, vs] = (jax.nn.silu(g_ref[:, vs]) * yn).astype(o_ref.dtype)


def _retention(pf, pb, b, s, col_q, col_k, col_v, col_g, tabs):
    cosf, sinf, dm, qd, kd, cd = tabs
    n = s // RET_CHUNK
    t = b * s
    c = RET_CHUNK
    wqk = RET_HEADS * RET_DK
    wv = RET_HEADS * RET_DV
    row = lambda bi, ni: bi * n + ni
    full = lambda a: pl.BlockSpec(a.shape, lambda bi, ni: (0,) * a.ndim)
    return pl.pallas_call(
        _ret_kernel,
        grid=(b, n),
        in_specs=[
            pl.BlockSpec((c, wqk), lambda bi, ni: (row(bi, ni), col_q // wqk)),
            pl.BlockSpec((c, wqk), lambda bi, ni: (row(bi, ni), col_k // wqk)),
            pl.BlockSpec((c, wv), lambda bi, ni: (row(bi, ni), col_v // wv)),
            pl.BlockSpec((c, wv), lambda bi, ni: (row(bi, ni), col_g // wv)),
            pl.BlockSpec((c, LANES), lambda bi, ni: (ni, 0)),
            pl.BlockSpec((c, LANES), lambda bi, ni: (ni, 0)),
            full(dm), full(qd), full(kd), full(cd),
        ],
        out_specs=pl.BlockSpec((c, wv), lambda bi, ni: (row(bi, ni), 0)),
        out_shape=jax.ShapeDtypeStruct((t, wv), BF16),
        scratch_shapes=[pltpu.VMEM((RET_HEADS, RET_DK, RET_DV), F32)],
        compiler_params=_cparams(("arbitrary",) * 2),
        name="retention",
    )(pf, pf, pb, pf, cosf, sinf, dm, qd, kd, cd)


def _retention_tables(s):
    half = RET_DK // 2
    inv = ROPE_BASE ** (-jnp.arange(half, dtype=F32) * 2.0 / RET_DK)
    ang = jnp.arange(s, dtype=F32)[:, None] * inv[None, :]
    cos, sin = jnp.cos(ang), jnp.sin(ang)
    cosf = jnp.concatenate([cos, cos, cos, cos], axis=1)
    sinf = jnp.concatenate([-sin, sin, -sin, sin], axis=1)
    lg = jnp.log(1.0 - 2.0 ** (-5.0 - jnp.arange(RET_HEADS, dtype=F32)))
    i = jnp.arange(RET_CHUNK, dtype=F32)
    diff = i[:, None] - i[None, :]
    dm = jnp.where(diff >= 0, jnp.exp(jnp.maximum(diff, 0.0)[None] * lg[:, None, None]), 0.0)
    kdec = jnp.exp((RET_CHUNK - 1.0 - i)[None, :] * lg[:, None])
    qdec = jnp.exp((i + 1.0)[None, :] * lg[:, None])
    expand = lambda a: jnp.repeat(a.T, RET_DK, axis=1)
    cd = jnp.broadcast_to(jnp.exp(RET_CHUNK * lg)[:, None, None], (RET_HEADS, 1, LANES))
    return cosf, sinf, dm, expand(qdec), expand(kdec), cd


def _layer_norm(z, g, b):
    mu = jnp.mean(z, axis=-1, keepdims=True)
    var = jnp.mean(jnp.square(z - mu), axis=-1, keepdims=True)
    return (z - mu) * lax.rsqrt(var + LN_EPS) * g + b


def _merge_kernel(ya_ref, yb_ref, ma_ref, mb_ref, x_ref, wa_ref, wb_ref, wo_ref, g_ref, b_ref,
                  wr_ref, rb_ref, tri_ref, o_ref, eidx_ref, wts_ref, rank_ref, cnt_ref, carry_ref, *, alpha):
    merged = (jax.nn.sigmoid(ma_ref[...]) * _dot(ya_ref[...], wa_ref[...])
              + jax.nn.sigmoid(mb_ref[...]) * _dot(yb_ref[...], wb_ref[...]))
    z = alpha * x_ref[...] + _dot(merged.astype(BF16), wo_ref[...])
    x1 = _layer_norm(z, g_ref[...], b_ref[...])
    o_ref[...] = x1
    _route(x1, wr_ref, rb_ref, tri_ref, eidx_ref, wts_ref, rank_ref, cnt_ref, carry_ref)


def _merge(ya, yb, pf, col_ma, col_mb, x, wa, wb, wo, g, bb, wr_t, rb_col, tri, alpha):
    t, d = x.shape
    tm = tri.shape[0]
    full = lambda a: pl.BlockSpec(a.shape, lambda i: (0,) * a.ndim)
    out2 = lambda dt: jax.ShapeDtypeStruct((2, t), dt)
    row2 = pl.BlockSpec((2, tm), lambda i: (0, i))
    return pl.pallas_call(
        functools.partial(_merge_kernel, alpha=alpha),
        grid=(t // tm,),
        in_specs=[pl.BlockSpec((tm, ya.shape[1]), lambda i: (i, 0)),
                  pl.BlockSpec((tm, yb.shape[1]), lambda i: (i, 0)),
                  pl.BlockSpec((tm, d), lambda i: (i, col_ma // d)),
                  pl.BlockSpec((tm, d), lambda i: (i, col_mb // d)),
                  pl.BlockSpec((tm, d), lambda i: (i, 0)),
                  full(wa), full(wb), full(wo), full(g), full(bb), full(wr_t), full(rb_col), full(tri)],
        out_specs=[pl.BlockSpec((tm, d), lambda i: (i, 0)), row2, row2, row2,
                   pl.BlockSpec((N_EXPERTS, LANES), lambda i: (0, 0))],
        out_shape=[jax.ShapeDtypeStruct((t, d), F32), out2(I32), out2(F32), out2(I32),
                   jax.ShapeDtypeStruct((N_EXPERTS, LANES), F32)],
        scratch_shapes=[pltpu.VMEM((N_EXPERTS, LANES), F32)],
        compiler_params=_cparams(("arbitrary",)),
        name="merge_ln1_router",
    )(ya, yb, pf, pf, x, wa, wb, wo, g, bb, wr_t, rb_col, tri)


def _route(x, wr_ref, rb_ref, tri_ref, eidx_ref, wts_ref, rank_ref, cnt_ref, carry_ref):
    step = pl.program_id(0)

    @pl.when(step == 0)
    def _():
        carry_ref[...] = jnp.zeros_like(carry_ref)

    logits = lax.dot_general(wr_ref[...], x, (((1,), (1,)), ((), ())),
                             precision=lax.Precision.HIGHEST, preferred_element_type=F32)
    tm = logits.shape[1]
    mx = jnp.max(logits, axis=0, keepdims=True)
    ex = jnp.exp(logits - mx)
    probs = ex / jnp.sum(ex, axis=0, keepdims=True)
    sel = probs + rb_ref[...]
    rows = [sel[e:e + 1] for e in range(N_EXPERTS)]
    gscore = []
    for gi in range(N_GROUPS):
        r = rows[gi * EPG:(gi + 1) * EPG]
        best2 = None
        for a in range(EPG):
            for c in range(a + 1, EPG):
                pair = r[a] + r[c]
                best2 = pair if best2 is None else jnp.maximum(best2, pair)
        gscore.append(best2)
    best = jnp.zeros((1, tm), I32)
    bscore = gscore[0]
    for gi in range(1, N_GROUPS):
        better = gscore[gi] > bscore
        best = jnp.where(better, gi, best)
        bscore = jnp.where(better, gscore[gi], bscore)
    ing = []
    for a in range(EPG):
        v = rows[a]
        for gi in range(1, N_GROUPS):
            v = jnp.where(best == gi, rows[gi * EPG + a], v)
        ing.append(v)
    i1 = jnp.zeros((1, tm), I32)
    v1 = ing[0]
    for a in range(1, EPG):
        better = ing[a] > v1
        i1 = jnp.where(better, a, i1)
        v1 = jnp.where(better, ing[a], v1)
    i2 = jnp.full((1, tm), -1, I32)
    v2 = jnp.full((1, tm), -jnp.inf, F32)
    for a in range(EPG):
        better = (i1 != a) & (ing[a] > v2)
        i2 = jnp.where(better, a, i2)
        v2 = jnp.where(better, ing[a], v2)
    e0 = best * EPG + i1
    e1 = best * EPG + i2
    eio = lax.broadcasted_iota(I32, (N_EXPERTS, tm), 0)
    oh0 = (eio == e0).astype(F32)
    oh1 = (eio == e1).astype(F32)
    p0 = jnp.sum(oh0 * probs, axis=0, keepdims=True)
    p1 = jnp.sum(oh1 * probs, axis=0, keepdims=True)
    den = p0 + p1
    eidx_ref[0:1, :] = e0
    eidx_ref[1:2, :] = e1
    wts_ref[0:1, :] = p0 / den
    wts_ref[1:2, :] = p1 / den
    tot = oh0 + oh1
    prefix = _dot(tot.astype(BF16), tri_ref[...]) + carry_ref[:, 0:1]
    rank_ref[0:1, :] = jnp.sum(oh0 * prefix, axis=0, keepdims=True).astype(I32)
    rank_ref[1:2, :] = jnp.sum(oh1 * prefix, axis=0, keepdims=True).astype(I32)
    carry_ref[...] = carry_ref[...] + jnp.sum(tot, axis=1, keepdims=True)
    cnt_ref[...] = carry_ref[...]


def _dispatch_kernel(dest_ref, zflag_ref, x_ref, xs_ref, zero_ref, sem, zsem, *, t_total):
    step = pl.program_id(0)
    tm = x_ref.shape[0]

    @pl.when(step == 0)
    def _():
        zero_ref[...] = jnp.zeros_like(zero_ref)

        def zero_block(bi, _):
            @pl.when(zflag_ref[bi] == 1)
            def _():
                r0 = pl.multiple_of(bi * MOE_BM, MOE_BM)
                cp = pltpu.make_async_copy(zero_ref, xs_ref.at[pl.ds(r0, MOE_BM)], zsem)
                cp.start()
                cp.wait()
            return 0

        lax.fori_loop(0, zflag_ref.shape[0], zero_block, 0)

    def copy(r, slot):
        dst = dest_ref[slot * t_total + step * tm + r]
        return pltpu.make_async_copy(x_ref.at[pl.ds(r, 1)], xs_ref.at[pl.ds(dst, 1)], sem)

    def issue(r, _):
        copy(r, 0).start()
        copy(r, 1).start()
        return 0

    def drain(r, _):
        copy(r, 0).wait()
        copy(r, 1).wait()
        return 0

    lax.fori_loop(0, tm, issue, 0, unroll=ROW_DMA_UNROLL)
    lax.fori_loop(0, tm, drain, 0, unroll=ROW_DMA_UNROLL)


def _dispatch(dest_flat, zflag, x, rows, tm):
    t, d = x.shape
    return pl.pallas_call(
        functools.partial(_dispatch_kernel, t_total=t),
        grid_spec=pltpu.PrefetchScalarGridSpec(
            num_scalar_prefetch=2,
            grid=(t // tm,),
            in_specs=[pl.BlockSpec((tm, d), lambda i, dr, zf: (i, 0))],
            out_specs=pl.BlockSpec(memory_space=pl.ANY),
            scratch_shapes=[pltpu.VMEM((MOE_BM, d), x.dtype), pltpu.SemaphoreType.DMA,
                            pltpu.SemaphoreType.DMA],
        ),
        out_shape=jax.ShapeDtypeStruct((rows, d), x.dtype),
        compiler_params=pltpu.CompilerParams(dimension_semantics=("arbitrary",), has_side_effects=True,
                                             vmem_limit_bytes=VMEM_LIMIT),
        name="moe_dispatch",
    )(dest_flat, zflag, x)


def _expert_kernel(be_ref, na_ref, x_ref, wg_ref, wu_ref, wd_ref, o_ref, wg_s, wu_s, wd_s):
    i = pl.program_id(0)
    active = i < na_ref[0]

    @pl.when((i == 0) | (be_ref[i] != be_ref[jnp.maximum(i - 1, 0)]))
    def _():
        wg_s[...] = wg_ref[...].astype(BF16)
        wu_s[...] = wu_ref[...].astype(BF16)
        wd_s[...] = wd_ref[...].astype(BF16)

    @pl.when(active)
    def _():
        xb = x_ref[...].astype(BF16)
        gate = _dot(xb, wg_s[...])
        up = _dot(xb, wu_s[...])
        hid = (jax.nn.silu(gate) * up).astype(BF16)
        o_ref[...] = _dot(hid, wd_s[...])

    @pl.when(jnp.logical_not(active))
    def _():
        o_ref[...] = jnp.zeros_like(o_ref)


def _experts(blk_e, n_active, xs, wg, wu, wd, layer):
    rows, d = xs.shape
    nb = rows // MOE_BM
    act = lambda i, be, na: jnp.minimum(i, na[0] - 1)
    wspec = lambda a: pl.BlockSpec((None, None) + a.shape[2:], lambda i, be, na: (layer, be[i], 0, 0))
    return pl.pallas_call(
        _expert_kernel,
        grid_spec=pltpu.PrefetchScalarGridSpec(
            num_scalar_prefetch=2,
            grid=(nb,),
            in_specs=[pl.BlockSpec((MOE_BM, d), lambda i, be, na: (act(i, be, na), 0)),
                      wspec(wg), wspec(wu), wspec(wd)],
            out_specs=pl.BlockSpec((MOE_BM, d), lambda i, be, na: (i, 0)),
            scratch_shapes=[pltpu.VMEM(wg.shape[2:], BF16), pltpu.VMEM(wu.shape[2:], BF16),
                            pltpu.VMEM(wd.shape[2:], BF16)],
        ),
        out_shape=jax.ShapeDtypeStruct((rows, d), F32),
        compiler_params=_cparams(("arbitrary",)),
        name="moe_experts",
    )(blk_e, n_active, xs, wg, wu, wd)


def _combine_kernel(dest_ref, x_ref, w_ref, g_ref, b_ref, ys_ref, o_ref, ob_ref, buf_ref, sem, *, alpha, t_total):
    step = pl.program_id(0)
    tm = x_ref.shape[0]

    def copy(r, slot):
        src = dest_ref[slot * t_total + step * tm + r]
        return pltpu.make_async_copy(ys_ref.at[pl.ds(src, 1)], buf_ref.at[slot, pl.ds(r, 1)], sem)

    def issue(r, _):
        copy(r, 0).start()
        copy(r, 1).start()
        return 0

    def drain(r, _):
        copy(r, 0).wait()
        copy(r, 1).wait()
        return 0

    lax.fori_loop(0, tm, issue, 0, unroll=ROW_DMA_UNROLL)
    lax.fori_loop(0, tm, drain, 0, unroll=ROW_DMA_UNROLL)
    w = w_ref[...]
    z = alpha * x_ref[...] + w[:, 0:1] * buf_ref[0] + w[:, 1:2] * buf_ref[1]
    y = _layer_norm(z, g_ref[...], b_ref[...])
    o_ref[...] = y
    ob_ref[...] = y.astype(BF16)


def _combine(dest_flat, x, wts_t, g, bb, ys, alpha, tm):
    t, d = x.shape
    return pl.pallas_call(
        functools.partial(_combine_kernel, alpha=alpha, t_total=t),
        grid_spec=pltpu.PrefetchScalarGridSpec(
            num_scalar_prefetch=1,
            grid=(t // tm,),
            in_specs=[pl.BlockSpec((tm, d), lambda i, dr: (i, 0)),
                      pl.BlockSpec((tm, 2), lambda i, dr: (i, 0)),
                      pl.BlockSpec(g.shape, lambda i, dr: (0, 0)),
                      pl.BlockSpec(bb.shape, lambda i, dr: (0, 0)),
                      pl.BlockSpec(memory_space=pl.ANY)],
            out_specs=[pl.BlockSpec((tm, d), lambda i, dr: (i, 0)), pl.BlockSpec((tm, d), lambda i, dr: (i, 0))],
            scratch_shapes=[pltpu.VMEM((2, tm, d), F32), pltpu.SemaphoreType.DMA],
        ),
        out_shape=[jax.ShapeDtypeStruct((t, d), F32), jax.ShapeDtypeStruct((t, d), BF16)],
        compiler_params=_cparams(("arbitrary",)),
        name="moe_combine_ln2",
    )(dest_flat, x, wts_t, g, bb, ys)


def _attention_tables(rel_bias, s):
    nc = s // CMP_STRIDE
    nqb = s // QB
    iq = jnp.arange(QB, dtype=I32)[None, None, :]
    x = jnp.arange(nc + (QB // CMP_STRIDE) * nqb, dtype=I32)[:, None]
    rel_c = iq[0] + CMP_STRIDE * (x - nc + 1) - (CMP_LEN - 1)
    jk = jnp.arange(QB, dtype=I32)[None, :, None]
    rel_t = jnp.arange(N_OFF, dtype=I32)[:, None, None] * QB + iq - jk
    jw = jnp.arange(WINDOW + QB, dtype=I32)[:, None]
    rel_w = iq[0] - jw + WINDOW
    rel_w = jnp.where(rel_w < WINDOW, rel_w, -1)
    flat = rel_bias.reshape(-1)
    r4 = NSA_HPG * QB
    cb = _bias_table(flat, jnp.maximum(rel_c, -1), rel_c.shape[0])
    bt = _bias_table(flat, jnp.maximum(rel_t, -1).reshape(N_OFF * QB, QB), QB, delta=True)
    bt = bt.reshape(NSA_GROUPS, N_OFF, QB, r4)
    wb = _bias_table(flat, jnp.maximum(rel_w, -1), WINDOW + QB)
    sat = jnp.repeat(rel_bias[REL_BUCKETS - 1].reshape(NSA_GROUPS, 1, NSA_HPG), QB, axis=2)
    hi = sat.astype(BF16)
    lo = (sat - hi.astype(F32)).astype(BF16)
    b31 = jnp.concatenate([hi, lo, jnp.zeros((NSA_GROUPS, AUG_ROWS - 2, r4), BF16)], axis=1)
    return cb, bt, wb, b31


def _selection_tables(s):
    nc = s // CMP_STRIDE
    ns = s // SLC_BLOCK
    c_start = np.arange(nc) * CMP_STRIDE
    s_start = np.arange(ns) * SLC_BLOCK
    cover = np.maximum(np.minimum(c_start[:, None] + CMP_LEN, s_start[None, :] + SLC_BLOCK)
                       - np.maximum(c_start[:, None], s_start[None, :]), 0).astype(np.float32) / CMP_LEN
    cover[nc - 1] = 0.0
    cover_rev_t = np.ascontiguousarray(cover[::-1].T)
    blockcols = np.where(np.arange(s)[:, None] // SLC_BLOCK == np.arange(ns)[None, :], -MASK_BIG, 0.0)
    return jnp.asarray(cover_rev_t, BF16), jnp.asarray(blockcols, BF16)


def _split_in_weights(w_in_l):
    o = np.cumsum((0, 512, 128, 128, 128, 128, 128, 128, 24, 512, 512, 1024, 1024, 1024, 1024))
    seg = lambda a: w_in_l[:, o[a]:o[a + 1]]
    wb = jnp.concatenate([seg(10), seg(0), seg(1), seg(2), seg(3), seg(4), seg(5), seg(6)], axis=1)
    gpad = jnp.pad(seg(7), ((0, 0), (0, LANES - 24)))
    wf = jnp.concatenate([seg(8), seg(9), seg(11), seg(12), seg(13), gpad], axis=1)
    return wb.astype(BF16), wf.astype(BF16)


def kernel(x, w_in, cmp_pos, cmp_w1, cmp_w2, w_br_a, w_br_b, w_o, ln1_g, ln1_b, w_router, router_bias,
           moe_w_gate, moe_w_up, moe_w_down, ln2_g, ln2_b, rel_bias):
    b, s, d = x.shape
    depth = w_in.shape[0]
    t = b * s
    alpha = (2 * depth) ** 0.25
    g, r, dh = NSA_GROUPS, NSA_HPG, NSA_DH
    nqb = s // QB
    nb16 = s // CMP_STRIDE
    n_sel = min(N_SELECT, s // SLC_BLOCK)

    cb, bt, wb, b31 = _attention_tables(rel_bias, s)
    cov_t, blockcols = _selection_tables(s)
    ret_tabs = _retention_tables(s)
    tri = jnp.asarray(np.triu(np.ones((512, 512), np.float32), 1), BF16)
    wr_t = w_router.T
    rb_col = router_bias.reshape(N_EXPERTS, 1)
    n_blk = t * 2 // MOE_BM + N_EXPERTS
    rows = n_blk * MOE_BM

    xf = x.reshape(t, d)
    xb = xf.astype(BF16)
    for l in range(depth):
        w_b, w_f = _split_in_weights(w_in[l])
        pb =_matmul(xb, w_b, BF16, 1024, 1152)
        pf = _matmul(xb, w_f, F32, 1024, 1408)
        c_qa = RET_HEADS * RET_DV
        c_kv = c_qa + NSA_HEADS * dh

        def kv_heads(slab):
            col = c_kv + slab * g * dh
            return pb[:, col:col + g * dh].reshape(b, s, g, dh).transpose(0, 2, 1, 3)
        x16 = jnp.stack([kv_heads(0), kv_heads(1)]).reshape(2, b, g, nb16, CMP_STRIDE * dh)
        w1 = cmp_w1[l].astype(BF16)
        half = CMP_STRIDE * dh
        w1ab = jnp.concatenate([w1[:, :half], w1[:, half:]], axis=2)
        pos8 = jnp.broadcast_to(cmp_pos[l].reshape(2, 1, CMP_LEN * dh), (2, 8, CMP_LEN * dh)).astype(BF16)
        kvc = _compress(x16, w1ab, _posterm(pos8, w1), cmp_w2[l].astype(BF16))
        gl = (pf[:, 4096:4096 + 24].reshape(b, nqb, QB, g, r, 3).transpose(0, 3, 1, 5, 4, 2)
              .reshape(b, g, nqb, 3, r * QB))
        ya = _nsa_attention(pb, c_qa, c_kv, kvc, gl, cb, bt, wb, cov_t, b31, blockcols, b, s, n_sel)

        yb = _retention(pf, pb, b, s, 0, 512, 0, 1024, ret_tabs)

        x1, eidx, wts, rank, cnt = _merge(ya, yb, pf, 2048, 3072, xf, w_br_a[l].astype(BF16),
                                          w_br_b[l].astype(BF16), w_o[l].astype(BF16), ln1_g[l].reshape(1, d),
                                          ln1_b[l].reshape(1, d), wr_t, rb_col, tri, alpha)

        counts = cnt[:, 0].astype(I32)
        padded = ((counts + MOE_BM - 1) // MOE_BM) * MOE_BM
        pad_end = jnp.cumsum(padded)
        pad_start = pad_end - padded
        e_iota = jnp.arange(N_EXPERTS, dtype=I32)[:, None, None]
        start_of = jnp.sum(jnp.where(eidx[None] == e_iota, pad_start[:, None, None], 0), axis=0)
        dest = (start_of + rank).reshape(-1)
        n_active = (pad_end[-1] // MOE_BM).astype(I32).reshape(1)
        blk = jnp.minimum(jnp.arange(n_blk, dtype=I32), n_active[0] - 1) * MOE_BM
        blk_e = jnp.minimum(jnp.sum(pad_end[None, :] <= blk[:, None], axis=1), N_EXPERTS - 1).astype(I32)
        blk_all = jnp.arange(n_blk, dtype=I32)
        last_of_expert = jnp.any(pad_end[None, :] == (blk_all[:, None] + 1) * MOE_BM, axis=1)
        zflag = (last_of_expert | (blk_all >= n_active[0])).astype(I32)
        xs = _dispatch(dest, zflag, x1, rows, 512)
        ys = _experts(blk_e, n_active, xs, moe_w_gate, moe_w_up, moe_w_down, l)
        xf, xb = _combine(dest, x1, wts.T, ln2_g[l].reshape(1, d), ln2_b[l].reshape(1, d), ys, alpha, 256)
    return xf.reshape(b, s, d)
```

```python
import functools
import math

import jax
import jax.numpy as jnp
import numpy as np
from jax import lax
from jax.experimental import pallas as pl
from jax.experimental.pallas import tpu as pltpu

F32 = jnp.float32
BF16 = jnp.bfloat16
I32 = jnp.int32

NSA_HEADS = 8
NSA_GROUPS = 2
NSA_HPG = 4
NSA_DH = 64
CMP_LEN = 32
CMP_STRIDE = 16
SLC_BLOCK = 64
N_SELECT = 16
WINDOW = 512
QB = 256
FORCE_SCORE = 1.0e4
RET_HEADS = 8
RET_DK = 64
RET_DV = 128
RET_CHUNK = 128
ROPE_BASE = 10000.0
REL_BUCKETS = 32
REL_MAX_EXACT = 16
REL_MAX_DIST = 2048
N_EXPERTS = 16
N_GROUPS = 4
EPG = 4
D_EXPERT = 512
LN_EPS = 1e-5
GN_EPS = 1e-5
NEG_INF = -1e30

LANES = 128
VMEM_LIMIT = 56 * 1024 * 1024

SEL_TK = 512
AUG_ROWS = 16
MASK_BIG = 2.0 ** 100
SAT_REL = math.ceil(REL_MAX_EXACT * (REL_MAX_DIST / REL_MAX_EXACT)
                    ** ((REL_BUCKETS - 1 - REL_MAX_EXACT) / (REL_BUCKETS - REL_MAX_EXACT)))
N_OFF = -(-(SAT_REL + QB - 1) // QB) + 1
RET_STEP = 2
MOE_BM = 256
ROW_DMA_UNROLL = 8


def _cparams(sem):
    return pltpu.CompilerParams(dimension_semantics=sem, vmem_limit_bytes=VMEM_LIMIT)


def _dot(a, b):
    return jnp.dot(a, b, preferred_element_type=F32)


def _dot_nt(a, b):
    return lax.dot_general(a, b, (((1,), (1,)), ((), ())), preferred_element_type=F32)


def _mm_kernel(a_ref, b_ref, o_ref):
    o_ref[...] = _dot(a_ref[...], b_ref[...]).astype(o_ref.dtype)


def _matmul(a, b, out_dtype, tm, tn):
    m, k = a.shape
    n = b.shape[1]
    return pl.pallas_call(
        _mm_kernel,
        grid=(n // tn, m // tm),
        in_specs=[pl.BlockSpec((tm, k), lambda j, i: (i, 0)),
                  pl.BlockSpec((k, tn), lambda j, i: (0, j))],
        out_specs=pl.BlockSpec((tm, tn), lambda j, i: (i, j)),
        out_shape=jax.ShapeDtypeStruct((m, n), out_dtype),
        compiler_params=_cparams(("arbitrary", "arbitrary")),
        name="proj_matmul",
    )(a, b)


def _bias_kernel(tab_ref, rel_ref, o_ref, *, delta):
    rel = rel_ref[...]
    n = jnp.maximum(rel, 0)
    nf = jnp.maximum(n, 1).astype(F32)
    large = REL_MAX_EXACT + (jnp.log(nf / REL_MAX_EXACT) / math.log(REL_MAX_DIST / REL_MAX_EXACT)
                             * (REL_BUCKETS - REL_MAX_EXACT)).astype(I32)
    large = jnp.minimum(large, REL_BUCKETS - 1)
    bucket = jnp.where(n < REL_MAX_EXACT, n, large)
    eqs = [bucket == k for k in range(REL_BUCKETS)]
    for h in range(NSA_HEADS):
        acc = jnp.full(rel.shape, NEG_INF, F32)
        for k in range(REL_BUCKETS):
            acc = jnp.where(eqs[k], tab_ref[k * NSA_HEADS + h], acc)
        if delta:
            acc = acc - tab_ref[(REL_BUCKETS - 1) * NSA_HEADS + h]
        hh = h % NSA_HPG
        o_ref[h // NSA_HPG, :, hh * QB:(hh + 1) * QB] = jnp.where(rel >= 0, acc, NEG_INF)


def _bias_table(rel_bias_flat, rel, tr, delta=False):
    r, c = rel.shape
    return pl.pallas_call(
        functools.partial(_bias_kernel, delta=delta),
        grid_spec=pltpu.PrefetchScalarGridSpec(
            num_scalar_prefetch=1,
            grid=(r // tr,),
            in_specs=[pl.BlockSpec((tr, c), lambda i, tab: (i, 0))],
            out_specs=pl.BlockSpec((NSA_GROUPS, tr, NSA_HPG * c), lambda i, tab: (0, i, 0)),
        ),
        out_shape=jax.ShapeDtypeStruct((NSA_GROUPS, r, NSA_HPG * c), F32),
        compiler_params=_cparams(("arbitrary",)),
        name="t5_bias_table",
    )(rel_bias_flat, rel)


def _compress_kernel(x_ref, w1_ref, pos_ref, w2_ref, o_ref):
    ab = _dot(x_ref[...], w1_ref[...])
    nb = ab.shape[0]
    nxt = pltpu.roll(ab, nb - 1, axis=0)
    nxt = pltpu.roll(nxt, NSA_DH, axis=1)
    pre = (ab + nxt)[:, :NSA_DH] + pos_ref[...]
    hid = jax.nn.gelu(pre)
    out = _dot(hid.astype(BF16), w2_ref[...]).astype(BF16)
    rows = lax.broadcasted_iota(I32, (nb, nb), 0)
    cols = lax.broadcasted_iota(I32, (nb, nb), 1)
    flip = jnp.where(rows + cols == nb - 1, 1.0, 0.0).astype(BF16)
    o_ref[...] = _dot(flip, out).astype(o_ref.dtype)


def _compress(x16, w1ab, posterm, w2):
    _, b, g, nb, wd = x16.shape
    return pl.pallas_call(
        _compress_kernel,
        grid=(2, b, g),
        in_specs=[pl.BlockSpec((None, None, None, nb, wd), lambda s, bi, gi: (s, bi, gi, 0, 0)),
                  pl.BlockSpec((None, wd, 2 * NSA_DH), lambda s, bi, gi: (s, 0, 0)),
                  pl.BlockSpec((None, 1, NSA_DH), lambda s, bi, gi: (s, 0, 0)),
                  pl.BlockSpec((None, NSA_DH, NSA_DH), lambda s, bi, gi: (s, 0, 0))],
        out_specs=pl.BlockSpec((None, None, None, nb, NSA_DH), lambda s, bi, gi: (s, bi, gi, 0, 0)),
        out_shape=jax.ShapeDtypeStruct((2, b, g, nb, NSA_DH), BF16),
        compiler_params=_cparams(("arbitrary",) * 3),
        name="nsa_compress",
    )(x16, w1ab, posterm, w2)


def _posterm_kernel(p_ref, w_ref, o_ref):
    o_ref[...] = _dot(p_ref[...], w_ref[...])[0:1]


def _posterm(pos8, w1):
    return pl.pallas_call(
        _posterm_kernel,
        grid=(2,),
        in_specs=[pl.BlockSpec((None, 8, pos8.shape[2]), lambda s: (s, 0, 0)),
                  pl.BlockSpec((None, w1.shape[1], NSA_DH), lambda s: (s, 0, 0))],
        out_specs=pl.BlockSpec((None, 1, NSA_DH), lambda s: (s, 0, 0)),
        out_shape=jax.ShapeDtypeStruct((2, 1, NSA_DH), F32),
        name="nsa_posterm",
    )(pos8, w1)


def _eye(n, scale=1.0):
    rows = lax.broadcasted_iota(I32, (n, n), 0)
    cols = lax.broadcasted_iota(I32, (n, n), 1)
    return jnp.where(rows == cols, scale, 0.0).astype(BF16)


def _nsa_kernel(q_ref, qn_ref, kc_ref, vcin_ref, ksin_ref, vsin_ref, kwin_ref, vwin_ref, gate_ref,
                cb_ref, bt_ref, wb_ref, cov_ref, b31_ref, bcols_ref, o_ref,
                s_ref, ks_ref, vs_ref, kw_ref, vw_ref, vc_ref, qt_ref, *, n_sel):
    gi = pl.program_id(1)
    i = pl.program_id(2)
    q0 = i * QB
    r4 = NSA_HPG * QB
    ns = cov_ref.shape[0]
    eye_dh = _eye(NSA_DH)

    def group_lanes(x):
        return jnp.where(gi == 0, x[:, 0:NSA_DH], x[:, NSA_DH:2 * NSA_DH])

    def q_transposed(ref):
        eye_q = _eye(NSA_DH, NSA_DH ** -0.5)
        qb = ref[...]
        return jnp.concatenate([_dot_nt(eye_q, qb[:, h * NSA_DH:(h + 1) * NSA_DH]) for h in range(NSA_HPG)],
                               axis=1).astype(BF16)

    @pl.when(i == 0)
    def _():
        s_len = ksin_ref.shape[0]
        ones_rows = jnp.where(lax.broadcasted_iota(I32, (AUG_ROWS, SEL_TK), 0) == 0, 1.0, 0.0).astype(BF16)
        ones_rows_qb = jnp.where(lax.broadcasted_iota(I32, (AUG_ROWS, QB), 0) == 0, 1.0, 0.0).astype(BF16)
        tail = jnp.where(lax.broadcasted_iota(I32, (SEL_TK, NSA_DH), 1) < 2, 1.0, 0.0).astype(BF16)
        depth = ks_ref.shape[-1]
        vc_ref[...] = _dot_nt(eye_dh, vcin_ref[...]).astype(BF16)
        qt_ref[...] = q_transposed(q_ref)
        pad_flag = jnp.where(lax.broadcasted_iota(I32, (WINDOW, 2 * NSA_DH), 1) == NSA_DH, 1.0, 0.0)
        kw_ref[0:WINDOW, :] = pad_flag.astype(BF16)
        for u in range(WINDOW // QB):
            vw_ref[u] = jnp.zeros(vw_ref.shape[1:], BF16)

        def chunk(c, _):
            r0 = pl.multiple_of(c * SEL_TK, SEL_TK)
            rows = pl.ds(r0, SEL_TK)
            k_aug = jnp.concatenate([group_lanes(ksin_ref[rows, :]), tail], axis=1)
            ks_ref[rows, 0:ns] = bcols_ref[rows, :]
            ks_ref[rows, ns:ns + 2 * NSA_DH] = k_aug
            if depth > ns + 2 * NSA_DH:
                ks_ref[rows, ns + 2 * NSA_DH:depth] = jnp.zeros((SEL_TK, depth - ns - 2 * NSA_DH), BF16)
            v_t = _dot_nt(eye_dh, group_lanes(vsin_ref[rows, :])).astype(BF16)
            vs_ref[c] = jnp.concatenate([v_t, ones_rows], axis=0)
            kw_ref[pl.ds(WINDOW + r0, SEL_TK), :] = jnp.concatenate(
                [group_lanes(kwin_ref[rows, :]), jnp.zeros((SEL_TK, NSA_DH), BF16)], axis=1)
            vw_t = _dot_nt(eye_dh, group_lanes(vwin_ref[rows, :])).astype(BF16)
            for u in range(SEL_TK // QB):
                vw_ref[WINDOW // QB + c * (SEL_TK // QB) + u] = jnp.concatenate(
                    [vw_t[:, u * QB:(u + 1) * QB], ones_rows_qb], axis=0)
            return 0

        lax.fori_loop(0, s_len // SEL_TK, chunk, 0)

    q = qt_ref[...]

    step_rows = QB // CMP_STRIDE
    cb = cb_ref[pl.ds(pl.multiple_of(i * step_rows, step_rows), kc_ref.shape[0]), :]
    s_c = _dot(kc_ref[...], q) + cb
    m_c = jnp.maximum(jnp.max(s_c, axis=0, keepdims=True), 0.1 * NEG_INF)
    p_c = jnp.exp(s_c - m_c)
    l_c = jnp.sum(p_c, axis=0, keepdims=True)
    p_c = p_c / jnp.maximum(l_c, 1e-30)
    o_c = _dot(vc_ref[...], p_c.astype(BF16))

    psum = p_c[:, 0:QB] + p_c[:, QB:2 * QB] + p_c[:, 2 * QB:3 * QB] + p_c[:, 3 * QB:4 * QB]
    p_hi = psum.astype(BF16)
    p_lo = (psum - p_hi.astype(F32)).astype(BF16)
    cov = cov_ref[...]
    imp_t = _dot(cov, p_hi) + _dot(cov, p_lo)
    ns = imp_t.shape[0]

    wk = WINDOW + QB
    kwin = kw_ref[pl.ds(pl.multiple_of(q0, QB), wk), :]
    pad_mask_rows = jnp.where(lax.broadcasted_iota(I32, (NSA_DH, r4), 0) == 0, -MASK_BIG, 0.0).astype(BF16)
    s_w = _dot(kwin, jnp.concatenate([q, pad_mask_rows], axis=0)) + wb_ref[...]
    m_w = jnp.max(s_w, axis=0, keepdims=True)
    p_w = jnp.exp(s_w - m_w).astype(BF16)
    acc_w = jnp.zeros((NSA_DH + AUG_ROWS, r4), F32)
    for u in range(wk // QB):
        acc_w = acc_w + _dot(vw_ref[i + u], p_w[u * QB:(u + 1) * QB])
    o_w = acc_w[0:NSA_DH] / acc_w[NSA_DH:NSA_DH + 1]

    jidx = lax.broadcasted_iota(I32, (ns, QB), 0)
    t = q0 + lax.broadcasted_iota(I32, (ns, QB), 1)
    cur = jnp.right_shift(t, 6)
    eligible = jidx <= cur
    forced = (jidx == 0) | (jidx == cur) | (jidx == cur - 1)
    score = jnp.where(eligible, jnp.where(forced, -2.0, imp_t), -1.0)
    sel = jnp.where(forced, 1.0, 0.0)
    for _ in range(n_sel - 3):
        mx = jnp.max(score, axis=0, keepdims=True)
        first = jnp.min(jnp.where(score == mx, jidx, ns), axis=0, keepdims=True)
        pick = jidx == first
        sel = jnp.where(pick, 1.0, sel)
        score = jnp.where(pick, -2.0, score)
    notsel = jnp.where(eligible, 1.0 - sel, 1.0).astype(BF16)

    depth = ks_ref.shape[-1]
    pad_rows = depth - ns - NSA_DH - AUG_ROWS
    qa = jnp.concatenate([jnp.concatenate([notsel] * NSA_HPG, axis=1), q, b31_ref[...],
                          jnp.zeros((pad_rows, r4), BF16)], axis=0)
    sub = SEL_TK // QB
    n_tiles = i // sub + 1
    n_far = jnp.maximum(i - N_OFF + 2, 0) // sub

    def put_scores(jt, slot):
        k0 = pl.multiple_of(jt * SEL_TK, SEL_TK)
        s_ref[slot] = _dot(ks_ref[pl.ds(k0, SEL_TK), :], qa)

    def absorb(jt, slot, carry, near):
        m_prev, acc = carry
        s = s_ref[slot]
        if near:
            offs = [jnp.clip(i - sub * jt - u, 0, N_OFF - 1) for u in range(sub)]
            s = s + jnp.concatenate([bt_ref[o] for o in offs], axis=0)
        m_new = jnp.maximum(m_prev, jnp.max(s, axis=0, keepdims=True))
        alpha = jnp.exp(m_prev - m_new)
        p = jnp.exp(s - m_new).astype(BF16)
        return m_new, alpha * acc + _dot(vs_ref[jt], p)

    def pair_body(pp, carry, near):
        ja = 2 * pp
        put_scores(ja + 1, 1)
        carry = absorb(ja, 0, carry, near)
        put_scores(jnp.minimum(ja + 2, last_tile), 0)
        return absorb(ja + 1, 1, carry, near)

    n_pairs = (n_tiles + 1) // 2
    last_tile = 2 * n_pairs - 1
    far_pairs = n_far // 2
    put_scores(0, 0)
    carry = (jnp.full((1, r4), -1e38, F32), jnp.zeros((NSA_DH + AUG_ROWS, r4), F32))
    carry = lax.fori_loop(0, far_pairs, functools.partial(pair_body, near=False), carry)
    _, acc_s = lax.fori_loop(far_pairs, n_pairs, functools.partial(pair_body, near=True), carry)
    o_s = acc_s[0:NSA_DH] / acc_s[NSA_DH:NSA_DH + 1]

    gates = jax.nn.sigmoid(gate_ref[...])
    o = (gates[0:1] * o_c + gates[1:2] * o_s + gates[2:3] * o_w).astype(BF16)
    qt_ref[...] = q_transposed(qn_ref)
    eye_l = _eye(LANES)
    o_ref[...] = jnp.concatenate(
        [jnp.concatenate([_dot_nt(eye_l, o[:, h * QB + u * LANES:h * QB + (u + 1) * LANES])
                          for u in range(QB // LANES)], axis=0) for h in range(NSA_HPG)],
        axis=1).astype(o_ref.dtype)


def _nsa_attention(pb, c_qa, c_kv, kvc, gates, cb, bt, wb, cov_t, b31, blockcols, b, s, n_sel):
    g, dh = NSA_GROUPS, NSA_DH
    r4 = NSA_HPG * QB
    nqb = s // QB
    nc = kvc.shape[3]
    ns = cov_t.shape[0]
    depth = -(-(ns + 2 * dh) // LANES) * LANES
    n_tiles = s // SEL_TK
    assert n_tiles % 2 == 0, "the selected sweep consumes key tiles in pairs"
    qw = NSA_HPG * dh
    once = pl.Buffered(1)
    slab = lambda k: pl.BlockSpec((s, 2 * dh), lambda bi, gi, i: (bi, c_kv // (2 * dh) + k), pipeline_mode=once)
    per_g = lambda a: pl.BlockSpec((None,) + a.shape[1:], lambda bi, gi, i: (gi,) + (0,) * (a.ndim - 1),
                                   pipeline_mode=once)
    const = lambda a: pl.BlockSpec(a.shape, lambda bi, gi, i: (0,) * a.ndim, pipeline_mode=once)
    return pl.pallas_call(
        functools.partial(_nsa_kernel, n_sel=n_sel),
        grid=(b, g, nqb),
        in_specs=[
            pl.BlockSpec((QB, qw), lambda bi, gi, i: (bi * nqb + i, c_qa // qw + gi)),
            pl.BlockSpec((QB, qw), lambda bi, gi, i: (bi * nqb + jnp.minimum(i + 1, nqb - 1), c_qa // qw + gi)),
            pl.BlockSpec((None, None, None, nc, dh), lambda bi, gi, i: (0, bi, gi, 0, 0)),
            pl.BlockSpec((None, None, None, nc, dh), lambda bi, gi, i: (1, bi, gi, 0, 0)),
            slab(2), slab(3), slab(4), slab(5),
            pl.BlockSpec((None, None, None, 3, r4), lambda bi, gi, i: (bi, gi, i, 0, 0)),
            per_g(cb), per_g(bt), per_g(wb), const(cov_t), per_g(b31), const(blockcols),
        ],
        out_specs=pl.BlockSpec((QB, qw), lambda bi, gi, i: (bi * nqb + i, gi)),
        out_shape=jax.ShapeDtypeStruct((b * s, g * qw), BF16),
        scratch_shapes=[pltpu.VMEM((2, SEL_TK, r4), F32),
                        pltpu.VMEM((s, depth), BF16),
                        pltpu.VMEM((n_tiles, dh + AUG_ROWS, SEL_TK), BF16),
                        pltpu.VMEM((s + WINDOW, 2 * dh), BF16),
                        pltpu.VMEM(((s + WINDOW) // QB, dh + AUG_ROWS, QB), BF16),
                        pltpu.VMEM((dh, nc), BF16),
                        pltpu.VMEM((dh, r4), BF16)],
        compiler_params=_cparams(("arbitrary",) * 3),
        name="nsa_attention",
    )(pb, pb, kvc, kvc, pb, pb, pb, pb, gates, cb, bt, wb, cov_t, b31, blockcols)


def _ret_kernel(q_ref, k_ref, v_ref, g_ref, cos_ref, sin_ref, dm_ref, qd_ref, kd_ref, cd_ref,
                o_ref, state_ref):
    n = pl.program_id(1)

    @pl.when(n == 0)
    def _():
        state_ref[...] = jnp.zeros_like(state_ref)

    lane = lax.broadcasted_iota(I32, (RET_CHUNK, LANES), 1)
    first_half = (lane & (RET_DK - 1)) < (RET_DK // 2)

    def rot(x, cosv, sinv):
        partner = jnp.where(first_half, pltpu.roll(x, LANES - RET_DK // 2, axis=1),
                            pltpu.roll(x, RET_DK // 2, axis=1))
        return x * cosv + partner * sinv

    states = [state_ref[h] for h in range(RET_HEADS)]
    for cc in range(RET_STEP):
        rows = slice(cc * RET_CHUNK, (cc + 1) * RET_CHUNK)
        cosv = cos_ref[rows, :]
        sinv = sin_ref[rows, :]
        for hp in range(RET_HEADS // 2):
            ls = slice(hp * LANES, (hp + 1) * LANES)
            q2 = rot(q_ref[rows, ls], cosv, sinv)
            k2 = rot(k_ref[rows, ls], cosv, sinv) * (RET_DK ** -0.5)
            qdec = (q2 * qd_ref[:, ls]).astype(BF16)
            kdec_t = (k2 * kd_ref[:, ls]).T.astype(BF16)
            q2b = q2.astype(BF16)
            k2b = k2.astype(BF16)
            for e in range(2):
                h = 2 * hp + e
                hs = slice(e * RET_DK, (e + 1) * RET_DK)
                vs = slice(h * RET_DV, (h + 1) * RET_DV)
                v = v_ref[rows, vs]
                st = states[h]
                sc = _dot_nt(q2b[:, hs], k2b[:, hs]) * dm_ref[h]
                y = _dot(sc.astype(BF16), v) + _dot(qdec[:, hs], st.astype(BF16))
                states[h] = cd_ref[h] * st + _dot(kdec_t[hs, :], v)
                mu = jnp.mean(y, axis=-1, keepdims=True)
                var = jnp.mean(jnp.square(y - mu), axis=-1, keepdims=True)
                yn = (y - mu) * lax.rsqrt(var + GN_EPS)
                o_ref[rows, vs] = (jax.nn.silu(g_ref[rows, vs]) * yn).astype(o_ref.dtype)
    for h in range(RET_HEADS):
        state_ref[h] = states[h]


def _retention(pf, pb, b, s, col_q, col_k, col_v, col_g, tabs):
    cosf, sinf, dm, qd, kd, cd = tabs
    c = RET_CHUNK * RET_STEP
    n = s // c
    t = b * s
    wqk = RET_HEADS * RET_DK
    wv = RET_HEADS * RET_DV
    row = lambda bi, ni: bi * n + ni
    full = lambda a: pl.BlockSpec(a.shape, lambda bi, ni: (0,) * a.ndim)
    return pl.pallas_call(
        _ret_kernel,
        grid=(b, n),
        in_specs=[
            pl.BlockSpec((c, wqk), lambda bi, ni: (row(bi, ni), col_q // wqk)),
            pl.BlockSpec((c, wqk), lambda bi, ni: (row(bi, ni), col_k // wqk)),
            pl.BlockSpec((c, wv), lambda bi, ni: (row(bi, ni), col_v // wv)),
            pl.BlockSpec((c, wv), lambda bi, ni: (row(bi, ni), col_g // wv)),
            pl.BlockSpec((c, LANES), lambda bi, ni: (ni, 0)),
            pl.BlockSpec((c, LANES), lambda bi, ni: (ni, 0)),
            full(dm), full(qd), full(kd), full(cd),
        ],
        out_specs=pl.BlockSpec((c, wv), lambda bi, ni: (row(bi, ni), 0)),
        out_shape=jax.ShapeDtypeStruct((t, wv), BF16),
        scratch_shapes=[pltpu.VMEM((RET_HEADS, RET_DK, RET_DV), F32)],
        compiler_params=_cparams(("arbitrary",) * 2),
        name="retention",
    )(pf, pf, pb, pf, cosf, sinf, dm, qd, kd, cd)


def _retention_tables(s):
    half = RET_DK // 2
    inv = ROPE_BASE ** (-jnp.arange(half, dtype=F32) * 2.0 / RET_DK)
    ang = jnp.arange(s, dtype=F32)[:, None] * inv[None, :]
    cos, sin = jnp.cos(ang), jnp.sin(ang)
    cosf = jnp.concatenate([cos, cos, cos, cos], axis=1)
    sinf = jnp.concatenate([-sin, sin, -sin, sin], axis=1)
    lg = jnp.log(1.0 - 2.0 ** (-5.0 - jnp.arange(RET_HEADS, dtype=F32)))
    i = jnp.arange(RET_CHUNK, dtype=F32)
    diff = i[:, None] - i[None, :]
    dm = jnp.where(diff >= 0, jnp.exp(jnp.maximum(diff, 0.0)[None] * lg[:, None, None]), 0.0)
    kdec = jnp.exp((RET_CHUNK - 1.0 - i)[None, :] * lg[:, None])
    qdec = jnp.exp((i + 1.0)[None, :] * lg[:, None])
    expand = lambda a: jnp.repeat(a.T, RET_DK, axis=1)
    cd = jnp.broadcast_to(jnp.exp(RET_CHUNK * lg)[:, None, None], (RET_HEADS, 1, LANES))
    return cosf, sinf, dm, expand(qdec), expand(kdec), cd


def _layer_norm(z, g, b):
    mu = jnp.mean(z, axis=-1, keepdims=True)
    var = jnp.mean(jnp.square(z - mu), axis=-1, keepdims=True)
    return (z - mu) * lax.rsqrt(var + LN_EPS) * g + b


def _merge_kernel(ya_ref, yb_ref, ma_ref, mb_ref, x_ref, wa_ref, wb_ref, wo_ref, g_ref, b_ref,
                  wr_ref, rb_ref, tri_ref, o_ref, eidx_ref, wts_ref, rank_ref, cnt_ref, carry_ref, *, alpha):
    merged = (jax.nn.sigmoid(ma_ref[...]) * _dot(ya_ref[...], wa_ref[...])
              + jax.nn.sigmoid(mb_ref[...]) * _dot(yb_ref[...], wb_ref[...]))
    z = alpha * x_ref[...] + _dot(merged.astype(BF16), wo_ref[...])
    x1 = _layer_norm(z, g_ref[...], b_ref[...])
    o_ref[...] = x1
    _route(x1, wr_ref, rb_ref, tri_ref, eidx_ref, wts_ref, rank_ref, cnt_ref, carry_ref)


def _merge(ya, yb, pf, col_ma, col_mb, x, wa, wb, wo, g, bb, wr_t, rb_col, tri, alpha):
    t, d = x.shape
    tm = tri.shape[0]
    full = lambda a: pl.BlockSpec(a.shape, lambda i: (0,) * a.ndim)
    out2 = lambda dt: jax.ShapeDtypeStruct((2, t), dt)
    row2 = pl.BlockSpec((2, tm), lambda i: (0, i))
    return pl.pallas_call(
        functools.partial(_merge_kernel, alpha=alpha),
        grid=(t // tm,),
        in_specs=[pl.BlockSpec((tm, ya.shape[1]), lambda i: (i, 0)),
                  pl.BlockSpec((tm, yb.shape[1]), lambda i: (i, 0)),
                  pl.BlockSpec((tm, d), lambda i: (i, col_ma // d)),
                  pl.BlockSpec((tm, d), lambda i: (i, col_mb // d)),
                  pl.BlockSpec((tm, d), lambda i: (i, 0)),
                  full(wa), full(wb), full(wo), full(g), full(bb), full(wr_t), full(rb_col), full(tri)],
        out_specs=[pl.BlockSpec((tm, d), lambda i: (i, 0)), row2, row2, row2,
                   pl.BlockSpec((N_EXPERTS, LANES), lambda i: (0, 0))],
        out_shape=[jax.ShapeDtypeStruct((t, d), F32), out2(I32), out2(F32), out2(I32),
                   jax.ShapeDtypeStruct((N_EXPERTS, LANES), F32)],
        scratch_shapes=[pltpu.VMEM((N_EXPERTS, LANES), F32)],
        compiler_params=_cparams(("arbitrary",)),
        name="merge_ln1_router",
    )(ya, yb, pf, pf, x, wa, wb, wo, g, bb, wr_t, rb_col, tri)


def _route(x, wr_ref, rb_ref, tri_ref, eidx_ref, wts_ref, rank_ref, cnt_ref, carry_ref):
    step = pl.program_id(0)

    @pl.when(step == 0)
    def _():
        carry_ref[...] = jnp.zeros_like(carry_ref)

    wr = wr_ref[...]
    w_hi = wr.astype(BF16)
    w_lo = (wr - w_hi.astype(F32)).astype(BF16)
    x_hi = x.astype(BF16)
    x_lo = (x - x_hi.astype(F32)).astype(BF16)
    logits = _dot_nt(w_hi, x_hi) + _dot_nt(w_hi, x_lo) + _dot_nt(w_lo, x_hi)
    tm = logits.shape[1]
    mx = jnp.max(logits, axis=0, keepdims=True)
    ex = jnp.exp(logits - mx)
    probs = ex / jnp.sum(ex, axis=0, keepdims=True)
    sel = probs + rb_ref[...]
    rows = [sel[e:e + 1] for e in range(N_EXPERTS)]
    gscore = []
    for gi in range(N_GROUPS):
        r = rows[gi * EPG:(gi + 1) * EPG]
        best2 = None
        for a in range(EPG):
            for c in range(a + 1, EPG):
                pair = r[a] + r[c]
                best2 = pair if best2 is None else jnp.maximum(best2, pair)
        gscore.append(best2)
    best = jnp.zeros((1, tm), I32)
    bscore = gscore[0]
    for gi in range(1, N_GROUPS):
        better = gscore[gi] > bscore
        best = jnp.where(better, gi, best)
        bscore = jnp.where(better, gscore[gi], bscore)
    ing = []
    for a in range(EPG):
        v = rows[a]
        for gi in range(1, N_GROUPS):
            v = jnp.where(best == gi, rows[gi * EPG + a], v)
        ing.append(v)
    i1 = jnp.zeros((1, tm), I32)
    v1 = ing[0]
    for a in range(1, EPG):
        better = ing[a] > v1
        i1 = jnp.where(better, a, i1)
        v1 = jnp.where(better, ing[a], v1)
    i2 = jnp.full((1, tm), -1, I32)
    v2 = jnp.full((1, tm), -jnp.inf, F32)
    for a in range(EPG):
        better = (i1 != a) & (ing[a] > v2)
        i2 = jnp.where(better, a, i2)
        v2 = jnp.where(better, ing[a], v2)
    e0 = best * EPG + i1
    e1 = best * EPG + i2
    eio = lax.broadcasted_iota(I32, (N_EXPERTS, tm), 0)
    oh0 = (eio == e0).astype(F32)
    oh1 = (eio == e1).astype(F32)
    p0 = jnp.sum(oh0 * probs, axis=0, keepdims=True)
    p1 = jnp.sum(oh1 * probs, axis=0, keepdims=True)
    den = p0 + p1
    eidx_ref[0:1, :] = e0
    eidx_ref[1:2, :] = e1
    wts_ref[0:1, :] = p0 / den
    wts_ref[1:2, :] = p1 / den
    tot = oh0 + oh1
    prefix = _dot(tot.astype(BF16), tri_ref[...]) + carry_ref[:, 0:1]
    rank_ref[0:1, :] = jnp.sum(oh0 * prefix, axis=0, keepdims=True).astype(I32)
    rank_ref[1:2, :] = jnp.sum(oh1 * prefix, axis=0, keepdims=True).astype(I32)
    carry_ref[...] = carry_ref[...] + jnp.sum(tot, axis=1, keepdims=True)
    cnt_ref[...] = carry_ref[...]


def _dispatch_kernel(dest_ref, zflag_ref, x_ref, xs_ref, zero_ref, sem, zsem, rows_ref, *, t_total):
    step = pl.program_id(0)
    tm = x_ref.shape[0]

    @pl.when(step == 0)
    def _():
        zero_ref[...] = jnp.zeros_like(zero_ref)

        def zero_block(bi, _):
            @pl.when(zflag_ref[bi] == 1)
            def _():
                r0 = pl.multiple_of(bi * MOE_BM, MOE_BM)
                cp = pltpu.make_async_copy(zero_ref, xs_ref.at[pl.ds(r0, MOE_BM)], zsem)
                cp.start()
                cp.wait()
            return 0

        lax.fori_loop(0, zflag_ref.shape[0], zero_block, 0)

    rows_ref[...] = x_ref[...].reshape(rows_ref.shape)

    def copy(r, slot):
        dst = dest_ref[slot * t_total + step * tm + r]
        return pltpu.make_async_copy(rows_ref.at[r], xs_ref.at[dst], sem)

    def issue(r, _):
        copy(r, 0).start()
        copy(r, 1).start()
        return 0

    def drain(r, _):
        copy(r, 0).wait()
        copy(r, 1).wait()
        return 0

    lax.fori_loop(0, tm, issue, 0, unroll=ROW_DMA_UNROLL)
    lax.fori_loop(0, tm, drain, 0, unroll=ROW_DMA_UNROLL)


def _dispatch(dest_flat, zflag, x, rows, tm):
    t, d = x.shape
    sub = d // LANES
    return pl.pallas_call(
        functools.partial(_dispatch_kernel, t_total=t),
        grid_spec=pltpu.PrefetchScalarGridSpec(
            num_scalar_prefetch=2,
            grid=(t // tm,),
            in_specs=[pl.BlockSpec((tm, d), lambda i, dr, zf: (i, 0))],
            out_specs=pl.BlockSpec(memory_space=pl.ANY),
            scratch_shapes=[pltpu.VMEM((MOE_BM, sub, LANES), x.dtype), pltpu.SemaphoreType.DMA,
                            pltpu.SemaphoreType.DMA, pltpu.VMEM((tm, sub, LANES), x.dtype)],
        ),
        out_shape=jax.ShapeDtypeStruct((rows, sub, LANES), x.dtype),
        compiler_params=pltpu.CompilerParams(dimension_semantics=("arbitrary",), has_side_effects=True,
                                             vmem_limit_bytes=VMEM_LIMIT),
        name="moe_dispatch",
    )(dest_flat, zflag, x)


def _expert_kernel(be_ref, na_ref, x_ref, wg_ref, wu_ref, wd_ref, o_ref, wg_s, wu_s, wd_s):
    i = pl.program_id(0)
    active = i < na_ref[0]

    @pl.when((i == 0) | (be_ref[i] != be_ref[jnp.maximum(i - 1, 0)]))
    def _():
        wg_s[...] = wg_ref[...].astype(BF16)
        wu_s[...] = wu_ref[...].astype(BF16)
        wd_s[...] = wd_ref[...].astype(BF16)

    @pl.when(active)
    def _():
        xb = x_ref[...].reshape(x_ref.shape[0], wg_s.shape[0]).astype(BF16)
        gate = _dot(xb, wg_s[...])
        up = _dot(xb, wu_s[...])
        hid = (jax.nn.silu(gate) * up).astype(BF16)
        o_ref[...] = _dot(hid, wd_s[...]).reshape(o_ref.shape)

    @pl.when(jnp.logical_not(active))
    def _():
        o_ref[...] = jnp.zeros_like(o_ref)


def _experts(blk_e, n_active, xs, wg, wu, wd, layer):
    rows = xs.shape[0]
    nb = rows // MOE_BM
    act = lambda i, be, na: jnp.minimum(i, na[0] - 1)
    wspec = lambda a: pl.BlockSpec((None, None) + a.shape[2:], lambda i, be, na: (layer, be[i], 0, 0))
    return pl.pallas_call(
        _expert_kernel,
        grid_spec=pltpu.PrefetchScalarGridSpec(
            num_scalar_prefetch=2,
            grid=(nb,),
            in_specs=[pl.BlockSpec((MOE_BM,) + xs.shape[1:], lambda i, be, na: (act(i, be, na), 0, 0)),
                      wspec(wg), wspec(wu), wspec(wd)],
            out_specs=pl.BlockSpec((MOE_BM,) + xs.shape[1:], lambda i, be, na: (i, 0, 0)),
            scratch_shapes=[pltpu.VMEM(wg.shape[2:], BF16), pltpu.VMEM(wu.shape[2:], BF16),
                            pltpu.VMEM(wd.shape[2:], BF16)],
        ),
        out_shape=jax.ShapeDtypeStruct(xs.shape, F32),
        compiler_params=_cparams(("arbitrary",)),
        name="moe_experts",
    )(blk_e, n_active, xs, wg, wu, wd)


def _combine_kernel(dest_ref, x_ref, w_ref, g_ref, b_ref, ys_ref, o_ref, ob_ref, buf_ref, sem, *, alpha, t_total):
    step = pl.program_id(0)
    tm = x_ref.shape[0]

    def copy(r, slot):
        src = dest_ref[slot * t_total + step * tm + r]
        return pltpu.make_async_copy(ys_ref.at[src], buf_ref.at[slot, r], sem)

    def issue(r, _):
        copy(r, 0).start()
        copy(r, 1).start()
        return 0

    def drain(r, _):
        copy(r, 0).wait()
        copy(r, 1).wait()
        return 0

    lax.fori_loop(0, tm, issue, 0, unroll=ROW_DMA_UNROLL)
    lax.fori_loop(0, tm, drain, 0, unroll=ROW_DMA_UNROLL)
    w = w_ref[...]
    y0 = buf_ref[0].reshape(x_ref.shape)
    y1 = buf_ref[1].reshape(x_ref.shape)
    z = alpha * x_ref[...] + w[:, 0:1] * y0 + w[:, 1:2] * y1
    y = _layer_norm(z, g_ref[...], b_ref[...])
    o_ref[...] = y
    ob_ref[...] = y.astype(BF16)


def _combine(dest_flat, x, wts_t, g, bb, ys, alpha, tm):
    t, d = x.shape
    return pl.pallas_call(
        functools.partial(_combine_kernel, alpha=alpha, t_total=t),
        grid_spec=pltpu.PrefetchScalarGridSpec(
            num_scalar_prefetch=1,
            grid=(t // tm,),
            in_specs=[pl.BlockSpec((tm, d), lambda i, dr: (i, 0)),
                      pl.BlockSpec((tm, 2), lambda i, dr: (i, 0)),
                      pl.BlockSpec(g.shape, lambda i, dr: (0, 0)),
                      pl.BlockSpec(bb.shape, lambda i, dr: (0, 0)),
                      pl.BlockSpec(memory_space=pl.ANY)],
            out_specs=[pl.BlockSpec((tm, d), lambda i, dr: (i, 0)), pl.BlockSpec((tm, d), lambda i, dr: (i, 0))],
            scratch_shapes=[pltpu.VMEM((2, tm, d // LANES, LANES), F32), pltpu.SemaphoreType.DMA],
        ),
        out_shape=[jax.ShapeDtypeStruct((t, d), F32), jax.ShapeDtypeStruct((t, d), BF16)],
        compiler_params=_cparams(("arbitrary",)),
        name="moe_combine_ln2",
    )(dest_flat, x, wts_t, g, bb, ys)


def _attention_tables(rel_bias, s):
    nc = s // CMP_STRIDE
    nqb = s // QB
    iq = jnp.arange(QB, dtype=I32)[None, None, :]
    x = jnp.arange(nc + (QB // CMP_STRIDE) * nqb, dtype=I32)[:, None]
    rel_c = iq[0] + CMP_STRIDE * (x - nc + 1) - (CMP_LEN - 1)
    jk = jnp.arange(QB, dtype=I32)[None, :, None]
    rel_t = jnp.arange(N_OFF, dtype=I32)[:, None, None] * QB + iq - jk
    jw = jnp.arange(WINDOW + QB, dtype=I32)[:, None]
    rel_w = iq[0] - jw + WINDOW
    rel_w = jnp.where(rel_w < WINDOW, rel_w, -1)
    flat = rel_bias.reshape(-1)
    r4 = NSA_HPG * QB
    cb = _bias_table(flat, jnp.maximum(rel_c, -1), rel_c.shape[0])
    bt = _bias_table(flat, jnp.maximum(rel_t, -1).reshape(N_OFF * QB, QB), QB, delta=True)
    bt = bt.reshape(NSA_GROUPS, N_OFF, QB, r4)
    wb = _bias_table(flat, jnp.maximum(rel_w, -1), WINDOW + QB)
    sat = jnp.repeat(rel_bias[REL_BUCKETS - 1].reshape(NSA_GROUPS, 1, NSA_HPG), QB, axis=2)
    hi = sat.astype(BF16)
    lo = (sat - hi.astype(F32)).astype(BF16)
    b31 = jnp.concatenate([hi, lo, jnp.zeros((NSA_GROUPS, AUG_ROWS - 2, r4), BF16)], axis=1)
    return cb, bt, wb, b31


def _selection_tables(s):
    nc = s // CMP_STRIDE
    ns = s // SLC_BLOCK
    c_start = np.arange(nc) * CMP_STRIDE
    s_start = np.arange(ns) * SLC_BLOCK
    cover = np.maximum(np.minimum(c_start[:, None] + CMP_LEN, s_start[None, :] + SLC_BLOCK)
                       - np.maximum(c_start[:, None], s_start[None, :]), 0).astype(np.float32) / CMP_LEN
    cover[nc - 1] = 0.0
    cover_rev_t = np.ascontiguousarray(cover[::-1].T)
    blockcols = np.where(np.arange(s)[:, None] // SLC_BLOCK == np.arange(ns)[None, :], -MASK_BIG, 0.0)
    return jnp.asarray(cover_rev_t, BF16), jnp.asarray(blockcols, BF16)


def _split_in_weights(w_in_l):
    o = np.cumsum((0, 512, 128, 128, 128, 128, 128, 128, 24, 512, 512, 1024, 1024, 1024, 1024))
    seg = lambda a: w_in_l[:, o[a]:o[a + 1]]
    wb = jnp.concatenate([seg(10), seg(0), seg(1), seg(2), seg(3), seg(4), seg(5), seg(6)], axis=1)
    gpad = jnp.pad(seg(7), ((0, 0), (0, LANES - 24)))
    wf = jnp.concatenate([seg(8), seg(9), seg(11), seg(12), seg(13), gpad], axis=1)
    return wb.astype(BF16), wf.astype(BF16)


def kernel(x, w_in, cmp_pos, cmp_w1, cmp_w2, w_br_a, w_br_b, w_o, ln1_g, ln1_b, w_router, router_bias,
           moe_w_gate, moe_w_up, moe_w_down, ln2_g, ln2_b, rel_bias):
    b, s, d = x.shape
    depth = w_in.shape[0]
    t = b * s
    alpha = (2 * depth) ** 0.25
    g, r, dh = NSA_GROUPS, NSA_HPG, NSA_DH
    nqb = s // QB
    nb16 = s // CMP_STRIDE
    n_sel = min(N_SELECT, s // SLC_BLOCK)

    cb, bt, wb, b31 = _attention_tables(rel_bias, s)
    cov_t, blockcols = _selection_tables(s)
    ret_tabs = _retention_tables(s)
    tri = jnp.asarray(np.triu(np.ones((512, 512), np.float32), 1), BF16)
    wr_t = w_router.T
    rb_col = router_bias.reshape(N_EXPERTS, 1)
    n_blk = t * 2 // MOE_BM + N_EXPERTS
    rows = n_blk * MOE_BM

    xf = x.reshape(t, d)
    xb = xf.astype(BF16)
    for l in range(depth):
        w_b, w_f = _split_in_weights(w_in[l])
        pb =_matmul(xb, w_b, BF16, 1024, 1152)
        pf = _matmul(xb, w_f, F32, 1024, 1408)
        c_qa = RET_HEADS * RET_DV
        c_kv = c_qa + NSA_HEADS * dh

        def kv_heads(slab):
            col = c_kv + slab * g * dh
            return pb[:, col:col + g * dh].reshape(b, s, g, dh).transpose(0, 2, 1, 3)
        x16 = jnp.stack([kv_heads(0), kv_heads(1)]).reshape(2, b, g, nb16, CMP_STRIDE * dh)
        w1 = cmp_w1[l].astype(BF16)
        half = CMP_STRIDE * dh
        w1ab = jnp.concatenate([w1[:, :half], w1[:, half:]], axis=2)
        pos8 = jnp.broadcast_to(cmp_pos[l].reshape(2, 1, CMP_LEN * dh), (2, 8, CMP_LEN * dh)).astype(BF16)
        kvc = _compress(x16, w1ab, _posterm(pos8, w1), cmp_w2[l].astype(BF16))
        gl = (pf[:, 4096:4096 + 24].reshape(b, nqb, QB, g, r, 3).transpose(0, 3, 1, 5, 4, 2)
              .reshape(b, g, nqb, 3, r * QB))
        ya = _nsa_attention(pb, c_qa, c_kv, kvc, gl, cb, bt, wb, cov_t, b31, blockcols, b, s, n_sel)

        yb = _retention(pf, pb, b, s, 0, 512, 0, 1024, ret_tabs)

        x1, eidx, wts, rank, cnt = _merge(ya, yb, pf, 2048, 3072, xf, w_br_a[l].astype(BF16),
                                          w_br_b[l].astype(BF16), w_o[l].astype(BF16), ln1_g[l].reshape(1, d),
                                          ln1_b[l].reshape(1, d), wr_t, rb_col, tri, alpha)

        counts = cnt[:, 0].astype(I32)
        padded = ((counts + MOE_BM - 1) // MOE_BM) * MOE_BM
        pad_end = jnp.cumsum(padded)
        pad_start = pad_end - padded
        e_iota = jnp.arange(N_EXPERTS, dtype=I32)[:, None, None]
        start_of = jnp.sum(jnp.where(eidx[None] == e_iota, pad_start[:, None, None], 0), axis=0)
        dest = (start_of + rank).reshape(-1)
        n_active = (pad_end[-1] // MOE_BM).astype(I32).reshape(1)
        blk = jnp.minimum(jnp.arange(n_blk, dtype=I32), n_active[0] - 1) * MOE_BM
        blk_e = jnp.minimum(jnp.sum(pad_end[None, :] <= blk[:, None], axis=1), N_EXPERTS - 1).astype(I32)
        blk_all = jnp.arange(n_blk, dtype=I32)
        last_of_expert = jnp.any(pad_end[None, :] == (blk_all[:, None] + 1) * MOE_BM, axis=1)
        zflag = (last_of_expert | (blk_all >= n_active[0])).astype(I32)
        xs = _dispatch(dest, zflag, x1, rows, 512)
        ys = _experts(blk_e, n_active, xs, moe_w_gate, moe_w_up, moe_w_down, l)
        xf, xb = _combine(dest, x1, wts.T, ln2_g[l].reshape(1, d), ln2_b[l].reshape(1, d), ys, alpha, 256)
    return xf.reshape(b, s, d)
```

```python
import functools
import math

import jax
import jax.numpy as jnp
import numpy as np
from jax import lax
from jax.experimental import pallas as pl
from jax.experimental.pallas import tpu as pltpu

F32 = jnp.float32
BF16 = jnp.bfloat16
I32 = jnp.int32

NSA_HEADS = 8
NSA_GROUPS = 2
NSA_HPG = 4
NSA_DH = 64
CMP_LEN = 32
CMP_STRIDE = 16
SLC_BLOCK = 64
N_SELECT = 16
WINDOW = 512
QB = 256
FORCE_SCORE = 1.0e4
RET_HEADS = 8
RET_DK = 64
RET_DV = 128
RET_CHUNK = 128
ROPE_BASE = 10000.0
REL_BUCKETS = 32
REL_MAX_EXACT = 16
REL_MAX_DIST = 2048
N_EXPERTS = 16
N_GROUPS = 4
EPG = 4
D_EXPERT = 512
LN_EPS = 1e-5
GN_EPS = 1e-5
NEG_INF = -1e30

LANES = 128
VMEM_LIMIT = 56 * 1024 * 1024

SEL_TK = 512
AUG_ROWS = 16
MASK_BIG = 2.0 ** 100
SAT_REL = math.ceil(REL_MAX_EXACT * (REL_MAX_DIST / REL_MAX_EXACT)
                    ** ((REL_BUCKETS - 1 - REL_MAX_EXACT) / (REL_BUCKETS - REL_MAX_EXACT)))
N_OFF = -(-(SAT_REL + QB - 1) // QB) + 1
RET_STEP = 2
MOE_BM = 256
ROW_DMA_UNROLL = 8


def _cparams(sem):
    return pltpu.CompilerParams(dimension_semantics=sem, vmem_limit_bytes=VMEM_LIMIT)


def _dot(a, b):
    return jnp.dot(a, b, preferred_element_type=F32)


def _dot_nt(a, b):
    return lax.dot_general(a, b, (((1,), (1,)), ((), ())), preferred_element_type=F32)


def _mm_kernel(a_ref, b_ref, o_ref):
    o_ref[...] = _dot(a_ref[...], b_ref[...]).astype(o_ref.dtype)


def _matmul(a, b, out_dtype, tm, tn):
    m, k = a.shape
    n = b.shape[1]
    return pl.pallas_call(
        _mm_kernel,
        grid=(n // tn, m // tm),
        in_specs=[pl.BlockSpec((tm, k), lambda j, i: (i, 0)),
                  pl.BlockSpec((k, tn), lambda j, i: (0, j))],
        out_specs=pl.BlockSpec((tm, tn), lambda j, i: (i, j)),
        out_shape=jax.ShapeDtypeStruct((m, n), out_dtype),
        compiler_params=_cparams(("arbitrary", "arbitrary")),
        name="proj_matmul",
    )(a, b)


def _bias_kernel(tab_ref, rel_ref, o_ref, *, delta):
    rel = rel_ref[...]
    n = jnp.maximum(rel, 0)
    nf = jnp.maximum(n, 1).astype(F32)
    large = REL_MAX_EXACT + (jnp.log(nf / REL_MAX_EXACT) / math.log(REL_MAX_DIST / REL_MAX_EXACT)
                             * (REL_BUCKETS - REL_MAX_EXACT)).astype(I32)
    large = jnp.minimum(large, REL_BUCKETS - 1)
    bucket = jnp.where(n < REL_MAX_EXACT, n, large)
    eqs = [bucket == k for k in range(REL_BUCKETS)]
    for h in range(NSA_HEADS):
        acc = jnp.full(rel.shape, NEG_INF, F32)
        for k in range(REL_BUCKETS):
            acc = jnp.where(eqs[k], tab_ref[k * NSA_HEADS + h], acc)
        if delta:
            acc = acc - tab_ref[(REL_BUCKETS - 1) * NSA_HEADS + h]
        hh = h % NSA_HPG
        o_ref[h // NSA_HPG, :, hh * QB:(hh + 1) * QB] = jnp.where(rel >= 0, acc, NEG_INF)


def _bias_table(rel_bias_flat, rel, tr, delta=False):
    r, c = rel.shape
    return pl.pallas_call(
        functools.partial(_bias_kernel, delta=delta),
        grid_spec=pltpu.PrefetchScalarGridSpec(
            num_scalar_prefetch=1,
            grid=(r // tr,),
            in_specs=[pl.BlockSpec((tr, c), lambda i, tab: (i, 0))],
            out_specs=pl.BlockSpec((NSA_GROUPS, tr, NSA_HPG * c), lambda i, tab: (0, i, 0)),
        ),
        out_shape=jax.ShapeDtypeStruct((NSA_GROUPS, r, NSA_HPG * c), F32),
        compiler_params=_cparams(("arbitrary",)),
        name="t5_bias_table",
    )(rel_bias_flat, rel)


def _compress_kernel(x_ref, w1_ref, pos_ref, w2_ref, o_ref):
    ab = _dot(x_ref[...], w1_ref[...])
    nb = ab.shape[0]
    nxt = pltpu.roll(ab, nb - 1, axis=0)
    nxt = pltpu.roll(nxt, NSA_DH, axis=1)
    pre = (ab + nxt)[:, :NSA_DH] + pos_ref[...]
    hid = jax.nn.gelu(pre)
    out = _dot(hid.astype(BF16), w2_ref[...]).astype(BF16)
    rows = lax.broadcasted_iota(I32, (nb, nb), 0)
    cols = lax.broadcasted_iota(I32, (nb, nb), 1)
    flip = jnp.where(rows + cols == nb - 1, 1.0, 0.0).astype(BF16)
    o_ref[...] = _dot(flip, out).astype(o_ref.dtype)


def _compress(x16, w1ab, posterm, w2):
    _, b, g, nb, wd = x16.shape
    return pl.pallas_call(
        _compress_kernel,
        grid=(2, b, g),
        in_specs=[pl.BlockSpec((None, None, None, nb, wd), lambda s, bi, gi: (s, bi, gi, 0, 0)),
                  pl.BlockSpec((None, wd, 2 * NSA_DH), lambda s, bi, gi: (s, 0, 0)),
                  pl.BlockSpec((None, 1, NSA_DH), lambda s, bi, gi: (s, 0, 0)),
                  pl.BlockSpec((None, NSA_DH, NSA_DH), lambda s, bi, gi: (s, 0, 0))],
        out_specs=pl.BlockSpec((None, None, None, nb, NSA_DH), lambda s, bi, gi: (s, bi, gi, 0, 0)),
        out_shape=jax.ShapeDtypeStruct((2, b, g, nb, NSA_DH), BF16),
        compiler_params=_cparams(("arbitrary",) * 3),
        name="nsa_compress",
    )(x16, w1ab, posterm, w2)


def _posterm_kernel(p_ref, w_ref, o_ref):
    o_ref[...] = _dot(p_ref[...], w_ref[...])[0:1]


def _posterm(pos8, w1):
    return pl.pallas_call(
        _posterm_kernel,
        grid=(2,),
        in_specs=[pl.BlockSpec((None, 8, pos8.shape[2]), lambda s: (s, 0, 0)),
                  pl.BlockSpec((None, w1.shape[1], NSA_DH), lambda s: (s, 0, 0))],
        out_specs=pl.BlockSpec((None, 1, NSA_DH), lambda s: (s, 0, 0)),
        out_shape=jax.ShapeDtypeStruct((2, 1, NSA_DH), F32),
        name="nsa_posterm",
    )(pos8, w1)


def _eye(n, scale=1.0):
    rows = lax.broadcasted_iota(I32, (n, n), 0)
    cols = lax.broadcasted_iota(I32, (n, n), 1)
    return jnp.where(rows == cols, scale, 0.0).astype(BF16)


def _nsa_kernel(q_ref, qn_ref, kc_ref, vcin_ref, ksin_ref, vsin_ref, kwin_ref, vwin_ref, gate_ref,
                cb_ref, bt_ref, wb_ref, cov_ref, b31_ref, bcols_ref, o_ref,
                s_ref, ks_ref, vs_ref, kw_ref, vw_ref, vc_ref, qt_ref, *, n_sel):
    gi = pl.program_id(1)
    i = pl.program_id(2)
    q0 = i * QB
    r4 = NSA_HPG * QB
    ns = cov_ref.shape[0]
    eye_dh = _eye(NSA_DH)

    def group_lanes(x):
        return jnp.where(gi == 0, x[:, 0:NSA_DH], x[:, NSA_DH:2 * NSA_DH])

    def q_transposed(ref):
        eye_q = _eye(NSA_DH, NSA_DH ** -0.5)
        qb = ref[...]
        return jnp.concatenate([_dot_nt(eye_q, qb[:, h * NSA_DH:(h + 1) * NSA_DH]) for h in range(NSA_HPG)],
                               axis=1).astype(BF16)

    @pl.when(i == 0)
    def _():
        s_len = ksin_ref.shape[0]
        ones_rows = jnp.where(lax.broadcasted_iota(I32, (AUG_ROWS, SEL_TK), 0) == 0, 1.0, 0.0).astype(BF16)
        ones_rows_qb = jnp.where(lax.broadcasted_iota(I32, (AUG_ROWS, QB), 0) == 0, 1.0, 0.0).astype(BF16)
        tail = jnp.where(lax.broadcasted_iota(I32, (SEL_TK, NSA_DH), 1) < 2, 1.0, 0.0).astype(BF16)
        depth = ks_ref.shape[-1]
        vc_ref[...] = _dot_nt(eye_dh, vcin_ref[...]).astype(BF16)
        qt_ref[...] = q_transposed(q_ref)
        pad_flag = jnp.where(lax.broadcasted_iota(I32, (WINDOW, 2 * NSA_DH), 1) == NSA_DH, 1.0, 0.0)
        kw_ref[0:WINDOW, :] = pad_flag.astype(BF16)
        for u in range(WINDOW // QB):
            vw_ref[u] = jnp.zeros(vw_ref.shape[1:], BF16)

        def chunk(c, _):
            r0 = pl.multiple_of(c * SEL_TK, SEL_TK)
            rows = pl.ds(r0, SEL_TK)
            k_aug = jnp.concatenate([group_lanes(ksin_ref[rows, :]), tail], axis=1)
            ks_ref[rows, 0:ns] = bcols_ref[rows, :]
            ks_ref[rows, ns:ns + 2 * NSA_DH] = k_aug
            if depth > ns + 2 * NSA_DH:
                ks_ref[rows, ns + 2 * NSA_DH:depth] = jnp.zeros((SEL_TK, depth - ns - 2 * NSA_DH), BF16)
            v_t = _dot_nt(eye_dh, group_lanes(vsin_ref[rows, :])).astype(BF16)
            vs_ref[c] = jnp.concatenate([v_t, ones_rows], axis=0)
            kw_ref[pl.ds(WINDOW + r0, SEL_TK), :] = jnp.concatenate(
                [group_lanes(kwin_ref[rows, :]), jnp.zeros((SEL_TK, NSA_DH), BF16)], axis=1)
            vw_t = _dot_nt(eye_dh, group_lanes(vwin_ref[rows, :])).astype(BF16)
            for u in range(SEL_TK // QB):
                vw_ref[WINDOW // QB + c * (SEL_TK // QB) + u] = jnp.concatenate(
                    [vw_t[:, u * QB:(u + 1) * QB], ones_rows_qb], axis=0)
            return 0

        lax.fori_loop(0, s_len // SEL_TK, chunk, 0)

    q = qt_ref[...]

    step_rows = QB // CMP_STRIDE
    cb = cb_ref[pl.ds(pl.multiple_of(i * step_rows, step_rows), kc_ref.shape[0]), :]
    s_c = _dot(kc_ref[...], q) + cb
    m_c = jnp.maximum(jnp.max(s_c, axis=0, keepdims=True), 0.1 * NEG_INF)
    p_c = jnp.exp(s_c - m_c)
    l_c = jnp.sum(p_c, axis=0, keepdims=True)
    p_c = p_c / jnp.maximum(l_c, 1e-30)
    o_c = _dot(vc_ref[...], p_c.astype(BF16))

    psum = p_c[:, 0:QB] + p_c[:, QB:2 * QB] + p_c[:, 2 * QB:3 * QB] + p_c[:, 3 * QB:4 * QB]
    p_hi = psum.astype(BF16)
    p_lo = (psum - p_hi.astype(F32)).astype(BF16)
    cov = cov_ref[...]
    imp_t = _dot(cov, p_hi) + _dot(cov, p_lo)
    ns = imp_t.shape[0]

    wk = WINDOW + QB
    kwin = kw_ref[pl.ds(pl.multiple_of(q0, QB), wk), :]
    pad_mask_rows = jnp.where(lax.broadcasted_iota(I32, (NSA_DH, r4), 0) == 0, -MASK_BIG, 0.0).astype(BF16)
    s_w = _dot(kwin, jnp.concatenate([q, pad_mask_rows], axis=0)) + wb_ref[...]
    m_w = jnp.max(s_w, axis=0, keepdims=True)
    p_w = jnp.exp(s_w - m_w).astype(BF16)
    acc_w = jnp.zeros((NSA_DH + AUG_ROWS, r4), F32)
    for u in range(wk // QB):
        acc_w = acc_w + _dot(vw_ref[i + u], p_w[u * QB:(u + 1) * QB])
    o_w = acc_w[0:NSA_DH] / acc_w[NSA_DH:NSA_DH + 1]

    jidx = lax.broadcasted_iota(I32, (ns, QB), 0)
    t = q0 + lax.broadcasted_iota(I32, (ns, QB), 1)
    cur = jnp.right_shift(t, 6)
    eligible = jidx <= cur
    forced = (jidx == 0) | (jidx == cur) | (jidx == cur - 1)
    score = jnp.where(eligible, jnp.where(forced, -2.0, imp_t), -1.0)
    sel = jnp.where(forced, 1.0, 0.0)
    for _ in range(n_sel - 3):
        mx = jnp.max(score, axis=0, keepdims=True)
        first = jnp.min(jnp.where(score == mx, jidx, ns), axis=0, keepdims=True)
        pick = jidx == first
        sel = jnp.where(pick, 1.0, sel)
        score = jnp.where(pick, -2.0, score)
    notsel = jnp.where(eligible, 1.0 - sel, 1.0).astype(BF16)

    depth = ks_ref.shape[-1]
    pad_rows = depth - ns - NSA_DH - AUG_ROWS
    qa = jnp.concatenate([jnp.concatenate([notsel] * NSA_HPG, axis=1), q, b31_ref[...],
                          jnp.zeros((pad_rows, r4), BF16)], axis=0)
    sub = SEL_TK // QB
    n_tiles = i // sub + 1
    n_far = jnp.maximum(i - N_OFF + 2, 0) // sub

    def put_scores(jt, slot):
        k0 = pl.multiple_of(jt * SEL_TK, SEL_TK)
        s_ref[slot] = _dot(ks_ref[pl.ds(k0, SEL_TK), :], qa)

    def absorb(jt, slot, carry, near):
        m_prev, acc = carry
        s = s_ref[slot]
        if near:
            offs = [jnp.clip(i - sub * jt - u, 0, N_OFF - 1) for u in range(sub)]
            s = s + jnp.concatenate([bt_ref[o] for o in offs], axis=0)
        m_new = jnp.maximum(m_prev, jnp.max(s, axis=0, keepdims=True))
        alpha = jnp.exp(m_prev - m_new)
        p = jnp.exp(s - m_new).astype(BF16)
        return m_new, alpha * acc + _dot(vs_ref[jt], p)

    def pair_body(pp, carry, near):
        ja = 2 * pp
        put_scores(ja + 1, 1)
        carry = absorb(ja, 0, carry, near)
        put_scores(jnp.minimum(ja + 2, last_tile), 0)
        return absorb(ja + 1, 1, carry, near)

    n_pairs = (n_tiles + 1) // 2
    last_tile = 2 * n_pairs - 1
    far_pairs = n_far // 2
    put_scores(0, 0)
    carry = (jnp.full((1, r4), -1e38, F32), jnp.zeros((NSA_DH + AUG_ROWS, r4), F32))
    carry = lax.fori_loop(0, far_pairs, functools.partial(pair_body, near=False), carry)
    _, acc_s = lax.fori_loop(far_pairs, n_pairs, functools.partial(pair_body, near=True), carry)
    o_s = acc_s[0:NSA_DH] / acc_s[NSA_DH:NSA_DH + 1]

    gates = jax.nn.sigmoid(gate_ref[...])
    o = (gates[0:1] * o_c + gates[1:2] * o_s + gates[2:3] * o_w).astype(BF16)
    qt_ref[...] = q_transposed(qn_ref)
    eye_l = _eye(LANES)
    o_ref[...] = jnp.concatenate(
        [jnp.concatenate([_dot_nt(eye_l, o[:, h * QB + u * LANES:h * QB + (u + 1) * LANES])
                          for u in range(QB // LANES)], axis=0) for h in range(NSA_HPG)],
        axis=1).astype(o_ref.dtype)


def _nsa_attention(pb, c_qa, c_kv, kvc, gates, cb, bt, wb, cov_t, b31, blockcols, b, s, n_sel):
    g, dh = NSA_GROUPS, NSA_DH
    r4 = NSA_HPG * QB
    nqb = s // QB
    nc = kvc.shape[3]
    ns = cov_t.shape[0]
    depth = -(-(ns + 2 * dh) // LANES) * LANES
    n_tiles = s // SEL_TK
    assert n_tiles % 2 == 0, "the selected sweep consumes key tiles in pairs"
    qw = NSA_HPG * dh
    once = pl.Buffered(1)
    slab = lambda k: pl.BlockSpec((s, 2 * dh), lambda bi, gi, i: (bi, c_kv // (2 * dh) + k), pipeline_mode=once)
    per_g = lambda a: pl.BlockSpec((None,) + a.shape[1:], lambda bi, gi, i: (gi,) + (0,) * (a.ndim - 1),
                                   pipeline_mode=once)
    const = lambda a: pl.BlockSpec(a.shape, lambda bi, gi, i: (0,) * a.ndim, pipeline_mode=once)
    return pl.pallas_call(
        functools.partial(_nsa_kernel, n_sel=n_sel),
        grid=(b, g, nqb),
        in_specs=[
            pl.BlockSpec((QB, qw), lambda bi, gi, i: (bi * nqb + i, c_qa // qw + gi)),
            pl.BlockSpec((QB, qw), lambda bi, gi, i: (bi * nqb + jnp.minimum(i + 1, nqb - 1), c_qa // qw + gi)),
            pl.BlockSpec((None, None, None, nc, dh), lambda bi, gi, i: (0, bi, gi, 0, 0)),
            pl.BlockSpec((None, None, None, nc, dh), lambda bi, gi, i: (1, bi, gi, 0, 0)),
            slab(2), slab(3), slab(4), slab(5),
            pl.BlockSpec((None, None, None, 3, r4), lambda bi, gi, i: (bi, gi, i, 0, 0)),
            per_g(cb), per_g(bt), per_g(wb), const(cov_t), per_g(b31), const(blockcols),
        ],
        out_specs=pl.BlockSpec((QB, qw), lambda bi, gi, i: (bi * nqb + i, gi)),
        out_shape=jax.ShapeDtypeStruct((b * s, g * qw), BF16),
        scratch_shapes=[pltpu.VMEM((2, SEL_TK, r4), F32),
                        pltpu.VMEM((s, depth), BF16),
                        pltpu.VMEM((n_tiles, dh + AUG_ROWS, SEL_TK), BF16),
                        pltpu.VMEM((s + WINDOW, 2 * dh), BF16),
                        pltpu.VMEM(((s + WINDOW) // QB, dh + AUG_ROWS, QB), BF16),
                        pltpu.VMEM((dh, nc), BF16),
                        pltpu.VMEM((dh, r4), BF16)],
        compiler_params=_cparams(("arbitrary",) * 3),
        name="nsa_attention",
    )(pb, pb, kvc, kvc, pb, pb, pb, pb, gates, cb, bt, wb, cov_t, b31, blockcols)


def _ret_kernel(q_ref, k_ref, v_ref, g_ref, cos_ref, sin_ref, dm_ref, qd_ref, kd_ref, cd_ref,
                o_ref, state_ref):
    n = pl.program_id(1)

    @pl.when(n == 0)
    def _():
        state_ref[...] = jnp.zeros_like(state_ref)

    lane = lax.broadcasted_iota(I32, (RET_CHUNK, LANES), 1)
    first_half = (lane & (RET_DK - 1)) < (RET_DK // 2)

    def rot(x, cosv, sinv):
        partner = jnp.where(first_half, pltpu.roll(x, LANES - RET_DK // 2, axis=1),
                            pltpu.roll(x, RET_DK // 2, axis=1))
        return x * cosv + partner * sinv

    states = [state_ref[h] for h in range(RET_HEADS)]
    for cc in range(RET_STEP):
        rows = slice(cc * RET_CHUNK, (cc + 1) * RET_CHUNK)
        cosv = cos_ref[rows, :]
        sinv = sin_ref[rows, :]
        for hp in range(RET_HEADS // 2):
            ls = slice(hp * LANES, (hp + 1) * LANES)
            q2 = rot(q_ref[rows, ls], cosv, sinv)
            k2 = rot(k_ref[rows, ls], cosv, sinv) * (RET_DK ** -0.5)
            qdec = (q2 * qd_ref[:, ls]).astype(BF16)
            kdec_t = (k2 * kd_ref[:, ls]).T.astype(BF16)
            q2b = q2.astype(BF16)
            k2b = k2.astype(BF16)
            for e in range(2):
                h = 2 * hp + e
                hs = slice(e * RET_DK, (e + 1) * RET_DK)
                vs = slice(h * RET_DV, (h + 1) * RET_DV)
                v = v_ref[rows, vs]
                st = states[h]
                sc = _dot_nt(q2b[:, hs], k2b[:, hs]) * dm_ref[h]
                y = _dot(sc.astype(BF16), v) + _dot(qdec[:, hs], st.astype(BF16))
                states[h] = cd_ref[h] * st + _dot(kdec_t[hs, :], v)
                mu = jnp.mean(y, axis=-1, keepdims=True)
                var = jnp.mean(jnp.square(y - mu), axis=-1, keepdims=True)
                yn = (y - mu) * lax.rsqrt(var + GN_EPS)
                o_ref[rows, vs] = (jax.nn.silu(g_ref[rows, vs]) * yn).astype(o_ref.dtype)
    for h in range(RET_HEADS):
        state_ref[h] = states[h]


def _retention(pf, pb, b, s, col_q, col_k, col_v, col_g, tabs):
    cosf, sinf, dm, qd, kd, cd = tabs
    c = RET_CHUNK * RET_STEP
    n = s // c
    t = b * s
    wqk = RET_HEADS * RET_DK
    wv = RET_HEADS * RET_DV
    row = lambda bi, ni: bi * n + ni
    full = lambda a: pl.BlockSpec(a.shape, lambda bi, ni: (0,) * a.ndim)
    return pl.pallas_call(
        _ret_kernel,
        grid=(b, n),
        in_specs=[
            pl.BlockSpec((c, wqk), lambda bi, ni: (row(bi, ni), col_q // wqk)),
            pl.BlockSpec((c, wqk), lambda bi, ni: (row(bi, ni), col_k // wqk)),
            pl.BlockSpec((c, wv), lambda bi, ni: (row(bi, ni), col_v // wv)),
            pl.BlockSpec((c, wv), lambda bi, ni: (row(bi, ni), col_g // wv)),
            pl.BlockSpec((c, LANES), lambda bi, ni: (ni, 0)),
            pl.BlockSpec((c, LANES), lambda bi, ni: (ni, 0)),
            full(dm), full(qd), full(kd), full(cd),
        ],
        out_specs=pl.BlockSpec((c, wv), lambda bi, ni: (row(bi, ni), 0)),
        out_shape=jax.ShapeDtypeStruct((t, wv), BF16),
        scratch_shapes=[pltpu.VMEM((RET_HEADS, RET_DK, RET_DV), F32)],
        compiler_params=_cparams(("arbitrary",) * 2),
        name="retention",
    )(pf, pf, pb, pf, cosf, sinf, dm, qd, kd, cd)


def _retention_tables(s):
    half = RET_DK // 2
    inv = ROPE_BASE ** (-jnp.arange(half, dtype=F32) * 2.0 / RET_DK)
    ang = jnp.arange(s, dtype=F32)[:, None] * inv[None, :]
    cos, sin = jnp.cos(ang), jnp.sin(ang)
    cosf = jnp.concatenate([cos, cos, cos, cos], axis=1)
    sinf = jnp.concatenate([-sin, sin, -sin, sin], axis=1)
    lg = jnp.log(1.0 - 2.0 ** (-5.0 - jnp.arange(RET_HEADS, dtype=F32)))
    i = jnp.arange(RET_CHUNK, dtype=F32)
    diff = i[:, None] - i[None, :]
    dm = jnp.where(diff >= 0, jnp.exp(jnp.maximum(diff, 0.0)[None] * lg[:, None, None]), 0.0)
    kdec = jnp.exp((RET_CHUNK - 1.0 - i)[None, :] * lg[:, None])
    qdec = jnp.exp((i + 1.0)[None, :] * lg[:, None])
    expand = lambda a: jnp.repeat(a.T, RET_DK, axis=1)
    cd = jnp.broadcast_to(jnp.exp(RET_CHUNK * lg)[:, None, None], (RET_HEADS, 1, LANES))
    return cosf, sinf, dm, expand(qdec), expand(kdec), cd


def _layer_norm(z, g, b):
    mu = jnp.mean(z, axis=-1, keepdims=True)
    var = jnp.mean(jnp.square(z - mu), axis=-1, keepdims=True)
    return (z - mu) * lax.rsqrt(var + LN_EPS) * g + b


def _merge_kernel(ya_ref, yb_ref, ma_ref, mb_ref, x_ref, wa_ref, wb_ref, wo_ref, g_ref, b_ref,
                  wr_ref, rb_ref, tri_ref, o_ref, eidx_ref, wts_ref, rank_ref, cnt_ref, carry_ref, *, alpha):
    merged = (jax.nn.sigmoid(ma_ref[...]) * _dot(ya_ref[...], wa_ref[...])
              + jax.nn.sigmoid(mb_ref[...]) * _dot(yb_ref[...], wb_ref[...]))
    z = alpha * x_ref[...] + _dot(merged.astype(BF16), wo_ref[...])
    x1 = _layer_norm(z, g_ref[...], b_ref[...])
    o_ref[...] = x1
    _route(x1, wr_ref, rb_ref, tri_ref, eidx_ref, wts_ref, rank_ref, cnt_ref, carry_ref)


def _merge(ya, yb, pf, col_ma, col_mb, x, wa, wb, wo, g, bb, wr_t, rb_col, tri, alpha):
    t, d = x.shape
    tm = tri.shape[0]
    full = lambda a: pl.BlockSpec(a.shape, lambda i: (0,) * a.ndim)
    out2 = lambda dt: jax.ShapeDtypeStruct((2, t), dt)
    row2 = pl.BlockSpec((2, tm), lambda i: (0, i))
    return pl.pallas_call(
        functools.partial(_merge_kernel, alpha=alpha),
        grid=(t // tm,),
        in_specs=[pl.BlockSpec((tm, ya.shape[1]), lambda i: (i, 0)),
                  pl.BlockSpec((tm, yb.shape[1]), lambda i: (i, 0)),
                  pl.BlockSpec((tm, d), lambda i: (i, col_ma // d)),
                  pl.BlockSpec((tm, d), lambda i: (i, col_mb // d)),
                  pl.BlockSpec((tm, d), lambda i: (i, 0)),
                  full(wa), full(wb), full(wo), full(g), full(bb), full(wr_t), full(rb_col), full(tri)],
        out_specs=[pl.BlockSpec((tm, d), lambda i: (i, 0)), row2, row2, row2,
                   pl.BlockSpec((N_EXPERTS, LANES), lambda i: (0, 0))],
        out_shape=[jax.ShapeDtypeStruct((t, d), F32), out2(I32), out2(F32), out2(I32),
                   jax.ShapeDtypeStruct((N_EXPERTS, LANES), F32)],
        scratch_shapes=[pltpu.VMEM((N_EXPERTS, LANES), F32)],
        compiler_params=_cparams(("arbitrary",)),
        name="merge_ln1_router",
    )(ya, yb, pf, pf, x, wa, wb, wo, g, bb, wr_t, rb_col, tri)


def _route(x, wr_ref, rb_ref, tri_ref, eidx_ref, wts_ref, rank_ref, cnt_ref, carry_ref):
    step = pl.program_id(0)

    @pl.when(step == 0)
    def _():
        carry_ref[...] = jnp.zeros_like(carry_ref)

    wr = wr_ref[...]
    w_hi = wr.astype(BF16)
    w_lo = (wr - w_hi.astype(F32)).astype(BF16)
    x_hi = x.astype(BF16)
    x_lo = (x - x_hi.astype(F32)).astype(BF16)
    logits = _dot_nt(w_hi, x_hi) + _dot_nt(w_hi, x_lo) + _dot_nt(w_lo, x_hi)
    tm = logits.shape[1]
    mx = jnp.max(logits, axis=0, keepdims=True)
    ex = jnp.exp(logits - mx)
    probs = ex / jnp.sum(ex, axis=0, keepdims=True)
    sel = probs + rb_ref[...]
    rows = [sel[e:e + 1] for e in range(N_EXPERTS)]
    gscore = []
    for gi in range(N_GROUPS):
        r = rows[gi * EPG:(gi + 1) * EPG]
        best2 = None
        for a in range(EPG):
            for c in range(a + 1, EPG):
                pair = r[a] + r[c]
                best2 = pair if best2 is None else jnp.maximum(best2, pair)
        gscore.append(best2)
    best = jnp.zeros((1, tm), I32)
    bscore = gscore[0]
    for gi in range(1, N_GROUPS):
        better = gscore[gi] > bscore
        best = jnp.where(better, gi, best)
        bscore = jnp.where(better, gscore[gi], bscore)
    ing = []
    for a in range(EPG):
        v = rows[a]
        for gi in range(1, N_GROUPS):
            v = jnp.where(best == gi, rows[gi * EPG + a], v)
        ing.append(v)
    i1 = jnp.zeros((1, tm), I32)
    v1 = ing[0]
    for a in range(1, EPG):
        better = ing[a] > v1
        i1 = jnp.where(better, a, i1)
        v1 = jnp.where(better, ing[a], v1)
    i2 = jnp.full((1, tm), -1, I32)
    v2 = jnp.full((1, tm), -jnp.inf, F32)
    for a in range(EPG):
        better = (i1 != a) & (ing[a] > v2)
        i2 = jnp.where(better, a, i2)
        v2 = jnp.where(better, ing[a], v2)
    e0 = best * EPG + i1
    e1 = best * EPG + i2
    eio = lax.broadcasted_iota(I32, (N_EXPERTS, tm), 0)
    oh0 = (eio == e0).astype(F32)
    oh1 = (eio == e1).astype(F32)
    p0 = jnp.sum(oh0 * probs, axis=0, keepdims=True)
    p1 = jnp.sum(oh1 * probs, axis=0, keepdims=True)
    den = p0 + p1
    eidx_ref[0:1, :] = e0
    eidx_ref[1:2, :] = e1
    wts_ref[0:1, :] = p0 / den
    wts_ref[1:2, :] = p1 / den
    tot = oh0 + oh1
    prefix = _dot(tot.astype(BF16), tri_ref[...]) + carry_ref[:, 0:1]
    rank_ref[0:1, :] = jnp.sum(oh0 * prefix, axis=0, keepdims=True).astype(I32)
    rank_ref[1:2, :] = jnp.sum(oh1 * prefix, axis=0, keepdims=True).astype(I32)
    carry_ref[...] = carry_ref[...] + jnp.sum(tot, axis=1, keepdims=True)
    cnt_ref[...] = carry_ref[...]


def _dispatch_kernel(dest_ref, zflag_ref, x_ref, xs_ref, zero_ref, sem, zsem, rows_ref, *, t_total):
    step = pl.program_id(0)
    tm = x_ref.shape[0]

    @pl.when(step == 0)
    def _():
        zero_ref[...] = jnp.zeros_like(zero_ref)

        def zero_block(bi, _):
            @pl.when(zflag_ref[bi] == 1)
            def _():
                r0 = pl.multiple_of(bi * MOE_BM, MOE_BM)
                cp = pltpu.make_async_copy(zero_ref, xs_ref.at[pl.ds(r0, MOE_BM)], zsem)
                cp.start()
                cp.wait()
            return 0

        lax.fori_loop(0, zflag_ref.shape[0], zero_block, 0)

    rows_ref[...] = x_ref[...].reshape(rows_ref.shape)

    def copy(r, slot):
        dst = dest_ref[slot * t_total + step * tm + r]
        return pltpu.make_async_copy(rows_ref.at[r], xs_ref.at[dst], sem)

    def issue(r, _):
        copy(r, 0).start()
        copy(r, 1).start()
        return 0

    def drain(r, _):
        copy(r, 0).wait()
        copy(r, 1).wait()
        return 0

    lax.fori_loop(0, tm, issue, 0, unroll=ROW_DMA_UNROLL)
    lax.fori_loop(0, tm, drain, 0, unroll=ROW_DMA_UNROLL)


def _dispatch(dest_flat, zflag, x, rows, tm):
    t, d = x.shape
    sub = d // LANES
    return pl.pallas_call(
        functools.partial(_dispatch_kernel, t_total=t),
        grid_spec=pltpu.PrefetchScalarGridSpec(
            num_scalar_prefetch=2,
            grid=(t // tm,),
            in_specs=[pl.BlockSpec((tm, d), lambda i, dr, zf: (i, 0))],
            out_specs=pl.BlockSpec(memory_space=pl.ANY),
            scratch_shapes=[pltpu.VMEM((MOE_BM, sub, LANES), x.dtype), pltpu.SemaphoreType.DMA,
                            pltpu.SemaphoreType.DMA, pltpu.VMEM((tm, sub, LANES), x.dtype)],
        ),
        out_shape=jax.ShapeDtypeStruct((rows, sub, LANES), x.dtype),
        compiler_params=pltpu.CompilerParams(dimension_semantics=("arbitrary",), has_side_effects=True,
                                             vmem_limit_bytes=VMEM_LIMIT),
        name="moe_dispatch",
    )(dest_flat, zflag, x)


def _expert_kernel(be_ref, na_ref, x_ref, wg_ref, wu_ref, wd_ref, o_ref, wg_s, wu_s, wd_s):
    i = pl.program_id(0)
    active = i < na_ref[0]

    @pl.when((i == 0) | (be_ref[i] != be_ref[jnp.maximum(i - 1, 0)]))
    def _():
        wg_s[...] = wg_ref[...].astype(BF16)
        wu_s[...] = wu_ref[...].astype(BF16)
        wd_s[...] = wd_ref[...].astype(BF16)

    @pl.when(active)
    def _():
        xb = x_ref[...].reshape(x_ref.shape[0], wg_s.shape[0]).astype(BF16)
        gate = _dot(xb, wg_s[...])
        up = _dot(xb, wu_s[...])
        hid = (jax.nn.silu(gate) * up).astype(BF16)
        o_ref[...] = _dot(hid, wd_s[...]).reshape(o_ref.shape)

    @pl.when(jnp.logical_not(active))
    def _():
        o_ref[...] = jnp.zeros_like(o_ref)


def _experts(blk_e, n_active, xs, wg, wu, wd, layer):
    rows = xs.shape[0]
    nb = rows // MOE_BM
    act = lambda i, be, na: jnp.minimum(i, na[0] - 1)
    wspec = lambda a: pl.BlockSpec((None, None) + a.shape[2:], lambda i, be, na: (layer, be[i], 0, 0))
    return pl.pallas_call(
        _expert_kernel,
        grid_spec=pltpu.PrefetchScalarGridSpec(
            num_scalar_prefetch=2,
            grid=(nb,),
            in_specs=[pl.BlockSpec((MOE_BM,) + xs.shape[1:], lambda i, be, na: (act(i, be, na), 0, 0)),
                      wspec(wg), wspec(wu), wspec(wd)],
            out_specs=pl.BlockSpec((MOE_BM,) + xs.shape[1:], lambda i, be, na: (i, 0, 0)),
            scratch_shapes=[pltpu.VMEM(wg.shape[2:], BF16), pltpu.VMEM(wu.shape[2:], BF16),
                            pltpu.VMEM(wd.shape[2:], BF16)],
        ),
        out_shape=jax.ShapeDtypeStruct(xs.shape, F32),
        compiler_params=_cparams(("arbitrary",)),
        name="moe_experts",
    )(blk_e, n_active, xs, wg, wu, wd)


def _combine_kernel(dest_ref, x_ref, w_ref, g_ref, b_ref, ys_ref, o_ref, ob_ref, buf_ref, sem, *, alpha, t_total):
    step = pl.program_id(0)
    n_steps = pl.num_programs(0)
    tm = x_ref.shape[0]

    def copy(tile, r, slot):
        src = dest_ref[slot * t_total + tile * tm + r]
        half = tile % 2
        return pltpu.make_async_copy(ys_ref.at[src], buf_ref.at[half, slot, r], sem.at[half])

    def issue_tile(tile):
        def body(r, _):
            copy(tile, r, 0).start()
            copy(tile, r, 1).start()
            return 0
        lax.fori_loop(0, tm, body, 0, unroll=ROW_DMA_UNROLL)

    @pl.when(step == 0)
    def _():
        issue_tile(0)

    @pl.when(step + 1 < n_steps)
    def _():
        issue_tile(step + 1)

    def drain(r, _):
        copy(step, r, 0).wait()
        copy(step, r, 1).wait()
        return 0

    lax.fori_loop(0, tm, drain, 0, unroll=ROW_DMA_UNROLL)
    w = w_ref[...]
    cur = step % 2
    y0 = buf_ref[cur, 0].reshape(x_ref.shape)
    y1 = buf_ref[cur, 1].reshape(x_ref.shape)
    z = alpha * x_ref[...] + w[:, 0:1] * y0 + w[:, 1:2] * y1
    y = _layer_norm(z, g_ref[...], b_ref[...])
    o_ref[...] = y
    ob_ref[...] = y.astype(BF16)


def _combine(dest_flat, x, wts_t, g, bb, ys, alpha, tm):
    t, d = x.shape
    return pl.pallas_call(
        functools.partial(_combine_kernel, alpha=alpha, t_total=t),
        grid_spec=pltpu.PrefetchScalarGridSpec(
            num_scalar_prefetch=1,
            grid=(t // tm,),
            in_specs=[pl.BlockSpec((tm, d), lambda i, dr: (i, 0)),
                      pl.BlockSpec((tm, 2), lambda i, dr: (i, 0)),
                      pl.BlockSpec(g.shape, lambda i, dr: (0, 0)),
                      pl.BlockSpec(bb.shape, lambda i, dr: (0, 0)),
                      pl.BlockSpec(memory_space=pl.ANY)],
            out_specs=[pl.BlockSpec((tm, d), lambda i, dr: (i, 0)), pl.BlockSpec((tm, d), lambda i, dr: (i, 0))],
            scratch_shapes=[pltpu.VMEM((2, 2, tm, d // LANES, LANES), F32), pltpu.SemaphoreType.DMA((2,))],
        ),
        out_shape=[jax.ShapeDtypeStruct((t, d), F32), jax.ShapeDtypeStruct((t, d), BF16)],
        compiler_params=_cparams(("arbitrary",)),
        name="moe_combine_ln2",
    )(dest_flat, x, wts_t, g, bb, ys)


def _attention_tables(rel_bias, s):
    nc = s // CMP_STRIDE
    nqb = s // QB
    iq = jnp.arange(QB, dtype=I32)[None, None, :]
    x = jnp.arange(nc + (QB // CMP_STRIDE) * nqb, dtype=I32)[:, None]
    rel_c = iq[0] + CMP_STRIDE * (x - nc + 1) - (CMP_LEN - 1)
    jk = jnp.arange(QB, dtype=I32)[None, :, None]
    rel_t = jnp.arange(N_OFF, dtype=I32)[:, None, None] * QB + iq - jk
    jw = jnp.arange(WINDOW + QB, dtype=I32)[:, None]
    rel_w = iq[0] - jw + WINDOW
    rel_w = jnp.where(rel_w < WINDOW, rel_w, -1)
    flat = rel_bias.reshape(-1)
    r4 = NSA_HPG * QB
    cb = _bias_table(flat, jnp.maximum(rel_c, -1), rel_c.shape[0])
    bt = _bias_table(flat, jnp.maximum(rel_t, -1).reshape(N_OFF * QB, QB), QB, delta=True)
    bt = bt.reshape(NSA_GROUPS, N_OFF, QB, r4)
    wb = _bias_table(flat, jnp.maximum(rel_w, -1), WINDOW + QB)
    sat = jnp.repeat(rel_bias[REL_BUCKETS - 1].reshape(NSA_GROUPS, 1, NSA_HPG), QB, axis=2)
    hi = sat.astype(BF16)
    lo = (sat - hi.astype(F32)).astype(BF16)
    b31 = jnp.concatenate([hi, lo, jnp.zeros((NSA_GROUPS, AUG_ROWS - 2, r4), BF16)], axis=1)
    return cb, bt, wb, b31


def _selection_tables(s):
    nc = s // CMP_STRIDE
    ns = s // SLC_BLOCK
    c_start = np.arange(nc) * CMP_STRIDE
    s_start = np.arange(ns) * SLC_BLOCK
    cover = np.maximum(np.minimum(c_start[:, None] + CMP_LEN, s_start[None, :] + SLC_BLOCK)
                       - np.maximum(c_start[:, None], s_start[None, :]), 0).astype(np.float32) / CMP_LEN
    cover[nc - 1] = 0.0
    cover_rev_t = np.ascontiguousarray(cover[::-1].T)
    blockcols = np.where(np.arange(s)[:, None] // SLC_BLOCK == np.arange(ns)[None, :], -MASK_BIG, 0.0)
    return jnp.asarray(cover_rev_t, BF16), jnp.asarray(blockcols, BF16)


def _split_in_weights(w_in_l):
    o = np.cumsum((0, 512, 128, 128, 128, 128, 128, 128, 24, 512, 512, 1024, 1024, 1024, 1024))
    seg = lambda a: w_in_l[:, o[a]:o[a + 1]]
    wb = jnp.concatenate([seg(10), seg(0), seg(1), seg(2), seg(3), seg(4), seg(5), seg(6)], axis=1)
    gpad = jnp.pad(seg(7), ((0, 0), (0, LANES - 24)))
    wf = jnp.concatenate([seg(8), seg(9), seg(11), seg(12), seg(13), gpad], axis=1)
    return wb.astype(BF16), wf.astype(BF16)


def kernel(x, w_in, cmp_pos, cmp_w1, cmp_w2, w_br_a, w_br_b, w_o, ln1_g, ln1_b, w_router, router_bias,
           moe_w_gate, moe_w_up, moe_w_down, ln2_g, ln2_b, rel_bias):
    b, s, d = x.shape
    depth = w_in.shape[0]
    t = b * s
    alpha = (2 * depth) ** 0.25
    g, r, dh = NSA_GROUPS, NSA_HPG, NSA_DH
    nqb = s // QB
    nb16 = s // CMP_STRIDE
    n_sel = min(N_SELECT, s // SLC_BLOCK)

    cb, bt, wb, b31 = _attention_tables(rel_bias, s)
    cov_t, blockcols = _selection_tables(s)
    ret_tabs = _retention_tables(s)
    tri = jnp.asarray(np.triu(np.ones((512, 512), np.float32), 1), BF16)
    wr_t = w_router.T
    rb_col = router_bias.reshape(N_EXPERTS, 1)
    n_blk = t * 2 // MOE_BM + N_EXPERTS
    rows = n_blk * MOE_BM

    xf = x.reshape(t, d)
    xb = xf.astype(BF16)
    for l in range(depth):
        w_b, w_f = _split_in_weights(w_in[l])
        pb =_matmul(xb, w_b, BF16, 1024, 1152)
        pf = _matmul(xb, w_f, F32, 1024, 1408)
        c_qa = RET_HEADS * RET_DV
        c_kv = c_qa + NSA_HEADS * dh

        def kv_heads(slab):
            col = c_kv + slab * g * dh
            return pb[:, col:col + g * dh].reshape(b, s, g, dh).transpose(0, 2, 1, 3)
        x16 = jnp.stack([kv_heads(0), kv_heads(1)]).reshape(2, b, g, nb16, CMP_STRIDE * dh)
        w1 = cmp_w1[l].astype(BF16)
        half = CMP_STRIDE * dh
        w1ab = jnp.concatenate([w1[:, :half], w1[:, half:]], axis=2)
        pos8 = jnp.broadcast_to(cmp_pos[l].reshape(2, 1, CMP_LEN * dh), (2, 8, CMP_LEN * dh)).astype(BF16)
        kvc = _compress(x16, w1ab, _posterm(pos8, w1), cmp_w2[l].astype(BF16))
        gl = (pf[:, 4096:4096 + 24].reshape(b, nqb, QB, g, r, 3).transpose(0, 3, 1, 5, 4, 2)
              .reshape(b, g, nqb, 3, r * QB))
        ya = _nsa_attention(pb, c_qa, c_kv, kvc, gl, cb, bt, wb, cov_t, b31, blockcols, b, s, n_sel)

        yb = _retention(pf, pb, b, s, 0, 512, 0, 1024, ret_tabs)

        x1, eidx, wts, rank, cnt = _merge(ya, yb, pf, 2048, 3072, xf, w_br_a[l].astype(BF16),
                                          w_br_b[l].astype(BF16), w_o[l].astype(BF16), ln1_g[l].reshape(1, d),
                                          ln1_b[l].reshape(1, d), wr_t, rb_col, tri, alpha)

        counts = cnt[:, 0].astype(I32)
        padded = ((counts + MOE_BM - 1) // MOE_BM) * MOE_BM
        pad_end = jnp.cumsum(padded)
        pad_start = pad_end - padded
        e_iota = jnp.arange(N_EXPERTS, dtype=I32)[:, None, None]
        start_of = jnp.sum(jnp.where(eidx[None] == e_iota, pad_start[:, None, None], 0), axis=0)
        dest = (start_of + rank).reshape(-1)
        n_active = (pad_end[-1] // MOE_BM).astype(I32).reshape(1)
        blk = jnp.minimum(jnp.arange(n_blk, dtype=I32), n_active[0] - 1) * MOE_BM
        blk_e = jnp.minimum(jnp.sum(pad_end[None, :] <= blk[:, None], axis=1), N_EXPERTS - 1).astype(I32)
        blk_all = jnp.arange(n_blk, dtype=I32)
        last_of_expert = jnp.any(pad_end[None, :] == (blk_all[:, None] + 1) * MOE_BM, axis=1)
        zflag = (last_of_expert | (blk_all >= n_active[0])).astype(I32)
        xs = _dispatch(dest, zflag, x1, rows, 512)
        ys = _experts(blk_e, n_active, xs, moe_w_gate, moe_w_up, moe_w_down, l)
        xf, xb = _combine(dest, x1, wts.T, ln2_g[l].reshape(1, d), ln2_b[l].reshape(1, d), ys, alpha, 256)
    return xf.reshape(b, s, d)
```

```python
import functools
import math

import jax
import jax.numpy as jnp
import numpy as np
from jax import lax
from jax.experimental import pallas as pl
from jax.experimental.pallas import tpu as pltpu

F32 = jnp.float32
BF16 = jnp.bfloat16
I32 = jnp.int32

NSA_HEADS = 8
NSA_GROUPS = 2
NSA_HPG = 4
NSA_DH = 64
CMP_LEN = 32
CMP_STRIDE = 16
SLC_BLOCK = 64
N_SELECT = 16
WINDOW = 512
QB = 256
FORCE_SCORE = 1.0e4
RET_HEADS = 8
RET_DK = 64
RET_DV = 128
RET_CHUNK = 128
ROPE_BASE = 10000.0
REL_BUCKETS = 32
REL_MAX_EXACT = 16
REL_MAX_DIST = 2048
N_EXPERTS = 16
N_GROUPS = 4
EPG = 4
D_EXPERT = 512
LN_EPS = 1e-5
GN_EPS = 1e-5
NEG_INF = -1e30

LANES = 128
VMEM_LIMIT = 56 * 1024 * 1024

SEL_TK = 512
AUG_ROWS = 16
MASK_BIG = 2.0 ** 100
SAT_REL = math.ceil(REL_MAX_EXACT * (REL_MAX_DIST / REL_MAX_EXACT)
                    ** ((REL_BUCKETS - 1 - REL_MAX_EXACT) / (REL_BUCKETS - REL_MAX_EXACT)))
N_OFF = -(-(SAT_REL + QB - 1) // QB) + 1
RET_STEP = 2
MOE_BM = 512
ROW_DMA_UNROLL = 8


def _cparams(sem):
    return pltpu.CompilerParams(dimension_semantics=sem, vmem_limit_bytes=VMEM_LIMIT)


def _dot(a, b):
    return jnp.dot(a, b, preferred_element_type=F32)


def _dot_nt(a, b):
    return lax.dot_general(a, b, (((1,), (1,)), ((), ())), preferred_element_type=F32)


def _mm_kernel(a_ref, b_ref, o_ref):
    o_ref[...] = _dot(a_ref[...], b_ref[...]).astype(o_ref.dtype)


def _matmul(a, b, out_dtype, tm, tn):
    m, k = a.shape
    n = b.shape[1]
    return pl.pallas_call(
        _mm_kernel,
        grid=(n // tn, m // tm),
        in_specs=[pl.BlockSpec((tm, k), lambda j, i: (i, 0)),
                  pl.BlockSpec((k, tn), lambda j, i: (0, j))],
        out_specs=pl.BlockSpec((tm, tn), lambda j, i: (i, j)),
        out_shape=jax.ShapeDtypeStruct((m, n), out_dtype),
        compiler_params=_cparams(("arbitrary", "arbitrary")),
        name="proj_matmul",
    )(a, b)


def _bias_kernel(tab_ref, rel_ref, o_ref, *, delta):
    rel = rel_ref[...]
    n = jnp.maximum(rel, 0)
    nf = jnp.maximum(n, 1).astype(F32)
    large = REL_MAX_EXACT + (jnp.log(nf / REL_MAX_EXACT) / math.log(REL_MAX_DIST / REL_MAX_EXACT)
                             * (REL_BUCKETS - REL_MAX_EXACT)).astype(I32)
    large = jnp.minimum(large, REL_BUCKETS - 1)
    bucket = jnp.where(n < REL_MAX_EXACT, n, large)
    eqs = [bucket == k for k in range(REL_BUCKETS)]
    for h in range(NSA_HEADS):
        acc = jnp.full(rel.shape, NEG_INF, F32)
        for k in range(REL_BUCKETS):
            acc = jnp.where(eqs[k], tab_ref[k * NSA_HEADS + h], acc)
        if delta:
            acc = acc - tab_ref[(REL_BUCKETS - 1) * NSA_HEADS + h]
        hh = h % NSA_HPG
        o_ref[h // NSA_HPG, :, hh * QB:(hh + 1) * QB] = jnp.where(rel >= 0, acc, NEG_INF)


def _bias_table(rel_bias_flat, rel, tr, delta=False):
    r, c = rel.shape
    return pl.pallas_call(
        functools.partial(_bias_kernel, delta=delta),
        grid_spec=pltpu.PrefetchScalarGridSpec(
            num_scalar_prefetch=1,
            grid=(r // tr,),
            in_specs=[pl.BlockSpec((tr, c), lambda i, tab: (i, 0))],
            out_specs=pl.BlockSpec((NSA_GROUPS, tr, NSA_HPG * c), lambda i, tab: (0, i, 0)),
        ),
        out_shape=jax.ShapeDtypeStruct((NSA_GROUPS, r, NSA_HPG * c), F32),
        compiler_params=_cparams(("arbitrary",)),
        name="t5_bias_table",
    )(rel_bias_flat, rel)


def _compress_kernel(x_ref, w1_ref, pos_ref, w2_ref, o_ref):
    ab = _dot(x_ref[...], w1_ref[...])
    nb = ab.shape[0]
    nxt = pltpu.roll(ab, nb - 1, axis=0)
    nxt = pltpu.roll(nxt, NSA_DH, axis=1)
    pre = (ab + nxt)[:, :NSA_DH] + pos_ref[...]
    hid = jax.nn.gelu(pre)
    out = _dot(hid.astype(BF16), w2_ref[...]).astype(BF16)
    rows = lax.broadcasted_iota(I32, (nb, nb), 0)
    cols = lax.broadcasted_iota(I32, (nb, nb), 1)
    flip = jnp.where(rows + cols == nb - 1, 1.0, 0.0).astype(BF16)
    o_ref[...] = _dot(flip, out).astype(o_ref.dtype)


def _compress(x16, w1ab, posterm, w2):
    _, b, g, nb, wd = x16.shape
    return pl.pallas_call(
        _compress_kernel,
        grid=(2, b, g),
        in_specs=[pl.BlockSpec((None, None, None, nb, wd), lambda s, bi, gi: (s, bi, gi, 0, 0)),
                  pl.BlockSpec((None, wd, 2 * NSA_DH), lambda s, bi, gi: (s, 0, 0)),
                  pl.BlockSpec((None, 1, NSA_DH), lambda s, bi, gi: (s, 0, 0)),
                  pl.BlockSpec((None, NSA_DH, NSA_DH), lambda s, bi, gi: (s, 0, 0))],
        out_specs=pl.BlockSpec((None, None, None, nb, NSA_DH), lambda s, bi, gi: (s, bi, gi, 0, 0)),
        out_shape=jax.ShapeDtypeStruct((2, b, g, nb, NSA_DH), BF16),
        compiler_params=_cparams(("arbitrary",) * 3),
        name="nsa_compress",
    )(x16, w1ab, posterm, w2)


def _posterm_kernel(p_ref, w_ref, o_ref):
    o_ref[...] = _dot(p_ref[...], w_ref[...])[0:1]


def _posterm(pos8, w1):
    return pl.pallas_call(
        _posterm_kernel,
        grid=(2,),
        in_specs=[pl.BlockSpec((None, 8, pos8.shape[2]), lambda s: (s, 0, 0)),
                  pl.BlockSpec((None, w1.shape[1], NSA_DH), lambda s: (s, 0, 0))],
        out_specs=pl.BlockSpec((None, 1, NSA_DH), lambda s: (s, 0, 0)),
        out_shape=jax.ShapeDtypeStruct((2, 1, NSA_DH), F32),
        name="nsa_posterm",
    )(pos8, w1)


def _eye(n, scale=1.0):
    rows = lax.broadcasted_iota(I32, (n, n), 0)
    cols = lax.broadcasted_iota(I32, (n, n), 1)
    return jnp.where(rows == cols, scale, 0.0).astype(BF16)


def _nsa_kernel(q_ref, qn_ref, kc_ref, vcin_ref, ksin_ref, vsin_ref, kwin_ref, vwin_ref, gate_ref,
                cb_ref, bt_ref, wb_ref, cov_ref, b31_ref, bcols_ref, o_ref,
                s_ref, ks_ref, vs_ref, kw_ref, vw_ref, vc_ref, qt_ref, *, n_sel):
    gi = pl.program_id(1)
    i = pl.program_id(2)
    q0 = i * QB
    r4 = NSA_HPG * QB
    ns = cov_ref.shape[0]
    eye_dh = _eye(NSA_DH)

    def group_lanes(x):
        return jnp.where(gi == 0, x[:, 0:NSA_DH], x[:, NSA_DH:2 * NSA_DH])

    def q_transposed(ref):
        eye_q = _eye(NSA_DH, NSA_DH ** -0.5)
        qb = ref[...]
        return jnp.concatenate([_dot_nt(eye_q, qb[:, h * NSA_DH:(h + 1) * NSA_DH]) for h in range(NSA_HPG)],
                               axis=1).astype(BF16)

    @pl.when(i == 0)
    def _():
        s_len = ksin_ref.shape[0]
        ones_rows = jnp.where(lax.broadcasted_iota(I32, (AUG_ROWS, SEL_TK), 0) == 0, 1.0, 0.0).astype(BF16)
        ones_rows_qb = jnp.where(lax.broadcasted_iota(I32, (AUG_ROWS, QB), 0) == 0, 1.0, 0.0).astype(BF16)
        tail = jnp.where(lax.broadcasted_iota(I32, (SEL_TK, NSA_DH), 1) < 2, 1.0, 0.0).astype(BF16)
        depth = ks_ref.shape[-1]
        vc_ref[...] = _dot_nt(eye_dh, vcin_ref[...]).astype(BF16)
        qt_ref[...] = q_transposed(q_ref)
        pad_flag = jnp.where(lax.broadcasted_iota(I32, (WINDOW, 2 * NSA_DH), 1) == NSA_DH, 1.0, 0.0)
        kw_ref[0:WINDOW, :] = pad_flag.astype(BF16)
        for u in range(WINDOW // QB):
            vw_ref[u] = jnp.zeros(vw_ref.shape[1:], BF16)

        def chunk(c, _):
            r0 = pl.multiple_of(c * SEL_TK, SEL_TK)
            rows = pl.ds(r0, SEL_TK)
            k_aug = jnp.concatenate([group_lanes(ksin_ref[rows, :]), tail], axis=1)
            ks_ref[rows, 0:ns] = bcols_ref[rows, :]
            ks_ref[rows, ns:ns + 2 * NSA_DH] = k_aug
            if depth > ns + 2 * NSA_DH:
                ks_ref[rows, ns + 2 * NSA_DH:depth] = jnp.zeros((SEL_TK, depth - ns - 2 * NSA_DH), BF16)
            v_t = _dot_nt(eye_dh, group_lanes(vsin_ref[rows, :])).astype(BF16)
            vs_ref[c] = jnp.concatenate([v_t, ones_rows], axis=0)
            kw_ref[pl.ds(WINDOW + r0, SEL_TK), :] = jnp.concatenate(
                [group_lanes(kwin_ref[rows, :]), jnp.zeros((SEL_TK, NSA_DH), BF16)], axis=1)
            vw_t = _dot_nt(eye_dh, group_lanes(vwin_ref[rows, :])).astype(BF16)
            for u in range(SEL_TK // QB):
                vw_ref[WINDOW // QB + c * (SEL_TK // QB) + u] = jnp.concatenate(
                    [vw_t[:, u * QB:(u + 1) * QB], ones_rows_qb], axis=0)
            return 0

        lax.fori_loop(0, s_len // SEL_TK, chunk, 0)

    q = qt_ref[...]

    step_rows = QB // CMP_STRIDE
    cb = cb_ref[pl.ds(pl.multiple_of(i * step_rows, step_rows), kc_ref.shape[0]), :]
    s_c = _dot(kc_ref[...], q) + cb
    m_c = jnp.maximum(jnp.max(s_c, axis=0, keepdims=True), 0.1 * NEG_INF)
    p_c = jnp.exp(s_c - m_c)
    l_c = jnp.sum(p_c, axis=0, keepdims=True)
    p_c = p_c / jnp.maximum(l_c, 1e-30)
    o_c = _dot(vc_ref[...], p_c.astype(BF16))

    psum = p_c[:, 0:QB] + p_c[:, QB:2 * QB] + p_c[:, 2 * QB:3 * QB] + p_c[:, 3 * QB:4 * QB]
    p_hi = psum.astype(BF16)
    p_lo = (psum - p_hi.astype(F32)).astype(BF16)
    cov = cov_ref[...]
    imp_t = _dot(cov, p_hi) + _dot(cov, p_lo)
    ns = imp_t.shape[0]

    wk = WINDOW + QB
    kwin = kw_ref[pl.ds(pl.multiple_of(q0, QB), wk), :]
    pad_mask_rows = jnp.where(lax.broadcasted_iota(I32, (NSA_DH, r4), 0) == 0, -MASK_BIG, 0.0).astype(BF16)
    s_w = _dot(kwin, jnp.concatenate([q, pad_mask_rows], axis=0)) + wb_ref[...]
    m_w = jnp.max(s_w, axis=0, keepdims=True)
    p_w = jnp.exp(s_w - m_w).astype(BF16)
    acc_w = jnp.zeros((NSA_DH + AUG_ROWS, r4), F32)
    for u in range(wk // QB):
        acc_w = acc_w + _dot(vw_ref[i + u], p_w[u * QB:(u + 1) * QB])
    o_w = acc_w[0:NSA_DH] / acc_w[NSA_DH:NSA_DH + 1]

    jidx = lax.broadcasted_iota(I32, (ns, QB), 0)
    t = q0 + lax.broadcasted_iota(I32, (ns, QB), 1)
    cur = jnp.right_shift(t, 6)
    eligible = jidx <= cur
    forced = (jidx == 0) | (jidx == cur) | (jidx == cur - 1)
    score = jnp.where(eligible, jnp.where(forced, -2.0, imp_t), -1.0)
    sel = jnp.where(forced, 1.0, 0.0)
    for _ in range(n_sel - 3):
        mx = jnp.max(score, axis=0, keepdims=True)
        first = jnp.min(jnp.where(score == mx, jidx, ns), axis=0, keepdims=True)
        pick = jidx == first
        sel = jnp.where(pick, 1.0, sel)
        score = jnp.where(pick, -2.0, score)
    notsel = jnp.where(eligible, 1.0 - sel, 1.0).astype(BF16)

    depth = ks_ref.shape[-1]
    pad_rows = depth - ns - NSA_DH - AUG_ROWS
    qa = jnp.concatenate([jnp.concatenate([notsel] * NSA_HPG, axis=1), q, b31_ref[...],
                          jnp.zeros((pad_rows, r4), BF16)], axis=0)
    sub = SEL_TK // QB
    n_tiles = i // sub + 1
    n_far = jnp.maximum(i - N_OFF + 2, 0) // sub

    def put_scores(jt, slot):
        k0 = pl.multiple_of(jt * SEL_TK, SEL_TK)
        s_ref[slot] = _dot(ks_ref[pl.ds(k0, SEL_TK), :], qa)

    def absorb(jt, slot, carry, near):
        m_prev, acc = carry
        s = s_ref[slot]
        if near:
            offs = [jnp.clip(i - sub * jt - u, 0, N_OFF - 1) for u in range(sub)]
            s = s + jnp.concatenate([bt_ref[o] for o in offs], axis=0)
        m_new = jnp.maximum(m_prev, jnp.max(s, axis=0, keepdims=True))
        alpha = jnp.exp(m_prev - m_new)
        p = jnp.exp(s - m_new).astype(BF16)
        return m_new, alpha * acc + _dot(vs_ref[jt], p)

    def pair_body(pp, carry, near):
        ja = 2 * pp
        put_scores(ja + 1, 1)
        carry = absorb(ja, 0, carry, near)
        put_scores(jnp.minimum(ja + 2, last_tile), 0)
        return absorb(ja + 1, 1, carry, near)

    n_pairs = (n_tiles + 1) // 2
    last_tile = 2 * n_pairs - 1
    far_pairs = n_far // 2
    put_scores(0, 0)
    carry = (jnp.full((1, r4), -1e38, F32), jnp.zeros((NSA_DH + AUG_ROWS, r4), F32))
    carry = lax.fori_loop(0, far_pairs, functools.partial(pair_body, near=False), carry)
    _, acc_s = lax.fori_loop(far_pairs, n_pairs, functools.partial(pair_body, near=True), carry)
    o_s = acc_s[0:NSA_DH] / acc_s[NSA_DH:NSA_DH + 1]

    gates = jax.nn.sigmoid(gate_ref[...])
    o = (gates[0:1] * o_c + gates[1:2] * o_s + gates[2:3] * o_w).astype(BF16)
    qt_ref[...] = q_transposed(qn_ref)
    eye_l = _eye(LANES)
    o_ref[...] = jnp.concatenate(
        [jnp.concatenate([_dot_nt(eye_l, o[:, h * QB + u * LANES:h * QB + (u + 1) * LANES])
                          for u in range(QB // LANES)], axis=0) for h in range(NSA_HPG)],
        axis=1).astype(o_ref.dtype)


def _nsa_attention(pb, c_qa, c_kv, kvc, gates, cb, bt, wb, cov_t, b31, blockcols, b, s, n_sel):
    g, dh = NSA_GROUPS, NSA_DH
    r4 = NSA_HPG * QB
    nqb = s // QB
    nc = kvc.shape[3]
    ns = cov_t.shape[0]
    depth = -(-(ns + 2 * dh) // LANES) * LANES
    n_tiles = s // SEL_TK
    assert n_tiles % 2 == 0, "the selected sweep consumes key tiles in pairs"
    qw = NSA_HPG * dh
    once = pl.Buffered(1)
    slab = lambda k: pl.BlockSpec((s, 2 * dh), lambda bi, gi, i: (bi, c_kv // (2 * dh) + k), pipeline_mode=once)
    per_g = lambda a: pl.BlockSpec((None,) + a.shape[1:], lambda bi, gi, i: (gi,) + (0,) * (a.ndim - 1),
                                   pipeline_mode=once)
    const = lambda a: pl.BlockSpec(a.shape, lambda bi, gi, i: (0,) * a.ndim, pipeline_mode=once)
    return pl.pallas_call(
        functools.partial(_nsa_kernel, n_sel=n_sel),
        grid=(b, g, nqb),
        in_specs=[
            pl.BlockSpec((QB, qw), lambda bi, gi, i: (bi * nqb + i, c_qa // qw + gi)),
            pl.BlockSpec((QB, qw), lambda bi, gi, i: (bi * nqb + jnp.minimum(i + 1, nqb - 1), c_qa // qw + gi)),
            pl.BlockSpec((None, None, None, nc, dh), lambda bi, gi, i: (0, bi, gi, 0, 0)),
            pl.BlockSpec((None, None, None, nc, dh), lambda bi, gi, i: (1, bi, gi, 0, 0)),
            slab(2), slab(3), slab(4), slab(5),
            pl.BlockSpec((None, None, None, 3, r4), lambda bi, gi, i: (bi, gi, i, 0, 0)),
            per_g(cb), per_g(bt), per_g(wb), const(cov_t), per_g(b31), const(blockcols),
        ],
        out_specs=pl.BlockSpec((QB, qw), lambda bi, gi, i: (bi * nqb + i, gi)),
        out_shape=jax.ShapeDtypeStruct((b * s, g * qw), BF16),
        scratch_shapes=[pltpu.VMEM((2, SEL_TK, r4), F32),
                        pltpu.VMEM((s, depth), BF16),
                        pltpu.VMEM((n_tiles, dh + AUG_ROWS, SEL_TK), BF16),
                        pltpu.VMEM((s + WINDOW, 2 * dh), BF16),
                        pltpu.VMEM(((s + WINDOW) // QB, dh + AUG_ROWS, QB), BF16),
                        pltpu.VMEM((dh, nc), BF16),
                        pltpu.VMEM((dh, r4), BF16)],
        compiler_params=_cparams(("arbitrary",) * 3),
        name="nsa_attention",
    )(pb, pb, kvc, kvc, pb, pb, pb, pb, gates, cb, bt, wb, cov_t, b31, blockcols)


def _ret_kernel(q_ref, k_ref, v_ref, g_ref, cos_ref, sin_ref, dm_ref, qd_ref, kd_ref, cd_ref,
                o_ref, state_ref):
    n = pl.program_id(1)

    @pl.when(n == 0)
    def _():
        state_ref[...] = jnp.zeros_like(state_ref)

    lane = lax.broadcasted_iota(I32, (RET_CHUNK, LANES), 1)
    first_half = (lane & (RET_DK - 1)) < (RET_DK // 2)

    def rot(x, cosv, sinv):
        partner = jnp.where(first_half, pltpu.roll(x, LANES - RET_DK // 2, axis=1),
                            pltpu.roll(x, RET_DK // 2, axis=1))
        return x * cosv + partner * sinv

    states = [state_ref[h] for h in range(RET_HEADS)]
    for cc in range(RET_STEP):
        rows = slice(cc * RET_CHUNK, (cc + 1) * RET_CHUNK)
        cosv = cos_ref[rows, :]
        sinv = sin_ref[rows, :]
        for hp in range(RET_HEADS // 2):
            ls = slice(hp * LANES, (hp + 1) * LANES)
            q2 = rot(q_ref[rows, ls], cosv, sinv)
            k2 = rot(k_ref[rows, ls], cosv, sinv) * (RET_DK ** -0.5)
            qdec = (q2 * qd_ref[:, ls]).astype(BF16)
            kdec_t = (k2 * kd_ref[:, ls]).T.astype(BF16)
            q2b = q2.astype(BF16)
            k2b = k2.astype(BF16)
            for e in range(2):
                h = 2 * hp + e
                hs = slice(e * RET_DK, (e + 1) * RET_DK)
                vs = slice(h * RET_DV, (h + 1) * RET_DV)
                v = v_ref[rows, vs]
                st = states[h]
                sc = _dot_nt(q2b[:, hs], k2b[:, hs]) * dm_ref[h]
                y = _dot(sc.astype(BF16), v) + _dot(qdec[:, hs], st.astype(BF16))
                states[h] = cd_ref[h] * st + _dot(kdec_t[hs, :], v)
                mu = jnp.mean(y, axis=-1, keepdims=True)
                var = jnp.mean(jnp.square(y - mu), axis=-1, keepdims=True)
                yn = (y - mu) * lax.rsqrt(var + GN_EPS)
                o_ref[rows, vs] = (jax.nn.silu(g_ref[rows, vs]) * yn).astype(o_ref.dtype)
    for h in range(RET_HEADS):
        state_ref[h] = states[h]


def _retention(pf, pb, b, s, col_q, col_k, col_v, col_g, tabs):
    cosf, sinf, dm, qd, kd, cd = tabs
    c = RET_CHUNK * RET_STEP
    n = s // c
    t = b * s
    wqk = RET_HEADS * RET_DK
    wv = RET_HEADS * RET_DV
    row = lambda bi, ni: bi * n + ni
    full = lambda a: pl.BlockSpec(a.shape, lambda bi, ni: (0,) * a.ndim)
    return pl.pallas_call(
        _ret_kernel,
        grid=(b, n),
        in_specs=[
            pl.BlockSpec((c, wqk), lambda bi, ni: (row(bi, ni), col_q // wqk)),
            pl.BlockSpec((c, wqk), lambda bi, ni: (row(bi, ni), col_k // wqk)),
            pl.BlockSpec((c, wv), lambda bi, ni: (row(bi, ni), col_v // wv)),
            pl.BlockSpec((c, wv), lambda bi, ni: (row(bi, ni), col_g // wv)),
            pl.BlockSpec((c, LANES), lambda bi, ni: (ni, 0)),
            pl.BlockSpec((c, LANES), lambda bi, ni: (ni, 0)),
            full(dm), full(qd), full(kd), full(cd),
        ],
        out_specs=pl.BlockSpec((c, wv), lambda bi, ni: (row(bi, ni), 0)),
        out_shape=jax.ShapeDtypeStruct((t, wv), BF16),
        scratch_shapes=[pltpu.VMEM((RET_HEADS, RET_DK, RET_DV), F32)],
        compiler_params=_cparams(("arbitrary",) * 2),
        name="retention",
    )(pf, pf, pb, pf, cosf, sinf, dm, qd, kd, cd)


def _retention_tables(s):
    half = RET_DK // 2
    inv = ROPE_BASE ** (-jnp.arange(half, dtype=F32) * 2.0 / RET_DK)
    ang = jnp.arange(s, dtype=F32)[:, None] * inv[None, :]
    cos, sin = jnp.cos(ang), jnp.sin(ang)
    cosf = jnp.concatenate([cos, cos, cos, cos], axis=1)
    sinf = jnp.concatenate([-sin, sin, -sin, sin], axis=1)
    lg = jnp.log(1.0 - 2.0 ** (-5.0 - jnp.arange(RET_HEADS, dtype=F32)))
    i = jnp.arange(RET_CHUNK, dtype=F32)
    diff = i[:, None] - i[None, :]
    dm = jnp.where(diff >= 0, jnp.exp(jnp.maximum(diff, 0.0)[None] * lg[:, None, None]), 0.0)
    kdec = jnp.exp((RET_CHUNK - 1.0 - i)[None, :] * lg[:, None])
    qdec = jnp.exp((i + 1.0)[None, :] * lg[:, None])
    expand = lambda a: jnp.repeat(a.T, RET_DK, axis=1)
    cd = jnp.broadcast_to(jnp.exp(RET_CHUNK * lg)[:, None, None], (RET_HEADS, 1, LANES))
    return cosf, sinf, dm, expand(qdec), expand(kdec), cd


def _layer_norm(z, g, b):
    mu = jnp.mean(z, axis=-1, keepdims=True)
    var = jnp.mean(jnp.square(z - mu), axis=-1, keepdims=True)
    return (z - mu) * lax.rsqrt(var + LN_EPS) * g + b


def _merge_kernel(ya_ref, yb_ref, ma_ref, mb_ref, x_ref, wa_ref, wb_ref, wo_ref, g_ref, b_ref,
                  wr_ref, rb_ref, tri_ref, o_ref, eidx_ref, wts_ref, rank_ref, cnt_ref, carry_ref, *, alpha):
    merged = (jax.nn.sigmoid(ma_ref[...]) * _dot(ya_ref[...], wa_ref[...])
              + jax.nn.sigmoid(mb_ref[...]) * _dot(yb_ref[...], wb_ref[...]))
    z = alpha * x_ref[...] + _dot(merged.astype(BF16), wo_ref[...])
    x1 = _layer_norm(z, g_ref[...], b_ref[...])
    o_ref[...] = x1
    _route(x1, wr_ref, rb_ref, tri_ref, eidx_ref, wts_ref, rank_ref, cnt_ref, carry_ref)


def _merge(ya, yb, pf, col_ma, col_mb, x, wa, wb, wo, g, bb, wr_t, rb_col, tri, alpha):
    t, d = x.shape
    tm = tri.shape[0]
    full = lambda a: pl.BlockSpec(a.shape, lambda i: (0,) * a.ndim)
    out2 = lambda dt: jax.ShapeDtypeStruct((2, t), dt)
    row2 = pl.BlockSpec((2, tm), lambda i: (0, i))
    return pl.pallas_call(
        functools.partial(_merge_kernel, alpha=alpha),
        grid=(t // tm,),
        in_specs=[pl.BlockSpec((tm, ya.shape[1]), lambda i: (i, 0)),
                  pl.BlockSpec((tm, yb.shape[1]), lambda i: (i, 0)),
                  pl.BlockSpec((tm, d), lambda i: (i, col_ma // d)),
                  pl.BlockSpec((tm, d), lambda i: (i, col_mb // d)),
                  pl.BlockSpec((tm, d), lambda i: (i, 0)),
                  full(wa), full(wb), full(wo), full(g), full(bb), full(wr_t), full(rb_col), full(tri)],
        out_specs=[pl.BlockSpec((tm, d), lambda i: (i, 0)), row2, row2, row2,
                   pl.BlockSpec((N_EXPERTS, LANES), lambda i: (0, 0))],
        out_shape=[jax.ShapeDtypeStruct((t, d), F32), out2(I32), out2(F32), out2(I32),
                   jax.ShapeDtypeStruct((N_EXPERTS, LANES), F32)],
        scratch_shapes=[pltpu.VMEM((N_EXPERTS, LANES), F32)],
        compiler_params=_cparams(("arbitrary",)),
        name="merge_ln1_router",
    )(ya, yb, pf, pf, x, wa, wb, wo, g, bb, wr_t, rb_col, tri)


def _route(x, wr_ref, rb_ref, tri_ref, eidx_ref, wts_ref, rank_ref, cnt_ref, carry_ref):
    step = pl.program_id(0)

    @pl.when(step == 0)
    def _():
        carry_ref[...] = jnp.zeros_like(carry_ref)

    wr = wr_ref[...]
    w_hi = wr.astype(BF16)
    w_lo = (wr - w_hi.astype(F32)).astype(BF16)
    x_hi = x.astype(BF16)
    x_lo = (x - x_hi.astype(F32)).astype(BF16)
    logits = _dot_nt(w_hi, x_hi) + _dot_nt(w_hi, x_lo) + _dot_nt(w_lo, x_hi)
    tm = logits.shape[1]
    mx = jnp.max(logits, axis=0, keepdims=True)
    ex = jnp.exp(logits - mx)
    probs = ex / jnp.sum(ex, axis=0, keepdims=True)
    sel = probs + rb_ref[...]
    rows = [sel[e:e + 1] for e in range(N_EXPERTS)]
    gscore = []
    for gi in range(N_GROUPS):
        r = rows[gi * EPG:(gi + 1) * EPG]
        best2 = None
        for a in range(EPG):
            for c in range(a + 1, EPG):
                pair = r[a] + r[c]
                best2 = pair if best2 is None else jnp.maximum(best2, pair)
        gscore.append(best2)
    best = jnp.zeros((1, tm), I32)
    bscore = gscore[0]
    for gi in range(1, N_GROUPS):
        better = gscore[gi] > bscore
        best = jnp.where(better, gi, best)
        bscore = jnp.where(better, gscore[gi], bscore)
    ing = []
    for a in range(EPG):
        v = rows[a]
        for gi in range(1, N_GROUPS):
            v = jnp.where(best == gi, rows[gi * EPG + a], v)
        ing.append(v)
    i1 = jnp.zeros((1, tm), I32)
    v1 = ing[0]
    for a in range(1, EPG):
        better = ing[a] > v1
        i1 = jnp.where(better, a, i1)
        v1 = jnp.where(better, ing[a], v1)
    i2 = jnp.full((1, tm), -1, I32)
    v2 = jnp.full((1, tm), -jnp.inf, F32)
    for a in range(EPG):
        better = (i1 != a) & (ing[a] > v2)
        i2 = jnp.where(better, a, i2)
        v2 = jnp.where(better, ing[a], v2)
    e0 = best * EPG + i1
    e1 = best * EPG + i2
    eio = lax.broadcasted_iota(I32, (N_EXPERTS, tm), 0)
    oh0 = (eio == e0).astype(F32)
    oh1 = (eio == e1).astype(F32)
    p0 = jnp.sum(oh0 * probs, axis=0, keepdims=True)
    p1 = jnp.sum(oh1 * probs, axis=0, keepdims=True)
    den = p0 + p1
    eidx_ref[0:1, :] = e0
    eidx_ref[1:2, :] = e1
    wts_ref[0:1, :] = p0 / den
    wts_ref[1:2, :] = p1 / den
    tot = oh0 + oh1
    prefix = _dot(tot.astype(BF16), tri_ref[...]) + carry_ref[:, 0:1]
    rank_ref[0:1, :] = jnp.sum(oh0 * prefix, axis=0, keepdims=True).astype(I32)
    rank_ref[1:2, :] = jnp.sum(oh1 * prefix, axis=0, keepdims=True).astype(I32)
    carry_ref[...] = carry_ref[...] + jnp.sum(tot, axis=1, keepdims=True)
    cnt_ref[...] = carry_ref[...]


def _dispatch_kernel(dest_ref, zflag_ref, x_ref, xs_ref, zero_ref, sem, zsem, rows_ref, *, t_total):
    step = pl.program_id(0)
    tm = x_ref.shape[0]

    @pl.when(step == 0)
    def _():
        zero_ref[...] = jnp.zeros_like(zero_ref)

        def zero_block(bi, _):
            @pl.when(zflag_ref[bi] == 1)
            def _():
                r0 = pl.multiple_of(bi * MOE_BM, MOE_BM)
                cp = pltpu.make_async_copy(zero_ref, xs_ref.at[pl.ds(r0, MOE_BM)], zsem)
                cp.start()
                cp.wait()
            return 0

        lax.fori_loop(0, zflag_ref.shape[0], zero_block, 0)

    rows_ref[...] = x_ref[...].reshape(rows_ref.shape)

    def copy(r, slot):
        dst = dest_ref[slot * t_total + step * tm + r]
        return pltpu.make_async_copy(rows_ref.at[r], xs_ref.at[dst], sem)

    def issue(r, _):
        copy(r, 0).start()
        copy(r, 1).start()
        return 0

    def drain(r, _):
        copy(r, 0).wait()
        copy(r, 1).wait()
        return 0

    lax.fori_loop(0, tm, issue, 0, unroll=ROW_DMA_UNROLL)
    lax.fori_loop(0, tm, drain, 0, unroll=ROW_DMA_UNROLL)


def _dispatch(dest_flat, zflag, x, rows, tm):
    t, d = x.shape
    sub = d // LANES
    return pl.pallas_call(
        functools.partial(_dispatch_kernel, t_total=t),
        grid_spec=pltpu.PrefetchScalarGridSpec(
            num_scalar_prefetch=2,
            grid=(t // tm,),
            in_specs=[pl.BlockSpec((tm, d), lambda i, dr, zf: (i, 0))],
            out_specs=pl.BlockSpec(memory_space=pl.ANY),
            scratch_shapes=[pltpu.VMEM((MOE_BM, sub, LANES), x.dtype), pltpu.SemaphoreType.DMA,
                            pltpu.SemaphoreType.DMA, pltpu.VMEM((tm, sub, LANES), x.dtype)],
        ),
        out_shape=jax.ShapeDtypeStruct((rows, sub, LANES), x.dtype),
        compiler_params=pltpu.CompilerParams(dimension_semantics=("arbitrary",), has_side_effects=True,
                                             vmem_limit_bytes=VMEM_LIMIT),
        name="moe_dispatch",
    )(dest_flat, zflag, x)


def _expert_kernel(be_ref, na_ref, x_ref, wg_ref, wu_ref, wd_ref, o_ref, wg_s, wu_s, wd_s):
    i = pl.program_id(0)
    active = i < na_ref[0]

    @pl.when((i == 0) | (be_ref[i] != be_ref[jnp.maximum(i - 1, 0)]))
    def _():
        wg_s[...] = wg_ref[...].astype(BF16)
        wu_s[...] = wu_ref[...].astype(BF16)
        wd_s[...] = wd_ref[...].astype(BF16)

    @pl.when(active)
    def _():
        xb = x_ref[...].reshape(x_ref.shape[0], wg_s.shape[0]).astype(BF16)
        gate = _dot(xb, wg_s[...])
        up = _dot(xb, wu_s[...])
        hid = (jax.nn.silu(gate) * up).astype(BF16)
        o_ref[...] = _dot(hid, wd_s[...]).reshape(o_ref.shape)

    @pl.when(jnp.logical_not(active))
    def _():
        o_ref[...] = jnp.zeros_like(o_ref)


def _experts(blk_e, n_active, xs, wg, wu, wd, layer):
    rows = xs.shape[0]
    nb = rows // MOE_BM
    act = lambda i, be, na: jnp.minimum(i, na[0] - 1)
    wspec = lambda a: pl.BlockSpec((None, None) + a.shape[2:], lambda i, be, na: (layer, be[i], 0, 0))
    return pl.pallas_call(
        _expert_kernel,
        grid_spec=pltpu.PrefetchScalarGridSpec(
            num_scalar_prefetch=2,
            grid=(nb,),
            in_specs=[pl.BlockSpec((MOE_BM,) + xs.shape[1:], lambda i, be, na: (act(i, be, na), 0, 0)),
                      wspec(wg), wspec(wu), wspec(wd)],
            out_specs=pl.BlockSpec((MOE_BM,) + xs.shape[1:], lambda i, be, na: (i, 0, 0)),
            scratch_shapes=[pltpu.VMEM(wg.shape[2:], BF16), pltpu.VMEM(wu.shape[2:], BF16),
                            pltpu.VMEM(wd.shape[2:], BF16)],
        ),
        out_shape=jax.ShapeDtypeStruct(xs.shape, F32),
        compiler_params=_cparams(("arbitrary",)),
        name="moe_experts",
    )(blk_e, n_active, xs, wg, wu, wd)


def _combine_kernel(dest_ref, x_ref, w_ref, g_ref, b_ref, ys_ref, o_ref, ob_ref, buf_ref, sem, *, alpha, t_total):
    step = pl.program_id(0)
    n_steps = pl.num_programs(0)
    tm = x_ref.shape[0]

    def copy(tile, r, slot):
        src = dest_ref[slot * t_total + tile * tm + r]
        half = tile % 2
        return pltpu.make_async_copy(ys_ref.at[src], buf_ref.at[half, slot, r], sem.at[half])

    def issue_tile(tile):
        def body(r, _):
            copy(tile, r, 0).start()
            copy(tile, r, 1).start()
            return 0
        lax.fori_loop(0, tm, body, 0, unroll=ROW_DMA_UNROLL)

    @pl.when(step == 0)
    def _():
        issue_tile(0)

    @pl.when(step + 1 < n_steps)
    def _():
        issue_tile(step + 1)

    def drain(r, _):
        copy(step, r, 0).wait()
        copy(step, r, 1).wait()
        return 0

    lax.fori_loop(0, tm, drain, 0, unroll=ROW_DMA_UNROLL)
    w = w_ref[...]
    cur = step % 2
    y0 = buf_ref[cur, 0].reshape(x_ref.shape)
    y1 = buf_ref[cur, 1].reshape(x_ref.shape)
    z = alpha * x_ref[...] + w[:, 0:1] * y0 + w[:, 1:2] * y1
    y = _layer_norm(z, g_ref[...], b_ref[...])
    o_ref[...] = y
    ob_ref[...] = y.astype(BF16)


def _combine(dest_flat, x, wts_t, g, bb, ys, alpha, tm):
    t, d = x.shape
    return pl.pallas_call(
        functools.partial(_combine_kernel, alpha=alpha, t_total=t),
        grid_spec=pltpu.PrefetchScalarGridSpec(
            num_scalar_prefetch=1,
            grid=(t // tm,),
            in_specs=[pl.BlockSpec((tm, d), lambda i, dr: (i, 0)),
                      pl.BlockSpec((tm, 2), lambda i, dr: (i, 0)),
                      pl.BlockSpec(g.shape, lambda i, dr: (0, 0)),
                      pl.BlockSpec(bb.shape, lambda i, dr: (0, 0)),
                      pl.BlockSpec(memory_space=pl.ANY)],
            out_specs=[pl.BlockSpec((tm, d), lambda i, dr: (i, 0)), pl.BlockSpec((tm, d), lambda i, dr: (i, 0))],
            scratch_shapes=[pltpu.VMEM((2, 2, tm, d // LANES, LANES), F32), pltpu.SemaphoreType.DMA((2,))],
        ),
        out_shape=[jax.ShapeDtypeStruct((t, d), F32), jax.ShapeDtypeStruct((t, d), BF16)],
        compiler_params=_cparams(("arbitrary",)),
        name="moe_combine_ln2",
    )(dest_flat, x, wts_t, g, bb, ys)


def _attention_tables(rel_bias, s):
    nc = s // CMP_STRIDE
    nqb = s // QB
    iq = jnp.arange(QB, dtype=I32)[None, None, :]
    x = jnp.arange(nc + (QB // CMP_STRIDE) * nqb, dtype=I32)[:, None]
    rel_c = iq[0] + CMP_STRIDE * (x - nc + 1) - (CMP_LEN - 1)
    jk = jnp.arange(QB, dtype=I32)[None, :, None]
    rel_t = jnp.arange(N_OFF, dtype=I32)[:, None, None] * QB + iq - jk
    jw = jnp.arange(WINDOW + QB, dtype=I32)[:, None]
    rel_w = iq[0] - jw + WINDOW
    rel_w = jnp.where(rel_w < WINDOW, rel_w, -1)
    flat = rel_bias.reshape(-1)
    r4 = NSA_HPG * QB
    cb = _bias_table(flat, jnp.maximum(rel_c, -1), rel_c.shape[0])
    bt = _bias_table(flat, jnp.maximum(rel_t, -1).reshape(N_OFF * QB, QB), QB, delta=True)
    bt = bt.reshape(NSA_GROUPS, N_OFF, QB, r4)
    wb = _bias_table(flat, jnp.maximum(rel_w, -1), WINDOW + QB)
    sat = jnp.repeat(rel_bias[REL_BUCKETS - 1].reshape(NSA_GROUPS, 1, NSA_HPG), QB, axis=2)
    hi = sat.astype(BF16)
    lo = (sat - hi.astype(F32)).astype(BF16)
    b31 = jnp.concatenate([hi, lo, jnp.zeros((NSA_GROUPS, AUG_ROWS - 2, r4), BF16)], axis=1)
    return cb, bt, wb, b31


def _selection_tables(s):
    nc = s // CMP_STRIDE
    ns = s // SLC_BLOCK
    c_start = np.arange(nc) * CMP_STRIDE
    s_start = np.arange(ns) * SLC_BLOCK
    cover = np.maximum(np.minimum(c_start[:, None] + CMP_LEN, s_start[None, :] + SLC_BLOCK)
                       - np.maximum(c_start[:, None], s_start[None, :]), 0).astype(np.float32) / CMP_LEN
    cover[nc - 1] = 0.0
    cover_rev_t = np.ascontiguousarray(cover[::-1].T)
    blockcols = np.where(np.arange(s)[:, None] // SLC_BLOCK == np.arange(ns)[None, :], -MASK_BIG, 0.0)
    return jnp.asarray(cover_rev_t, BF16), jnp.asarray(blockcols, BF16)


def _split_in_weights(w_in_l):
    o = np.cumsum((0, 512, 128, 128, 128, 128, 128, 128, 24, 512, 512, 1024, 1024, 1024, 1024))
    seg = lambda a: w_in_l[:, o[a]:o[a + 1]]
    wb = jnp.concatenate([seg(10), seg(0), seg(1), seg(2), seg(3), seg(4), seg(5), seg(6)], axis=1)
    gpad = jnp.pad(seg(7), ((0, 0), (0, LANES - 24)))
    wf = jnp.concatenate([seg(8), seg(9), seg(11), seg(12), seg(13), gpad], axis=1)
    return wb.astype(BF16), wf.astype(BF16)


def kernel(x, w_in, cmp_pos, cmp_w1, cmp_w2, w_br_a, w_br_b, w_o, ln1_g, ln1_b, w_router, router_bias,
           moe_w_gate, moe_w_up, moe_w_down, ln2_g, ln2_b, rel_bias):
    b, s, d = x.shape
    depth = w_in.shape[0]
    t = b * s
    alpha = (2 * depth) ** 0.25
    g, r, dh = NSA_GROUPS, NSA_HPG, NSA_DH
    nqb = s // QB
    nb16 = s // CMP_STRIDE
    n_sel = min(N_SELECT, s // SLC_BLOCK)

    cb, bt, wb, b31 = _attention_tables(rel_bias, s)
    cov_t, blockcols = _selection_tables(s)
    ret_tabs = _retention_tables(s)
    tri = jnp.asarray(np.triu(np.ones((512, 512), np.float32), 1), BF16)
    wr_t = w_router.T
    rb_col = router_bias.reshape(N_EXPERTS, 1)
    n_blk = t * 2 // MOE_BM + N_EXPERTS
    rows = n_blk * MOE_BM

    xf = x.reshape(t, d)
    xb = xf.astype(BF16)
    for l in range(depth):
        w_b, w_f = _split_in_weights(w_in[l])
        pb =_matmul(xb, w_b, BF16, 1024, 1152)
        pf = _matmul(xb, w_f, F32, 1024, 1408)
        c_qa = RET_HEADS * RET_DV
        c_kv = c_qa + NSA_HEADS * dh

        def kv_heads(slab):
            col = c_kv + slab * g * dh
            return pb[:, col:col + g * dh].reshape(b, s, g, dh).transpose(0, 2, 1, 3)
        x16 = jnp.stack([kv_heads(0), kv_heads(1)]).reshape(2, b, g, nb16, CMP_STRIDE * dh)
        w1 = cmp_w1[l].astype(BF16)
        half = CMP_STRIDE * dh
        w1ab = jnp.concatenate([w1[:, :half], w1[:, half:]], axis=2)
        pos8 = jnp.broadcast_to(cmp_pos[l].reshape(2, 1, CMP_LEN * dh), (2, 8, CMP_LEN * dh)).astype(BF16)
        kvc = _compress(x16, w1ab, _posterm(pos8, w1), cmp_w2[l].astype(BF16))
        gl = (pf[:, 4096:4096 + 24].reshape(b, nqb, QB, g, r, 3).transpose(0, 3, 1, 5, 4, 2)
              .reshape(b, g, nqb, 3, r * QB))
        ya = _nsa_attention(pb, c_qa, c_kv, kvc, gl, cb, bt, wb, cov_t, b31, blockcols, b, s, n_sel)

        yb = _retention(pf, pb, b, s, 0, 512, 0, 1024, ret_tabs)

        x1, eidx, wts, rank, cnt = _merge(ya, yb, pf, 2048, 3072, xf, w_br_a[l].astype(BF16),
                                          w_br_b[l].astype(BF16), w_o[l].astype(BF16), ln1_g[l].reshape(1, d),
                                          ln1_b[l].reshape(1, d), wr_t, rb_col, tri, alpha)

        counts = cnt[:, 0].astype(I32)
        padded = ((counts + MOE_BM - 1) // MOE_BM) * MOE_BM
        pad_end = jnp.cumsum(padded)
        pad_start = pad_end - padded
        e_iota = jnp.arange(N_EXPERTS, dtype=I32)[:, None, None]
        start_of = jnp.sum(jnp.where(eidx[None] == e_iota, pad_start[:, None, None], 0), axis=0)
        dest = (start_of + rank).reshape(-1)
        n_active = (pad_end[-1] // MOE_BM).astype(I32).reshape(1)
        blk = jnp.minimum(jnp.arange(n_blk, dtype=I32), n_active[0] - 1) * MOE_BM
        blk_e = jnp.minimum(jnp.sum(pad_end[None, :] <= blk[:, None], axis=1), N_EXPERTS - 1).astype(I32)
        blk_all = jnp.arange(n_blk, dtype=I32)
        last_of_expert = jnp.any(pad_end[None, :] == (blk_all[:, None] + 1) * MOE_BM, axis=1)
        zflag = (last_of_expert | (blk_all >= n_active[0])).astype(I32)
        xs = _dispatch(dest, zflag, x1, rows, 512)
        ys = _experts(blk_e, n_active, xs, moe_w_gate, moe_w_up, moe_w_down, l)
        xf, xb = _combine(dest, x1, wts.T, ln2_g[l].reshape(1, d), ln2_b[l].reshape(1, d), ys, alpha, 256)
    return xf.reshape(b, s, d)
```

```python
import functools
import math

import jax
import jax.numpy as jnp
import numpy as np
from jax import lax
from jax.experimental import pallas as pl
from jax.experimental.pallas import tpu as pltpu

F32 = jnp.float32
BF16 = jnp.bfloat16
I32 = jnp.int32

NSA_HEADS = 8
NSA_GROUPS = 2
NSA_HPG = 4
NSA_DH = 64
CMP_LEN = 32
CMP_STRIDE = 16
SLC_BLOCK = 64
N_SELECT = 16
WINDOW = 512
QB = 256
FORCE_SCORE = 1.0e4
RET_HEADS = 8
RET_DK = 64
RET_DV = 128
RET_CHUNK = 128
ROPE_BASE = 10000.0
REL_BUCKETS = 32
REL_MAX_EXACT = 16
REL_MAX_DIST = 2048
N_EXPERTS = 16
N_GROUPS = 4
EPG = 4
D_EXPERT = 512
LN_EPS = 1e-5
GN_EPS = 1e-5
NEG_INF = -1e30

LANES = 128
VMEM_LIMIT = 56 * 1024 * 1024

SEL_TK = 512
AUG_ROWS = 16
MASK_BIG = 2.0 ** 100
SAT_REL = math.ceil(REL_MAX_EXACT * (REL_MAX_DIST / REL_MAX_EXACT)
                    ** ((REL_BUCKETS - 1 - REL_MAX_EXACT) / (REL_BUCKETS - REL_MAX_EXACT)))
N_OFF = -(-(SAT_REL + QB - 1) // QB) + 1
RET_STEP = 2
MOE_BM = 512
ROW_DMA_UNROLL = 8


def _cparams(sem):
    return pltpu.CompilerParams(dimension_semantics=sem, vmem_limit_bytes=VMEM_LIMIT)


def _dot(a, b):
    return jnp.dot(a, b, preferred_element_type=F32)


def _dot_nt(a, b):
    return lax.dot_general(a, b, (((1,), (1,)), ((), ())), preferred_element_type=F32)


def _mm_kernel(a_ref, b_ref, o_ref):
    o_ref[...] = _dot(a_ref[...], b_ref[...]).astype(o_ref.dtype)


def _matmul(a, b, out_dtype, tm, tn):
    m, k = a.shape
    n = b.shape[1]
    return pl.pallas_call(
        _mm_kernel,
        grid=(n // tn, m // tm),
        in_specs=[pl.BlockSpec((tm, k), lambda j, i: (i, 0)),
                  pl.BlockSpec((k, tn), lambda j, i: (0, j))],
        out_specs=pl.BlockSpec((tm, tn), lambda j, i: (i, j)),
        out_shape=jax.ShapeDtypeStruct((m, n), out_dtype),
        compiler_params=_cparams(("arbitrary", "arbitrary")),
        name="proj_matmul",
    )(a, b)


def _bias_kernel(tab_ref, rel_ref, o_ref, *, delta):
    rel = rel_ref[...]
    n = jnp.maximum(rel, 0)
    nf = jnp.maximum(n, 1).astype(F32)
    large = REL_MAX_EXACT + (jnp.log(nf / REL_MAX_EXACT) / math.log(REL_MAX_DIST / REL_MAX_EXACT)
                             * (REL_BUCKETS - REL_MAX_EXACT)).astype(I32)
    large = jnp.minimum(large, REL_BUCKETS - 1)
    bucket = jnp.where(n < REL_MAX_EXACT, n, large)
    eqs = [bucket == k for k in range(REL_BUCKETS)]
    for h in range(NSA_HEADS):
        acc = jnp.full(rel.shape, NEG_INF, F32)
        for k in range(REL_BUCKETS):
            acc = jnp.where(eqs[k], tab_ref[k * NSA_HEADS + h], acc)
        if delta:
            acc = acc - tab_ref[(REL_BUCKETS - 1) * NSA_HEADS + h]
        hh = h % NSA_HPG
        o_ref[h // NSA_HPG, :, hh * QB:(hh + 1) * QB] = jnp.where(rel >= 0, acc, NEG_INF)


def _bias_table(rel_bias_flat, rel, tr, delta=False):
    r, c = rel.shape
    return pl.pallas_call(
        functools.partial(_bias_kernel, delta=delta),
        grid_spec=pltpu.PrefetchScalarGridSpec(
            num_scalar_prefetch=1,
            grid=(r // tr,),
            in_specs=[pl.BlockSpec((tr, c), lambda i, tab: (i, 0))],
            out_specs=pl.BlockSpec((NSA_GROUPS, tr, NSA_HPG * c), lambda i, tab: (0, i, 0)),
        ),
        out_shape=jax.ShapeDtypeStruct((NSA_GROUPS, r, NSA_HPG * c), F32),
        compiler_params=_cparams(("arbitrary",)),
        name="t5_bias_table",
    )(rel_bias_flat, rel)


def _compress_kernel(x_ref, w1_ref, pos_ref, w2_ref, o_ref):
    ab = _dot(x_ref[...], w1_ref[...])
    nb = ab.shape[0]
    nxt = pltpu.roll(ab, nb - 1, axis=0)
    nxt = pltpu.roll(nxt, NSA_DH, axis=1)
    pre = (ab + nxt)[:, :NSA_DH] + pos_ref[...]
    hid = jax.nn.gelu(pre)
    out = _dot(hid.astype(BF16), w2_ref[...]).astype(BF16)
    rows = lax.broadcasted_iota(I32, (nb, nb), 0)
    cols = lax.broadcasted_iota(I32, (nb, nb), 1)
    flip = jnp.where(rows + cols == nb - 1, 1.0, 0.0).astype(BF16)
    o_ref[...] = _dot(flip, out).astype(o_ref.dtype)


def _compress(x16, w1ab, posterm, w2):
    _, b, g, nb, wd = x16.shape
    return pl.pallas_call(
        _compress_kernel,
        grid=(2, b, g),
        in_specs=[pl.BlockSpec((None, None, None, nb, wd), lambda s, bi, gi: (s, bi, gi, 0, 0)),
                  pl.BlockSpec((None, wd, 2 * NSA_DH), lambda s, bi, gi: (s, 0, 0)),
                  pl.BlockSpec((None, 1, NSA_DH), lambda s, bi, gi: (s, 0, 0)),
                  pl.BlockSpec((None, NSA_DH, NSA_DH), lambda s, bi, gi: (s, 0, 0))],
        out_specs=pl.BlockSpec((None, None, None, nb, NSA_DH), lambda s, bi, gi: (s, bi, gi, 0, 0)),
        out_shape=jax.ShapeDtypeStruct((2, b, g, nb, NSA_DH), BF16),
        compiler_params=_cparams(("arbitrary",) * 3),
        name="nsa_compress",
    )(x16, w1ab, posterm, w2)


def _posterm_kernel(p_ref, w_ref, o_ref):
    o_ref[...] = _dot(p_ref[...], w_ref[...])[0:1]


def _posterm(pos8, w1):
    return pl.pallas_call(
        _posterm_kernel,
        grid=(2,),
        in_specs=[pl.BlockSpec((None, 8, pos8.shape[2]), lambda s: (s, 0, 0)),
                  pl.BlockSpec((None, w1.shape[1], NSA_DH), lambda s: (s, 0, 0))],
        out_specs=pl.BlockSpec((None, 1, NSA_DH), lambda s: (s, 0, 0)),
        out_shape=jax.ShapeDtypeStruct((2, 1, NSA_DH), F32),
        name="nsa_posterm",
    )(pos8, w1)


def _eye(n, scale=1.0):
    rows = lax.broadcasted_iota(I32, (n, n), 0)
    cols = lax.broadcasted_iota(I32, (n, n), 1)
    return jnp.where(rows == cols, scale, 0.0).astype(BF16)


def _nsa_kernel(q_ref, qn_ref, kc_ref, vcin_ref, ksin_ref, vsin_ref, kwin_ref, vwin_ref, gate_ref,
                cb_ref, bt_ref, wb_ref, cov_ref, b31_ref, bcols_ref, o_ref,
                s_ref, ks_ref, vs_ref, kw_ref, vw_ref, vc_ref, qt_ref, *, n_sel):
    gi = pl.program_id(1)
    i = pl.program_id(2)
    q0 = i * QB
    r4 = NSA_HPG * QB
    ns = cov_ref.shape[0]
    eye_dh = _eye(NSA_DH)

    def group_lanes(x):
        return jnp.where(gi == 0, x[:, 0:NSA_DH], x[:, NSA_DH:2 * NSA_DH])

    def q_transposed(ref):
        eye_q = _eye(NSA_DH, NSA_DH ** -0.5)
        qb = ref[...]
        return jnp.concatenate([_dot_nt(eye_q, qb[:, h * NSA_DH:(h + 1) * NSA_DH]) for h in range(NSA_HPG)],
                               axis=1).astype(BF16)

    @pl.when(i == 0)
    def _():
        s_len = ksin_ref.shape[0]
        ones_rows = jnp.where(lax.broadcasted_iota(I32, (AUG_ROWS, SEL_TK), 0) == 0, 1.0, 0.0).astype(BF16)
        ones_rows_qb = jnp.where(lax.broadcasted_iota(I32, (AUG_ROWS, QB), 0) == 0, 1.0, 0.0).astype(BF16)
        tail = jnp.where(lax.broadcasted_iota(I32, (SEL_TK, NSA_DH), 1) < 2, 1.0, 0.0).astype(BF16)
        depth = ks_ref.shape[-1]
        vc_ref[...] = _dot_nt(eye_dh, vcin_ref[...]).astype(BF16)
        qt_ref[...] = q_transposed(q_ref)
        pad_flag = jnp.where(lax.broadcasted_iota(I32, (WINDOW, 2 * NSA_DH), 1) == NSA_DH, 1.0, 0.0)
        kw_ref[0:WINDOW, :] = pad_flag.astype(BF16)
        for u in range(WINDOW // QB):
            vw_ref[u] = jnp.zeros(vw_ref.shape[1:], BF16)

        def chunk(c, _):
            r0 = pl.multiple_of(c * SEL_TK, SEL_TK)
            rows = pl.ds(r0, SEL_TK)
            k_aug = jnp.concatenate([group_lanes(ksin_ref[rows, :]), tail], axis=1)
            ks_ref[rows, 0:ns] = bcols_ref[rows, :]
            ks_ref[rows, ns:ns + 2 * NSA_DH] = k_aug
            if depth > ns + 2 * NSA_DH:
                ks_ref[rows, ns + 2 * NSA_DH:depth] = jnp.zeros((SEL_TK, depth - ns - 2 * NSA_DH), BF16)
            v_t = _dot_nt(eye_dh, group_lanes(vsin_ref[rows, :])).astype(BF16)
            vs_ref[c] = jnp.concatenate([v_t, ones_rows], axis=0)
            kw_ref[pl.ds(WINDOW + r0, SEL_TK), :] = jnp.concatenate(
                [group_lanes(kwin_ref[rows, :]), jnp.zeros((SEL_TK, NSA_DH), BF16)], axis=1)
            vw_t = _dot_nt(eye_dh, group_lanes(vwin_ref[rows, :])).astype(BF16)
            for u in range(SEL_TK // QB):
                vw_ref[WINDOW // QB + c * (SEL_TK // QB) + u] = jnp.concatenate(
                    [vw_t[:, u * QB:(u + 1) * QB], ones_rows_qb], axis=0)
            return 0

        lax.fori_loop(0, s_len // SEL_TK, chunk, 0)

    q = qt_ref[...]

    step_rows = QB // CMP_STRIDE
    cb = cb_ref[pl.ds(pl.multiple_of(i * step_rows, step_rows), kc_ref.shape[0]), :]
    s_c = _dot(kc_ref[...], q) + cb
    m_c = jnp.maximum(jnp.max(s_c, axis=0, keepdims=True), 0.1 * NEG_INF)
    p_c = jnp.exp(s_c - m_c)
    l_c = jnp.sum(p_c, axis=0, keepdims=True)
    p_c = p_c / jnp.maximum(l_c, 1e-30)
    o_c = _dot(vc_ref[...], p_c.astype(BF16))

    psum = p_c[:, 0:QB] + p_c[:, QB:2 * QB] + p_c[:, 2 * QB:3 * QB] + p_c[:, 3 * QB:4 * QB]
    p_hi = psum.astype(BF16)
    p_lo = (psum - p_hi.astype(F32)).astype(BF16)
    cov = cov_ref[...]
    imp_t = _dot(cov, p_hi) + _dot(cov, p_lo)
    ns = imp_t.shape[0]

    wk = WINDOW + QB
    kwin = kw_ref[pl.ds(pl.multiple_of(q0, QB), wk), :]
    pad_mask_rows = jnp.where(lax.broadcasted_iota(I32, (NSA_DH, r4), 0) == 0, -MASK_BIG, 0.0).astype(BF16)
    s_w = _dot(kwin, jnp.concatenate([q, pad_mask_rows], axis=0)) + wb_ref[...]
    m_w = jnp.max(s_w, axis=0, keepdims=True)
    p_w = jnp.exp(s_w - m_w).astype(BF16)
    acc_w = jnp.zeros((NSA_DH + AUG_ROWS, r4), F32)
    for u in range(wk // QB):
        acc_w = acc_w + _dot(vw_ref[i + u], p_w[u * QB:(u + 1) * QB])
    o_w = acc_w[0:NSA_DH] / acc_w[NSA_DH:NSA_DH + 1]

    jidx = lax.broadcasted_iota(I32, (ns, QB), 0)
    t = q0 + lax.broadcasted_iota(I32, (ns, QB), 1)
    cur = jnp.right_shift(t, 6)
    eligible = jidx <= cur
    forced = (jidx == 0) | (jidx == cur) | (jidx == cur - 1)
    score = jnp.where(eligible, jnp.where(forced, -2.0, imp_t), -1.0)
    sel = jnp.where(forced, 1.0, 0.0)
    for _ in range(n_sel - 3):
        mx = jnp.max(score, axis=0, keepdims=True)
        first = jnp.min(jnp.where(score == mx, jidx, ns), axis=0, keepdims=True)
        pick = jidx == first
        sel = jnp.where(pick, 1.0, sel)
        score = jnp.where(pick, -2.0, score)
    notsel = jnp.where(eligible, 1.0 - sel, 1.0).astype(BF16)

    depth = ks_ref.shape[-1]
    pad_rows = depth - ns - NSA_DH - AUG_ROWS
    qa = jnp.concatenate([jnp.concatenate([notsel] * NSA_HPG, axis=1), q, b31_ref[...],
                          jnp.zeros((pad_rows, r4), BF16)], axis=0)
    sub = SEL_TK // QB
    n_tiles = i // sub + 1
    n_far = jnp.maximum(i - N_OFF + 2, 0) // sub

    def put_scores(jt, slot):
        k0 = pl.multiple_of(jt * SEL_TK, SEL_TK)
        s_ref[slot] = _dot(ks_ref[pl.ds(k0, SEL_TK), :], qa)

    def absorb(jt, slot, carry, near):
        m_prev, acc = carry
        s = s_ref[slot]
        if near:
            offs = [jnp.clip(i - sub * jt - u, 0, N_OFF - 1) for u in range(sub)]
            s = s + jnp.concatenate([bt_ref[o] for o in offs], axis=0)
        m_new = jnp.maximum(m_prev, jnp.max(s, axis=0, keepdims=True))
        alpha = jnp.exp(m_prev - m_new)
        p = jnp.exp(s - m_new).astype(BF16)
        return m_new, alpha * acc + _dot(vs_ref[jt], p)

    def pair_body(pp, carry, near):
        ja = 2 * pp
        put_scores(ja + 1, 1)
        carry = absorb(ja, 0, carry, near)
        put_scores(jnp.minimum(ja + 2, last_tile), 0)
        return absorb(ja + 1, 1, carry, near)

    n_pairs = (n_tiles + 1) // 2
    last_tile = 2 * n_pairs - 1
    far_pairs = n_far // 2
    put_scores(0, 0)
    carry = (jnp.full((1, r4), -1e38, F32), jnp.zeros((NSA_DH + AUG_ROWS, r4), F32))
    carry = lax.fori_loop(0, far_pairs, functools.partial(pair_body, near=False), carry)
    _, acc_s = lax.fori_loop(far_pairs, n_pairs, functools.partial(pair_body, near=True), carry)
    o_s = acc_s[0:NSA_DH] / acc_s[NSA_DH:NSA_DH + 1]

    gates = jax.nn.sigmoid(gate_ref[...])
    o = (gates[0:1] * o_c + gates[1:2] * o_s + gates[2:3] * o_w).astype(BF16)
    qt_ref[...] = q_transposed(qn_ref)
    eye_l = _eye(LANES)
    o_ref[...] = jnp.concatenate(
        [jnp.concatenate([_dot_nt(eye_l, o[:, h * QB + u * LANES:h * QB + (u + 1) * LANES])
                          for u in range(QB // LANES)], axis=0) for h in range(NSA_HPG)],
        axis=1).astype(o_ref.dtype)


def _nsa_attention(pb, c_qa, c_kv, kvc, gates, cb, bt, wb, cov_t, b31, blockcols, b, s, n_sel):
    g, dh = NSA_GROUPS, NSA_DH
    r4 = NSA_HPG * QB
    nqb = s // QB
    nc = kvc.shape[3]
    ns = cov_t.shape[0]
    depth = -(-(ns + 2 * dh) // LANES) * LANES
    n_tiles = s // SEL_TK
    assert n_tiles % 2 == 0, "the selected sweep consumes key tiles in pairs"
    qw = NSA_HPG * dh
    once = pl.Buffered(1)
    slab = lambda k: pl.BlockSpec((s, 2 * dh), lambda bi, gi, i: (bi, c_kv // (2 * dh) + k), pipeline_mode=once)
    per_g = lambda a: pl.BlockSpec((None,) + a.shape[1:], lambda bi, gi, i: (gi,) + (0,) * (a.ndim - 1),
                                   pipeline_mode=once)
    const = lambda a: pl.BlockSpec(a.shape, lambda bi, gi, i: (0,) * a.ndim, pipeline_mode=once)
    return pl.pallas_call(
        functools.partial(_nsa_kernel, n_sel=n_sel),
        grid=(b, g, nqb),
        in_specs=[
            pl.BlockSpec((QB, qw), lambda bi, gi, i: (bi * nqb + i, c_qa // qw + gi)),
            pl.BlockSpec((QB, qw), lambda bi, gi, i: (bi * nqb + jnp.minimum(i + 1, nqb - 1), c_qa // qw + gi)),
            pl.BlockSpec((None, None, None, nc, dh), lambda bi, gi, i: (0, bi, gi, 0, 0)),
            pl.BlockSpec((None, None, None, nc, dh), lambda bi, gi, i: (1, bi, gi, 0, 0)),
            slab(2), slab(3), slab(4), slab(5),
            pl.BlockSpec((None, None, None, 3, r4), lambda bi, gi, i: (bi, gi, i, 0, 0)),
            per_g(cb), per_g(bt), per_g(wb), const(cov_t), per_g(b31), const(blockcols),
        ],
        out_specs=pl.BlockSpec((QB, qw), lambda bi, gi, i: (bi * nqb + i, gi)),
        out_shape=jax.ShapeDtypeStruct((b * s, g * qw), BF16),
        scratch_shapes=[pltpu.VMEM((2, SEL_TK, r4), F32),
                        pltpu.VMEM((s, depth), BF16),
                        pltpu.VMEM((n_tiles, dh + AUG_ROWS, SEL_TK), BF16),
                        pltpu.VMEM((s + WINDOW, 2 * dh), BF16),
                        pltpu.VMEM(((s + WINDOW) // QB, dh + AUG_ROWS, QB), BF16),
                        pltpu.VMEM((dh, nc), BF16),
                        pltpu.VMEM((dh, r4), BF16)],
        compiler_params=_cparams(("arbitrary",) * 3),
        name="nsa_attention",
    )(pb, pb, kvc, kvc, pb, pb, pb, pb, gates, cb, bt, wb, cov_t, b31, blockcols)


def _ret_kernel(q_ref, k_ref, v_ref, g_ref, cos_ref, sin_ref, dm_ref, qd_ref, kd_ref, cd_ref,
                o_ref, state_ref):
    n = pl.program_id(1)

    @pl.when(n == 0)
    def _():
        state_ref[...] = jnp.zeros_like(state_ref)

    lane = lax.broadcasted_iota(I32, (RET_CHUNK, LANES), 1)
    first_half = (lane & (RET_DK - 1)) < (RET_DK // 2)

    def rot(x, cosv, sinv):
        partner = jnp.where(first_half, pltpu.roll(x, LANES - RET_DK // 2, axis=1),
                            pltpu.roll(x, RET_DK // 2, axis=1))
        return x * cosv + partner * sinv

    states = [state_ref[h] for h in range(RET_HEADS)]
    for cc in range(RET_STEP):
        rows = slice(cc * RET_CHUNK, (cc + 1) * RET_CHUNK)
        cosv = cos_ref[rows, :]
        sinv = sin_ref[rows, :]
        for hp in range(RET_HEADS // 2):
            ls = slice(hp * LANES, (hp + 1) * LANES)
            q2 = rot(q_ref[rows, ls], cosv, sinv)
            k2 = rot(k_ref[rows, ls], cosv, sinv) * (RET_DK ** -0.5)
            qdec = (q2 * qd_ref[:, ls]).astype(BF16)
            kdec_t = (k2 * kd_ref[:, ls]).T.astype(BF16)
            q2b = q2.astype(BF16)
            k2b = k2.astype(BF16)
            for e in range(2):
                h = 2 * hp + e
                hs = slice(e * RET_DK, (e + 1) * RET_DK)
                vs = slice(h * RET_DV, (h + 1) * RET_DV)
                v = v_ref[rows, vs]
                st = states[h]
                sc = _dot_nt(q2b[:, hs], k2b[:, hs]) * dm_ref[h]
                y = _dot(sc.astype(BF16), v) + _dot(qdec[:, hs], st.astype(BF16))
                states[h] = cd_ref[h] * st + _dot(kdec_t[hs, :], v)
                mu = jnp.mean(y, axis=-1, keepdims=True)
                var = jnp.mean(jnp.square(y - mu), axis=-1, keepdims=True)
                yn = (y - mu) * lax.rsqrt(var + GN_EPS)
                o_ref[rows, vs] = (jax.nn.silu(g_ref[rows, vs]) * yn).astype(o_ref.dtype)
    for h in range(RET_HEADS):
        state_ref[h] = states[h]


def _retention(pf, pb, b, s, col_q, col_k, col_v, col_g, tabs):
    cosf, sinf, dm, qd, kd, cd = tabs
    c = RET_CHUNK * RET_STEP
    n = s // c
    t = b * s
    wqk = RET_HEADS * RET_DK
    wv = RET_HEADS * RET_DV
    row = lambda bi, ni: bi * n + ni
    full = lambda a: pl.BlockSpec(a.shape, lambda bi, ni: (0,) * a.ndim)
    return pl.pallas_call(
        _ret_kernel,
        grid=(b, n),
        in_specs=[
            pl.BlockSpec((c, wqk), lambda bi, ni: (row(bi, ni), col_q // wqk)),
            pl.BlockSpec((c, wqk), lambda bi, ni: (row(bi, ni), col_k // wqk)),
            pl.BlockSpec((c, wv), lambda bi, ni: (row(bi, ni), col_v // wv)),
            pl.BlockSpec((c, wv), lambda bi, ni: (row(bi, ni), col_g // wv)),
            pl.BlockSpec((c, LANES), lambda bi, ni: (ni, 0)),
            pl.BlockSpec((c, LANES), lambda bi, ni: (ni, 0)),
            full(dm), full(qd), full(kd), full(cd),
        ],
        out_specs=pl.BlockSpec((c, wv), lambda bi, ni: (row(bi, ni), 0)),
        out_shape=jax.ShapeDtypeStruct((t, wv), BF16),
        scratch_shapes=[pltpu.VMEM((RET_HEADS, RET_DK, RET_DV), F32)],
        compiler_params=_cparams(("arbitrary",) * 2),
        name="retention",
    )(pf, pf, pb, pf, cosf, sinf, dm, qd, kd, cd)


def _retention_tables(s):
    half = RET_DK // 2
    inv = ROPE_BASE ** (-jnp.arange(half, dtype=F32) * 2.0 / RET_DK)
    ang = jnp.arange(s, dtype=F32)[:, None] * inv[None, :]
    cos, sin = jnp.cos(ang), jnp.sin(ang)
    cosf = jnp.concatenate([cos, cos, cos, cos], axis=1)
    sinf = jnp.concatenate([-sin, sin, -sin, sin], axis=1)
    lg = jnp.log(1.0 - 2.0 ** (-5.0 - jnp.arange(RET_HEADS, dtype=F32)))
    i = jnp.arange(RET_CHUNK, dtype=F32)
    diff = i[:, None] - i[None, :]
    dm = jnp.where(diff >= 0, jnp.exp(jnp.maximum(diff, 0.0)[None] * lg[:, None, None]), 0.0)
    kdec = jnp.exp((RET_CHUNK - 1.0 - i)[None, :] * lg[:, None])
    qdec = jnp.exp((i + 1.0)[None, :] * lg[:, None])
    expand = lambda a: jnp.repeat(a.T, RET_DK, axis=1)
    cd = jnp.broadcast_to(jnp.exp(RET_CHUNK * lg)[:, None, None], (RET_HEADS, 1, LANES))
    return cosf, sinf, dm, expand(qdec), expand(kdec), cd


def _layer_norm(z, g, b):
    mu = jnp.mean(z, axis=-1, keepdims=True)
    var = jnp.mean(jnp.square(z - mu), axis=-1, keepdims=True)
    return (z - mu) * lax.rsqrt(var + LN_EPS) * g + b


def _merge_kernel(ya_ref, yb_ref, ma_ref, mb_ref, x_ref, wa_ref, wb_ref, wo_ref, g_ref, b_ref,
                  wr_ref, rb_ref, tri_ref, o_ref, eidx_ref, wts_ref, rank_ref, cnt_ref, carry_ref, *, alpha):
    merged = (jax.nn.sigmoid(ma_ref[...]) * _dot(ya_ref[...], wa_ref[...])
              + jax.nn.sigmoid(mb_ref[...]) * _dot(yb_ref[...], wb_ref[...]))
    z = alpha * x_ref[...] + _dot(merged.astype(BF16), wo_ref[...])
    x1 = _layer_norm(z, g_ref[...], b_ref[...])
    o_ref[...] = x1
    _route(x1, wr_ref, rb_ref, tri_ref, eidx_ref, wts_ref, rank_ref, cnt_ref, carry_ref)


def _merge(ya, yb, pf, col_ma, col_mb, x, wa, wb, wo, g, bb, wr_t, rb_col, tri, alpha):
    t, d = x.shape
    tm = tri.shape[0]
    full = lambda a: pl.BlockSpec(a.shape, lambda i: (0,) * a.ndim)
    out2 = lambda dt: jax.ShapeDtypeStruct((2, t), dt)
    row2 = pl.BlockSpec((2, tm), lambda i: (0, i))
    return pl.pallas_call(
        functools.partial(_merge_kernel, alpha=alpha),
        grid=(t // tm,),
        in_specs=[pl.BlockSpec((tm, ya.shape[1]), lambda i: (i, 0)),
                  pl.BlockSpec((tm, yb.shape[1]), lambda i: (i, 0)),
                  pl.BlockSpec((tm, d), lambda i: (i, col_ma // d)),
                  pl.BlockSpec((tm, d), lambda i: (i, col_mb // d)),
                  pl.BlockSpec((tm, d), lambda i: (i, 0)),
                  full(wa), full(wb), full(wo), full(g), full(bb), full(wr_t), full(rb_col), full(tri)],
        out_specs=[pl.BlockSpec((tm, d), lambda i: (i, 0)), row2, row2, row2,
                   pl.BlockSpec((N_EXPERTS, LANES), lambda i: (0, 0))],
        out_shape=[jax.ShapeDtypeStruct((t, d), F32), out2(I32), out2(F32), out2(I32),
                   jax.ShapeDtypeStruct((N_EXPERTS, LANES), F32)],
        scratch_shapes=[pltpu.VMEM((N_EXPERTS, LANES), F32)],
        compiler_params=_cparams(("arbitrary",)),
        name="merge_ln1_router",
    )(ya, yb, pf, pf, x, wa, wb, wo, g, bb, wr_t, rb_col, tri)


def _route(x, wr_ref, rb_ref, tri_ref, eidx_ref, wts_ref, rank_ref, cnt_ref, carry_ref):
    step = pl.program_id(0)

    @pl.when(step == 0)
    def _():
        carry_ref[...] = jnp.zeros_like(carry_ref)

    wr = wr_ref[...]
    w_hi = wr.astype(BF16)
    w_lo = (wr - w_hi.astype(F32)).astype(BF16)
    x_hi = x.astype(BF16)
    x_lo = (x - x_hi.astype(F32)).astype(BF16)
    logits = _dot_nt(w_hi, x_hi) + _dot_nt(w_hi, x_lo) + _dot_nt(w_lo, x_hi)
    tm = logits.shape[1]
    mx = jnp.max(logits, axis=0, keepdims=True)
    ex = jnp.exp(logits - mx)
    probs = ex / jnp.sum(ex, axis=0, keepdims=True)
    sel = probs + rb_ref[...]
    rows = [sel[e:e + 1] for e in range(N_EXPERTS)]
    gscore = []
    for gi in range(N_GROUPS):
        r = rows[gi * EPG:(gi + 1) * EPG]
        best2 = None
        for a in range(EPG):
            for c in range(a + 1, EPG):
                pair = r[a] + r[c]
                best2 = pair if best2 is None else jnp.maximum(best2, pair)
        gscore.append(best2)
    best = jnp.zeros((1, tm), I32)
    bscore = gscore[0]
    for gi in range(1, N_GROUPS):
        better = gscore[gi] > bscore
        best = jnp.where(better, gi, best)
        bscore = jnp.where(better, gscore[gi], bscore)
    ing = []
    for a in range(EPG):
        v = rows[a]
        for gi in range(1, N_GROUPS):
            v = jnp.where(best == gi, rows[gi * EPG + a], v)
        ing.append(v)
    i1 = jnp.zeros((1, tm), I32)
    v1 = ing[0]
    for a in range(1, EPG):
        better = ing[a] > v1
        i1 = jnp.where(better, a, i1)
        v1 = jnp.where(better, ing[a], v1)
    i2 = jnp.full((1, tm), -1, I32)
    v2 = jnp.full((1, tm), -jnp.inf, F32)
    for a in range(EPG):
        better = (i1 != a) & (ing[a] > v2)
        i2 = jnp.where(better, a, i2)
        v2 = jnp.where(better, ing[a], v2)
    e0 = best * EPG + i1
    e1 = best * EPG + i2
    eio = lax.broadcasted_iota(I32, (N_EXPERTS, tm), 0)
    oh0 = (eio == e0).astype(F32)
    oh1 = (eio == e1).astype(F32)
    p0 = jnp.sum(oh0 * probs, axis=0, keepdims=True)
    p1 = jnp.sum(oh1 * probs, axis=0, keepdims=True)
    den = p0 + p1
    eidx_ref[0:1, :] = e0
    eidx_ref[1:2, :] = e1
    wts_ref[0:1, :] = p0 / den
    wts_ref[1:2, :] = p1 / den
    tot = oh0 + oh1
    prefix = _dot(tot.astype(BF16), tri_ref[...]) + carry_ref[:, 0:1]
    rank_ref[0:1, :] = jnp.sum(oh0 * prefix, axis=0, keepdims=True).astype(I32)
    rank_ref[1:2, :] = jnp.sum(oh1 * prefix, axis=0, keepdims=True).astype(I32)
    carry_ref[...] = carry_ref[...] + jnp.sum(tot, axis=1, keepdims=True)
    cnt_ref[...] = carry_ref[...]


def _dispatch_kernel(dest_ref, zflag_ref, x_ref, xs_ref, zero_ref, sem, zsem, rows_ref, *, t_total):
    step = pl.program_id(0)
    tm = x_ref.shape[0]

    @pl.when(step == 0)
    def _():
        zero_ref[...] = jnp.zeros_like(zero_ref)

        def zero_block(bi, _):
            @pl.when(zflag_ref[bi] == 1)
            def _():
                r0 = pl.multiple_of(bi * MOE_BM, MOE_BM)
                cp = pltpu.make_async_copy(zero_ref, xs_ref.at[pl.ds(r0, MOE_BM)], zsem)
                cp.start()
                cp.wait()
            return 0

        lax.fori_loop(0, zflag_ref.shape[0], zero_block, 0)

    rows_ref[...] = x_ref[...].reshape(rows_ref.shape)

    def copy(r, slot):
        dst = dest_ref[slot * t_total + step * tm + r]
        return pltpu.make_async_copy(rows_ref.at[r], xs_ref.at[dst], sem)

    def issue(r, _):
        copy(r, 0).start(priority=0)
        copy(r, 1).start(priority=1)
        return 0

    def drain(r, _):
        copy(r, 0).wait()
        copy(r, 1).wait()
        return 0

    lax.fori_loop(0, tm, issue, 0, unroll=ROW_DMA_UNROLL)
    lax.fori_loop(0, tm, drain, 0, unroll=ROW_DMA_UNROLL)


def _dispatch(dest_flat, zflag, x, rows, tm):
    t, d = x.shape
    sub = d // LANES
    return pl.pallas_call(
        functools.partial(_dispatch_kernel, t_total=t),
        grid_spec=pltpu.PrefetchScalarGridSpec(
            num_scalar_prefetch=2,
            grid=(t // tm,),
            in_specs=[pl.BlockSpec((tm, d), lambda i, dr, zf: (i, 0))],
            out_specs=pl.BlockSpec(memory_space=pl.ANY),
            scratch_shapes=[pltpu.VMEM((MOE_BM, sub, LANES), x.dtype), pltpu.SemaphoreType.DMA,
                            pltpu.SemaphoreType.DMA, pltpu.VMEM((tm, sub, LANES), x.dtype)],
        ),
        out_shape=jax.ShapeDtypeStruct((rows, sub, LANES), x.dtype),
        compiler_params=pltpu.CompilerParams(dimension_semantics=("arbitrary",), has_side_effects=True,
                                             vmem_limit_bytes=VMEM_LIMIT),
        name="moe_dispatch",
    )(dest_flat, zflag, x)


def _expert_kernel(be_ref, na_ref, x_ref, wg_ref, wu_ref, wd_ref, o_ref, wg_s, wu_s, wd_s):
    i = pl.program_id(0)
    active = i < na_ref[0]

    @pl.when((i == 0) | (be_ref[i] != be_ref[jnp.maximum(i - 1, 0)]))
    def _():
        wg_s[...] = wg_ref[...].astype(BF16)
        wu_s[...] = wu_ref[...].astype(BF16)
        wd_s[...] = wd_ref[...].astype(BF16)

    @pl.when(active)
    def _():
        xb = x_ref[...].reshape(x_ref.shape[0], wg_s.shape[0]).astype(BF16)
        gate = _dot(xb, wg_s[...])
        up = _dot(xb, wu_s[...])
        hid = (jax.nn.silu(gate) * up).astype(BF16)
        o_ref[...] = _dot(hid, wd_s[...]).reshape(o_ref.shape)

    @pl.when(jnp.logical_not(active))
    def _():
        o_ref[...] = jnp.zeros_like(o_ref)


def _experts(blk_e, n_active, xs, wg, wu, wd, layer):
    rows = xs.shape[0]
    nb = rows // MOE_BM
    act = lambda i, be, na: jnp.minimum(i, na[0] - 1)
    wspec = lambda a: pl.BlockSpec((None, None) + a.shape[2:], lambda i, be, na: (layer, be[i], 0, 0))
    return pl.pallas_call(
        _expert_kernel,
        grid_spec=pltpu.PrefetchScalarGridSpec(
            num_scalar_prefetch=2,
            grid=(nb,),
            in_specs=[pl.BlockSpec((MOE_BM,) + xs.shape[1:], lambda i, be, na: (act(i, be, na), 0, 0)),
                      wspec(wg), wspec(wu), wspec(wd)],
            out_specs=pl.BlockSpec((MOE_BM,) + xs.shape[1:], lambda i, be, na: (i, 0, 0)),
            scratch_shapes=[pltpu.VMEM(wg.shape[2:], BF16), pltpu.VMEM(wu.shape[2:], BF16),
                            pltpu.VMEM(wd.shape[2:], BF16)],
        ),
        out_shape=jax.ShapeDtypeStruct(xs.shape, F32),
        compiler_params=_cparams(("arbitrary",)),
        name="moe_experts",
    )(blk_e, n_active, xs, wg, wu, wd)


def _combine_kernel(dest_ref, x_ref, w_ref, g_ref, b_ref, ys_ref, o_ref, ob_ref, buf_ref, sem, *, alpha, t_total):
    step = pl.program_id(0)
    n_steps = pl.num_programs(0)
    tm = x_ref.shape[0]

    def copy(tile, r, slot):
        src = dest_ref[slot * t_total + tile * tm + r]
        half = tile % 2
        return pltpu.make_async_copy(ys_ref.at[src], buf_ref.at[half, slot, r], sem.at[half])

    def issue_tile(tile):
        def body(r, _):
            copy(tile, r, 0).start(priority=0)
            copy(tile, r, 1).start(priority=1)
            return 0
        lax.fori_loop(0, tm, body, 0, unroll=ROW_DMA_UNROLL)

    @pl.when(step == 0)
    def _():
        issue_tile(0)

    @pl.when(step + 1 < n_steps)
    def _():
        issue_tile(step + 1)

    def drain(r, _):
        copy(step, r, 0).wait()
        copy(step, r, 1).wait()
        return 0

    lax.fori_loop(0, tm, drain, 0, unroll=ROW_DMA_UNROLL)
    w = w_ref[...]
    cur = step % 2
    y0 = buf_ref[cur, 0].reshape(x_ref.shape)
    y1 = buf_ref[cur, 1].reshape(x_ref.shape)
    z = alpha * x_ref[...] + w[:, 0:1] * y0 + w[:, 1:2] * y1
    y = _layer_norm(z, g_ref[...], b_ref[...])
    o_ref[...] = y
    ob_ref[...] = y.astype(BF16)


def _combine(dest_flat, x, wts_t, g, bb, ys, alpha, tm):
    t, d = x.shape
    return pl.pallas_call(
        functools.partial(_combine_kernel, alpha=alpha, t_total=t),
        grid_spec=pltpu.PrefetchScalarGridSpec(
            num_scalar_prefetch=1,
            grid=(t // tm,),
            in_specs=[pl.BlockSpec((tm, d), lambda i, dr: (i, 0)),
                      pl.BlockSpec((tm, 2), lambda i, dr: (i, 0)),
                      pl.BlockSpec(g.shape, lambda i, dr: (0, 0)),
                      pl.BlockSpec(bb.shape, lambda i, dr: (0, 0)),
                      pl.BlockSpec(memory_space=pl.ANY)],
            out_specs=[pl.BlockSpec((tm, d), lambda i, dr: (i, 0)), pl.BlockSpec((tm, d), lambda i, dr: (i, 0))],
            scratch_shapes=[pltpu.VMEM((2, 2, tm, d // LANES, LANES), F32), pltpu.SemaphoreType.DMA((2,))],
        ),
        out_shape=[jax.ShapeDtypeStruct((t, d), F32), jax.ShapeDtypeStruct((t, d), BF16)],
        compiler_params=_cparams(("arbitrary",)),
        name="moe_combine_ln2",
    )(dest_flat, x, wts_t, g, bb, ys)


def _attention_tables(rel_bias, s):
    nc = s // CMP_STRIDE
    nqb = s // QB
    iq = jnp.arange(QB, dtype=I32)[None, None, :]
    x = jnp.arange(nc + (QB // CMP_STRIDE) * nqb, dtype=I32)[:, None]
    rel_c = iq[0] + CMP_STRIDE * (x - nc + 1) - (CMP_LEN - 1)
    jk = jnp.arange(QB, dtype=I32)[None, :, None]
    rel_t = jnp.arange(N_OFF, dtype=I32)[:, None, None] * QB + iq - jk
    jw = jnp.arange(WINDOW + QB, dtype=I32)[:, None]
    rel_w = iq[0] - jw + WINDOW
    rel_w = jnp.where(rel_w < WINDOW, rel_w, -1)
    flat = rel_bias.reshape(-1)
    r4 = NSA_HPG * QB
    cb = _bias_table(flat, jnp.maximum(rel_c, -1), rel_c.shape[0])
    bt = _bias_table(flat, jnp.maximum(rel_t, -1).reshape(N_OFF * QB, QB), QB, delta=True)
    bt = bt.reshape(NSA_GROUPS, N_OFF, QB, r4)
    wb = _bias_table(flat, jnp.maximum(rel_w, -1), WINDOW + QB)
    sat = jnp.repeat(rel_bias[REL_BUCKETS - 1].reshape(NSA_GROUPS, 1, NSA_HPG), QB, axis=2)
    hi = sat.astype(BF16)
    lo = (sat - hi.astype(F32)).astype(BF16)
    b31 = jnp.concatenate([hi, lo, jnp.zeros((NSA_GROUPS, AUG_ROWS - 2, r4), BF16)], axis=1)
    return cb, bt, wb, b31


def _selection_tables(s):
    nc = s // CMP_STRIDE
    ns = s // SLC_BLOCK
    c_start = np.arange(nc) * CMP_STRIDE
    s_start = np.arange(ns) * SLC_BLOCK
    cover = np.maximum(np.minimum(c_start[:, None] + CMP_LEN, s_start[None, :] + SLC_BLOCK)
                       - np.maximum(c_start[:, None], s_start[None, :]), 0).astype(np.float32) / CMP_LEN
    cover[nc - 1] = 0.0
    cover_rev_t = np.ascontiguousarray(cover[::-1].T)
    blockcols = np.where(np.arange(s)[:, None] // SLC_BLOCK == np.arange(ns)[None, :], -MASK_BIG, 0.0)
    return jnp.asarray(cover_rev_t, BF16), jnp.asarray(blockcols, BF16)


def _split_in_weights(w_in_l):
    o = np.cumsum((0, 512, 128, 128, 128, 128, 128, 128, 24, 512, 512, 1024, 1024, 1024, 1024))
    seg = lambda a: w_in_l[:, o[a]:o[a + 1]]
    wb = jnp.concatenate([seg(10), seg(0), seg(1), seg(2), seg(3), seg(4), seg(5), seg(6)], axis=1)
    gpad = jnp.pad(seg(7), ((0, 0), (0, LANES - 24)))
    wf = jnp.concatenate([seg(8), seg(9), seg(11), seg(12), seg(13), gpad], axis=1)
    return wb.astype(BF16), wf.astype(BF16)


def kernel(x, w_in, cmp_pos, cmp_w1, cmp_w2, w_br_a, w_br_b, w_o, ln1_g, ln1_b, w_router, router_bias,
           moe_w_gate, moe_w_up, moe_w_down, ln2_g, ln2_b, rel_bias):
    b, s, d = x.shape
    depth = w_in.shape[0]
    t = b * s
    alpha = (2 * depth) ** 0.25
    g, r, dh = NSA_GROUPS, NSA_HPG, NSA_DH
    nqb = s // QB
    nb16 = s // CMP_STRIDE
    n_sel = min(N_SELECT, s // SLC_BLOCK)

    cb, bt, wb, b31 = _attention_tables(rel_bias, s)
    cov_t, blockcols = _selection_tables(s)
    ret_tabs = _retention_tables(s)
    tri = jnp.asarray(np.triu(np.ones((512, 512), np.float32), 1), BF16)
    wr_t = w_router.T
    rb_col = router_bias.reshape(N_EXPERTS, 1)
    n_blk = t * 2 // MOE_BM + N_EXPERTS
    rows = n_blk * MOE_BM

    xf = x.reshape(t, d)
    xb = xf.astype(BF16)
    for l in range(depth):
        w_b, w_f = _split_in_weights(w_in[l])
        pb =_matmul(xb, w_b, BF16, 1024, 1152)
        pf = _matmul(xb, w_f, F32, 1024, 1408)
        c_qa = RET_HEADS * RET_DV
        c_kv = c_qa + NSA_HEADS * dh

        def kv_heads(slab):
            col = c_kv + slab * g * dh
            return pb[:, col:col + g * dh].reshape(b, s, g, dh).transpose(0, 2, 1, 3)
        x16 = jnp.stack([kv_heads(0), kv_heads(1)]).reshape(2, b, g, nb16, CMP_STRIDE * dh)
        w1 = cmp_w1[l].astype(BF16)
        half = CMP_STRIDE * dh
        w1ab = jnp.concatenate([w1[:, :half], w1[:, half:]], axis=2)
        pos8 = jnp.broadcast_to(cmp_pos[l].reshape(2, 1, CMP_LEN * dh), (2, 8, CMP_LEN * dh)).astype(BF16)
        kvc = _compress(x16, w1ab, _posterm(pos8, w1), cmp_w2[l].astype(BF16))
        gl = (pf[:, 4096:4096 + 24].reshape(b, nqb, QB, g, r, 3).transpose(0, 3, 1, 5, 4, 2)
              .reshape(b, g, nqb, 3, r * QB))
        ya = _nsa_attention(pb, c_qa, c_kv, kvc, gl, cb, bt, wb, cov_t, b31, blockcols, b, s, n_sel)

        yb = _retention(pf, pb, b, s, 0, 512, 0, 1024, ret_tabs)

        x1, eidx, wts, rank, cnt = _merge(ya, yb, pf, 2048, 3072, xf, w_br_a[l].astype(BF16),
                                          w_br_b[l].astype(BF16), w_o[l].astype(BF16), ln1_g[l].reshape(1, d),
                                          ln1_b[l].reshape(1, d), wr_t, rb_col, tri, alpha)

        counts = cnt[:, 0].astype(I32)
        padded = ((counts + MOE_BM - 1) // MOE_BM) * MOE_BM
        pad_end = jnp.cumsum(padded)
        pad_start = pad_end - padded
        e_iota = jnp.arange(N_EXPERTS, dtype=I32)[:, None, None]
        start_of = jnp.sum(jnp.where(eidx[None] == e_iota, pad_start[:, None, None], 0), axis=0)
        dest = (start_of + rank).reshape(-1)
        n_active = (pad_end[-1] // MOE_BM).astype(I32).reshape(1)
        blk = jnp.minimum(jnp.arange(n_blk, dtype=I32), n_active[0] - 1) * MOE_BM
        blk_e = jnp.minimum(jnp.sum(pad_end[None, :] <= blk[:, None], axis=1), N_EXPERTS - 1).astype(I32)
        blk_all = jnp.arange(n_blk, dtype=I32)
        last_of_expert = jnp.any(pad_end[None, :] == (blk_all[:, None] + 1) * MOE_BM, axis=1)
        zflag = (last_of_expert | (blk_all >= n_active[0])).astype(I32)
        xs = _dispatch(dest, zflag, x1, rows, 512)
        ys = _experts(blk_e, n_active, xs, moe_w_gate, moe_w_up, moe_w_down, l)
        xf, xb = _combine(dest, x1, wts.T, ln2_g[l].reshape(1, d), ln2_b[l].reshape(1, d), ys, alpha, 256)
    return xf.reshape(b, s, d)
```
